```python
import jax
import jax.numpy as jnp
from jax import lax
import numpy as np

D_MODEL = 2048
BATCH = 1
SEQ = 8192
DEPTH = 1

GRID_W = 64
CTX_LEN = 256
MIX_WIDTH = D_MODEL
ATT_WIDTH = MIX_WIDTH // 2
RWKV_WIDTH = MIX_WIDTH - ATT_WIDTH
ATT_HEAD_DIM = 128
N_Q_HEADS = ATT_WIDTH // ATT_HEAD_DIM
N_KV_HEADS = 2
GQA_GROUP = N_Q_HEADS // N_KV_HEADS
KV_WIDTH = N_KV_HEADS * ATT_HEAD_DIM
WINDOW = 128
BLOCK = 128
ROPE_BASE = 10000.0
RWKV_HEAD_DIM = 64
N_RWKV_HEADS = RWKV_WIDTH // RWKV_HEAD_DIM
LORA_W = 64
LORA_A = 64
LORA_G = 128
N_DIR = 2
ATT_IN = ATT_WIDTH + 2 * KV_WIDTH
RWKV_IN = 3 * RWKV_WIDTH + N_DIR * (LORA_W + LORA_A) + LORA_G
IN_WIDTH = ATT_IN + RWKV_IN
D_FF = 4 * D_MODEL
N_MOD = 6
NORM_EPS = 1e-6
GN_EPS = 64e-5

kernel_name = 'hybrid_swa_rwkv7_diffusion_block'


def rms_norm(x, g):
    xf = x.astype(jnp.float32)
    y = xf * lax.rsqrt(jnp.mean(xf * xf, axis=-1, keepdims=True) + NORM_EPS)
    return (y * g.astype(jnp.float32)).astype(x.dtype)


def modulate(x, g, shift, scale):
    return rms_norm(x, g) * (1 + scale) + shift


def squared_relu_mlp(h, w1, w2):
    return jnp.square(jax.nn.relu(h @ w1)) @ w2


def axial_rope_tables(n_tokens):
    rows = n_tokens // GRID_W
    row_pos = jnp.repeat(jnp.arange(rows, dtype=jnp.float32), GRID_W)
    col_pos = jnp.tile(jnp.arange(GRID_W, dtype=jnp.float32), rows)
    n_freq = ATT_HEAD_DIM // 4
    inv_freq = ROPE_BASE ** (-jnp.arange(n_freq, dtype=jnp.float32) / n_freq)
    ang_r = row_pos[:, None] * inv_freq[None, :]
    ang_c = col_pos[:, None] * inv_freq[None, :]
    return jnp.cos(ang_r), jnp.sin(ang_r), jnp.cos(ang_c), jnp.sin(ang_c)


def rope_rotate(x, cos, sin):
    n = cos.shape[-1]
    x1, x2 = x[..., :n], x[..., n:]
    cs, sn = cos[:, None, :], sin[:, None, :]
    return jnp.concatenate([x1 * cs - x2 * sn, x2 * cs + x1 * sn], axis=-1)


def apply_axial_rope(x, cos_r, sin_r, cos_c, sin_c):
    half = ATT_HEAD_DIM // 2
    out = jnp.concatenate([rope_rotate(x[..., :half], cos_r, sin_r),
                           rope_rotate(x[..., half:], cos_c, sin_c)], axis=-1)
    return out.astype(x.dtype)


def split_projection(p):
    B, T, _ = p.shape
    q = p[..., :ATT_WIDTH].reshape(B, T, N_Q_HEADS, ATT_HEAD_DIM)
    k = p[..., ATT_WIDTH:ATT_WIDTH + KV_WIDTH].reshape(B, T, N_KV_HEADS, ATT_HEAD_DIM)
    v = p[..., ATT_WIDTH + KV_WIDTH:ATT_IN].reshape(B, T, N_KV_HEADS, ATT_HEAD_DIM)
    return q, k, v, p[..., ATT_IN:]


def window_attention_with_context(q, k, v, kc, vc, sink):
    B, S = q.shape[:2]
    nb = S // BLOCK
    span = 3 * BLOCK
    scale = ATT_HEAD_DIM ** -0.5
    qb = q.reshape(B, nb, BLOCK, N_KV_HEADS, GQA_GROUP, ATT_HEAD_DIM)

    def band(t):
        tb = jnp.pad(t, ((0, 0), (BLOCK, BLOCK), (0, 0), (0, 0)))
        tb = tb.reshape(B, nb + 2, BLOCK, N_KV_HEADS, ATT_HEAD_DIM)
        return jnp.concatenate([tb[:, :-2], tb[:, 1:-1], tb[:, 2:]], axis=2)

    k_band, v_band = band(k), band(v)
    blk = jnp.arange(nb)[:, None, None]
    q_pos = blk * BLOCK + jnp.arange(BLOCK)[None, :, None]
    k_pos = (blk - 1) * BLOCK + jnp.arange(span)[None, None, :]
    valid = (jnp.abs(k_pos - q_pos) <= WINDOW) & (k_pos >= 0) & (k_pos < S)
    s_loc = jnp.einsum('bnqhgd,bnkhd->bnhgqk', qb, k_band).astype(jnp.float32) * scale
    s_loc = jnp.where(valid[None, :, None, None], s_loc, -jnp.inf)
    s_ctx = jnp.einsum('bnqhgd,bchd->bnhgqc', qb, kc).astype(jnp.float32) * scale
    s_sink = jnp.broadcast_to(sink.astype(jnp.float32).reshape(1, 1, N_KV_HEADS, GQA_GROUP, 1, 1),
                              s_ctx.shape[:-1] + (1,))
    p = jax.nn.softmax(jnp.concatenate([s_loc, s_ctx, s_sink], axis=-1), axis=-1).astype(v.dtype)
    o = (jnp.einsum('bnhgqk,bnkhd->bnqhgd', p[..., :span], v_band)
         + jnp.einsum('bnhgqc,bchd->bnqhgd', p[..., span:-1], vc))
    return o.reshape(B, S, ATT_WIDTH)


def context_self_attention(qc, kc, vc, sink):
    B, C = qc.shape[:2]
    scale = ATT_HEAD_DIM ** -0.5
    qg = qc.reshape(B, C, N_KV_HEADS, GQA_GROUP, ATT_HEAD_DIM)
    s = jnp.einsum('bqhgd,bkhd->bhgqk', qg, kc).astype(jnp.float32) * scale
    s_sink = jnp.broadcast_to(sink.astype(jnp.float32).reshape(1, N_KV_HEADS, GQA_GROUP, 1, 1),
                              s.shape[:-1] + (1,))
    p = jax.nn.softmax(jnp.concatenate([s, s_sink], axis=-1), axis=-1).astype(vc.dtype)
    o = jnp.einsum('bhgqk,bkhd->bqhgd', p[..., :-1], vc)
    return o.reshape(B, C, ATT_WIDTH)


def token_shift(x, mu_prev, mu_next):
    xp = jnp.pad(x[:, :-1], ((0, 0), (1, 0), (0, 0)))
    xn = jnp.pad(x[:, 1:], ((0, 0), (0, 1), (0, 0)))
    return x + mu_prev * (xp - x) + mu_next * (xn - x)


def rwkv_streams(slab, mu_prev, mu_next, w0, w2, a0, a2, g2, k_k, k_a):
    B, T, _ = slab.shape
    H, N, R = N_RWKV_HEADS, RWKV_HEAD_DIM, RWKV_WIDTH
    s = token_shift(slab, mu_prev, mu_next)
    r, k, v = s[..., :R], s[..., R:2 * R], s[..., 2 * R:3 * R]
    o1 = 3 * R + N_DIR * LORA_W
    o2 = o1 + N_DIR * LORA_A
    w_lo = s[..., 3 * R:o1].reshape(B, T, N_DIR, LORA_W)
    a_lo = s[..., o1:o2].reshape(B, T, N_DIR, LORA_A)
    g_lo = s[..., o2:]
    w_log = -jax.nn.softplus(-(w0 + jnp.einsum('btdl,dlr->btdr', jnp.tanh(w_lo), w2)).astype(jnp.float32)) - 0.5
    decay = jnp.exp(-jnp.exp(w_log))
    a = jax.nn.sigmoid((a0 + jnp.einsum('btdl,dlr->btdr', a_lo, a2)).astype(jnp.float32))
    g = jax.nn.sigmoid(g_lo) @ g2
    kkf = (k * k_k).astype(jnp.float32).reshape(B, T, H, N)
    kk = kkf / jnp.maximum(jnp.sqrt(jnp.sum(kkf * kkf, axis=-1, keepdims=True)), 1e-12)
    k_dir = k.astype(jnp.float32)[:, :, None, :] * (1 + (a - 1) * k_a.astype(jnp.float32))
    heads = lambda t: t.reshape(t.shape[:-1] + (H, N))
    return (heads(r), kk, heads(v), g, heads(decay), heads(a), heads(k_dir))


def rwkv_direction(state0, streams, d, emit):
    r, kk, v, g, decay, a, k_dir = streams
    xs = (r, decay[:, :, d], k_dir[:, :, d], v, -kk, kk * a[:, :, d])
    xs = tuple(jnp.moveaxis(t.astype(jnp.float32), 1, 0) for t in xs)

    def step(state, inp):
        r_t, w_t, k_t, v_t, a_t, b_t = inp
        sa = jnp.einsum('bhvk,bhk->bhv', state, a_t)
        state = (state * w_t[:, :, None, :] + sa[..., None] * b_t[:, :, None, :]
                 + v_t[..., None] * k_t[:, :, None, :])
        if emit:
            return state, jnp.einsum('bhvk,bhk->bhv', state, r_t)
        return state, None

    state, ys = lax.scan(step, state0, xs, reverse=(d == 1))
    if emit:
        ys = jnp.moveaxis(ys, 0, 1)
    return state, ys


def rwkv_readout(streams, y_f, y_b, r_k, ln_w, ln_b):
    r, kk, v, g, decay, a, k_dir = streams
    H, N = N_RWKV_HEADS, RWKV_HEAD_DIM
    y = y_f + y_b
    mu = jnp.mean(y, axis=-1, keepdims=True)
    var = jnp.mean(jnp.square(y - mu), axis=-1, keepdims=True)
    y = (y - mu) * lax.rsqrt(var + GN_EPS) * ln_w.astype(jnp.float32).reshape(H, N) + ln_b.astype(jnp.float32).reshape(H, N)
    k_bonus = 0.5 * (k_dir[:, :, 0] + k_dir[:, :, 1])
    y = y + jnp.sum(r.astype(jnp.float32) * k_bonus * r_k.astype(jnp.float32), axis=-1, keepdims=True) * v.astype(jnp.float32)
    B, T = y.shape[:2]
    return (y.reshape(B, T, RWKV_WIDTH) * g.astype(jnp.float32)).astype(g.dtype)


def setup_inputs(seed: int = 0) -> dict:
    key = jax.random.key(seed)
    ks = jax.random.split(key, 32)
    L, D, R = DEPTH, D_MODEL, RWKV_WIDTH
    f32 = jnp.float32

    def nrm(k, shape, scale):
        return jax.random.normal(k, shape, f32) * scale

    def gain(k, shape):
        return 1.0 + nrm(k, shape, 0.02)

    return {
        'x': nrm(ks[0], (BATCH, SEQ, D), 1.0),
        'c': nrm(ks[1], (BATCH, D), 1.0),
        'ctx': nrm(ks[2], (BATCH, CTX_LEN, D), 1.0),
        'c_ctx': nrm(ks[3], (D,), 1.0),
        'w_ada': nrm(ks[4], (L, D, N_MOD * D), D ** -0.5),
        'b_ada': nrm(ks[5], (L, N_MOD * D), 0.01),
        'g_pre_mix': gain(ks[6], (L, D)),
        'g_post_mix': gain(ks[7], (L, D)),
        'g_pre_mlp': gain(ks[8], (L, D)),
        'g_post_mlp': gain(ks[9], (L, D)),
        'w_in': nrm(ks[10], (L, D, IN_WIDTH), D ** -0.5),
        'attn_sink': nrm(ks[11], (L, N_Q_HEADS), 0.5),
        'rwkv_mu_prev': jax.random.uniform(ks[12], (L, RWKV_IN), f32, 0.0, 0.5),
        'rwkv_mu_next': jax.random.uniform(ks[13], (L, RWKV_IN), f32, 0.0, 0.5),
        'rwkv_w0': nrm(ks[14], (L, N_DIR, R), 0.5),
        'rwkv_w2': nrm(ks[15], (L, N_DIR, LORA_W, R), 0.5 * LORA_W ** -0.5),
        'rwkv_a0': nrm(ks[16], (L, N_DIR, R), 0.5),
        'rwkv_a2': nrm(ks[17], (L, N_DIR, LORA_A, R), 0.5 * LORA_A ** -0.5),
        'rwkv_g2': nrm(ks[18], (L, LORA_G, R), LORA_G ** -0.5),
        'rwkv_k_k': 0.85 + nrm(ks[19], (L, R), 0.05),
        'rwkv_k_a': 1.0 + nrm(ks[20], (L, R), 0.05),
        'rwkv_r_k': nrm(ks[21], (L, N_RWKV_HEADS, RWKV_HEAD_DIM), 0.1),
        'rwkv_ln_w': gain(ks[22], (L, R)),
        'rwkv_ln_b': nrm(ks[23], (L, R), 0.01),
        'w_out': nrm(ks[24], (L, D, D), D ** -0.5),
        'w_mlp_in': nrm(ks[25], (L, D, D_FF), D ** -0.5),
        'w_mlp_out': nrm(ks[26], (L, D_FF, D), D_FF ** -0.5),
    }


def reference(x, c, ctx, c_ctx, w_ada, b_ada, g_pre_mix, g_post_mix, g_pre_mlp, g_post_mlp,
              w_in, attn_sink, rwkv_mu_prev, rwkv_mu_next, rwkv_w0, rwkv_w2, rwkv_a0, rwkv_a2,
              rwkv_g2, rwkv_k_k, rwkv_k_a, rwkv_r_k, rwkv_ln_w, rwkv_ln_b, w_out, w_mlp_in, w_mlp_out):
    B, S, _ = x.shape
    rope = axial_rope_tables(S)
    silu_c = jax.nn.silu(c)
    silu_cc = jax.nn.silu(c_ctx)
    zero_state = jnp.zeros((B, N_RWKV_HEADS, RWKV_HEAD_DIM, RWKV_HEAD_DIM), jnp.float32)
    for l in range(DEPTH):
        last = l == DEPTH - 1
        mod = (silu_c @ w_ada[l] + b_ada[l])[:, None, :]
        sh1, sc1, gt1, sh2, sc2, gt2 = jnp.split(mod, N_MOD, axis=-1)
        cmod = silu_cc @ w_ada[l] + b_ada[l]
        csh1, csc1, cgt1, csh2, csc2, cgt2 = jnp.split(cmod, N_MOD, axis=-1)
        rwkv_p = (rwkv_mu_prev[l], rwkv_mu_next[l], rwkv_w0[l], rwkv_w2[l], rwkv_a0[l],
                  rwkv_a2[l], rwkv_g2[l], rwkv_k_k[l], rwkv_k_a[l])

        h = modulate(x, g_pre_mix[l], sh1, sc1)
        hc = modulate(ctx, g_pre_mix[l], csh1, csc1)
        q, k, v, slab = split_projection(h @ w_in[l])
        qc, kc, vc, slab_c = split_projection(hc @ w_in[l])
        q = apply_axial_rope(q, *rope)
        k = apply_axial_rope(k, *rope)
        att = window_attention_with_context(q, k, v, kc, vc, attn_sink[l])

        st_c = rwkv_streams(slab_c, *rwkv_p)
        sf_c, yf_c = rwkv_direction(zero_state, st_c, 0, not last)
        sb_c, yb_c = rwkv_direction(zero_state, st_c, 1, not last)
        st = rwkv_streams(slab, *rwkv_p)
        _, yf = rwkv_direction(sf_c, st, 0, True)
        _, yb = rwkv_direction(sb_c, st, 1, True)
        rec = rwkv_readout(st, yf, yb, rwkv_r_k[l], rwkv_ln_w[l], rwkv_ln_b[l])
        mix = jnp.concatenate([att, rec], axis=-1) @ w_out[l]

        if not last:
            att_c = context_self_attention(qc, kc, vc, attn_sink[l])
            rec_c = rwkv_readout(st_c, yf_c, yb_c, rwkv_r_k[l], rwkv_ln_w[l], rwkv_ln_b[l])
            mix_c = jnp.concatenate([att_c, rec_c], axis=-1) @ w_out[l]
            ctx = ctx + cgt1 * rms_norm(mix_c, g_post_mix[l])
            ff_c = squared_relu_mlp(modulate(ctx, g_pre_mlp[l], csh2, csc2), w_mlp_in[l], w_mlp_out[l])
            ctx = ctx + cgt2 * rms_norm(ff_c, g_post_mlp[l])

        x = x + gt1 * rms_norm(mix, g_post_mix[l])
        ff = squared_relu_mlp(modulate(x, g_pre_mlp[l], sh2, sc2), w_mlp_in[l], w_mlp_out[l])
        x = x + gt2 * rms_norm(ff, g_post_mlp[l])
    return x
```

```python
import functools

import jax
import jax.numpy as jnp
from jax import lax
from jax.experimental import pallas as pl
from jax.experimental.pallas import tpu as pltpu

F32 = jnp.float32
BF16 = jnp.bfloat16

GRID_W = 64
ATT_HEAD_DIM = 128
N_Q_HEADS = 8
N_KV_HEADS = 2
GQA_GROUP = N_Q_HEADS // N_KV_HEADS
ATT_WIDTH = N_Q_HEADS * ATT_HEAD_DIM
KV_WIDTH = N_KV_HEADS * ATT_HEAD_DIM
ATT_IN = ATT_WIDTH + 2 * KV_WIDTH
WINDOW = 128
BLOCK = 128
ROPE_BASE = 10000.0
RWKV_HEAD_DIM = 64
RWKV_WIDTH = 1024
N_RWKV_HEADS = RWKV_WIDTH // RWKV_HEAD_DIM
LORA_W = 64
LORA_A = 64
LORA_G = 128
N_MOD = 6
NORM_EPS = 1e-6
GN_EPS = 64e-5

LANES = 128
CHUNK = 64
PAIR = 2 * RWKV_HEAD_DIM
N_PAIRS = RWKV_WIDTH // PAIR
VMEM_CAP = 56 * 1024 * 1024


def _vmem(nbytes):
    return int(min(VMEM_CAP, max(16 * 1024 * 1024, nbytes)))


def _dot(a, b):
    return jnp.dot(a, b, preferred_element_type=F32)


def _dot_nt(a, b):
    return lax.dot_general(a, b, (((1,), (1,)), ((), ())), preferred_element_type=F32)


def _dot_tn(a, b):
    return lax.dot_general(a, b, (((0,), (0,)), ((), ())), preferred_element_type=F32)


def _bdot(a, b):
    return _dot(a.astype(BF16), b.astype(BF16))


def _bdot_nt(a, b):
    return _dot_nt(a.astype(BF16), b.astype(BF16))


def _bdot_tn(a, b):
    return _dot_tn(a.astype(BF16), b.astype(BF16))


def _split_dot(x, m):
    hi = x.astype(BF16)
    lo = (x - hi.astype(F32)).astype(BF16)
    return _dot(hi, m) + _dot(lo, m)


def _head_sum(x, seg, expand):
    return _split_dot(_split_dot(x, seg), expand)


def _ada_kernel(c_ref, w_ref, b_ref, o_ref):
    c = c_ref[...]
    s = c * jax.nn.sigmoid(c)
    o_ref[...] = _bdot(s, w_ref[...]) + b_ref[...]


def _ada(c8, w_ada, b_ada):
    d, n = w_ada.shape
    tn = 1024
    return pl.pallas_call(
        _ada_kernel,
        grid=(n // tn,),
        in_specs=[pl.BlockSpec((8, d), lambda j: (0, 0)),
                  pl.BlockSpec((d, tn), lambda j: (0, j)),
                  pl.BlockSpec((1, tn), lambda j: (0, j))],
        out_specs=pl.BlockSpec((8, tn), lambda j: (0, j)),
        out_shape=jax.ShapeDtypeStruct((8, n), F32),
        compiler_params=pltpu.CompilerParams(
            dimension_semantics=("arbitrary",),
            vmem_limit_bytes=_vmem(2 * d * tn * 4 + 4 * 1024 * 1024)),
        name="ada",
    )(c8, w_ada, b_ada)


def _norm_mod_kernel(x_ref, g_ref, sc_ref, sh_ref, o_ref):
    x = x_ref[...]
    ms = jnp.mean(x * x, axis=-1, keepdims=True)
    y = x * lax.rsqrt(ms + NORM_EPS) * g_ref[...]
    o_ref[...] = (y * (1.0 + sc_ref[0]) + sh_ref[0]).astype(BF16)


def _norm_mod(xx, gain, sc2, sh2, n_lat_tiles, tm):
    m, d = xx.shape
    sel = lambda i: (jnp.where(i >= n_lat_tiles, 1, 0), 0, 0)
    return pl.pallas_call(
        _norm_mod_kernel,
        grid=(m // tm,),
        in_specs=[pl.BlockSpec((tm, d), lambda i: (i, 0)),
                  pl.BlockSpec((1, d), lambda i: (0, 0)),
                  pl.BlockSpec((1, 1, d), sel),
                  pl.BlockSpec((1, 1, d), sel)],
        out_specs=pl.BlockSpec((tm, d), lambda i: (i, 0)),
        out_shape=jax.ShapeDtypeStruct((m, d), BF16),
        compiler_params=pltpu.CompilerParams(
            dimension_semantics=("arbitrary",),
            vmem_limit_bytes=_vmem(2 * tm * d * 6 + 4 * 1024 * 1024)),
        name="norm_mod",
    )(xx, gain, sc2, sh2)


def _rope_heads(acc, cos, sin, n_heads):
    lane = lax.broadcasted_iota(jnp.int32, (1, ATT_HEAD_DIM), 1)
    first = (lane % 64) < 32
    outs = []
    for h in range(n_heads):
        xh = acc[:, h * ATT_HEAD_DIM:(h + 1) * ATT_HEAD_DIM]
        partner = jnp.where(first, pltpu.roll(xh, 96, 1), pltpu.roll(xh, 32, 1))
        outs.append(xh * cos + partner * sin)
    return outs


def _proj_kernel(a_ref, w_ref, cos_ref, sin_ref, o_ref, *, rope_heads):
    acc = _dot(a_ref[...], w_ref[...])
    if rope_heads:
        roped = _rope_heads(acc, cos_ref[...], sin_ref[...], rope_heads)
        for h, r in enumerate(roped):
            o_ref[:, h * ATT_HEAD_DIM:(h + 1) * ATT_HEAD_DIM] = r
        o_ref[:, rope_heads * ATT_HEAD_DIM:] = acc[:, rope_heads * ATT_HEAD_DIM:]
    else:
        o_ref[...] = acc


def _proj(a, w, cos, sin, tm, tn, rope_heads):
    m, k = a.shape
    n = w.shape[1]
    return pl.pallas_call(
        functools.partial(_proj_kernel, rope_heads=rope_heads),
        grid=(m // tm, n // tn),
        in_specs=[pl.BlockSpec((tm, k), lambda i, j: (i, 0)),
                  pl.BlockSpec((k, tn), lambda i, j: (0, j)),
                  pl.BlockSpec((tm, ATT_HEAD_DIM), lambda i, j: (i, 0)),
                  pl.BlockSpec((tm, ATT_HEAD_DIM), lambda i, j: (i, 0))],
        out_specs=pl.BlockSpec((tm, tn), lambda i, j: (i, j)),
        out_shape=jax.ShapeDtypeStruct((m, n), F32),
        compiler_params=pltpu.CompilerParams(
            dimension_semantics=("arbitrary", "arbitrary"),
            vmem_limit_bytes=_vmem(2 * (tm * k * 2 + k * tn * 2 + tm * tn * 4 + 2 * tm * 512)
                                   + tm * tn * 8 + 4 * 1024 * 1024)),
        name="proj_qkv" if rope_heads else "proj_slab",
    )(a, w, cos, sin)


def _attn_kernel(sink_ref, q_ref, kp_ref, kc_ref, kn_ref, vp_ref, vc_ref, vn_ref,
                 kx_ref, vx_ref, o_ref, *, nb, n_ctx):
    i = pl.program_id(0)
    scale = ATT_HEAD_DIM ** -0.5
    rows = GQA_GROUP * BLOCK
    span = 3 * BLOCK
    ncol = span + n_ctx
    row = lax.broadcasted_iota(jnp.int32, (rows, ncol), 0) % BLOCK
    col = lax.broadcasted_iota(jnp.int32, (rows, ncol), 1)
    rel = col - BLOCK - row
    lo = jnp.where(i > 0, 0, BLOCK)
    hi = jnp.where(i < nb - 1, span, 2 * BLOCK)
    band = (jnp.abs(rel) <= WINDOW) & (col >= lo) & (col < hi)
    valid = band | (col >= span)
    for hk in range(N_KV_HEADS):
        ks = slice(hk * ATT_HEAD_DIM, (hk + 1) * ATT_HEAD_DIM)
        q = jnp.concatenate(
            [q_ref[:, (hk * GQA_GROUP + g) * ATT_HEAD_DIM:(hk * GQA_GROUP + g + 1) * ATT_HEAD_DIM]
             for g in range(GQA_GROUP)], axis=0)
        k_all = jnp.concatenate([kp_ref[:, ks], kc_ref[:, ks], kn_ref[:, ks], kx_ref[:, ks]], axis=0)
        v_all = jnp.concatenate([vp_ref[:, ks], vc_ref[:, ks], vn_ref[:, ks], vx_ref[:, ks]], axis=0)
        s = _bdot_nt(q, k_all) * scale
        s = jnp.where(valid, s, -jnp.inf)
        sink = jnp.concatenate(
            [jnp.full((BLOCK, 1), sink_ref[hk * GQA_GROUP + g], F32) for g in range(GQA_GROUP)], axis=0)
        m = jnp.maximum(jnp.max(s, axis=-1, keepdims=True), sink)
        e = jnp.exp(s - m)
        denom = jnp.sum(e, axis=-1, keepdims=True) + jnp.exp(sink - m)
        o = _bdot(e, v_all) / denom
        for g in range(GQA_GROUP):
            h = hk * GQA_GROUP + g
            o_ref[:, h * ATT_HEAD_DIM:(h + 1) * ATT_HEAD_DIM] = o[g * BLOCK:(g + 1) * BLOCK].astype(BF16)


def _attention(qkv, sink, s_len, n_ctx):
    nb = s_len // BLOCK
    kcol = ATT_WIDTH // KV_WIDTH
    vcol = kcol + 1
    ctx_blk = s_len // n_ctx
    prev = lambda i: jnp.maximum(i - 1, 0)
    nxt = lambda i: jnp.minimum(i + 1, nb - 1)
    kv = lambda rowf, c: pl.BlockSpec((BLOCK, KV_WIDTH), lambda i: (rowf(i), c))
    same = lambda i: i
    return pl.pallas_call(
        functools.partial(_attn_kernel, nb=nb, n_ctx=n_ctx),
        grid=(nb,),
        in_specs=[pl.BlockSpec(memory_space=pltpu.SMEM),
                  pl.BlockSpec((BLOCK, ATT_WIDTH), lambda i: (i, 0)),
                  kv(prev, kcol), kv(same, kcol), kv(nxt, kcol),
                  kv(prev, vcol), kv(same, vcol), kv(nxt, vcol),
                  pl.BlockSpec((n_ctx, KV_WIDTH), lambda i: (ctx_blk, kcol)),
                  pl.BlockSpec((n_ctx, KV_WIDTH), lambda i: (ctx_blk, vcol))],
        out_specs=pl.BlockSpec((BLOCK, ATT_WIDTH), lambda i: (i, 0)),
        out_shape=jax.ShapeDtypeStruct((s_len, ATT_WIDTH), BF16),
        compiler_params=pltpu.CompilerParams(
            dimension_semantics=("arbitrary",),
            vmem_limit_bytes=_vmem(32 * 1024 * 1024)),
        name="attention",
    )(sink, qkv, qkv, qkv, qkv, qkv, qkv, qkv, qkv, qkv)


def _prep_kernel(x_ref, xp_ref, xn_ref, mup_ref, mun_ref, w2_ref, w0_ref, a2_ref, a0_ref, g2_ref,
                 kk_ref, ka_ref, rk_ref, seg_ref, exp_ref,
                 r_o, v_o, kk_o, g_o, bon_o, lwf_o, lwb_o, kdf_o, kdb_o, bf_o, bb_o,
                 *, tm, start_tiles, end_tiles):
    i = pl.program_id(0)
    r_w = RWKV_WIDTH
    x = x_ref[...]
    row = lax.broadcasted_iota(jnp.int32, (tm, 1), 0)
    is_start = functools.reduce(jnp.logical_or, [i == t for t in start_tiles])
    is_end = functools.reduce(jnp.logical_or, [i == t for t in end_tiles])
    keep_prev = jnp.full((1, 1), jnp.where(is_start, 0, 1), jnp.int32) == 1
    keep_next = jnp.full((1, 1), jnp.where(is_end, 0, 1), jnp.int32) == 1
    prev_row = jnp.where(keep_prev, xp_ref[7:8, :], 0.0)
    next_row = jnp.where(keep_next, xn_ref[0:1, :], 0.0)
    xp = jnp.where(row == 0, prev_row, pltpu.roll(x, 1, 0))
    xn = jnp.where(row == tm - 1, next_row, pltpu.roll(x, tm - 1, 0))
    s = x + mup_ref[...] * (xp - x) + mun_ref[...] * (xn - x)

    r = s[:, :r_w]
    k = s[:, r_w:2 * r_w]
    v = s[:, 2 * r_w:3 * r_w]
    o1 = 3 * r_w
    o2 = o1 + 2 * LORA_W
    o3 = o2 + 2 * LORA_A
    w_lo = s[:, o1:o2]
    a_lo = s[:, o2:o3]
    g_lo = s[:, o3:]

    zw = w0_ref[...] + _bdot(jnp.tanh(w_lo), w2_ref[...])
    lw = -jnp.exp(-jax.nn.softplus(-zw) - 0.5)
    a = jax.nn.sigmoid(a0_ref[...] + _bdot(a_lo, a2_ref[...]))
    g = _bdot(jax.nn.sigmoid(g_lo), g2_ref[...])

    kkf = k * kk_ref[...]
    ssq = _head_sum(kkf * kkf, seg_ref[...], exp_ref[...])
    kk = kkf / jnp.maximum(jnp.sqrt(ssq), 1e-12)
    ka = ka_ref[...]
    kd_f = k * (1.0 + (a[:, :r_w] - 1.0) * ka)
    kd_b = k * (1.0 + (a[:, r_w:] - 1.0) * ka)
    k_bonus = 0.5 * (kd_f + kd_b)
    bonus = _head_sum(r * k_bonus * rk_ref[...], seg_ref[...], exp_ref[...]) * v

    r_o[...] = r
    v_o[...] = v
    kk_o[...] = kk
    g_o[...] = g
    bon_o[...] = bonus
    lwf_o[...] = lw[:, :r_w]
    lwb_o[...] = lw[:, r_w:]
    kdf_o[...] = kd_f
    kdb_o[...] = kd_b
    bf_o[...] = kk * a[:, :r_w]
    bb_o[...] = kk * a[:, r_w:]


def _rwkv_prep(slab, mup, mun, w2blk, w0cat, a2blk, a0cat, g2, k_k, k_a, r_k, seg, expand, s_len, n_ctx, tm):
    m, width = slab.shape
    nt = m // tm
    hb = tm // 8
    nblk8 = m // 8
    start_tiles = (0, s_len // tm)
    end_tiles = (s_len // tm - 1, nt - 1)
    full = lambda arr: pl.BlockSpec(arr.shape, lambda i: (0,) * arr.ndim)
    consts = (mup, mun, w2blk, w0cat, a2blk, a0cat, g2, k_k, k_a, r_k, seg, expand)
    out_spec = pl.BlockSpec((tm, RWKV_WIDTH), lambda i: (i, 0))
    n_out = 11
    return pl.pallas_call(
        functools.partial(_prep_kernel, tm=tm, start_tiles=start_tiles, end_tiles=end_tiles),
        grid=(nt,),
        in_specs=[pl.BlockSpec((tm, width), lambda i: (i, 0)),
                  pl.BlockSpec((8, width), lambda i: (jnp.maximum(i * hb - 1, 0), 0)),
                  pl.BlockSpec((8, width), lambda i: (jnp.minimum((i + 1) * hb, nblk8 - 1), 0))]
                 + [full(c) for c in consts],
        out_specs=[out_spec] * n_out,
        out_shape=[jax.ShapeDtypeStruct((m, RWKV_WIDTH), F32)] * n_out,
        compiler_params=pltpu.CompilerParams(
            dimension_semantics=("arbitrary",),
            vmem_limit_bytes=_vmem(2 * tm * width * 4 + 2 * n_out * tm * RWKV_WIDTH * 4
                                   + 12 * tm * RWKV_WIDTH * 4 + 8 * 1024 * 1024)),
        name="rwkv_prep",
    )(slab, slab, slab, *consts)


def _stack(x, m0):
    return jnp.concatenate([jnp.where(m0, x, 0.0), jnp.where(m0, 0.0, x)], axis=0)


def _scan_direction(r_ref, v_ref, kk_ref, lw_ref, kd_ref, b_ref, y_ref, s_ref, d, reverse):
    L = CHUNK
    lw = lw_ref[...]
    ti = lax.broadcasted_iota(jnp.int32, (L, L), 0)
    tj = lax.broadcasted_iota(jnp.int32, (L, L), 1)
    tri = jnp.where((ti <= tj) if reverse else (ti >= tj), 1.0, 0.0).astype(F32)
    c = jnp.dot(tri, lw, precision=lax.Precision.HIGHEST, preferred_element_type=F32)
    p_cum = jnp.exp(c)
    p_inv = jnp.exp(-c)
    p_prev = jnp.exp(c - lw)
    kk = kk_ref[...]
    a_t = -kk * p_prev
    r_t = r_ref[...] * p_cum
    b_t = b_ref[...] * p_inv
    k_t = kd_ref[...] * p_inv
    last = 0 if reverse else L - 1
    p_end = p_cum[last:last + 1, :]
    b_h = b_t * p_end
    k_h = k_t * p_end
    v = v_ref[...]

    lane = lax.broadcasted_iota(jnp.int32, (L, PAIR), 1)
    m0 = lane < RWKV_HEAD_DIM
    gi = lax.broadcasted_iota(jnp.int32, (2 * L, 2 * L), 0)
    gj = lax.broadcasted_iota(jnp.int32, (2 * L, 2 * L), 1)
    same = (gi // L) == (gj // L)
    if reverse:
        strict = same & (gi < gj)
        incl = same & (gi <= gj)
    else:
        strict = same & (gi > gj)
        incl = same & (gi >= gj)

    for p in range(N_PAIRS):
        sl = slice(p * PAIR, (p + 1) * PAIR)
        a_s = _stack(a_t[:, sl], m0)
        r_s = _stack(r_t[:, sl], m0)
        b_s = _stack(b_t[:, sl], m0)
        k_s = _stack(k_t[:, sl], m0)
        v_s = _stack(v[:, sl], m0)
        bh_s = _stack(b_h[:, sl], m0)
        kh_s = _stack(k_h[:, sl], m0)
        ar = jnp.concatenate([a_s, r_s], axis=0).astype(BF16)
        bk = jnp.concatenate([b_s, k_s], axis=0).astype(BF16)
        g = _dot_nt(ar, bk)
        a_ab = jnp.where(strict, g[:2 * L, :2 * L], 0.0)
        a_ak = jnp.where(strict, g[:2 * L, 2 * L:], 0.0)
        a_rb = jnp.where(incl, g[2 * L:, :2 * L], 0.0)
        a_rk = jnp.where(incl, g[2 * L:, 2 * L:], 0.0)
        state = s_ref[d, p]
        xs = _dot_nt(ar, state.astype(BF16))
        av = _bdot(jnp.concatenate([a_ak, a_rk], axis=0), v_s)
        w = xs[:2 * L] + av[:2 * L]
        am = a_ab
        w = w + _bdot(am, w)
        for _ in range(5):
            am = _bdot(am, am)
            w = w + _bdot(am, w)
        sa = w
        ys = xs[2 * L:] + av[2 * L:] + _bdot(a_rb, sa)
        y_ref[:, sl] = ys[:L] + ys[L:]
        upd = _bdot_tn(jnp.concatenate([sa, v_s], axis=0), jnp.concatenate([bh_s, kh_s], axis=0))
        s_ref[d, p] = state * p_end[:, sl] + upd


def _scan_kernel(rf, vf, kkf, lwf, kdf, bf, rb, vb, kkb, lwb, kdb, bb, yf, yb, s_ref):
    @pl.when(pl.program_id(0) == 0)
    def _():
        s_ref[...] = jnp.zeros_like(s_ref)

    _scan_direction(rf, vf, kkf, lwf, kdf, bf, yf, s_ref, 0, False)
    _scan_direction(rb, vb, kkb, lwb, kdb, bb, yb, s_ref, 1, True)


def _rwkv_scan(r, v, kk, lwf, lwb, kdf, kdb, bf, bb, s_len, n_ctx):
    m = r.shape[0]
    nc = m // CHUNK
    nc_lat = s_len // CHUNK
    nc_ctx = n_ctx // CHUNK
    fwd = lambda i: (jnp.where(i < nc_ctx, nc_lat + i, i - nc_ctx), 0)
    bwd = lambda i: (nc - 1 - i, 0)
    fs = pl.BlockSpec((CHUNK, RWKV_WIDTH), fwd)
    bs = pl.BlockSpec((CHUNK, RWKV_WIDTH), bwd)
    return pl.pallas_call(
        _scan_kernel,
        grid=(nc,),
        in_specs=[fs] * 6 + [bs] * 6,
        out_specs=[fs, bs],
        out_shape=[jax.ShapeDtypeStruct((m, RWKV_WIDTH), F32)] * 2,
        scratch_shapes=[pltpu.VMEM((2, N_PAIRS, PAIR, PAIR), F32)],
        compiler_params=pltpu.CompilerParams(
            dimension_semantics=("arbitrary",),
            vmem_limit_bytes=_vmem(40 * 1024 * 1024)),
        name="rwkv_scan",
    )(r, v, kk, lwf, kdf, bf, r, v, kk, lwb, kdb, bb)


def _out_kernel(att_ref, yf_ref, yb_ref, g_ref, bon_ref, x_ref, wa_ref, wr_ref, lnw_ref, lnb_ref,
                seg_ref, exp_ref, gpm_ref, gt1_ref, gmlp_ref, sc2_ref, sh2_ref, x1_ref, h2_ref):
    y = yf_ref[...] + yb_ref[...]
    inv_n = 1.0 / RWKV_HEAD_DIM
    mu = _head_sum(y, seg_ref[...], exp_ref[...]) * inv_n
    yc = y - mu
    var = _head_sum(yc * yc, seg_ref[...], exp_ref[...]) * inv_n
    yn = yc * lax.rsqrt(var + GN_EPS) * lnw_ref[...] + lnb_ref[...]
    rec = (yn + bon_ref[...]) * g_ref[...]
    mix = _dot(att_ref[...], wa_ref[...]) + _bdot(rec, wr_ref[...])
    ms = jnp.mean(mix * mix, axis=-1, keepdims=True)
    x1 = x_ref[...] + gt1_ref[...] * (mix * lax.rsqrt(ms + NORM_EPS) * gpm_ref[...])
    x1_ref[...] = x1
    ms2 = jnp.mean(x1 * x1, axis=-1, keepdims=True)
    h2 = x1 * lax.rsqrt(ms2 + NORM_EPS) * gmlp_ref[...]
    h2_ref[...] = (h2 * (1.0 + sc2_ref[...]) + sh2_ref[...]).astype(BF16)


def _out_proj(att, yf, yb, g, bonus, x, wa, wr, lnw, lnb, seg, expand, gpm, gt1, gmlp, sc2, sh2, tm):
    s_len, d = x.shape
    row = lambda w: pl.BlockSpec((tm, w), lambda i: (i, 0))
    full = lambda arr: pl.BlockSpec(arr.shape, lambda i: (0,) * arr.ndim)
    consts = (wa, wr, lnw, lnb, seg, expand, gpm, gt1, gmlp, sc2, sh2)
    return pl.pallas_call(
        _out_kernel,
        grid=(s_len // tm,),
        in_specs=[row(ATT_WIDTH)] + [row(RWKV_WIDTH)] * 4 + [row(d)] + [full(c) for c in consts],
        out_specs=[row(d), row(d)],
        out_shape=[jax.ShapeDtypeStruct((s_len, d), F32), jax.ShapeDtypeStruct((s_len, d), BF16)],
        compiler_params=pltpu.CompilerParams(
            dimension_semantics=("arbitrary",),
            vmem_limit_bytes=_vmem(2 * tm * (ATT_WIDTH * 2 + 4 * RWKV_WIDTH * 4 + d * 4 + d * 4 + d * 2)
                                   + 2 * 2 * d * d + 8 * tm * d * 4 + 4 * 1024 * 1024)),
        name="out_proj",
    )(att, yf, yb, g, bonus, x, *consts)


def _mlp_kernel(h_ref, w1_ref, w2_ref, x1_ref, g_ref, gt_ref, o_ref, acc_ref):
    f = pl.program_id(1)
    u = jnp.maximum(_dot(h_ref[...], w1_ref[...]), 0.0)
    part = _dot((u * u).astype(BF16), w2_ref[...])

    @pl.when(f == 0)
    def _():
        acc_ref[...] = part

    @pl.when(f > 0)
    def _():
        acc_ref[...] += part

    @pl.when(f == pl.num_programs(1) - 1)
    def _():
        ff = acc_ref[...]
        ms = jnp.mean(ff * ff, axis=-1, keepdims=True)
        o_ref[...] = x1_ref[...] + gt_ref[...] * (ff * lax.rsqrt(ms + NORM_EPS) * g_ref[...])


def _mlp(h2, w1, w2, x1, gain, gt2, tm, tf):
    s_len, d = x1.shape
    dff = w1.shape[1]
    return pl.pallas_call(
        _mlp_kernel,
        grid=(s_len // tm, dff // tf),
        in_specs=[pl.BlockSpec((tm, d), lambda i, f: (i, 0)),
                  pl.BlockSpec((d, tf), lambda i, f: (0, f)),
                  pl.BlockSpec((tf, d), lambda i, f: (f, 0)),
                  pl.BlockSpec((tm, d), lambda i, f: (i, 0)),
                  pl.BlockSpec((1, d), lambda i, f: (0, 0)),
                  pl.BlockSpec((1, d), lambda i, f: (0, 0))],
        out_specs=pl.BlockSpec((tm, d), lambda i, f: (i, 0)),
        out_shape=jax.ShapeDtypeStruct((s_len, d), F32),
        scratch_shapes=[pltpu.VMEM((tm, d), F32)],
        compiler_params=pltpu.CompilerParams(
            dimension_semantics=("arbitrary", "arbitrary"),
            vmem_limit_bytes=_vmem(2 * (tm * d * 2 + 2 * d * tf * 2 + 2 * tm * d * 4)
                                   + tm * d * 4 + 3 * tm * tf * 4 + 4 * 1024 * 1024)),
        name="mlp",
    )(h2, w1, w2, x1, gain, gt2)


def _rope_tables(s_len, n_ctx):
    rows = s_len // GRID_W
    row_pos = jnp.repeat(jnp.arange(rows, dtype=F32), GRID_W)
    col_pos = jnp.tile(jnp.arange(GRID_W, dtype=F32), rows)
    n_freq = ATT_HEAD_DIM // 4
    inv_freq = ROPE_BASE ** (-jnp.arange(n_freq, dtype=F32) / n_freq)
    ang_r = row_pos[:, None] * inv_freq[None, :]
    ang_c = col_pos[:, None] * inv_freq[None, :]
    cos = jnp.concatenate([jnp.cos(ang_r)] * 2 + [jnp.cos(ang_c)] * 2, axis=-1)
    sin = jnp.concatenate([-jnp.sin(ang_r), jnp.sin(ang_r), -jnp.sin(ang_c), jnp.sin(ang_c)], axis=-1)
    cos = jnp.concatenate([cos, jnp.ones((n_ctx, ATT_HEAD_DIM), F32)], axis=0)
    sin = jnp.concatenate([sin, jnp.zeros((n_ctx, ATT_HEAD_DIM), F32)], axis=0)
    return cos, sin


def _block_diag2(w):
    _, k, n = w.shape
    z = jnp.zeros((k, n), w.dtype)
    return jnp.concatenate([jnp.concatenate([w[0], z], axis=1), jnp.concatenate([z, w[1]], axis=1)], axis=0)


def _layer(x2, ctx2, c8, l, w_ada, b_ada, g_pre_mix, g_post_mix, g_pre_mlp, g_post_mlp, w_in, attn_sink,
           rwkv_mu_prev, rwkv_mu_next, rwkv_w0, rwkv_w2, rwkv_a0, rwkv_a2, rwkv_g2, rwkv_k_k, rwkv_k_a,
           rwkv_r_k, rwkv_ln_w, rwkv_ln_b, w_out, w_mlp_in, w_mlp_out):
    s_len, d = x2.shape
    n_ctx = ctx2.shape[0]
    row = lambda v: v.reshape(1, -1)

    mod = _ada(c8, w_ada[l], row(b_ada[l]))
    sh1, sc1, gt1, sh2, sc2, gt2 = [mod[:2, j * d:(j + 1) * d] for j in range(N_MOD)]

    tm_rows = 256
    xx = jnp.concatenate([x2, ctx2], axis=0)
    h = _norm_mod(xx, row(g_pre_mix[l]), sc1.reshape(2, 1, d), sh1.reshape(2, 1, d), s_len // tm_rows, tm_rows)

    cos, sin = _rope_tables(s_len, n_ctx)
    w_in_b = w_in[l].astype(BF16)
    m_all = s_len + n_ctx
    tm_proj = 768 if m_all % 768 == 0 else 256
    qkv = _proj(h, w_in_b[:, :ATT_IN], cos, sin, tm_proj, ATT_IN, N_Q_HEADS + N_KV_HEADS)
    rwkv_in = w_in_b.shape[1] - ATT_IN
    slab = _proj(h, w_in_b[:, ATT_IN:], cos, sin, tm_proj, rwkv_in // 3, 0)

    att = _attention(qkv, attn_sink[l], s_len, n_ctx)

    head = jnp.arange(RWKV_WIDTH) // RWKV_HEAD_DIM
    seg = (head[:, None] == jnp.arange(LANES)[None, :]).astype(BF16)
    expand = seg.T
    streams = _rwkv_prep(
        slab, row(rwkv_mu_prev[l]), row(rwkv_mu_next[l]),
        _block_diag2(rwkv_w2[l]).astype(BF16), rwkv_w0[l].reshape(1, -1),
        _block_diag2(rwkv_a2[l]).astype(BF16), rwkv_a0[l].reshape(1, -1),
        rwkv_g2[l].astype(BF16), row(rwkv_k_k[l]), row(rwkv_k_a[l]), rwkv_r_k[l].reshape(1, -1),
        seg, expand, s_len, n_ctx, tm_rows)
    r, v, kk, g, bonus, lwf, lwb, kdf, kdb, bf, bb = streams
    yf, yb = _rwkv_scan(r, v, kk, lwf, lwb, kdf, kdb, bf, bb, s_len, n_ctx)

    w_out_b = w_out[l].astype(BF16)
    x1, h2 = _out_proj(att, yf, yb, g, bonus, x2, w_out_b[:ATT_WIDTH], w_out_b[ATT_WIDTH:],
                       row(rwkv_ln_w[l]), row(rwkv_ln_b[l]), seg, expand,
                       row(g_post_mix[l]), gt1[0:1], row(g_pre_mlp[l]), sc2[0:1], sh2[0:1], 512)
    return _mlp(h2, w_mlp_in[l].astype(BF16), w_mlp_out[l].astype(BF16), x1, row(g_post_mlp[l]), gt2[0:1],
                512, 1024)


def kernel(x, c, ctx, c_ctx, w_ada, b_ada, g_pre_mix, g_post_mix, g_pre_mlp, g_post_mlp, w_in, attn_sink,
           rwkv_mu_prev, rwkv_mu_next, rwkv_w0, rwkv_w2, rwkv_a0, rwkv_a2, rwkv_g2, rwkv_k_k, rwkv_k_a,
           rwkv_r_k, rwkv_ln_w, rwkv_ln_b, w_out, w_mlp_in, w_mlp_out):
    b, s_len, d = x.shape
    depth = w_ada.shape[0]
    assert b == 1 and depth == 1, "single sample, single layer (context stream is not carried to a next layer)"
    c8 = jnp.zeros((8, d), F32).at[0].set(c[0]).at[1].set(c_ctx)
    out = _layer(x[0], ctx[0], c8, 0, w_ada, b_ada, g_pre_mix, g_post_mix, g_pre_mlp, g_post_mlp, w_in,
                 attn_sink, rwkv_mu_prev, rwkv_mu_next, rwkv_w0, rwkv_w2, rwkv_a0, rwkv_a2, rwkv_g2,
                 rwkv_k_k, rwkv_k_a, rwkv_r_k, rwkv_ln_w, rwkv_ln_b, w_out, w_mlp_in, w_mlp_out)
    return out[None]
```

```python
import functools

import jax
import jax.numpy as jnp
from jax import lax
from jax.experimental import pallas as pl
from jax.experimental.pallas import tpu as pltpu

F32 = jnp.float32
BF16 = jnp.bfloat16

GRID_W = 64
ATT_HEAD_DIM = 128
N_Q_HEADS = 8
N_KV_HEADS = 2
GQA_GROUP = N_Q_HEADS // N_KV_HEADS
ATT_WIDTH = N_Q_HEADS * ATT_HEAD_DIM
KV_WIDTH = N_KV_HEADS * ATT_HEAD_DIM
ATT_IN = ATT_WIDTH + 2 * KV_WIDTH
WINDOW = 128
BLOCK = 128
ROPE_BASE = 10000.0
RWKV_HEAD_DIM = 64
RWKV_WIDTH = 1024
N_RWKV_HEADS = RWKV_WIDTH // RWKV_HEAD_DIM
LORA_W = 64
LORA_A = 64
LORA_G = 128
N_MOD = 6
NORM_EPS = 1e-6
GN_EPS = 64e-5
DECAY_SCALE = 0.6065306597126334

LANES = 128
CHUNK = 64
PAIR = 2 * RWKV_HEAD_DIM
N_PAIRS = RWKV_WIDTH // PAIR
VMEM_CAP = 56 * 1024 * 1024


def _vmem(nbytes):
    return int(min(VMEM_CAP, max(16 * 1024 * 1024, nbytes)))


def _dot(a, b):
    return jnp.dot(a, b, preferred_element_type=F32)


def _dot_nt(a, b):
    return lax.dot_general(a, b, (((1,), (1,)), ((), ())), preferred_element_type=F32)


def _dot_tn(a, b):
    return lax.dot_general(a, b, (((0,), (0,)), ((), ())), preferred_element_type=F32)


def _bdot(a, b):
    return _dot(a.astype(BF16), b.astype(BF16))


def _bdot_nt(a, b):
    return _dot_nt(a.astype(BF16), b.astype(BF16))


def _bdot_tn(a, b):
    return _dot_tn(a.astype(BF16), b.astype(BF16))


def _split_dot(x, m):
    hi = x.astype(BF16)
    lo = (x - hi.astype(F32)).astype(BF16)
    return _dot(hi, m) + _dot(lo, m)


def _head_sum(x, seg, expand):
    return _split_dot(_split_dot(x, seg), expand)


def _ada_kernel(c_ref, w_ref, b_ref, o_ref):
    c = c_ref[...]
    s = c * jax.nn.sigmoid(c)
    o_ref[...] = _bdot(s, w_ref[...]) + b_ref[...]


def _ada(c8, w_ada, b_ada):
    d, n = w_ada.shape
    tn = 1024
    return pl.pallas_call(
        _ada_kernel,
        grid=(n // tn,),
        in_specs=[pl.BlockSpec((8, d), lambda j: (0, 0)),
                  pl.BlockSpec((d, tn), lambda j: (0, j)),
                  pl.BlockSpec((1, tn), lambda j: (0, j))],
        out_specs=pl.BlockSpec((8, tn), lambda j: (0, j)),
        out_shape=jax.ShapeDtypeStruct((8, n), F32),
        compiler_params=pltpu.CompilerParams(
            dimension_semantics=("arbitrary",),
            vmem_limit_bytes=_vmem(2 * d * tn * 4 + 4 * 1024 * 1024)),
        name="ada",
    )(c8, w_ada, b_ada)


def _norm_mod_kernel(x_ref, c_ref, g_ref, sc_ref, sh_ref, o_ref, *, n_lat_tiles):
    def emit(src_ref):
        x = src_ref[...]
        ms = jnp.mean(x * x, axis=-1, keepdims=True)
        y = x * lax.rsqrt(ms + NORM_EPS) * g_ref[...]
        o_ref[...] = (y * (1.0 + sc_ref[0]) + sh_ref[0]).astype(BF16)

    i = pl.program_id(0)
    pl.when(i < n_lat_tiles)(lambda: emit(x_ref))
    pl.when(i >= n_lat_tiles)(lambda: emit(c_ref))


def _norm_mod(x2, ctx2, gain, sc2, sh2, tm):
    s_len, d = x2.shape
    n_lat_tiles = s_len // tm
    n_tiles = n_lat_tiles + ctx2.shape[0] // tm
    sel = lambda i: (jnp.where(i >= n_lat_tiles, 1, 0), 0, 0)
    return pl.pallas_call(
        functools.partial(_norm_mod_kernel, n_lat_tiles=n_lat_tiles),
        grid=(n_tiles,),
        in_specs=[pl.BlockSpec((tm, d), lambda i: (jnp.minimum(i, n_lat_tiles - 1), 0)),
                  pl.BlockSpec((tm, d), lambda i: (jnp.maximum(i - n_lat_tiles, 0), 0)),
                  pl.BlockSpec((1, d), lambda i: (0, 0)),
                  pl.BlockSpec((1, 1, d), sel),
                  pl.BlockSpec((1, 1, d), sel)],
        out_specs=pl.BlockSpec((tm, d), lambda i: (i, 0)),
        out_shape=jax.ShapeDtypeStruct((n_tiles * tm, d), BF16),
        compiler_params=pltpu.CompilerParams(
            dimension_semantics=("arbitrary",),
            vmem_limit_bytes=_vmem(2 * tm * d * 10 + 4 * 1024 * 1024)),
        name="norm_mod",
    )(x2, ctx2, gain, sc2, sh2)


def _rope_heads(acc, cos, sin, n_heads):
    lane = lax.broadcasted_iota(jnp.int32, (1, ATT_HEAD_DIM), 1)
    first = (lane % 64) < 32
    outs = []
    for h in range(n_heads):
        xh = acc[:, h * ATT_HEAD_DIM:(h + 1) * ATT_HEAD_DIM]
        partner = jnp.where(first, pltpu.roll(xh, 96, 1), pltpu.roll(xh, 32, 1))
        outs.append(xh * cos + partner * sin)
    return outs


def _proj_kernel(a_ref, w_ref, cos_ref, sin_ref, o_ref, *, rope_heads):
    acc = _dot(a_ref[...], w_ref[...])
    if rope_heads:
        roped = _rope_heads(acc, cos_ref[...], sin_ref[...], rope_heads)
        for h, r in enumerate(roped):
            o_ref[:, h * ATT_HEAD_DIM:(h + 1) * ATT_HEAD_DIM] = r
        o_ref[:, rope_heads * ATT_HEAD_DIM:] = acc[:, rope_heads * ATT_HEAD_DIM:]
    else:
        o_ref[...] = acc


def _proj(a, w, cos, sin, tm, tn, rope_heads):
    m, k = a.shape
    n = w.shape[1]
    return pl.pallas_call(
        functools.partial(_proj_kernel, rope_heads=rope_heads),
        grid=(m // tm, n // tn),
        in_specs=[pl.BlockSpec((tm, k), lambda i, j: (i, 0)),
                  pl.BlockSpec((k, tn), lambda i, j: (0, j)),
                  pl.BlockSpec((tm, ATT_HEAD_DIM), lambda i, j: (i, 0)),
                  pl.BlockSpec((tm, ATT_HEAD_DIM), lambda i, j: (i, 0))],
        out_specs=pl.BlockSpec((tm, tn), lambda i, j: (i, j)),
        out_shape=jax.ShapeDtypeStruct((m, n), F32),
        compiler_params=pltpu.CompilerParams(
            dimension_semantics=("arbitrary", "arbitrary"),
            vmem_limit_bytes=_vmem(2 * (tm * k * 2 + k * tn * 2 + tm * tn * 4 + 2 * tm * 512)
                                   + tm * tn * 8 + 4 * 1024 * 1024)),
        name="proj_qkv" if rope_heads else "proj_slab",
    )(a, w, cos, sin)


def _attn_kernel(sink_ref, q_ref, kp_ref, kc_ref, kn_ref, vp_ref, vc_ref, vn_ref,
                 kx_ref, vx_ref, o_ref, *, nb, n_ctx):
    i = pl.program_id(0)
    scale = ATT_HEAD_DIM ** -0.5
    rows = GQA_GROUP * BLOCK
    span = 3 * BLOCK
    ncol = span + n_ctx
    row = lax.broadcasted_iota(jnp.int32, (rows, ncol), 0) % BLOCK
    col = lax.broadcasted_iota(jnp.int32, (rows, ncol), 1)
    rel = col - BLOCK - row
    lo = jnp.where(i > 0, 0, BLOCK)
    hi = jnp.where(i < nb - 1, span, 2 * BLOCK)
    band = (jnp.abs(rel) <= WINDOW) & (col >= lo) & (col < hi)
    valid = band | (col >= span)
    for hk in range(N_KV_HEADS):
        ks = slice(hk * ATT_HEAD_DIM, (hk + 1) * ATT_HEAD_DIM)
        q = jnp.concatenate(
            [q_ref[:, (hk * GQA_GROUP + g) * ATT_HEAD_DIM:(hk * GQA_GROUP + g + 1) * ATT_HEAD_DIM]
             for g in range(GQA_GROUP)], axis=0)
        k_all = jnp.concatenate([kp_ref[:, ks], kc_ref[:, ks], kn_ref[:, ks], kx_ref[:, ks]], axis=0)
        v_all = jnp.concatenate([vp_ref[:, ks], vc_ref[:, ks], vn_ref[:, ks], vx_ref[:, ks]], axis=0)
        s = _bdot_nt(q, k_all) * scale
        s = jnp.where(valid, s, -jnp.inf)
        sink = jnp.concatenate(
            [jnp.full((BLOCK, 1), sink_ref[hk * GQA_GROUP + g], F32) for g in range(GQA_GROUP)], axis=0)
        m = jnp.maximum(jnp.max(s, axis=-1, keepdims=True), sink)
        e = jnp.exp(s - m)
        denom = jnp.sum(e, axis=-1, keepdims=True) + jnp.exp(sink - m)
        o = _bdot(e, v_all) / denom
        for g in range(GQA_GROUP):
            h = hk * GQA_GROUP + g
            o_ref[:, h * ATT_HEAD_DIM:(h + 1) * ATT_HEAD_DIM] = o[g * BLOCK:(g + 1) * BLOCK].astype(BF16)


def _attention(qkv, sink, s_len, n_ctx):
    nb = s_len // BLOCK
    kcol = ATT_WIDTH // KV_WIDTH
    vcol = kcol + 1
    ctx_blk = s_len // n_ctx
    prev = lambda i: jnp.maximum(i - 1, 0)
    nxt = lambda i: jnp.minimum(i + 1, nb - 1)
    kv = lambda rowf, c: pl.BlockSpec((BLOCK, KV_WIDTH), lambda i: (rowf(i), c))
    same = lambda i: i
    return pl.pallas_call(
        functools.partial(_attn_kernel, nb=nb, n_ctx=n_ctx),
        grid=(nb,),
        in_specs=[pl.BlockSpec(memory_space=pltpu.SMEM),
                  pl.BlockSpec((BLOCK, ATT_WIDTH), lambda i: (i, 0)),
                  kv(prev, kcol), kv(same, kcol), kv(nxt, kcol),
                  kv(prev, vcol), kv(same, vcol), kv(nxt, vcol),
                  pl.BlockSpec((n_ctx, KV_WIDTH), lambda i: (ctx_blk, kcol)),
                  pl.BlockSpec((n_ctx, KV_WIDTH), lambda i: (ctx_blk, vcol))],
        out_specs=pl.BlockSpec((BLOCK, ATT_WIDTH), lambda i: (i, 0)),
        out_shape=jax.ShapeDtypeStruct((s_len, ATT_WIDTH), BF16),
        compiler_params=pltpu.CompilerParams(
            dimension_semantics=("arbitrary",),
            vmem_limit_bytes=_vmem(32 * 1024 * 1024)),
        name="attention",
    )(sink, qkv, qkv, qkv, qkv, qkv, qkv, qkv, qkv, qkv)


def _prep_kernel(x_ref, xp_ref, xn_ref, mup_ref, mun_ref, w2_ref, w0_ref, a2_ref, a0_ref, g2_ref,
                 kk_ref, ka_ref, rk_ref, seg_ref, exp_ref,
                 r_o, v_o, kk_o, g_o, bon_o, lwf_o, lwb_o, kdf_o, kdb_o, bf_o, bb_o,
                 *, tm, start_tiles, end_tiles):
    i = pl.program_id(0)
    r_w = RWKV_WIDTH
    x = x_ref[...]
    is_start = functools.reduce(jnp.logical_or, [i == t for t in start_tiles])
    is_end = functools.reduce(jnp.logical_or, [i == t for t in end_tiles])
    keep_prev = jnp.full((1, 1), jnp.where(is_start, 0, 1), jnp.int32) == 1
    keep_next = jnp.full((1, 1), jnp.where(is_end, 0, 1), jnp.int32) == 1
    prev_row = jnp.where(keep_prev, xp_ref[7:8, :], 0.0)
    next_row = jnp.where(keep_next, xn_ref[0:1, :], 0.0)
    xp = pltpu.roll(x, 1, 0)
    xn = pltpu.roll(x, tm - 1, 0)
    row8 = lax.broadcasted_iota(jnp.int32, (8, 1), 0)
    xp = jnp.concatenate([jnp.where(row8 == 0, prev_row, xp[:8]), xp[8:]], axis=0)
    xn = jnp.concatenate([xn[:tm - 8], jnp.where(row8 == 7, next_row, xn[tm - 8:])], axis=0)
    mup = mup_ref[...]
    mun = mun_ref[...]
    s = x * (1.0 - mup - mun) + xp * mup + xn * mun

    r = s[:, :r_w]
    k = s[:, r_w:2 * r_w]
    v = s[:, 2 * r_w:3 * r_w]
    o1 = 3 * r_w
    o2 = o1 + 2 * LORA_W
    o3 = o2 + 2 * LORA_A
    w_lo = s[:, o1:o2]
    a_lo = s[:, o2:o3]
    g_lo = s[:, o3:]

    zw = w0_ref[...] + _bdot(jnp.tanh(w_lo), w2_ref[...])
    lw = -DECAY_SCALE * jax.nn.sigmoid(zw)
    a = jax.nn.sigmoid(a0_ref[...] + _bdot(a_lo, a2_ref[...]))
    g = _bdot(jax.nn.sigmoid(g_lo), g2_ref[...])

    kkf = k * kk_ref[...]
    ssq = _head_sum(kkf * kkf, seg_ref[...], exp_ref[...])
    kk = kkf * jnp.minimum(lax.rsqrt(ssq), 1e12)
    ka = ka_ref[...]
    kd_f = k * (1.0 + (a[:, :r_w] - 1.0) * ka)
    kd_b = k * (1.0 + (a[:, r_w:] - 1.0) * ka)
    k_bonus = 0.5 * (kd_f + kd_b)
    bonus = _head_sum(r * k_bonus * rk_ref[...], seg_ref[...], exp_ref[...]) * v

    r_o[...] = r
    v_o[...] = v
    kk_o[...] = kk
    g_o[...] = g
    bon_o[...] = bonus
    lwf_o[...] = lw[:, :r_w]
    lwb_o[...] = lw[:, r_w:]
    kdf_o[...] = kd_f
    kdb_o[...] = kd_b
    bf_o[...] = kk * a[:, :r_w]
    bb_o[...] = kk * a[:, r_w:]


def _rwkv_prep(slab, mup, mun, w2blk, w0cat, a2blk, a0cat, g2, k_k, k_a, r_k, seg, expand, s_len, n_ctx, tm):
    m, width = slab.shape
    nt = m // tm
    hb = tm // 8
    nblk8 = m // 8
    start_tiles = (0, s_len // tm)
    end_tiles = (s_len // tm - 1, nt - 1)
    full = lambda arr: pl.BlockSpec(arr.shape, lambda i: (0,) * arr.ndim)
    consts = (mup, mun, w2blk, w0cat, a2blk, a0cat, g2, k_k, k_a, r_k, seg, expand)
    out_spec = pl.BlockSpec((tm, RWKV_WIDTH), lambda i: (i, 0))
    n_out = 11
    return pl.pallas_call(
        functools.partial(_prep_kernel, tm=tm, start_tiles=start_tiles, end_tiles=end_tiles),
        grid=(nt,),
        in_specs=[pl.BlockSpec((tm, width), lambda i: (i, 0)),
                  pl.BlockSpec((8, width), lambda i: (jnp.maximum(i * hb - 1, 0), 0)),
                  pl.BlockSpec((8, width), lambda i: (jnp.minimum((i + 1) * hb, nblk8 - 1), 0))]
                 + [full(c) for c in consts],
        out_specs=[out_spec] * n_out,
        out_shape=[jax.ShapeDtypeStruct((m, RWKV_WIDTH), F32)] * n_out,
        compiler_params=pltpu.CompilerParams(
            dimension_semantics=("arbitrary",),
            vmem_limit_bytes=_vmem(2 * tm * width * 4 + 2 * n_out * tm * RWKV_WIDTH * 4
                                   + 12 * tm * RWKV_WIDTH * 4 + 8 * 1024 * 1024)),
        name="rwkv_prep",
    )(slab, slab, slab, *consts)


def _stack(x, m0):
    return jnp.concatenate([jnp.where(m0, x, 0.0), jnp.where(m0, 0.0, x)], axis=0)


def _scan_streams(r_ref, v_ref, kk_ref, lw_ref, kd_ref, b_ref, reverse):
    L = CHUNK
    lw = lw_ref[...]
    ti = lax.broadcasted_iota(jnp.int32, (L, L), 0)
    tj = lax.broadcasted_iota(jnp.int32, (L, L), 1)
    tri = jnp.where((ti <= tj) if reverse else (ti >= tj), 1.0, 0.0).astype(F32)
    c = jnp.dot(tri, lw, precision=lax.Precision.HIGHEST, preferred_element_type=F32)
    p_cum = jnp.exp(c)
    p_inv = jnp.exp(-c)
    p_prev = jnp.exp(c - lw)
    last = 0 if reverse else L - 1
    p_end = p_cum[last:last + 1, :]
    b_t = b_ref[...] * p_inv
    k_t = kd_ref[...] * p_inv
    gi = lax.broadcasted_iota(jnp.int32, (2 * L, 2 * L), 0)
    gj = lax.broadcasted_iota(jnp.int32, (2 * L, 2 * L), 1)
    same = (gi // L) == (gj // L)
    strict = same & ((gi < gj) if reverse else (gi > gj))
    incl = same & ((gi <= gj) if reverse else (gi >= gj))
    return dict(a=-kk_ref[...] * p_prev, r=r_ref[...] * p_cum, b=b_t, k=k_t, bh=b_t * p_end, kh=k_t * p_end,
                v=v_ref[...], p_end=p_end, strict=strict, incl=incl)


def _scan_kernel(rf, vf, kkf, lwf, kdf, bf, rb, vb, kkb, lwb, kdb, bb, yf, yb, s_ref):
    @pl.when(pl.program_id(0) == 0)
    def _():
        s_ref[...] = jnp.zeros_like(s_ref)

    L = CHUNK
    streams = (_scan_streams(rf, vf, kkf, lwf, kdf, bf, False), _scan_streams(rb, vb, kkb, lwb, kdb, bb, True))
    y_refs = (yf, yb)
    m0 = lax.broadcasted_iota(jnp.int32, (L, PAIR), 1) < RWKV_HEAD_DIM
    chains = [(d, p) for d in range(2) for p in range(N_PAIRS)]

    def stacked(name, d, p):
        return _stack(streams[d][name][:, p * PAIR:(p + 1) * PAIR], m0)

    ar = [jnp.concatenate([stacked("a", d, p), stacked("r", d, p)], axis=0).astype(BF16) for d, p in chains]
    bk = [jnp.concatenate([stacked("b", d, p), stacked("k", d, p)], axis=0).astype(BF16) for d, p in chains]
    v_s = [stacked("v", d, p).astype(BF16) for d, p in chains]
    state = [s_ref[d, p] for d, p in chains]
    g = [_dot_nt(x, z) for x, z in zip(ar, bk)]
    xs = [_dot_nt(x, s.astype(BF16)) for x, s in zip(ar, state)]
    a_ab, a_kv, a_rb = [], [], []
    for (d, _), gc in zip(chains, g):
        strict, incl = streams[d]["strict"], streams[d]["incl"]
        a_ab.append(jnp.where(strict, gc[:2 * L, :2 * L], 0.0).astype(BF16))
        a_kv.append(jnp.concatenate([jnp.where(strict, gc[:2 * L, 2 * L:], 0.0),
                                     jnp.where(incl, gc[2 * L:, 2 * L:], 0.0)], axis=0).astype(BF16))
        a_rb.append(jnp.where(incl, gc[2 * L:, :2 * L], 0.0).astype(BF16))
    av = [_dot(a, v) for a, v in zip(a_kv, v_s)]
    w = [x[:2 * L] + y[:2 * L] for x, y in zip(xs, av)]
    am = a_ab
    w = [wc + _dot(a, wc.astype(BF16)) for a, wc in zip(am, w)]
    for _ in range(5):
        am = [_dot(a, a).astype(BF16) for a in am]
        w = [wc + _dot(a, wc.astype(BF16)) for a, wc in zip(am, w)]
    sa = [wc.astype(BF16) for wc in w]
    ys = [x[2 * L:] + y[2 * L:] + _dot(a, s) for x, y, a, s in zip(xs, av, a_rb, sa)]
    for (d, p), yc in zip(chains, ys):
        y_refs[d][:, p * PAIR:(p + 1) * PAIR] = yc[:L] + yc[L:]
    for (d, p), s_old, sac, vc in zip(chains, state, sa, v_s):
        bhkh = jnp.concatenate([stacked("bh", d, p), stacked("kh", d, p)], axis=0).astype(BF16)
        upd = _dot_tn(jnp.concatenate([sac, vc], axis=0), bhkh)
        s_ref[d, p] = s_old * streams[d]["p_end"][:, p * PAIR:(p + 1) * PAIR] + upd


def _rwkv_scan(r, v, kk, lwf, lwb, kdf, kdb, bf, bb, s_len, n_ctx):
    m = r.shape[0]
    nc = m // CHUNK
    nc_lat = s_len // CHUNK
    nc_ctx = n_ctx // CHUNK
    fwd = lambda i: (jnp.where(i < nc_ctx, nc_lat + i, i - nc_ctx), 0)
    bwd = lambda i: (nc - 1 - i, 0)
    fs = pl.BlockSpec((CHUNK, RWKV_WIDTH), fwd)
    bs = pl.BlockSpec((CHUNK, RWKV_WIDTH), bwd)
    return pl.pallas_call(
        _scan_kernel,
        grid=(nc,),
        in_specs=[fs] * 6 + [bs] * 6,
        out_specs=[fs, bs],
        out_shape=[jax.ShapeDtypeStruct((m, RWKV_WIDTH), F32)] * 2,
        scratch_shapes=[pltpu.VMEM((2, N_PAIRS, PAIR, PAIR), F32)],
        compiler_params=pltpu.CompilerParams(
            dimension_semantics=("arbitrary",),
            vmem_limit_bytes=_vmem(40 * 1024 * 1024)),
        name="rwkv_scan",
    )(r, v, kk, lwf, kdf, bf, r, v, kk, lwb, kdb, bb)


def _out_kernel(att_ref, yf_ref, yb_ref, g_ref, bon_ref, x_ref, wa_ref, wr_ref, lnw_ref, lnb_ref,
                seg_ref, exp_ref, gpm_ref, gt1_ref, gmlp_ref, sc2_ref, sh2_ref, x1_ref, h2_ref):
    y = yf_ref[...] + yb_ref[...]
    inv_n = 1.0 / RWKV_HEAD_DIM
    mu = _head_sum(y, seg_ref[...], exp_ref[...]) * inv_n
    yc = y - mu
    var = _head_sum(yc * yc, seg_ref[...], exp_ref[...]) * inv_n
    yn = yc * lax.rsqrt(var + GN_EPS) * lnw_ref[...] + lnb_ref[...]
    rec = (yn + bon_ref[...]) * g_ref[...]
    mix = _dot(att_ref[...], wa_ref[...]) + _bdot(rec, wr_ref[...])
    ms = jnp.mean(mix * mix, axis=-1, keepdims=True)
    x1 = x_ref[...] + gt1_ref[...] * (mix * lax.rsqrt(ms + NORM_EPS) * gpm_ref[...])
    x1_ref[...] = x1
    ms2 = jnp.mean(x1 * x1, axis=-1, keepdims=True)
    h2 = x1 * lax.rsqrt(ms2 + NORM_EPS) * gmlp_ref[...]
    h2_ref[...] = (h2 * (1.0 + sc2_ref[...]) + sh2_ref[...]).astype(BF16)


def _out_proj(att, yf, yb, g, bonus, x, wa, wr, lnw, lnb, seg, expand, gpm, gt1, gmlp, sc2, sh2, tm):
    s_len, d = x.shape
    row = lambda w: pl.BlockSpec((tm, w), lambda i: (i, 0))
    full = lambda arr: pl.BlockSpec(arr.shape, lambda i: (0,) * arr.ndim)
    consts = (wa, wr, lnw, lnb, seg, expand, gpm, gt1, gmlp, sc2, sh2)
    return pl.pallas_call(
        _out_kernel,
        grid=(s_len // tm,),
        in_specs=[row(ATT_WIDTH)] + [row(RWKV_WIDTH)] * 4 + [row(d)] + [full(c) for c in consts],
        out_specs=[row(d), row(d)],
        out_shape=[jax.ShapeDtypeStruct((s_len, d), F32), jax.ShapeDtypeStruct((s_len, d), BF16)],
        compiler_params=pltpu.CompilerParams(
            dimension_semantics=("arbitrary",),
            vmem_limit_bytes=_vmem(2 * tm * (ATT_WIDTH * 2 + 4 * RWKV_WIDTH * 4 + d * 4 + d * 4 + d * 2)
                                   + 2 * 2 * d * d + 8 * tm * d * 4 + 4 * 1024 * 1024)),
        name="out_proj",
    )(att, yf, yb, g, bonus, x, *consts)


def _mlp_kernel(h_ref, w1_ref, w2_ref, x1_ref, g_ref, gt_ref, o_ref, acc_ref):
    f = pl.program_id(1)
    u = jnp.maximum(_dot(h_ref[...], w1_ref[...]), 0.0)
    part = _dot((u * u).astype(BF16), w2_ref[...])

    @pl.when(f == 0)
    def _():
        acc_ref[...] = part

    @pl.when(f > 0)
    def _():
        acc_ref[...] += part

    @pl.when(f == pl.num_programs(1) - 1)
    def _():
        ff = acc_ref[...]
        ms = jnp.mean(ff * ff, axis=-1, keepdims=True)
        o_ref[...] = x1_ref[...] + gt_ref[...] * (ff * lax.rsqrt(ms + NORM_EPS) * g_ref[...])


def _mlp(h2, w1, w2, x1, gain, gt2, tm, tf):
    s_len, d = x1.shape
    dff = w1.shape[1]
    return pl.pallas_call(
        _mlp_kernel,
        grid=(s_len // tm, dff // tf),
        in_specs=[pl.BlockSpec((tm, d), lambda i, f: (i, 0)),
                  pl.BlockSpec((d, tf), lambda i, f: (0, f)),
                  pl.BlockSpec((tf, d), lambda i, f: (f, 0)),
                  pl.BlockSpec((tm, d), lambda i, f: (i, 0)),
                  pl.BlockSpec((1, d), lambda i, f: (0, 0)),
                  pl.BlockSpec((1, d), lambda i, f: (0, 0))],
        out_specs=pl.BlockSpec((tm, d), lambda i, f: (i, 0)),
        out_shape=jax.ShapeDtypeStruct((s_len, d), F32),
        scratch_shapes=[pltpu.VMEM((tm, d), F32)],
        compiler_params=pltpu.CompilerParams(
            dimension_semantics=("arbitrary", "arbitrary"),
            vmem_limit_bytes=_vmem(2 * (tm * d * 2 + 2 * d * tf * 2 + 2 * tm * d * 4)
                                   + tm * d * 4 + 3 * tm * tf * 4 + 4 * 1024 * 1024)),
        name="mlp",
    )(h2, w1, w2, x1, gain, gt2)


def _rope_tables(s_len, n_ctx):
    rows = s_len // GRID_W
    row_pos = jnp.repeat(jnp.arange(rows, dtype=F32), GRID_W)
    col_pos = jnp.tile(jnp.arange(GRID_W, dtype=F32), rows)
    n_freq = ATT_HEAD_DIM // 4
    inv_freq = ROPE_BASE ** (-jnp.arange(n_freq, dtype=F32) / n_freq)
    ang_r = row_pos[:, None] * inv_freq[None, :]
    ang_c = col_pos[:, None] * inv_freq[None, :]
    cos = jnp.concatenate([jnp.cos(ang_r)] * 2 + [jnp.cos(ang_c)] * 2, axis=-1)
    sin = jnp.concatenate([-jnp.sin(ang_r), jnp.sin(ang_r), -jnp.sin(ang_c), jnp.sin(ang_c)], axis=-1)
    cos = jnp.concatenate([cos, jnp.ones((n_ctx, ATT_HEAD_DIM), F32)], axis=0)
    sin = jnp.concatenate([sin, jnp.zeros((n_ctx, ATT_HEAD_DIM), F32)], axis=0)
    return cos, sin


def _block_diag2(w):
    _, k, n = w.shape
    z = jnp.zeros((k, n), w.dtype)
    return jnp.concatenate([jnp.concatenate([w[0], z], axis=1), jnp.concatenate([z, w[1]], axis=1)], axis=0)


def _layer(x2, ctx2, c8, l, w_ada, b_ada, g_pre_mix, g_post_mix, g_pre_mlp, g_post_mlp, w_in, attn_sink,
           rwkv_mu_prev, rwkv_mu_next, rwkv_w0, rwkv_w2, rwkv_a0, rwkv_a2, rwkv_g2, rwkv_k_k, rwkv_k_a,
           rwkv_r_k, rwkv_ln_w, rwkv_ln_b, w_out, w_mlp_in, w_mlp_out):
    s_len, d = x2.shape
    n_ctx = ctx2.shape[0]
    row = lambda v: v.reshape(1, -1)

    mod = _ada(c8, w_ada[l], row(b_ada[l]))
    sh1, sc1, gt1, sh2, sc2, gt2 = [mod[:2, j * d:(j + 1) * d] for j in range(N_MOD)]

    tm_rows = 256
    h = _norm_mod(x2, ctx2, row(g_pre_mix[l]), sc1.reshape(2, 1, d), sh1.reshape(2, 1, d), tm_rows)

    cos, sin = _rope_tables(s_len, n_ctx)
    m_all = s_len + n_ctx
    tm_proj = 768 if m_all % 768 == 0 else 256
    qkv = _proj(h, w_in[l][:, :ATT_IN].astype(BF16), cos, sin, tm_proj, ATT_IN, N_Q_HEADS + N_KV_HEADS)
    rwkv_in = w_in.shape[2] - ATT_IN
    slab = _proj(h, w_in[l][:, ATT_IN:].astype(BF16), cos, sin, tm_proj, rwkv_in // 3, 0)

    att = _attention(qkv, attn_sink[l], s_len, n_ctx)

    head = jnp.arange(RWKV_WIDTH) // RWKV_HEAD_DIM
    seg = (head[:, None] == jnp.arange(LANES)[None, :]).astype(BF16)
    expand = seg.T
    streams = _rwkv_prep(
        slab, row(rwkv_mu_prev[l]), row(rwkv_mu_next[l]),
        _block_diag2(rwkv_w2[l]).astype(BF16), rwkv_w0[l].reshape(1, -1),
        _block_diag2(rwkv_a2[l]).astype(BF16), rwkv_a0[l].reshape(1, -1),
        rwkv_g2[l].astype(BF16), row(rwkv_k_k[l]), row(rwkv_k_a[l]), rwkv_r_k[l].reshape(1, -1),
        seg, expand, s_len, n_ctx, tm_rows)
    r, v, kk, g, bonus, lwf, lwb, kdf, kdb, bf, bb = streams
    yf, yb = _rwkv_scan(r, v, kk, lwf, lwb, kdf, kdb, bf, bb, s_len, n_ctx)

    x1, h2 = _out_proj(att, yf, yb, g, bonus, x2,
                       w_out[l][:ATT_WIDTH].astype(BF16), w_out[l][ATT_WIDTH:].astype(BF16),
                       row(rwkv_ln_w[l]), row(rwkv_ln_b[l]), seg, expand,
                       row(g_post_mix[l]), gt1[0:1], row(g_pre_mlp[l]), sc2[0:1], sh2[0:1], 512)
    return _mlp(h2, w_mlp_in[l].astype(BF16), w_mlp_out[l].astype(BF16), x1, row(g_post_mlp[l]), gt2[0:1],
                512, 1024)


def kernel(x, c, ctx, c_ctx, w_ada, b_ada, g_pre_mix, g_post_mix, g_pre_mlp, g_post_mlp, w_in, attn_sink,
           rwkv_mu_prev, rwkv_mu_next, rwkv_w0, rwkv_w2, rwkv_a0, rwkv_a2, rwkv_g2, rwkv_k_k, rwkv_k_a,
           rwkv_r_k, rwkv_ln_w, rwkv_ln_b, w_out, w_mlp_in, w_mlp_out):
    b, s_len, d = x.shape
    depth = w_ada.shape[0]
    assert b == 1 and depth == 1, "single sample, single layer (context stream is not carried to a next layer)"
    c8 = jnp.zeros((8, d), F32).at[0].set(c[0]).at[1].set(c_ctx)
    out = _layer(x[0], ctx[0], c8, 0, w_ada, b_ada, g_pre_mix, g_post_mix, g_pre_mlp, g_post_mlp, w_in,
                 attn_sink, rwkv_mu_prev, rwkv_mu_next, rwkv_w0, rwkv_w2, rwkv_a0, rwkv_a2, rwkv_g2,
                 rwkv_k_k, rwkv_k_a, rwkv_r_k, rwkv_ln_w, rwkv_ln_b, w_out, w_mlp_in, w_mlp_out)
    return out[None]
```

```python
import functools

import jax
import jax.numpy as jnp
from jax import lax
from jax.experimental import pallas as pl
from jax.experimental.pallas import tpu as pltpu

F32 = jnp.float32
BF16 = jnp.bfloat16

GRID_W = 64
ATT_HEAD_DIM = 128
N_Q_HEADS = 8
N_KV_HEADS = 2
GQA_GROUP = N_Q_HEADS // N_KV_HEADS
ATT_WIDTH = N_Q_HEADS * ATT_HEAD_DIM
KV_WIDTH = N_KV_HEADS * ATT_HEAD_DIM
ATT_IN = ATT_WIDTH + 2 * KV_WIDTH
WINDOW = 128
BLOCK = 128
ROPE_BASE = 10000.0
RWKV_HEAD_DIM = 64
RWKV_WIDTH = 1024
N_RWKV_HEADS = RWKV_WIDTH // RWKV_HEAD_DIM
LORA_W = 64
LORA_A = 64
LORA_G = 128
N_MOD = 6
NORM_EPS = 1e-6
GN_EPS = 64e-5
DECAY_SCALE = 0.6065306597126334

LANES = 128
CHUNK = 64
PAIR = 2 * RWKV_HEAD_DIM
N_PAIRS = RWKV_WIDTH // PAIR
OUT_SUB_BLOCKS = 4
VMEM_CAP = 56 * 1024 * 1024


def _vmem(nbytes):
    return int(min(VMEM_CAP, max(16 * 1024 * 1024, nbytes)))


def _dot(a, b):
    return jnp.dot(a, b, preferred_element_type=F32)


def _dot_nt(a, b):
    return lax.dot_general(a, b, (((1,), (1,)), ((), ())), preferred_element_type=F32)


def _dot_tn(a, b):
    return lax.dot_general(a, b, (((0,), (0,)), ((), ())), preferred_element_type=F32)


def _bdot(a, b):
    return _dot(a.astype(BF16), b.astype(BF16))


def _bdot_nt(a, b):
    return _dot_nt(a.astype(BF16), b.astype(BF16))


def _bdot_tn(a, b):
    return _dot_tn(a.astype(BF16), b.astype(BF16))


def _split_dot(x, m):
    hi = x.astype(BF16)
    lo = (x - hi.astype(F32)).astype(BF16)
    return _dot(hi, m) + _dot(lo, m)


def _head_sum(x, seg, expand):
    return _split_dot(_split_dot(x, seg), expand)


def _ada_kernel(c_ref, w_ref, b_ref, o_ref):
    c = c_ref[...]
    s = c * jax.nn.sigmoid(c)
    o_ref[...] = _bdot(s, w_ref[...]) + b_ref[...]


def _ada(c8, w_ada, b_ada):
    d, n = w_ada.shape
    tn = 1024
    return pl.pallas_call(
        _ada_kernel,
        grid=(n // tn,),
        in_specs=[pl.BlockSpec((8, d), lambda j: (0, 0)),
                  pl.BlockSpec((d, tn), lambda j: (0, j)),
                  pl.BlockSpec((1, tn), lambda j: (0, j))],
        out_specs=pl.BlockSpec((8, tn), lambda j: (0, j)),
        out_shape=jax.ShapeDtypeStruct((8, n), F32),
        compiler_params=pltpu.CompilerParams(
            dimension_semantics=("arbitrary",),
            vmem_limit_bytes=_vmem(2 * d * tn * 4 + 4 * 1024 * 1024)),
        name="ada",
    )(c8, w_ada, b_ada)


def _norm_mod_kernel(x_ref, c_ref, g_ref, sc_ref, sh_ref, o_ref, *, n_lat_tiles):
    def emit(src_ref):
        x = src_ref[...]
        ms = jnp.mean(x * x, axis=-1, keepdims=True)
        y = x * lax.rsqrt(ms + NORM_EPS) * g_ref[...]
        o_ref[...] = (y * (1.0 + sc_ref[0]) + sh_ref[0]).astype(BF16)

    i = pl.program_id(0)
    pl.when(i < n_lat_tiles)(lambda: emit(x_ref))
    pl.when(i >= n_lat_tiles)(lambda: emit(c_ref))


def _norm_mod(x2, ctx2, gain, sc2, sh2, tm):
    s_len, d = x2.shape
    n_lat_tiles = s_len // tm
    n_tiles = n_lat_tiles + ctx2.shape[0] // tm
    sel = lambda i: (jnp.where(i >= n_lat_tiles, 1, 0), 0, 0)
    return pl.pallas_call(
        functools.partial(_norm_mod_kernel, n_lat_tiles=n_lat_tiles),
        grid=(n_tiles,),
        in_specs=[pl.BlockSpec((tm, d), lambda i: (jnp.minimum(i, n_lat_tiles - 1), 0)),
                  pl.BlockSpec((tm, d), lambda i: (jnp.maximum(i - n_lat_tiles, 0), 0)),
                  pl.BlockSpec((1, d), lambda i: (0, 0)),
                  pl.BlockSpec((1, 1, d), sel),
                  pl.BlockSpec((1, 1, d), sel)],
        out_specs=pl.BlockSpec((tm, d), lambda i: (i, 0)),
        out_shape=jax.ShapeDtypeStruct((n_tiles * tm, d), BF16),
        compiler_params=pltpu.CompilerParams(
            dimension_semantics=("arbitrary",),
            vmem_limit_bytes=_vmem(2 * tm * d * 10 + 4 * 1024 * 1024)),
        name="norm_mod",
    )(x2, ctx2, gain, sc2, sh2)


def _rope_heads(acc, cos, sin, n_heads):
    lane = lax.broadcasted_iota(jnp.int32, (1, ATT_HEAD_DIM), 1)
    first = (lane % 64) < 32
    outs = []
    for h in range(n_heads):
        xh = acc[:, h * ATT_HEAD_DIM:(h + 1) * ATT_HEAD_DIM]
        partner = jnp.where(first, pltpu.roll(xh, 96, 1), pltpu.roll(xh, 32, 1))
        outs.append(xh * cos + partner * sin)
    return outs


def _proj_kernel(a_ref, w_ref, cos_ref, sin_ref, o_ref, *, rope_heads):
    acc = _dot(a_ref[...], w_ref[...])
    if rope_heads:
        roped = _rope_heads(acc, cos_ref[...], sin_ref[...], rope_heads)
        for h, r in enumerate(roped):
            o_ref[:, h * ATT_HEAD_DIM:(h + 1) * ATT_HEAD_DIM] = r
        o_ref[:, rope_heads * ATT_HEAD_DIM:] = acc[:, rope_heads * ATT_HEAD_DIM:]
    else:
        o_ref[...] = acc


def _proj(a, w, cos, sin, tm, tn, rope_heads):
    m, k = a.shape
    n = w.shape[1]
    return pl.pallas_call(
        functools.partial(_proj_kernel, rope_heads=rope_heads),
        grid=(m // tm, n // tn),
        in_specs=[pl.BlockSpec((tm, k), lambda i, j: (i, 0)),
                  pl.BlockSpec((k, tn), lambda i, j: (0, j)),
                  pl.BlockSpec((tm, ATT_HEAD_DIM), lambda i, j: (i, 0)),
                  pl.BlockSpec((tm, ATT_HEAD_DIM), lambda i, j: (i, 0))],
        out_specs=pl.BlockSpec((tm, tn), lambda i, j: (i, j)),
        out_shape=jax.ShapeDtypeStruct((m, n), F32),
        compiler_params=pltpu.CompilerParams(
            dimension_semantics=("arbitrary", "arbitrary"),
            vmem_limit_bytes=_vmem(2 * (tm * k * 2 + k * tn * 2 + tm * tn * 4 + 2 * tm * 512)
                                   + tm * tn * 8 + 4 * 1024 * 1024)),
        name="proj_qkv" if rope_heads else "proj_slab",
    )(a, w, cos, sin)


def _attn_kernel(sink_ref, q_ref, kp_ref, kc_ref, kn_ref, vp_ref, vc_ref, vn_ref,
                 kx_ref, vx_ref, o_ref, *, nb, n_ctx):
    i = pl.program_id(0)
    scale = ATT_HEAD_DIM ** -0.5
    rows = GQA_GROUP * BLOCK
    span = 3 * BLOCK
    ncol = span + n_ctx
    row = lax.broadcasted_iota(jnp.int32, (rows, ncol), 0) % BLOCK
    col = lax.broadcasted_iota(jnp.int32, (rows, ncol), 1)
    rel = col - BLOCK - row
    lo = jnp.where(i > 0, 0, BLOCK)
    hi = jnp.where(i < nb - 1, span, 2 * BLOCK)
    band = (jnp.abs(rel) <= WINDOW) & (col >= lo) & (col < hi)
    valid = band | (col >= span)
    for hk in range(N_KV_HEADS):
        ks = slice(hk * ATT_HEAD_DIM, (hk + 1) * ATT_HEAD_DIM)
        q = jnp.concatenate(
            [q_ref[:, (hk * GQA_GROUP + g) * ATT_HEAD_DIM:(hk * GQA_GROUP + g + 1) * ATT_HEAD_DIM]
             for g in range(GQA_GROUP)], axis=0)
        k_all = jnp.concatenate([kp_ref[:, ks], kc_ref[:, ks], kn_ref[:, ks], kx_ref[:, ks]], axis=0)
        v_all = jnp.concatenate([vp_ref[:, ks], vc_ref[:, ks], vn_ref[:, ks], vx_ref[:, ks]], axis=0)
        s = _bdot_nt(q, k_all) * scale
        s = jnp.where(valid, s, -jnp.inf)
        sink = jnp.concatenate(
            [jnp.full((BLOCK, 1), sink_ref[hk * GQA_GROUP + g], F32) for g in range(GQA_GROUP)], axis=0)
        m = jnp.maximum(jnp.max(s, axis=-1, keepdims=True), sink)
        e = jnp.exp(s - m)
        denom = jnp.sum(e, axis=-1, keepdims=True) + jnp.exp(sink - m)
        o = _bdot(e, v_all) / denom
        for g in range(GQA_GROUP):
            h = hk * GQA_GROUP + g
            o_ref[:, h * ATT_HEAD_DIM:(h + 1) * ATT_HEAD_DIM] = o[g * BLOCK:(g + 1) * BLOCK].astype(BF16)


def _attention(qkv, sink, s_len, n_ctx):
    nb = s_len // BLOCK
    kcol = ATT_WIDTH // KV_WIDTH
    vcol = kcol + 1
    ctx_blk = s_len // n_ctx
    prev = lambda i: jnp.maximum(i - 1, 0)
    nxt = lambda i: jnp.minimum(i + 1, nb - 1)
    kv = lambda rowf, c: pl.BlockSpec((BLOCK, KV_WIDTH), lambda i: (rowf(i), c))
    same = lambda i: i
    return pl.pallas_call(
        functools.partial(_attn_kernel, nb=nb, n_ctx=n_ctx),
        grid=(nb,),
        in_specs=[pl.BlockSpec(memory_space=pltpu.SMEM),
                  pl.BlockSpec((BLOCK, ATT_WIDTH), lambda i: (i, 0)),
                  kv(prev, kcol), kv(same, kcol), kv(nxt, kcol),
                  kv(prev, vcol), kv(same, vcol), kv(nxt, vcol),
                  pl.BlockSpec((n_ctx, KV_WIDTH), lambda i: (ctx_blk, kcol)),
                  pl.BlockSpec((n_ctx, KV_WIDTH), lambda i: (ctx_blk, vcol))],
        out_specs=pl.BlockSpec((BLOCK, ATT_WIDTH), lambda i: (i, 0)),
        out_shape=jax.ShapeDtypeStruct((s_len, ATT_WIDTH), BF16),
        compiler_params=pltpu.CompilerParams(
            dimension_semantics=("arbitrary",),
            vmem_limit_bytes=_vmem(32 * 1024 * 1024)),
        name="attention",
    )(sink, qkv, qkv, qkv, qkv, qkv, qkv, qkv, qkv, qkv)


def _prep_kernel(x_ref, xp_ref, xn_ref, mup_ref, mun_ref, w2_ref, w0_ref, a2_ref, a0_ref, g2_ref,
                 kk_ref, ka_ref, rk_ref, seg_ref, exp_ref,
                 r_o, v_o, kk_o, g_o, bon_o, lwf_o, lwb_o, kdf_o, kdb_o, bf_o, bb_o,
                 *, tm, start_tiles, end_tiles):
    i = pl.program_id(0)
    r_w = RWKV_WIDTH
    x = x_ref[...]
    is_start = functools.reduce(jnp.logical_or, [i == t for t in start_tiles])
    is_end = functools.reduce(jnp.logical_or, [i == t for t in end_tiles])
    keep_prev = jnp.full((1, 1), jnp.where(is_start, 0, 1), jnp.int32) == 1
    keep_next = jnp.full((1, 1), jnp.where(is_end, 0, 1), jnp.int32) == 1
    prev_row = jnp.where(keep_prev, xp_ref[7:8, :], 0.0)
    next_row = jnp.where(keep_next, xn_ref[0:1, :], 0.0)
    xp = pltpu.roll(x, 1, 0)
    xn = pltpu.roll(x, tm - 1, 0)
    row8 = lax.broadcasted_iota(jnp.int32, (8, 1), 0)
    xp = jnp.concatenate([jnp.where(row8 == 0, prev_row, xp[:8]), xp[8:]], axis=0)
    xn = jnp.concatenate([xn[:tm - 8], jnp.where(row8 == 7, next_row, xn[tm - 8:])], axis=0)
    mup = mup_ref[...]
    mun = mun_ref[...]
    s = x * (1.0 - mup - mun) + xp * mup + xn * mun

    r = s[:, :r_w]
    k = s[:, r_w:2 * r_w]
    v = s[:, 2 * r_w:3 * r_w]
    o1 = 3 * r_w
    o2 = o1 + 2 * LORA_W
    o3 = o2 + 2 * LORA_A
    w_lo = s[:, o1:o2]
    a_lo = s[:, o2:o3]
    g_lo = s[:, o3:]

    zw = w0_ref[...] + _bdot(jnp.tanh(w_lo), w2_ref[...])
    lw = -DECAY_SCALE * jax.nn.sigmoid(zw)
    a = jax.nn.sigmoid(a0_ref[...] + _bdot(a_lo, a2_ref[...]))
    g = _bdot(jax.nn.sigmoid(g_lo), g2_ref[...])

    kkf = k * kk_ref[...]
    ssq = _head_sum(kkf * kkf, seg_ref[...], exp_ref[...])
    kk = kkf * jnp.minimum(lax.rsqrt(ssq), 1e12)
    ka = ka_ref[...]
    kd_f = k * (1.0 + (a[:, :r_w] - 1.0) * ka)
    kd_b = k * (1.0 + (a[:, r_w:] - 1.0) * ka)
    k_bonus = 0.5 * (kd_f + kd_b)
    bonus = _head_sum(r * k_bonus * rk_ref[...], seg_ref[...], exp_ref[...]) * v

    r_o[...] = r
    v_o[...] = v
    kk_o[...] = kk
    g_o[...] = g
    bon_o[...] = bonus
    lwf_o[...] = lw[:, :r_w]
    lwb_o[...] = lw[:, r_w:]
    kdf_o[...] = kd_f
    kdb_o[...] = kd_b
    bf_o[...] = kk * a[:, :r_w]
    bb_o[...] = kk * a[:, r_w:]


def _rwkv_prep(slab, mup, mun, w2blk, w0cat, a2blk, a0cat, g2, k_k, k_a, r_k, seg, expand, s_len, n_ctx, tm):
    m, width = slab.shape
    nt = m // tm
    hb = tm // 8
    nblk8 = m // 8
    start_tiles = (0, s_len // tm)
    end_tiles = (s_len // tm - 1, nt - 1)
    full = lambda arr: pl.BlockSpec(arr.shape, lambda i: (0,) * arr.ndim)
    consts = (mup, mun, w2blk, w0cat, a2blk, a0cat, g2, k_k, k_a, r_k, seg, expand)
    out_spec = pl.BlockSpec((tm, RWKV_WIDTH), lambda i: (i, 0))
    n_out = 11
    return pl.pallas_call(
        functools.partial(_prep_kernel, tm=tm, start_tiles=start_tiles, end_tiles=end_tiles),
        grid=(nt,),
        in_specs=[pl.BlockSpec((tm, width), lambda i: (i, 0)),
                  pl.BlockSpec((8, width), lambda i: (jnp.maximum(i * hb - 1, 0), 0)),
                  pl.BlockSpec((8, width), lambda i: (jnp.minimum((i + 1) * hb, nblk8 - 1), 0))]
                 + [full(c) for c in consts],
        out_specs=[out_spec] * n_out,
        out_shape=[jax.ShapeDtypeStruct((m, RWKV_WIDTH), F32)] * n_out,
        compiler_params=pltpu.CompilerParams(
            dimension_semantics=("arbitrary",),
            vmem_limit_bytes=_vmem(2 * tm * width * 4 + 2 * n_out * tm * RWKV_WIDTH * 4
                                   + 12 * tm * RWKV_WIDTH * 4 + 8 * 1024 * 1024)),
        name="rwkv_prep",
    )(slab, slab, slab, *consts)


def _stack(x, m0):
    return jnp.concatenate([jnp.where(m0, x, 0.0), jnp.where(m0, 0.0, x)], axis=0)


def _scan_streams(r_ref, v_ref, kk_ref, lw_ref, kd_ref, b_ref, reverse):
    L = CHUNK
    lw = lw_ref[...]
    ti = lax.broadcasted_iota(jnp.int32, (L, L), 0)
    tj = lax.broadcasted_iota(jnp.int32, (L, L), 1)
    tri = jnp.where((ti <= tj) if reverse else (ti >= tj), 1.0, 0.0).astype(BF16)
    lw_hi = lw.astype(BF16)
    lw_lo = (lw - lw_hi.astype(F32)).astype(BF16)
    c = _dot(tri, lw_hi) + _dot(tri, lw_lo)
    p_cum = jnp.exp(c)
    p_inv = jnp.exp(-c)
    p_prev = jnp.exp(c - lw)
    last = 0 if reverse else L - 1
    p_end = p_cum[last:last + 1, :]
    b_t = b_ref[...] * p_inv
    k_t = kd_ref[...] * p_inv
    gi = lax.broadcasted_iota(jnp.int32, (2 * L, 2 * L), 0)
    gj = lax.broadcasted_iota(jnp.int32, (2 * L, 2 * L), 1)
    same = (gi // L) == (gj // L)
    strict = same & ((gi < gj) if reverse else (gi > gj))
    incl = same & ((gi <= gj) if reverse else (gi >= gj))
    return dict(a=-kk_ref[...] * p_prev, r=r_ref[...] * p_cum, b=b_t, k=k_t, bh=b_t * p_end, kh=k_t * p_end,
                v=v_ref[...], p_end=p_end, strict=strict, incl=incl)


def _scan_kernel(rf, vf, kkf, lwf, kdf, bf, rb, vb, kkb, lwb, kdb, bb, yf, yb, s_ref):
    @pl.when(pl.program_id(0) == 0)
    def _():
        s_ref[...] = jnp.zeros_like(s_ref)

    L = CHUNK
    streams = (_scan_streams(rf, vf, kkf, lwf, kdf, bf, False), _scan_streams(rb, vb, kkb, lwb, kdb, bb, True))
    y_refs = (yf, yb)
    m0 = lax.broadcasted_iota(jnp.int32, (L, PAIR), 1) < RWKV_HEAD_DIM
    chains = [(d, p) for d in range(2) for p in range(N_PAIRS)]

    def stacked(name, d, p):
        return _stack(streams[d][name][:, p * PAIR:(p + 1) * PAIR], m0)

    ar = [jnp.concatenate([stacked("a", d, p), stacked("r", d, p)], axis=0).astype(BF16) for d, p in chains]
    bk = [jnp.concatenate([stacked("b", d, p), stacked("k", d, p)], axis=0).astype(BF16) for d, p in chains]
    v_s = [stacked("v", d, p).astype(BF16) for d, p in chains]
    state = [s_ref[d, p] for d, p in chains]
    g = [_dot_nt(x, z) for x, z in zip(ar, bk)]
    xs = [_dot_nt(x, s.astype(BF16)) for x, s in zip(ar, state)]
    a_ab, a_kv, a_rb = [], [], []
    for (d, _), gc in zip(chains, g):
        strict, incl = streams[d]["strict"], streams[d]["incl"]
        a_ab.append(jnp.where(strict, gc[:2 * L, :2 * L], 0.0).astype(BF16))
        a_kv.append(jnp.concatenate([jnp.where(strict, gc[:2 * L, 2 * L:], 0.0),
                                     jnp.where(incl, gc[2 * L:, 2 * L:], 0.0)], axis=0).astype(BF16))
        a_rb.append(jnp.where(incl, gc[2 * L:, :2 * L], 0.0).astype(BF16))
    av = [_dot(a, v) for a, v in zip(a_kv, v_s)]
    w = [x[:2 * L] + y[:2 * L] for x, y in zip(xs, av)]
    am = a_ab
    w = [wc + _dot(a, wc.astype(BF16)) for a, wc in zip(am, w)]
    for _ in range(5):
        am = [_dot(a, a).astype(BF16) for a in am]
        w = [wc + _dot(a, wc.astype(BF16)) for a, wc in zip(am, w)]
    sa = [wc.astype(BF16) for wc in w]
    ys = [x[2 * L:] + y[2 * L:] + _dot(a, s) for x, y, a, s in zip(xs, av, a_rb, sa)]
    for (d, p), yc in zip(chains, ys):
        y_refs[d][:, p * PAIR:(p + 1) * PAIR] = yc[:L] + yc[L:]
    for (d, p), s_old, sac, vc in zip(chains, state, sa, v_s):
        bhkh = jnp.concatenate([stacked("bh", d, p), stacked("kh", d, p)], axis=0).astype(BF16)
        upd = _dot_tn(jnp.concatenate([sac, vc], axis=0), bhkh)
        s_ref[d, p] = s_old * streams[d]["p_end"][:, p * PAIR:(p + 1) * PAIR] + upd


def _rwkv_scan(r, v, kk, lwf, lwb, kdf, kdb, bf, bb, s_len, n_ctx):
    m = r.shape[0]
    nc = m // CHUNK
    nc_lat = s_len // CHUNK
    nc_ctx = n_ctx // CHUNK
    fwd = lambda i: (jnp.where(i < nc_ctx, nc_lat + i, i - nc_ctx), 0)
    bwd = lambda i: (nc - 1 - i, 0)
    fs = pl.BlockSpec((CHUNK, RWKV_WIDTH), fwd)
    bs = pl.BlockSpec((CHUNK, RWKV_WIDTH), bwd)
    return pl.pallas_call(
        _scan_kernel,
        grid=(nc,),
        in_specs=[fs] * 6 + [bs] * 6,
        out_specs=[fs, bs],
        out_shape=[jax.ShapeDtypeStruct((m, RWKV_WIDTH), F32)] * 2,
        scratch_shapes=[pltpu.VMEM((2, N_PAIRS, PAIR, PAIR), F32)],
        compiler_params=pltpu.CompilerParams(
            dimension_semantics=("arbitrary",),
            vmem_limit_bytes=_vmem(2 * 14 * CHUNK * RWKV_WIDTH * 4 + 2 * N_PAIRS * PAIR * PAIR * 4
                                   + 24 * 1024 * 1024)),
        name="rwkv_scan",
    )(r, v, kk, lwf, kdf, bf, r, v, kk, lwb, kdb, bb)


def _out_kernel(att_ref, yf_ref, yb_ref, g_ref, bon_ref, x_ref, wa_ref, wr_ref, lnw_ref, lnb_ref,
                seg_ref, exp_ref, gpm_ref, gt1_ref, gmlp_ref, sc2_ref, sh2_ref, x1_ref, h2_ref):
    mix_att = _dot(att_ref[...], wa_ref[...])
    inv_n = 1.0 / RWKV_HEAD_DIM
    seg, expand = seg_ref[...], exp_ref[...]
    rows = att_ref.shape[0] // OUT_SUB_BLOCKS
    blocks = [pl.ds(n * rows, rows) for n in range(OUT_SUB_BLOCKS)]
    y = [yf_ref[b, :] + yb_ref[b, :] for b in blocks]
    s1 = [_bdot(v, seg) for v in y]
    yc = [v - _split_dot(s, expand) * inv_n for v, s in zip(y, s1)]
    s2 = [_bdot(c * c, seg) for c in yc]
    var = [_bdot(s, expand) * inv_n for s in s2]
    rec = [((c * lax.rsqrt(vr + GN_EPS) * lnw_ref[...] + lnb_ref[...]) + bon_ref[b, :]) * g_ref[b, :]
           for c, vr, b in zip(yc, var, blocks)]
    mix = mix_att + _bdot(jnp.concatenate(rec, axis=0), wr_ref[...])
    ms = jnp.mean(mix * mix, axis=-1, keepdims=True)
    x1 = x_ref[...] + gt1_ref[...] * (mix * lax.rsqrt(ms + NORM_EPS) * gpm_ref[...])
    x1_ref[...] = x1
    ms2 = jnp.mean(x1 * x1, axis=-1, keepdims=True)
    h2 = x1 * lax.rsqrt(ms2 + NORM_EPS) * gmlp_ref[...]
    h2_ref[...] = (h2 * (1.0 + sc2_ref[...]) + sh2_ref[...]).astype(BF16)


def _out_proj(att, yf, yb, g, bonus, x, wa, wr, lnw, lnb, seg, expand, gpm, gt1, gmlp, sc2, sh2, tm):
    s_len, d = x.shape
    row = lambda w: pl.BlockSpec((tm, w), lambda i: (i, 0))
    full = lambda arr: pl.BlockSpec(arr.shape, lambda i: (0,) * arr.ndim)
    consts = (wa, wr, lnw, lnb, seg, expand, gpm, gt1, gmlp, sc2, sh2)
    return pl.pallas_call(
        _out_kernel,
        grid=(s_len // tm,),
        in_specs=[row(ATT_WIDTH)] + [row(RWKV_WIDTH)] * 4 + [row(d)] + [full(c) for c in consts],
        out_specs=[row(d), row(d)],
        out_shape=[jax.ShapeDtypeStruct((s_len, d), F32), jax.ShapeDtypeStruct((s_len, d), BF16)],
        compiler_params=pltpu.CompilerParams(
            dimension_semantics=("arbitrary",),
            vmem_limit_bytes=_vmem(2 * tm * (ATT_WIDTH * 2 + 4 * RWKV_WIDTH * 4 + d * 4 + d * 4 + d * 2)
                                   + 2 * 2 * d * d + 8 * tm * d * 4 + 4 * 1024 * 1024)),
        name="out_proj",
    )(att, yf, yb, g, bonus, x, *consts)


def _mlp_kernel(h_ref, w1_ref, w2_ref, x1_ref, g_ref, gt_ref, o_ref, acc_ref):
    f = pl.program_id(1)

    @pl.when(f == 0)
    def _():
        acc_ref[...] = jnp.zeros_like(acc_ref)

    u = jnp.maximum(_dot(h_ref[...], w1_ref[...]), 0.0)
    acc_ref[...] += _dot((u * u).astype(BF16), w2_ref[...])

    @pl.when(f == pl.num_programs(1) - 1)
    def _():
        ff = acc_ref[...]
        ms = jnp.mean(ff * ff, axis=-1, keepdims=True)
        o_ref[...] = x1_ref[...] + gt_ref[...] * (ff * lax.rsqrt(ms + NORM_EPS) * g_ref[...])


def _mlp(h2, w1, w2, x1, gain, gt2, tm, tf):
    s_len, d = x1.shape
    dff = w1.shape[1]
    return pl.pallas_call(
        _mlp_kernel,
        grid=(s_len // tm, dff // tf),
        in_specs=[pl.BlockSpec((tm, d), lambda i, f: (i, 0)),
                  pl.BlockSpec((d, tf), lambda i, f: (0, f)),
                  pl.BlockSpec((tf, d), lambda i, f: (f, 0)),
                  pl.BlockSpec((tm, d), lambda i, f: (i, 0)),
                  pl.BlockSpec((1, d), lambda i, f: (0, 0)),
                  pl.BlockSpec((1, d), lambda i, f: (0, 0))],
        out_specs=pl.BlockSpec((tm, d), lambda i, f: (i, 0)),
        out_shape=jax.ShapeDtypeStruct((s_len, d), F32),
        scratch_shapes=[pltpu.VMEM((tm, d), F32)],
        compiler_params=pltpu.CompilerParams(
            dimension_semantics=("arbitrary", "arbitrary"),
            vmem_limit_bytes=_vmem(2 * (tm * d * 2 + 2 * d * tf * 2 + 2 * tm * d * 4)
                                   + tm * d * 4 + 3 * tm * tf * 4 + 4 * 1024 * 1024)),
        name="mlp",
    )(h2, w1, w2, x1, gain, gt2)


def _rope_tables(s_len, n_ctx):
    rows = s_len // GRID_W
    n_freq = ATT_HEAD_DIM // 4
    inv_freq = ROPE_BASE ** (-jnp.arange(n_freq, dtype=F32) / n_freq)
    ang_r = jnp.arange(rows, dtype=F32)[:, None] * inv_freq[None, :]
    ang_c = jnp.arange(GRID_W, dtype=F32)[:, None] * inv_freq[None, :]
    expand_r = lambda t: jnp.repeat(t, GRID_W, axis=0)
    expand_c = lambda t: jnp.tile(t, (rows, 1))
    cos_r, sin_r = expand_r(jnp.cos(ang_r)), expand_r(jnp.sin(ang_r))
    cos_c, sin_c = expand_c(jnp.cos(ang_c)), expand_c(jnp.sin(ang_c))
    cos = jnp.concatenate([cos_r, cos_r, cos_c, cos_c], axis=-1)
    sin = jnp.concatenate([-sin_r, sin_r, -sin_c, sin_c], axis=-1)
    cos = jnp.concatenate([cos, jnp.ones((n_ctx, ATT_HEAD_DIM), F32)], axis=0)
    sin = jnp.concatenate([sin, jnp.zeros((n_ctx, ATT_HEAD_DIM), F32)], axis=0)
    return cos, sin


def _block_diag2(w):
    _, k, n = w.shape
    z = jnp.zeros((k, n), w.dtype)
    return jnp.concatenate([jnp.concatenate([w[0], z], axis=1), jnp.concatenate([z, w[1]], axis=1)], axis=0)


def _layer(x2, ctx2, c8, l, w_ada, b_ada, g_pre_mix, g_post_mix, g_pre_mlp, g_post_mlp, w_in, attn_sink,
           rwkv_mu_prev, rwkv_mu_next, rwkv_w0, rwkv_w2, rwkv_a0, rwkv_a2, rwkv_g2, rwkv_k_k, rwkv_k_a,
           rwkv_r_k, rwkv_ln_w, rwkv_ln_b, w_out, w_mlp_in, w_mlp_out):
    s_len, d = x2.shape
    n_ctx = ctx2.shape[0]
    row = lambda v: v.reshape(1, -1)

    mod = _ada(c8, w_ada[l], row(b_ada[l]))
    sh1, sc1, gt1, sh2, sc2, gt2 = [mod[:2, j * d:(j + 1) * d] for j in range(N_MOD)]

    tm_rows = 256
    h = _norm_mod(x2, ctx2, row(g_pre_mix[l]), sc1.reshape(2, 1, d), sh1.reshape(2, 1, d), tm_rows)

    cos, sin = _rope_tables(s_len, n_ctx)
    m_all = s_len + n_ctx
    tm_proj = 768 if m_all % 768 == 0 else 256
    qkv = _proj(h, w_in[l][:, :ATT_IN].astype(BF16), cos, sin, tm_proj, ATT_IN, N_Q_HEADS + N_KV_HEADS)
    rwkv_in = w_in.shape[2] - ATT_IN
    slab = _proj(h, w_in[l][:, ATT_IN:].astype(BF16), cos, sin, tm_proj, rwkv_in // 3, 0)

    att = _attention(qkv, attn_sink[l], s_len, n_ctx)

    head = jnp.arange(RWKV_WIDTH) // RWKV_HEAD_DIM
    seg = (head[:, None] == jnp.arange(LANES)[None, :]).astype(BF16)
    expand = seg.T
    streams = _rwkv_prep(
        slab, row(rwkv_mu_prev[l]), row(rwkv_mu_next[l]),
        _block_diag2(rwkv_w2[l]).astype(BF16), rwkv_w0[l].reshape(1, -1),
        _block_diag2(rwkv_a2[l]).astype(BF16), rwkv_a0[l].reshape(1, -1),
        rwkv_g2[l].astype(BF16), row(rwkv_k_k[l]), row(rwkv_k_a[l]), rwkv_r_k[l].reshape(1, -1),
        seg, expand, s_len, n_ctx, tm_rows)
    r, v, kk, g, bonus, lwf, lwb, kdf, kdb, bf, bb = streams
    yf, yb = _rwkv_scan(r, v, kk, lwf, lwb, kdf, kdb, bf, bb, s_len, n_ctx)

    x1, h2 = _out_proj(att, yf, yb, g, bonus, x2,
                       w_out[l][:ATT_WIDTH].astype(BF16), w_out[l][ATT_WIDTH:].astype(BF16),
                       row(rwkv_ln_w[l]), row(rwkv_ln_b[l]), seg, expand,
                       row(g_post_mix[l]), gt1[0:1], row(g_pre_mlp[l]), sc2[0:1], sh2[0:1], 512)
    return _mlp(h2, w_mlp_in[l].astype(BF16), w_mlp_out[l].astype(BF16), x1, row(g_post_mlp[l]), gt2[0:1],
                512, 1024)


def kernel(x, c, ctx, c_ctx, w_ada, b_ada, g_pre_mix, g_post_mix, g_pre_mlp, g_post_mlp, w_in, attn_sink,
           rwkv_mu_prev, rwkv_mu_next, rwkv_w0, rwkv_w2, rwkv_a0, rwkv_a2, rwkv_g2, rwkv_k_k, rwkv_k_a,
           rwkv_r_k, rwkv_ln_w, rwkv_ln_b, w_out, w_mlp_in, w_mlp_out):
    b, s_len, d = x.shape
    depth = w_ada.shape[0]
    assert b == 1 and depth == 1, "single sample, single layer (context stream is not carried to a next layer)"
    c8 = jnp.zeros((8, d), F32).at[0].set(c[0]).at[1].set(c_ctx)
    out = _layer(x[0], ctx[0], c8, 0, w_ada, b_ada, g_pre_mix, g_post_mix, g_pre_mlp, g_post_mlp, w_in,
                 attn_sink, rwkv_mu_prev, rwkv_mu_next, rwkv_w0, rwkv_w2, rwkv_a0, rwkv_a2, rwkv_g2,
                 rwkv_k_k, rwkv_k_a, rwkv_r_k, rwkv_ln_w, rwkv_ln_b, w_out, w_mlp_in, w_mlp_out)
    return out[None]
```

```python
import functools

import jax
import jax.numpy as jnp
from jax import lax
from jax.experimental import pallas as pl
from jax.experimental.pallas import tpu as pltpu

F32 = jnp.float32
BF16 = jnp.bfloat16

GRID_W = 64
ATT_HEAD_DIM = 128
N_Q_HEADS = 8
N_KV_HEADS = 2
GQA_GROUP = N_Q_HEADS // N_KV_HEADS
ATT_WIDTH = N_Q_HEADS * ATT_HEAD_DIM
KV_WIDTH = N_KV_HEADS * ATT_HEAD_DIM
ATT_IN = ATT_WIDTH + 2 * KV_WIDTH
WINDOW = 128
BLOCK = 128
ROPE_BASE = 10000.0
RWKV_HEAD_DIM = 64
RWKV_WIDTH = 1024
N_RWKV_HEADS = RWKV_WIDTH // RWKV_HEAD_DIM
LORA_W = 64
LORA_A = 64
LORA_G = 128
N_MOD = 6
NORM_EPS = 1e-6
GN_EPS = 64e-5
DECAY_SCALE = 0.6065306597126334

LANES = 128
CHUNK = 64
PAIR = 2 * RWKV_HEAD_DIM
N_PAIRS = RWKV_WIDTH // PAIR
OUT_SUB_BLOCKS = 4
VMEM_CAP = 56 * 1024 * 1024


def _vmem(nbytes):
    return int(min(VMEM_CAP, max(16 * 1024 * 1024, nbytes)))


def _dot(a, b):
    return jnp.dot(a, b, preferred_element_type=F32)


def _dot_nt(a, b):
    return lax.dot_general(a, b, (((1,), (1,)), ((), ())), preferred_element_type=F32)


def _dot_tn(a, b):
    return lax.dot_general(a, b, (((0,), (0,)), ((), ())), preferred_element_type=F32)


def _bdot(a, b):
    return _dot(a.astype(BF16), b.astype(BF16))


def _bdot_nt(a, b):
    return _dot_nt(a.astype(BF16), b.astype(BF16))


def _bdot_tn(a, b):
    return _dot_tn(a.astype(BF16), b.astype(BF16))


def _split_dot(x, m):
    hi = x.astype(BF16)
    lo = (x - hi.astype(F32)).astype(BF16)
    return _dot(hi, m) + _dot(lo, m)


def _head_sum(x, seg, expand):
    return _split_dot(_split_dot(x, seg), expand)


def _ada_kernel(c_ref, w_ref, b_ref, o_ref):
    c = c_ref[...]
    s = c * jax.nn.sigmoid(c)
    o_ref[...] = _bdot(s, w_ref[...]) + b_ref[...]


def _ada(c8, w_ada, b_ada):
    d, n = w_ada.shape
    tn = 1024
    return pl.pallas_call(
        _ada_kernel,
        grid=(n // tn,),
        in_specs=[pl.BlockSpec((8, d), lambda j: (0, 0)),
                  pl.BlockSpec((d, tn), lambda j: (0, j)),
                  pl.BlockSpec((1, tn), lambda j: (0, j))],
        out_specs=pl.BlockSpec((8, tn), lambda j: (0, j)),
        out_shape=jax.ShapeDtypeStruct((8, n), F32),
        compiler_params=pltpu.CompilerParams(
            dimension_semantics=("arbitrary",),
            vmem_limit_bytes=_vmem(2 * d * tn * 4 + 4 * 1024 * 1024)),
        name="ada",
    )(c8, w_ada, b_ada)


def _norm_mod_kernel(x_ref, c_ref, g_ref, sc_ref, sh_ref, o_ref, *, n_lat_tiles):
    def emit(src_ref):
        x = src_ref[...]
        ms = jnp.mean(x * x, axis=-1, keepdims=True)
        y = x * lax.rsqrt(ms + NORM_EPS) * g_ref[...]
        o_ref[...] = (y * (1.0 + sc_ref[0]) + sh_ref[0]).astype(BF16)

    i = pl.program_id(0)
    pl.when(i < n_lat_tiles)(lambda: emit(x_ref))
    pl.when(i >= n_lat_tiles)(lambda: emit(c_ref))


def _norm_mod(x2, ctx2, gain, sc2, sh2, tm):
    s_len, d = x2.shape
    n_lat_tiles = s_len // tm
    n_tiles = n_lat_tiles + ctx2.shape[0] // tm
    sel = lambda i: (jnp.where(i >= n_lat_tiles, 1, 0), 0, 0)
    return pl.pallas_call(
        functools.partial(_norm_mod_kernel, n_lat_tiles=n_lat_tiles),
        grid=(n_tiles,),
        in_specs=[pl.BlockSpec((tm, d), lambda i: (jnp.minimum(i, n_lat_tiles - 1), 0)),
                  pl.BlockSpec((tm, d), lambda i: (jnp.maximum(i - n_lat_tiles, 0), 0)),
                  pl.BlockSpec((1, d), lambda i: (0, 0)),
                  pl.BlockSpec((1, 1, d), sel),
                  pl.BlockSpec((1, 1, d), sel)],
        out_specs=pl.BlockSpec((tm, d), lambda i: (i, 0)),
        out_shape=jax.ShapeDtypeStruct((n_tiles * tm, d), BF16),
        compiler_params=pltpu.CompilerParams(
            dimension_semantics=("arbitrary",),
            vmem_limit_bytes=_vmem(2 * tm * d * 10 + 4 * 1024 * 1024)),
        name="norm_mod",
    )(x2, ctx2, gain, sc2, sh2)


def _rope_heads(acc, cos, sin, n_heads):
    lane = lax.broadcasted_iota(jnp.int32, (1, ATT_HEAD_DIM), 1)
    first = (lane % 64) < 32
    outs = []
    for h in range(n_heads):
        xh = acc[:, h * ATT_HEAD_DIM:(h + 1) * ATT_HEAD_DIM]
        partner = jnp.where(first, pltpu.roll(xh, 96, 1), pltpu.roll(xh, 32, 1))
        outs.append(xh * cos + partner * sin)
    return outs


def _proj_kernel(a_ref, w_ref, cos_ref, sin_ref, o_ref, *, rope_heads):
    acc = _dot(a_ref[...], w_ref[...])
    if rope_heads:
        roped = _rope_heads(acc, cos_ref[...], sin_ref[...], rope_heads)
        for h, r in enumerate(roped):
            o_ref[:, h * ATT_HEAD_DIM:(h + 1) * ATT_HEAD_DIM] = r
        o_ref[:, rope_heads * ATT_HEAD_DIM:] = acc[:, rope_heads * ATT_HEAD_DIM:]
    else:
        o_ref[...] = acc


def _proj(a, w, cos, sin, tm, tn, rope_heads):
    m, k = a.shape
    n = w.shape[1]
    return pl.pallas_call(
        functools.partial(_proj_kernel, rope_heads=rope_heads),
        grid=(m // tm, n // tn),
        in_specs=[pl.BlockSpec((tm, k), lambda i, j: (i, 0)),
                  pl.BlockSpec((k, tn), lambda i, j: (0, j)),
                  pl.BlockSpec((tm, ATT_HEAD_DIM), lambda i, j: (i, 0)),
                  pl.BlockSpec((tm, ATT_HEAD_DIM), lambda i, j: (i, 0))],
        out_specs=pl.BlockSpec((tm, tn), lambda i, j: (i, j)),
        out_shape=jax.ShapeDtypeStruct((m, n), F32),
        compiler_params=pltpu.CompilerParams(
            dimension_semantics=("arbitrary", "arbitrary"),
            vmem_limit_bytes=_vmem(2 * (tm * k * 2 + k * tn * 2 + tm * tn * 4 + 2 * tm * 512)
                                   + tm * tn * 8 + 4 * 1024 * 1024)),
        name="proj_qkv" if rope_heads else "proj_slab",
    )(a, w, cos, sin)


def _attn_kernel(sink_ref, q_ref, kp_ref, kc_ref, kn_ref, vp_ref, vc_ref, vn_ref,
                 kx_ref, vx_ref, o_ref, *, nb, n_ctx):
    i = pl.program_id(0)
    scale = ATT_HEAD_DIM ** -0.5
    rows = GQA_GROUP * BLOCK
    span = 3 * BLOCK
    ncol = span + n_ctx
    row = lax.broadcasted_iota(jnp.int32, (rows, ncol), 0) % BLOCK
    col = lax.broadcasted_iota(jnp.int32, (rows, ncol), 1)
    rel = col - BLOCK - row
    lo = jnp.where(i > 0, 0, BLOCK)
    hi = jnp.where(i < nb - 1, span, 2 * BLOCK)
    band = (jnp.abs(rel) <= WINDOW) & (col >= lo) & (col < hi)
    valid = band | (col >= span)
    heads = range(N_KV_HEADS)
    cols = [slice(hk * ATT_HEAD_DIM, (hk + 1) * ATT_HEAD_DIM) for hk in heads]
    q = [(jnp.concatenate(
        [q_ref[:, (hk * GQA_GROUP + g) * ATT_HEAD_DIM:(hk * GQA_GROUP + g + 1) * ATT_HEAD_DIM]
         for g in range(GQA_GROUP)], axis=0) * scale).astype(BF16) for hk in heads]
    k_all = [jnp.concatenate([kp_ref[:, ks], kc_ref[:, ks], kn_ref[:, ks], kx_ref[:, ks]], axis=0).astype(BF16)
             for ks in cols]
    v_all = [jnp.concatenate([vp_ref[:, ks], vc_ref[:, ks], vn_ref[:, ks], vx_ref[:, ks]], axis=0).astype(BF16)
             for ks in cols]
    sink = [jnp.concatenate([jnp.full((BLOCK, 1), sink_ref[hk * GQA_GROUP + g], F32) for g in range(GQA_GROUP)],
                            axis=0) for hk in heads]
    s = [jnp.where(valid, _dot_nt(qh, kh), -jnp.inf) for qh, kh in zip(q, k_all)]
    m = [jnp.maximum(jnp.max(sh, axis=-1, keepdims=True), sk) for sh, sk in zip(s, sink)]
    e = [jnp.exp(sh - mh) for sh, mh in zip(s, m)]
    denom = [jnp.sum(eh, axis=-1, keepdims=True) + jnp.exp(sk - mh) for eh, sk, mh in zip(e, sink, m)]
    o = [_dot(eh.astype(BF16), vh) * (1.0 / dh) for eh, vh, dh in zip(e, v_all, denom)]
    for hk in heads:
        for g in range(GQA_GROUP):
            h = hk * GQA_GROUP + g
            o_ref[:, h * ATT_HEAD_DIM:(h + 1) * ATT_HEAD_DIM] = o[hk][g * BLOCK:(g + 1) * BLOCK].astype(BF16)


def _attention(qkv, sink, s_len, n_ctx):
    nb = s_len // BLOCK
    kcol = ATT_WIDTH // KV_WIDTH
    vcol = kcol + 1
    ctx_blk = s_len // n_ctx
    prev = lambda i: jnp.maximum(i - 1, 0)
    nxt = lambda i: jnp.minimum(i + 1, nb - 1)
    kv = lambda rowf, c: pl.BlockSpec((BLOCK, KV_WIDTH), lambda i: (rowf(i), c))
    same = lambda i: i
    return pl.pallas_call(
        functools.partial(_attn_kernel, nb=nb, n_ctx=n_ctx),
        grid=(nb,),
        in_specs=[pl.BlockSpec(memory_space=pltpu.SMEM),
                  pl.BlockSpec((BLOCK, ATT_WIDTH), lambda i: (i, 0)),
                  kv(prev, kcol), kv(same, kcol), kv(nxt, kcol),
                  kv(prev, vcol), kv(same, vcol), kv(nxt, vcol),
                  pl.BlockSpec((n_ctx, KV_WIDTH), lambda i: (ctx_blk, kcol)),
                  pl.BlockSpec((n_ctx, KV_WIDTH), lambda i: (ctx_blk, vcol))],
        out_specs=pl.BlockSpec((BLOCK, ATT_WIDTH), lambda i: (i, 0)),
        out_shape=jax.ShapeDtypeStruct((s_len, ATT_WIDTH), BF16),
        compiler_params=pltpu.CompilerParams(
            dimension_semantics=("arbitrary",),
            vmem_limit_bytes=_vmem(32 * 1024 * 1024)),
        name="attention",
    )(sink, qkv, qkv, qkv, qkv, qkv, qkv, qkv, qkv, qkv)


def _prep_kernel(x_ref, xp_ref, xn_ref, mup_ref, mun_ref, w2_ref, w0_ref, a2_ref, a0_ref, g2_ref,
                 kk_ref, ka_ref, rk_ref, seg_ref, exp_ref,
                 r_o, v_o, kk_o, g_o, bon_o, lwf_o, lwb_o, kdf_o, kdb_o, bf_o, bb_o,
                 *, tm, start_tiles, end_tiles):
    i = pl.program_id(0)
    r_w = RWKV_WIDTH
    x = x_ref[...]
    is_start = functools.reduce(jnp.logical_or, [i == t for t in start_tiles])
    is_end = functools.reduce(jnp.logical_or, [i == t for t in end_tiles])
    keep_prev = jnp.full((1, 1), jnp.where(is_start, 0, 1), jnp.int32) == 1
    keep_next = jnp.full((1, 1), jnp.where(is_end, 0, 1), jnp.int32) == 1
    prev_row = jnp.where(keep_prev, xp_ref[7:8, :], 0.0)
    next_row = jnp.where(keep_next, xn_ref[0:1, :], 0.0)
    xp = pltpu.roll(x, 1, 0)
    xn = pltpu.roll(x, tm - 1, 0)
    row8 = lax.broadcasted_iota(jnp.int32, (8, 1), 0)
    xp = jnp.concatenate([jnp.where(row8 == 0, prev_row, xp[:8]), xp[8:]], axis=0)
    xn = jnp.concatenate([xn[:tm - 8], jnp.where(row8 == 7, next_row, xn[tm - 8:])], axis=0)
    mup = mup_ref[...]
    mun = mun_ref[...]
    s = x * (1.0 - mup - mun) + xp * mup + xn * mun

    r = s[:, :r_w]
    k = s[:, r_w:2 * r_w]
    v = s[:, 2 * r_w:3 * r_w]
    o1 = 3 * r_w
    o2 = o1 + 2 * LORA_W
    o3 = o2 + 2 * LORA_A
    w_lo = s[:, o1:o2]
    a_lo = s[:, o2:o3]
    g_lo = s[:, o3:]

    zw = w0_ref[...] + _bdot(jnp.tanh(w_lo), w2_ref[...])
    lw = -DECAY_SCALE * jax.nn.sigmoid(zw)
    a = jax.nn.sigmoid(a0_ref[...] + _bdot(a_lo, a2_ref[...]))
    g = _bdot(jax.nn.sigmoid(g_lo), g2_ref[...])

    kkf = k * kk_ref[...]
    ssq = _head_sum(kkf * kkf, seg_ref[...], exp_ref[...])
    kk = kkf * jnp.minimum(lax.rsqrt(ssq), 1e12)
    ka = ka_ref[...]
    kd_f = k * (1.0 + (a[:, :r_w] - 1.0) * ka)
    kd_b = k * (1.0 + (a[:, r_w:] - 1.0) * ka)
    k_bonus = 0.5 * (kd_f + kd_b)
    bonus = _head_sum(r * k_bonus * rk_ref[...], seg_ref[...], exp_ref[...]) * v

    r_o[...] = r
    v_o[...] = v
    kk_o[...] = kk
    g_o[...] = g
    bon_o[...] = bonus
    lwf_o[...] = lw[:, :r_w]
    lwb_o[...] = lw[:, r_w:]
    kdf_o[...] = kd_f
    kdb_o[...] = kd_b
    bf_o[...] = kk * a[:, :r_w]
    bb_o[...] = kk * a[:, r_w:]


def _rwkv_prep(slab, mup, mun, w2blk, w0cat, a2blk, a0cat, g2, k_k, k_a, r_k, seg, expand, s_len, n_ctx, tm):
    m, width = slab.shape
    nt = m // tm
    hb = tm // 8
    nblk8 = m // 8
    start_tiles = (0, s_len // tm)
    end_tiles = (s_len // tm - 1, nt - 1)
    full = lambda arr: pl.BlockSpec(arr.shape, lambda i: (0,) * arr.ndim)
    consts = (mup, mun, w2blk, w0cat, a2blk, a0cat, g2, k_k, k_a, r_k, seg, expand)
    out_spec = pl.BlockSpec((tm, RWKV_WIDTH), lambda i: (i, 0))
    n_out = 11
    return pl.pallas_call(
        functools.partial(_prep_kernel, tm=tm, start_tiles=start_tiles, end_tiles=end_tiles),
        grid=(nt,),
        in_specs=[pl.BlockSpec((tm, width), lambda i: (i, 0)),
                  pl.BlockSpec((8, width), lambda i: (jnp.maximum(i * hb - 1, 0), 0)),
                  pl.BlockSpec((8, width), lambda i: (jnp.minimum((i + 1) * hb, nblk8 - 1), 0))]
                 + [full(c) for c in consts],
        out_specs=[out_spec] * n_out,
        out_shape=[jax.ShapeDtypeStruct((m, RWKV_WIDTH), F32)] * n_out,
        compiler_params=pltpu.CompilerParams(
            dimension_semantics=("arbitrary",),
            vmem_limit_bytes=_vmem(2 * tm * width * 4 + 2 * n_out * tm * RWKV_WIDTH * 4
                                   + 12 * tm * RWKV_WIDTH * 4 + 8 * 1024 * 1024)),
        name="rwkv_prep",
    )(slab, slab, slab, *consts)


def _stack(x, m0):
    return jnp.concatenate([jnp.where(m0, x, 0.0), jnp.where(m0, 0.0, x)], axis=0)


def _scan_streams(r_ref, v_ref, kk_ref, lw_ref, kd_ref, b_ref, reverse):
    L = CHUNK
    lw = lw_ref[...]
    ti = lax.broadcasted_iota(jnp.int32, (L, L), 0)
    tj = lax.broadcasted_iota(jnp.int32, (L, L), 1)
    tri = jnp.where((ti <= tj) if reverse else (ti >= tj), 1.0, 0.0).astype(BF16)
    lw_hi = lw.astype(BF16)
    lw_lo = (lw - lw_hi.astype(F32)).astype(BF16)
    c = _dot(tri, lw_hi) + _dot(tri, lw_lo)
    p_cum = jnp.exp(c)
    p_inv = jnp.exp(-c)
    p_prev = jnp.exp(c - lw)
    last = 0 if reverse else L - 1
    p_end = p_cum[last:last + 1, :]
    b_t = b_ref[...] * p_inv
    k_t = kd_ref[...] * p_inv
    t_idx = lax.broadcasted_iota(jnp.int32, (2 * L, PAIR), 0) % L
    s_idx = lax.broadcasted_iota(jnp.int32, (2 * L, PAIR), 1) % RWKV_HEAD_DIM
    strict = (s_idx > t_idx) if reverse else (s_idx < t_idx)
    incl = (s_idx >= t_idx) if reverse else (s_idx <= t_idx)
    return dict(a=-kk_ref[...] * p_prev, r=r_ref[...] * p_cum, b=b_t, k=k_t, bh=b_t * p_end, kh=k_t * p_end,
                v=v_ref[...], p_end=p_end, strict=strict, incl=incl)


def _scan_kernel(rf, vf, kkf, lwf, kdf, bf, rb, vb, kkb, lwb, kdb, bb, yf, yb, s_ref):
    @pl.when(pl.program_id(0) == 0)
    def _():
        s_ref[...] = jnp.zeros_like(s_ref)

    L = CHUNK
    streams = (_scan_streams(rf, vf, kkf, lwf, kdf, bf, False), _scan_streams(rb, vb, kkb, lwb, kdb, bb, True))
    y_refs = (yf, yb)
    m0 = lax.broadcasted_iota(jnp.int32, (L, PAIR), 1) < RWKV_HEAD_DIM
    hi_lanes = lax.broadcasted_iota(jnp.int32, (2 * L, PAIR), 1) >= RWKV_HEAD_DIM
    same_head = ((lax.broadcasted_iota(jnp.int32, (PAIR, PAIR), 0) // RWKV_HEAD_DIM)
                 == (lax.broadcasted_iota(jnp.int32, (PAIR, PAIR), 1) // RWKV_HEAD_DIM))
    chains = [(d, p) for d in range(2) for p in range(N_PAIRS)]

    def side(name, d, p):
        return streams[d][name][:, p * PAIR:(p + 1) * PAIR]

    def fold(z):
        return jnp.where(m0, z[:L], z[L:])

    def rows_of(z):
        return jnp.concatenate([z[:L, :RWKV_HEAD_DIM], pltpu.roll(z[L:], RWKV_HEAD_DIM, 1)[:, :RWKV_HEAD_DIM]],
                               axis=0)

    ar_stack = [jnp.concatenate([_stack(side("a", d, p), m0), _stack(side("r", d, p), m0)], axis=0).astype(BF16)
                for d, p in chains]
    ar_side = [jnp.concatenate([side("a", d, p), side("r", d, p)], axis=0).astype(BF16) for d, p in chains]
    bk = [jnp.concatenate([side("b", d, p), side("k", d, p)], axis=0).astype(BF16) for d, p in chains]
    v_b = [side("v", d, p).astype(BF16) for d, p in chains]
    state = [s_ref[d, p] for d, p in chains]
    g = [_dot_nt(x, z) for x, z in zip(ar_stack, bk)]
    xs = [_dot_nt(x, s.astype(BF16)) for x, s in zip(ar_side, state)]
    g_a = [jnp.where(streams[d]["strict"], gc[:2 * L], 0.0) for (d, _), gc in zip(chains, g)]
    g_r = [jnp.where(streams[d]["incl"], gc[2 * L:], 0.0).astype(BF16) for (d, _), gc in zip(chains, g)]
    zeros = jnp.zeros((L, PAIR), BF16)
    av = [fold(_dot(jnp.where(hi_lanes, ga, 0.0).astype(BF16), jnp.concatenate([zeros, v], axis=0)))
          for ga, v in zip(g_a, v_b)]
    w = [x[:L] + y for x, y in zip(xs, av)]
    am_rows = [ga[:, :RWKV_HEAD_DIM].astype(BF16) for ga in g_a]
    am_side = [jnp.where(m0, ga[:L], pltpu.roll(ga[L:], RWKV_HEAD_DIM, 1)).astype(BF16) for ga in g_a]
    w = [wc + fold(_dot(a, wc.astype(BF16))) for a, wc in zip(am_rows, w)]
    for _ in range(5):
        sq = [_dot(a, s) for a, s in zip(am_rows, am_side)]
        am_rows = [rows_of(z).astype(BF16) for z in sq]
        am_side = [fold(z).astype(BF16) for z in sq]
        w = [wc + fold(_dot(a, wc.astype(BF16))) for a, wc in zip(am_rows, w)]
    sa = [wc.astype(BF16) for wc in w]
    ys = [x[L:] + fold(_dot(gr, jnp.concatenate([s, v], axis=0))) for x, gr, s, v in zip(xs, g_r, sa, v_b)]
    for (d, p), yc in zip(chains, ys):
        y_refs[d][:, p * PAIR:(p + 1) * PAIR] = yc
    for (d, p), s_old, sac, vc in zip(chains, state, sa, v_b):
        bhkh = jnp.concatenate([side("bh", d, p), side("kh", d, p)], axis=0).astype(BF16)
        upd = _dot_tn(jnp.concatenate([sac, vc], axis=0), bhkh)
        s_ref[d, p] = s_old * side("p_end", d, p) + jnp.where(same_head, upd, 0.0)


def _rwkv_scan(r, v, kk, lwf, lwb, kdf, kdb, bf, bb, s_len, n_ctx):
    m = r.shape[0]
    nc = m // CHUNK
    nc_lat = s_len // CHUNK
    nc_ctx = n_ctx // CHUNK
    fwd = lambda i: (jnp.where(i < nc_ctx, nc_lat + i, i - nc_ctx), 0)
    bwd = lambda i: (nc - 1 - i, 0)
    fs = pl.BlockSpec((CHUNK, RWKV_WIDTH), fwd)
    bs = pl.BlockSpec((CHUNK, RWKV_WIDTH), bwd)
    return pl.pallas_call(
        _scan_kernel,
        grid=(nc,),
        in_specs=[fs] * 6 + [bs] * 6,
        out_specs=[fs, bs],
        out_shape=[jax.ShapeDtypeStruct((m, RWKV_WIDTH), F32)] * 2,
        scratch_shapes=[pltpu.VMEM((2, N_PAIRS, PAIR, PAIR), F32)],
        compiler_params=pltpu.CompilerParams(
            dimension_semantics=("arbitrary",),
            vmem_limit_bytes=_vmem(2 * 14 * CHUNK * RWKV_WIDTH * 4 + 2 * N_PAIRS * PAIR * PAIR * 4
                                   + 24 * 1024 * 1024)),
        name="rwkv_scan",
    )(r, v, kk, lwf, kdf, bf, r, v, kk, lwb, kdb, bb)


def _out_kernel(att_ref, yf_ref, yb_ref, g_ref, bon_ref, x_ref, wa_ref, wr_ref, lnw_ref, lnb_ref,
                seg_ref, exp_ref, gpm_ref, gt1_ref, gmlp_ref, sc2_ref, sh2_ref, x1_ref, h2_ref):
    mix_att = _dot(att_ref[...], wa_ref[...])
    inv_n = 1.0 / RWKV_HEAD_DIM
    seg, expand = seg_ref[...], exp_ref[...]
    rows = att_ref.shape[0] // OUT_SUB_BLOCKS
    blocks = [pl.ds(n * rows, rows) for n in range(OUT_SUB_BLOCKS)]
    y = [yf_ref[b, :] + yb_ref[b, :] for b in blocks]
    s1 = [_bdot(v, seg) for v in y]
    yc = [v - _split_dot(s, expand) * inv_n for v, s in zip(y, s1)]
    s2 = [_bdot(c * c, seg) for c in yc]
    var = [_bdot(s, expand) * inv_n for s in s2]
    rec = [((c * lax.rsqrt(vr + GN_EPS) * lnw_ref[...] + lnb_ref[...]) + bon_ref[b, :]) * g_ref[b, :]
           for c, vr, b in zip(yc, var, blocks)]
    mix = mix_att + _bdot(jnp.concatenate(rec, axis=0), wr_ref[...])
    ms = jnp.mean(mix * mix, axis=-1, keepdims=True)
    x1 = x_ref[...] + gt1_ref[...] * (mix * lax.rsqrt(ms + NORM_EPS) * gpm_ref[...])
    x1_ref[...] = x1
    ms2 = jnp.mean(x1 * x1, axis=-1, keepdims=True)
    h2 = x1 * lax.rsqrt(ms2 + NORM_EPS) * gmlp_ref[...]
    h2_ref[...] = (h2 * (1.0 + sc2_ref[...]) + sh2_ref[...]).astype(BF16)


def _out_proj(att, yf, yb, g, bonus, x, wa, wr, lnw, lnb, seg, expand, gpm, gt1, gmlp, sc2, sh2, tm):
    s_len, d = x.shape
    row = lambda w: pl.BlockSpec((tm, w), lambda i: (i, 0))
    full = lambda arr: pl.BlockSpec(arr.shape, lambda i: (0,) * arr.ndim)
    consts = (wa, wr, lnw, lnb, seg, expand, gpm, gt1, gmlp, sc2, sh2)
    return pl.pallas_call(
        _out_kernel,
        grid=(s_len // tm,),
        in_specs=[row(ATT_WIDTH)] + [row(RWKV_WIDTH)] * 4 + [row(d)] + [full(c) for c in consts],
        out_specs=[row(d), row(d)],
        out_shape=[jax.ShapeDtypeStruct((s_len, d), F32), jax.ShapeDtypeStruct((s_len, d), BF16)],
        compiler_params=pltpu.CompilerParams(
            dimension_semantics=("arbitrary",),
            vmem_limit_bytes=_vmem(2 * tm * (ATT_WIDTH * 2 + 4 * RWKV_WIDTH * 4 + d * 4 + d * 4 + d * 2)
                                   + 2 * 2 * d * d + 8 * tm * d * 4 + 4 * 1024 * 1024)),
        name="out_proj",
    )(att, yf, yb, g, bonus, x, *consts)


def _mlp_kernel(h_ref, w1_ref, w2_ref, x1_ref, g_ref, gt_ref, o_ref, acc_ref):
    f = pl.program_id(1)

    @pl.when(f == 0)
    def _():
        acc_ref[...] = jnp.zeros_like(acc_ref)

    u = jnp.maximum(_dot(h_ref[...], w1_ref[...]), 0.0)
    acc_ref[...] += _dot((u * u).astype(BF16), w2_ref[...])

    @pl.when(f == pl.num_programs(1) - 1)
    def _():
        ff = acc_ref[...]
        ms = jnp.mean(ff * ff, axis=-1, keepdims=True)
        o_ref[...] = x1_ref[...] + gt_ref[...] * (ff * lax.rsqrt(ms + NORM_EPS) * g_ref[...])


def _mlp(h2, w1, w2, x1, gain, gt2, tm, tf):
    s_len, d = x1.shape
    dff = w1.shape[1]
    return pl.pallas_call(
        _mlp_kernel,
        grid=(s_len // tm, dff // tf),
        in_specs=[pl.BlockSpec((tm, d), lambda i, f: (i, 0)),
                  pl.BlockSpec((d, tf), lambda i, f: (0, f)),
                  pl.BlockSpec((tf, d), lambda i, f: (f, 0)),
                  pl.BlockSpec((tm, d), lambda i, f: (i, 0)),
                  pl.BlockSpec((1, d), lambda i, f: (0, 0)),
                  pl.BlockSpec((1, d), lambda i, f: (0, 0))],
        out_specs=pl.BlockSpec((tm, d), lambda i, f: (i, 0)),
        out_shape=jax.ShapeDtypeStruct((s_len, d), F32),
        scratch_shapes=[pltpu.VMEM((tm, d), F32)],
        compiler_params=pltpu.CompilerParams(
            dimension_semantics=("arbitrary", "arbitrary"),
            vmem_limit_bytes=_vmem(2 * (tm * d * 2 + 2 * d * tf * 2 + 2 * tm * d * 4)
                                   + tm * d * 4 + 3 * tm * tf * 4 + 4 * 1024 * 1024)),
        name="mlp",
    )(h2, w1, w2, x1, gain, gt2)


def _rope_tables(s_len, n_ctx):
    rows = s_len // GRID_W
    n_freq = ATT_HEAD_DIM // 4
    inv_freq = ROPE_BASE ** (-jnp.arange(n_freq, dtype=F32) / n_freq)
    ang_r = jnp.arange(rows, dtype=F32)[:, None] * inv_freq[None, :]
    ang_c = jnp.arange(GRID_W, dtype=F32)[:, None] * inv_freq[None, :]
    expand_r = lambda t: jnp.repeat(t, GRID_W, axis=0)
    expand_c = lambda t: jnp.tile(t, (rows, 1))
    cos_r, sin_r = expand_r(jnp.cos(ang_r)), expand_r(jnp.sin(ang_r))
    cos_c, sin_c = expand_c(jnp.cos(ang_c)), expand_c(jnp.sin(ang_c))
    cos = jnp.concatenate([cos_r, cos_r, cos_c, cos_c], axis=-1)
    sin = jnp.concatenate([-sin_r, sin_r, -sin_c, sin_c], axis=-1)
    cos = jnp.concatenate([cos, jnp.ones((n_ctx, ATT_HEAD_DIM), F32)], axis=0)
    sin = jnp.concatenate([sin, jnp.zeros((n_ctx, ATT_HEAD_DIM), F32)], axis=0)
    return cos, sin


def _block_diag2(w):
    _, k, n = w.shape
    z = jnp.zeros((k, n), w.dtype)
    return jnp.concatenate([jnp.concatenate([w[0], z], axis=1), jnp.concatenate([z, w[1]], axis=1)], axis=0)


def _layer(x2, ctx2, c8, l, w_ada, b_ada, g_pre_mix, g_post_mix, g_pre_mlp, g_post_mlp, w_in, attn_sink,
           rwkv_mu_prev, rwkv_mu_next, rwkv_w0, rwkv_w2, rwkv_a0, rwkv_a2, rwkv_g2, rwkv_k_k, rwkv_k_a,
           rwkv_r_k, rwkv_ln_w, rwkv_ln_b, w_out, w_mlp_in, w_mlp_out):
    s_len, d = x2.shape
    n_ctx = ctx2.shape[0]
    row = lambda v: v.reshape(1, -1)

    mod = _ada(c8, w_ada[l], row(b_ada[l]))
    sh1, sc1, gt1, sh2, sc2, gt2 = [mod[:2, j * d:(j + 1) * d] for j in range(N_MOD)]

    tm_rows = 256
    h = _norm_mod(x2, ctx2, row(g_pre_mix[l]), sc1.reshape(2, 1, d), sh1.reshape(2, 1, d), tm_rows)

    cos, sin = _rope_tables(s_len, n_ctx)
    m_all = s_len + n_ctx
    tm_proj = 768 if m_all % 768 == 0 else 256
    qkv = _proj(h, w_in[l][:, :ATT_IN].astype(BF16), cos, sin, tm_proj, ATT_IN, N_Q_HEADS + N_KV_HEADS)
    rwkv_in = w_in.shape[2] - ATT_IN
    slab = _proj(h, w_in[l][:, ATT_IN:].astype(BF16), cos, sin, tm_proj, rwkv_in // 3, 0)

    att = _attention(qkv, attn_sink[l], s_len, n_ctx)

    head = jnp.arange(RWKV_WIDTH) // RWKV_HEAD_DIM
    seg = (head[:, None] == jnp.arange(LANES)[None, :]).astype(BF16)
    expand = seg.T
    streams = _rwkv_prep(
        slab, row(rwkv_mu_prev[l]), row(rwkv_mu_next[l]),
        _block_diag2(rwkv_w2[l]).astype(BF16), rwkv_w0[l].reshape(1, -1),
        _block_diag2(rwkv_a2[l]).astype(BF16), rwkv_a0[l].reshape(1, -1),
        rwkv_g2[l].astype(BF16), row(rwkv_k_k[l]), row(rwkv_k_a[l]), rwkv_r_k[l].reshape(1, -1),
        seg, expand, s_len, n_ctx, tm_rows)
    r, v, kk, g, bonus, lwf, lwb, kdf, kdb, bf, bb = streams
    yf, yb = _rwkv_scan(r, v, kk, lwf, lwb, kdf, kdb, bf, bb, s_len, n_ctx)

    x1, h2 = _out_proj(att, yf, yb, g, bonus, x2,
                       w_out[l][:ATT_WIDTH].astype(BF16), w_out[l][ATT_WIDTH:].astype(BF16),
                       row(rwkv_ln_w[l]), row(rwkv_ln_b[l]), seg, expand,
                       row(g_post_mix[l]), gt1[0:1], row(g_pre_mlp[l]), sc2[0:1], sh2[0:1], 512)
    return _mlp(h2, w_mlp_in[l].astype(BF16), w_mlp_out[l].astype(BF16), x1, row(g_post_mlp[l]), gt2[0:1],
                512, 1024)


def kernel(x, c, ctx, c_ctx, w_ada, b_ada, g_pre_mix, g_post_mix, g_pre_mlp, g_post_mlp, w_in, attn_sink,
           rwkv_mu_prev, rwkv_mu_next, rwkv_w0, rwkv_w2, rwkv_a0, rwkv_a2, rwkv_g2, rwkv_k_k, rwkv_k_a,
           rwkv_r_k, rwkv_ln_w, rwkv_ln_b, w_out, w_mlp_in, w_mlp_out):
    b, s_len, d = x.shape
    depth = w_ada.shape[0]
    assert b == 1 and depth == 1, "single sample, single layer (context stream is not carried to a next layer)"
    c8 = jnp.zeros((8, d), F32).at[0].set(c[0]).at[1].set(c_ctx)
    out = _layer(x[0], ctx[0], c8, 0, w_ada, b_ada, g_pre_mix, g_post_mix, g_pre_mlp, g_post_mlp, w_in,
                 attn_sink, rwkv_mu_prev, rwkv_mu_next, rwkv_w0, rwkv_w2, rwkv_a0, rwkv_a2, rwkv_g2,
                 rwkv_k_k, rwkv_k_a, rwkv_r_k, rwkv_ln_w, rwkv_ln_b, w_out, w_mlp_in, w_mlp_out)
    return out[None]
```

```python
import functools

import jax
import jax.numpy as jnp
from jax import lax
from jax.experimental import pallas as pl
from jax.experimental.pallas import tpu as pltpu

F32 = jnp.float32
BF16 = jnp.bfloat16

GRID_W = 64
ATT_HEAD_DIM = 128
N_Q_HEADS = 8
N_KV_HEADS = 2
GQA_GROUP = N_Q_HEADS // N_KV_HEADS
ATT_WIDTH = N_Q_HEADS * ATT_HEAD_DIM
KV_WIDTH = N_KV_HEADS * ATT_HEAD_DIM
ATT_IN = ATT_WIDTH + 2 * KV_WIDTH
WINDOW = 128
BLOCK = 128
ROPE_BASE = 10000.0
RWKV_HEAD_DIM = 64
RWKV_WIDTH = 1024
N_RWKV_HEADS = RWKV_WIDTH // RWKV_HEAD_DIM
LORA_W = 64
LORA_A = 64
LORA_G = 128
N_MOD = 6
NORM_EPS = 1e-6
GN_EPS = 64e-5
DECAY_SCALE = 0.6065306597126334

LANES = 128
CHUNK = 64
PAIR = 2 * RWKV_HEAD_DIM
N_PAIRS = RWKV_WIDTH // PAIR
PREP_OUT_DTYPES = (BF16, BF16, BF16, F32, F32, F32, F32, BF16, BF16, BF16, BF16)
CAST_WIDTH = 1024
OUT_SUB_BLOCKS = 4
VMEM_CAP = 56 * 1024 * 1024


def _vmem(nbytes):
    return int(min(VMEM_CAP, max(16 * 1024 * 1024, nbytes)))


def _dot(a, b):
    return jnp.dot(a, b, preferred_element_type=F32)


def _dot_nt(a, b):
    return lax.dot_general(a, b, (((1,), (1,)), ((), ())), preferred_element_type=F32)


def _dot_tn(a, b):
    return lax.dot_general(a, b, (((0,), (0,)), ((), ())), preferred_element_type=F32)


def _bdot(a, b):
    return _dot(a.astype(BF16), b.astype(BF16))


def _bdot_nt(a, b):
    return _dot_nt(a.astype(BF16), b.astype(BF16))


def _bdot_tn(a, b):
    return _dot_tn(a.astype(BF16), b.astype(BF16))


def _split_dot(x, m):
    hi = x.astype(BF16)
    lo = (x - hi.astype(F32)).astype(BF16)
    return _dot(hi, m) + _dot(lo, m)


def _head_sum(x, seg, expand):
    return _split_dot(_split_dot(x, seg), expand)


def _ada_kernel(c_ref, w_ref, b_ref, o_ref):
    c = c_ref[...]
    s = c * jax.nn.sigmoid(c)
    o_ref[...] = _bdot(s, w_ref[...]) + b_ref[...]


def _ada(c8, w_ada, b_ada):
    d, n = w_ada.shape
    tn = 1024
    return pl.pallas_call(
        _ada_kernel,
        grid=(n // tn,),
        in_specs=[pl.BlockSpec((8, d), lambda j: (0, 0)),
                  pl.BlockSpec((d, tn), lambda j: (0, j)),
                  pl.BlockSpec((1, tn), lambda j: (0, j))],
        out_specs=pl.BlockSpec((8, tn), lambda j: (0, j)),
        out_shape=jax.ShapeDtypeStruct((8, n), F32),
        compiler_params=pltpu.CompilerParams(
            dimension_semantics=("arbitrary",),
            vmem_limit_bytes=_vmem(2 * d * tn * 4 + 4 * 1024 * 1024)),
        name="ada",
    )(c8, w_ada, b_ada)


def _norm_mod_kernel(x_ref, c_ref, g_ref, sc_ref, sh_ref, o_ref, *, n_lat_tiles):
    def emit(src_ref):
        x = src_ref[...]
        ms = jnp.mean(x * x, axis=-1, keepdims=True)
        y = x * lax.rsqrt(ms + NORM_EPS) * g_ref[...]
        o_ref[...] = (y * (1.0 + sc_ref[0]) + sh_ref[0]).astype(BF16)

    i = pl.program_id(0)
    pl.when(i < n_lat_tiles)(lambda: emit(x_ref))
    pl.when(i >= n_lat_tiles)(lambda: emit(c_ref))


def _norm_mod(x2, ctx2, gain, sc2, sh2, tm):
    s_len, d = x2.shape
    n_lat_tiles = s_len // tm
    n_tiles = n_lat_tiles + ctx2.shape[0] // tm
    sel = lambda i: (jnp.where(i >= n_lat_tiles, 1, 0), 0, 0)
    return pl.pallas_call(
        functools.partial(_norm_mod_kernel, n_lat_tiles=n_lat_tiles),
        grid=(n_tiles,),
        in_specs=[pl.BlockSpec((tm, d), lambda i: (jnp.minimum(i, n_lat_tiles - 1), 0)),
                  pl.BlockSpec((tm, d), lambda i: (jnp.maximum(i - n_lat_tiles, 0), 0)),
                  pl.BlockSpec((1, d), lambda i: (0, 0)),
                  pl.BlockSpec((1, 1, d), sel),
                  pl.BlockSpec((1, 1, d), sel)],
        out_specs=pl.BlockSpec((tm, d), lambda i: (i, 0)),
        out_shape=jax.ShapeDtypeStruct((n_tiles * tm, d), BF16),
        compiler_params=pltpu.CompilerParams(
            dimension_semantics=("arbitrary",),
            vmem_limit_bytes=_vmem(2 * tm * d * 10 + 4 * 1024 * 1024)),
        name="norm_mod",
    )(x2, ctx2, gain, sc2, sh2)


def _rope_heads(acc, cos, sin, n_heads):
    lane = lax.broadcasted_iota(jnp.int32, (1, ATT_HEAD_DIM), 1)
    first = (lane % 64) < 32
    outs = []
    for h in range(n_heads):
        xh = acc[:, h * ATT_HEAD_DIM:(h + 1) * ATT_HEAD_DIM]
        partner = jnp.where(first, pltpu.roll(xh, 96, 1), pltpu.roll(xh, 32, 1))
        outs.append(xh * cos + partner * sin)
    return outs


def _proj_kernel(a_ref, w_ref, cos_ref, sin_ref, o_ref, *, rope_heads):
    acc = _dot(a_ref[...], w_ref[...])
    if rope_heads:
        roped = _rope_heads(acc, cos_ref[...], sin_ref[...], rope_heads)
        for h, r in enumerate(roped):
            o_ref[:, h * ATT_HEAD_DIM:(h + 1) * ATT_HEAD_DIM] = r
        o_ref[:, rope_heads * ATT_HEAD_DIM:] = acc[:, rope_heads * ATT_HEAD_DIM:]
    else:
        o_ref[...] = acc


def _proj(a, w, cos, sin, tm, tn, rope_heads):
    m, k = a.shape
    n = w.shape[1]
    return pl.pallas_call(
        functools.partial(_proj_kernel, rope_heads=rope_heads),
        grid=(m // tm, n // tn),
        in_specs=[pl.BlockSpec((tm, k), lambda i, j: (i, 0)),
                  pl.BlockSpec((k, tn), lambda i, j: (0, j)),
                  pl.BlockSpec((tm, ATT_HEAD_DIM), lambda i, j: (i, 0)),
                  pl.BlockSpec((tm, ATT_HEAD_DIM), lambda i, j: (i, 0))],
        out_specs=pl.BlockSpec((tm, tn), lambda i, j: (i, j)),
        out_shape=jax.ShapeDtypeStruct((m, n), F32),
        compiler_params=pltpu.CompilerParams(
            dimension_semantics=("arbitrary", "arbitrary"),
            vmem_limit_bytes=_vmem(2 * (tm * k * 2 + k * tn * 2 + tm * tn * 4 + 2 * tm * 512)
                                   + tm * tn * 8 + 4 * 1024 * 1024)),
        name="proj_qkv" if rope_heads else "proj_slab",
    )(a, w, cos, sin)


def _attn_kernel(sink_ref, q_ref, kp_ref, kc_ref, kn_ref, vp_ref, vc_ref, vn_ref,
                 kx_ref, vx_ref, o_ref, *, nb, n_ctx):
    i = pl.program_id(0)
    scale = ATT_HEAD_DIM ** -0.5
    rows = GQA_GROUP * BLOCK
    span = 3 * BLOCK
    ncol = span + n_ctx
    row = lax.broadcasted_iota(jnp.int32, (rows, ncol), 0) % BLOCK
    col = lax.broadcasted_iota(jnp.int32, (rows, ncol), 1)
    rel = col - BLOCK - row
    lo = jnp.where(i > 0, 0, BLOCK)
    hi = jnp.where(i < nb - 1, span, 2 * BLOCK)
    band = (jnp.abs(rel) <= WINDOW) & (col >= lo) & (col < hi)
    valid = band | (col >= span)
    heads = range(N_KV_HEADS)
    cols = [slice(hk * ATT_HEAD_DIM, (hk + 1) * ATT_HEAD_DIM) for hk in heads]
    q = [(jnp.concatenate(
        [q_ref[:, (hk * GQA_GROUP + g) * ATT_HEAD_DIM:(hk * GQA_GROUP + g + 1) * ATT_HEAD_DIM]
         for g in range(GQA_GROUP)], axis=0) * scale).astype(BF16) for hk in heads]
    k_all = [jnp.concatenate([kp_ref[:, ks], kc_ref[:, ks], kn_ref[:, ks], kx_ref[:, ks]], axis=0).astype(BF16)
             for ks in cols]
    v_all = [jnp.concatenate([vp_ref[:, ks], vc_ref[:, ks], vn_ref[:, ks], vx_ref[:, ks]], axis=0).astype(BF16)
             for ks in cols]
    sink = [jnp.concatenate([jnp.full((BLOCK, 1), sink_ref[hk * GQA_GROUP + g], F32) for g in range(GQA_GROUP)],
                            axis=0) for hk in heads]
    s = [jnp.where(valid, _dot_nt(qh, kh), -jnp.inf) for qh, kh in zip(q, k_all)]
    m = [jnp.maximum(jnp.max(sh, axis=-1, keepdims=True), sk) for sh, sk in zip(s, sink)]
    e = [jnp.exp(sh - mh) for sh, mh in zip(s, m)]
    denom = [jnp.sum(eh, axis=-1, keepdims=True) + jnp.exp(sk - mh) for eh, sk, mh in zip(e, sink, m)]
    o = [_dot(eh.astype(BF16), vh) * (1.0 / dh) for eh, vh, dh in zip(e, v_all, denom)]
    for hk in heads:
        for g in range(GQA_GROUP):
            h = hk * GQA_GROUP + g
            o_ref[:, h * ATT_HEAD_DIM:(h + 1) * ATT_HEAD_DIM] = o[hk][g * BLOCK:(g + 1) * BLOCK].astype(BF16)


def _attention(qkv, sink, s_len, n_ctx):
    nb = s_len // BLOCK
    kcol = ATT_WIDTH // KV_WIDTH
    vcol = kcol + 1
    ctx_blk = s_len // n_ctx
    prev = lambda i: jnp.maximum(i - 1, 0)
    nxt = lambda i: jnp.minimum(i + 1, nb - 1)
    kv = lambda rowf, c: pl.BlockSpec((BLOCK, KV_WIDTH), lambda i: (rowf(i), c))
    same = lambda i: i
    return pl.pallas_call(
        functools.partial(_attn_kernel, nb=nb, n_ctx=n_ctx),
        grid=(nb,),
        in_specs=[pl.BlockSpec(memory_space=pltpu.SMEM),
                  pl.BlockSpec((BLOCK, ATT_WIDTH), lambda i: (i, 0)),
                  kv(prev, kcol), kv(same, kcol), kv(nxt, kcol),
                  kv(prev, vcol), kv(same, vcol), kv(nxt, vcol),
                  pl.BlockSpec((n_ctx, KV_WIDTH), lambda i: (ctx_blk, kcol)),
                  pl.BlockSpec((n_ctx, KV_WIDTH), lambda i: (ctx_blk, vcol))],
        out_specs=pl.BlockSpec((BLOCK, ATT_WIDTH), lambda i: (i, 0)),
        out_shape=jax.ShapeDtypeStruct((s_len, ATT_WIDTH), BF16),
        compiler_params=pltpu.CompilerParams(
            dimension_semantics=("arbitrary",),
            vmem_limit_bytes=_vmem(32 * 1024 * 1024)),
        name="attention",
    )(sink, qkv, qkv, qkv, qkv, qkv, qkv, qkv, qkv, qkv)


def _prep_kernel(x_ref, xp_ref, xn_ref, mup_ref, mun_ref, w2_ref, w0_ref, a2_ref, a0_ref, g2_ref,
                 kk_ref, ka_ref, rk_ref, seg_ref, exp_ref,
                 r_o, v_o, kk_o, g_o, bon_o, lwf_o, lwb_o, kdf_o, kdb_o, bf_o, bb_o,
                 *, tm, start_tiles, end_tiles):
    i = pl.program_id(0)
    r_w = RWKV_WIDTH
    x = x_ref[...]
    is_start = functools.reduce(jnp.logical_or, [i == t for t in start_tiles])
    is_end = functools.reduce(jnp.logical_or, [i == t for t in end_tiles])
    keep_prev = jnp.full((1, 1), jnp.where(is_start, 0, 1), jnp.int32) == 1
    keep_next = jnp.full((1, 1), jnp.where(is_end, 0, 1), jnp.int32) == 1
    prev_row = jnp.where(keep_prev, xp_ref[7:8, :], 0.0)
    next_row = jnp.where(keep_next, xn_ref[0:1, :], 0.0)
    xp = pltpu.roll(x, 1, 0)
    xn = pltpu.roll(x, tm - 1, 0)
    row8 = lax.broadcasted_iota(jnp.int32, (8, 1), 0)
    xp = jnp.concatenate([jnp.where(row8 == 0, prev_row, xp[:8]), xp[8:]], axis=0)
    xn = jnp.concatenate([xn[:tm - 8], jnp.where(row8 == 7, next_row, xn[tm - 8:])], axis=0)
    mup = mup_ref[...]
    mun = mun_ref[...]
    s = x * (1.0 - mup - mun) + xp * mup + xn * mun

    r = s[:, :r_w]
    k = s[:, r_w:2 * r_w]
    v = s[:, 2 * r_w:3 * r_w]
    o1 = 3 * r_w
    o2 = o1 + 2 * LORA_W
    o3 = o2 + 2 * LORA_A
    w_lo = s[:, o1:o2]
    a_lo = s[:, o2:o3]
    g_lo = s[:, o3:]

    zw = w0_ref[...] + _bdot(jnp.tanh(w_lo), w2_ref[...])
    lw = -DECAY_SCALE * jax.nn.sigmoid(zw)
    a = jax.nn.sigmoid(a0_ref[...] + _bdot(a_lo, a2_ref[...]))
    g = _bdot(jax.nn.sigmoid(g_lo), g2_ref[...])

    kkf = k * kk_ref[...]
    ssq = _head_sum(kkf * kkf, seg_ref[...], exp_ref[...])
    kk = kkf * jnp.minimum(lax.rsqrt(ssq), 1e12)
    ka = ka_ref[...]
    kd_f = k * (1.0 + (a[:, :r_w] - 1.0) * ka)
    kd_b = k * (1.0 + (a[:, r_w:] - 1.0) * ka)
    k_bonus = 0.5 * (kd_f + kd_b)
    bonus = _head_sum(r * k_bonus * rk_ref[...], seg_ref[...], exp_ref[...]) * v

    r_o[...] = r.astype(r_o.dtype)
    v_o[...] = v.astype(v_o.dtype)
    kk_o[...] = kk.astype(kk_o.dtype)
    g_o[...] = g
    bon_o[...] = bonus
    lwf_o[...] = lw[:, :r_w]
    lwb_o[...] = lw[:, r_w:]
    kdf_o[...] = kd_f.astype(kdf_o.dtype)
    kdb_o[...] = kd_b.astype(kdb_o.dtype)
    bf_o[...] = (kk * a[:, :r_w]).astype(bf_o.dtype)
    bb_o[...] = (kk * a[:, r_w:]).astype(bb_o.dtype)


def _rwkv_prep(slab, mup, mun, w2blk, w0cat, a2blk, a0cat, g2, k_k, k_a, r_k, seg, expand, s_len, n_ctx, tm):
    m, width = slab.shape
    nt = m // tm
    hb = tm // 8
    nblk8 = m // 8
    start_tiles = (0, s_len // tm)
    end_tiles = (s_len // tm - 1, nt - 1)
    full = lambda arr: pl.BlockSpec(arr.shape, lambda i: (0,) * arr.ndim)
    consts = (mup, mun, w2blk, w0cat, a2blk, a0cat, g2, k_k, k_a, r_k, seg, expand)
    out_spec = pl.BlockSpec((tm, RWKV_WIDTH), lambda i: (i, 0))
    n_out = len(PREP_OUT_DTYPES)
    return pl.pallas_call(
        functools.partial(_prep_kernel, tm=tm, start_tiles=start_tiles, end_tiles=end_tiles),
        grid=(nt,),
        in_specs=[pl.BlockSpec((tm, width), lambda i: (i, 0)),
                  pl.BlockSpec((8, width), lambda i: (jnp.maximum(i * hb - 1, 0), 0)),
                  pl.BlockSpec((8, width), lambda i: (jnp.minimum((i + 1) * hb, nblk8 - 1), 0))]
                 + [full(c) for c in consts],
        out_specs=[out_spec] * n_out,
        out_shape=[jax.ShapeDtypeStruct((m, RWKV_WIDTH), dt) for dt in PREP_OUT_DTYPES],
        compiler_params=pltpu.CompilerParams(
            dimension_semantics=("arbitrary",),
            vmem_limit_bytes=_vmem(2 * tm * width * 4 + 2 * n_out * tm * RWKV_WIDTH * 4
                                   + 12 * tm * RWKV_WIDTH * 4 + 8 * 1024 * 1024)),
        name="rwkv_prep",
    )(slab, slab, slab, *consts)


def _stack(x, m0):
    return jnp.concatenate([jnp.where(m0, x, 0.0), jnp.where(m0, 0.0, x)], axis=0)


def _scan_streams(r_ref, v_ref, kk_ref, lw_ref, kd_ref, b_ref, reverse):
    L = CHUNK
    lw = lw_ref[...]
    ti = lax.broadcasted_iota(jnp.int32, (L, L), 0)
    tj = lax.broadcasted_iota(jnp.int32, (L, L), 1)
    tri = jnp.where((ti <= tj) if reverse else (ti >= tj), 1.0, 0.0).astype(BF16)
    lw_hi = lw.astype(BF16)
    lw_lo = (lw - lw_hi.astype(F32)).astype(BF16)
    c = _dot(tri, lw_hi) + _dot(tri, lw_lo)
    p_cum = jnp.exp(c)
    p_inv = jnp.exp(-c)
    p_prev = jnp.exp(c - lw)
    last = 0 if reverse else L - 1
    p_end = p_cum[last:last + 1, :]
    b_t = b_ref[...] * p_inv
    k_t = kd_ref[...] * p_inv
    gi = lax.broadcasted_iota(jnp.int32, (2 * L, 2 * L), 0)
    gj = lax.broadcasted_iota(jnp.int32, (2 * L, 2 * L), 1)
    same = (gi // L) == (gj // L)
    strict = same & ((gi < gj) if reverse else (gi > gj))
    incl = same & ((gi <= gj) if reverse else (gi >= gj))
    return dict(a=-kk_ref[...] * p_prev, r=r_ref[...] * p_cum, b=b_t, k=k_t, bh=b_t * p_end, kh=k_t * p_end,
                v=v_ref[...], p_end=p_end, strict=strict, incl=incl)


def _scan_kernel(rf, vf, kkf, lwf, kdf, bf, rb, vb, kkb, lwb, kdb, bb, *rest, n_cast):
    cast_in, (yf, yb), cast_out, s_ref = rest[:n_cast], rest[n_cast:n_cast + 2], rest[n_cast + 2:-1], rest[-1]

    @pl.when(pl.program_id(0) == 0)
    def _():
        s_ref[...] = jnp.zeros_like(s_ref)

    for src, dst in zip(cast_in, cast_out):
        dst[...] = src[...].astype(dst.dtype)

    L = CHUNK
    streams = (_scan_streams(rf, vf, kkf, lwf, kdf, bf, False), _scan_streams(rb, vb, kkb, lwb, kdb, bb, True))
    y_refs = (yf, yb)
    m0 = lax.broadcasted_iota(jnp.int32, (L, PAIR), 1) < RWKV_HEAD_DIM
    chains = [(d, p) for d in range(2) for p in range(N_PAIRS)]

    def stacked(name, d, p):
        return _stack(streams[d][name][:, p * PAIR:(p + 1) * PAIR], m0)

    ar = [jnp.concatenate([stacked("a", d, p), stacked("r", d, p)], axis=0).astype(BF16) for d, p in chains]
    bk = [jnp.concatenate([stacked("b", d, p), stacked("k", d, p)], axis=0).astype(BF16) for d, p in chains]
    v_s = [stacked("v", d, p).astype(BF16) for d, p in chains]
    state = [s_ref[d, p] for d, p in chains]
    g = [_dot_nt(x, z) for x, z in zip(ar, bk)]
    xs = [_dot_nt(x, s.astype(BF16)) for x, s in zip(ar, state)]
    a_ab, a_kv, a_rb = [], [], []
    for (d, _), gc in zip(chains, g):
        strict, incl = streams[d]["strict"], streams[d]["incl"]
        a_ab.append(jnp.where(strict, gc[:2 * L, :2 * L], 0.0).astype(BF16))
        a_kv.append(jnp.concatenate([jnp.where(strict, gc[:2 * L, 2 * L:], 0.0),
                                     jnp.where(incl, gc[2 * L:, 2 * L:], 0.0)], axis=0).astype(BF16))
        a_rb.append(jnp.where(incl, gc[2 * L:, :2 * L], 0.0).astype(BF16))
    av = [_dot(a, v) for a, v in zip(a_kv, v_s)]
    w = [x[:2 * L] + y[:2 * L] for x, y in zip(xs, av)]
    am = a_ab
    w = [wc + _dot(a, wc.astype(BF16)) for a, wc in zip(am, w)]
    for _ in range(5):
        am = [_dot(a, a).astype(BF16) for a in am]
        w = [wc + _dot(a, wc.astype(BF16)) for a, wc in zip(am, w)]
    sa = [wc.astype(BF16) for wc in w]
    ys = [x[2 * L:] + y[2 * L:] + _dot(a, s) for x, y, a, s in zip(xs, av, a_rb, sa)]
    for (d, p), yc in zip(chains, ys):
        y_refs[d][:, p * PAIR:(p + 1) * PAIR] = yc[:L] + yc[L:]
    for (d, p), s_old, sac, vc in zip(chains, state, sa, v_s):
        bhkh = jnp.concatenate([stacked("bh", d, p), stacked("kh", d, p)], axis=0).astype(BF16)
        upd = _dot_tn(jnp.concatenate([sac, vc], axis=0), bhkh)
        s_ref[d, p] = s_old * streams[d]["p_end"][:, p * PAIR:(p + 1) * PAIR] + upd


def _cast_slices(w, n_steps):
    flat = w.reshape(-1, CAST_WIDTH)
    n_blk = 1
    while n_blk * 2 <= n_steps and flat.shape[0] % (n_blk * 2) == 0 and (flat.shape[0] // (n_blk * 2)) % 16 == 0:
        n_blk *= 2
    rows = flat.shape[0] // n_blk
    return flat, pl.BlockSpec((rows, CAST_WIDTH), lambda i: (jnp.minimum(i, n_blk - 1), 0))


def _rwkv_scan(r, v, kk, lwf, lwb, kdf, kdb, bf, bb, s_len, n_ctx, cast_weights):
    m = r.shape[0]
    nc = m // CHUNK
    nc_lat = s_len // CHUNK
    nc_ctx = n_ctx // CHUNK
    fwd = lambda i: (jnp.where(i < nc_ctx, nc_lat + i, i - nc_ctx), 0)
    bwd = lambda i: (nc - 1 - i, 0)
    fs = pl.BlockSpec((CHUNK, RWKV_WIDTH), fwd)
    bs = pl.BlockSpec((CHUNK, RWKV_WIDTH), bwd)
    flats, cast_specs = zip(*[_cast_slices(w, nc) for w in cast_weights])
    cast_bytes = sum(spec.block_shape[0] * CAST_WIDTH * 6 for spec in cast_specs)
    outs = pl.pallas_call(
        functools.partial(_scan_kernel, n_cast=len(flats)),
        grid=(nc,),
        in_specs=[fs] * 6 + [bs] * 6 + list(cast_specs),
        out_specs=[fs, bs] + list(cast_specs),
        out_shape=[jax.ShapeDtypeStruct((m, RWKV_WIDTH), F32)] * 2
                  + [jax.ShapeDtypeStruct(f.shape, BF16) for f in flats],
        scratch_shapes=[pltpu.VMEM((2, N_PAIRS, PAIR, PAIR), F32)],
        compiler_params=pltpu.CompilerParams(
            dimension_semantics=("arbitrary",),
            vmem_limit_bytes=_vmem(2 * 14 * CHUNK * RWKV_WIDTH * 4 + 2 * N_PAIRS * PAIR * PAIR * 4
                                   + 2 * cast_bytes + 24 * 1024 * 1024)),
        name="rwkv_scan",
    )(r, v, kk, lwf, kdf, bf, r, v, kk, lwb, kdb, bb, *flats)
    return outs[0], outs[1], [o.reshape(w.shape) for o, w in zip(outs[2:], cast_weights)]


def _out_kernel(att_ref, yf_ref, yb_ref, g_ref, bon_ref, x_ref, wa_ref, wr_ref, lnw_ref, lnb_ref,
                seg_ref, exp_ref, gpm_ref, gt1_ref, gmlp_ref, sc2_ref, sh2_ref, x1_ref, h2_ref):
    mix_att = _dot(att_ref[...], wa_ref[...])
    inv_n = 1.0 / RWKV_HEAD_DIM
    seg, expand = seg_ref[...], exp_ref[...]
    rows = att_ref.shape[0] // OUT_SUB_BLOCKS
    blocks = [pl.ds(n * rows, rows) for n in range(OUT_SUB_BLOCKS)]
    y = [yf_ref[b, :] + yb_ref[b, :] for b in blocks]
    s1 = [_bdot(v, seg) for v in y]
    yc = [v - _split_dot(s, expand) * inv_n for v, s in zip(y, s1)]
    s2 = [_bdot(c * c, seg) for c in yc]
    var = [_bdot(s, expand) * inv_n for s in s2]
    rec = [((c * lax.rsqrt(vr + GN_EPS) * lnw_ref[...] + lnb_ref[...]) + bon_ref[b, :]) * g_ref[b, :]
           for c, vr, b in zip(yc, var, blocks)]
    mix = mix_att + _bdot(jnp.concatenate(rec, axis=0), wr_ref[...])
    ms = jnp.mean(mix * mix, axis=-1, keepdims=True)
    x1 = x_ref[...] + gt1_ref[...] * (mix * lax.rsqrt(ms + NORM_EPS) * gpm_ref[...])
    x1_ref[...] = x1
    ms2 = jnp.mean(x1 * x1, axis=-1, keepdims=True)
    h2 = x1 * lax.rsqrt(ms2 + NORM_EPS) * gmlp_ref[...]
    h2_ref[...] = (h2 * (1.0 + sc2_ref[...]) + sh2_ref[...]).astype(BF16)


def _out_proj(att, yf, yb, g, bonus, x, w_out, lnw, lnb, seg, expand, gpm, gt1, gmlp, sc2, sh2, tm):
    s_len, d = x.shape
    assert ATT_WIDTH == RWKV_WIDTH and w_out.shape == (ATT_WIDTH + RWKV_WIDTH, d)
    row = lambda w: pl.BlockSpec((tm, w), lambda i: (i, 0))
    full = lambda arr: pl.BlockSpec(arr.shape, lambda i: (0,) * arr.ndim)
    w_half = lambda n: pl.BlockSpec((ATT_WIDTH, d), lambda i: (n, 0))
    consts = (lnw, lnb, seg, expand, gpm, gt1, gmlp, sc2, sh2)
    return pl.pallas_call(
        _out_kernel,
        grid=(s_len // tm,),
        in_specs=[row(ATT_WIDTH)] + [row(RWKV_WIDTH)] * 4 + [row(d), w_half(0), w_half(1)]
                 + [full(c) for c in consts],
        out_specs=[row(d), row(d)],
        out_shape=[jax.ShapeDtypeStruct((s_len, d), F32), jax.ShapeDtypeStruct((s_len, d), BF16)],
        compiler_params=pltpu.CompilerParams(
            dimension_semantics=("arbitrary",),
            vmem_limit_bytes=_vmem(2 * tm * (ATT_WIDTH * 2 + 4 * RWKV_WIDTH * 4 + d * 4 + d * 4 + d * 2)
                                   + 2 * 2 * d * d + 8 * tm * d * 4 + 4 * 1024 * 1024)),
        name="out_proj",
    )(att, yf, yb, g, bonus, x, w_out, w_out, *consts)


def _mlp_kernel(h_ref, w1_ref, w2_ref, x1_ref, g_ref, gt_ref, o_ref, acc_ref):
    f = pl.program_id(1)

    @pl.when(f == 0)
    def _():
        acc_ref[...] = jnp.zeros_like(acc_ref)

    u = jnp.maximum(_dot(h_ref[...], w1_ref[...]), 0.0)
    acc_ref[...] += _dot((u * u).astype(BF16), w2_ref[...])

    @pl.when(f == pl.num_programs(1) - 1)
    def _():
        ff = acc_ref[...]
        ms = jnp.mean(ff * ff, axis=-1, keepdims=True)
        o_ref[...] = x1_ref[...] + gt_ref[...] * (ff * lax.rsqrt(ms + NORM_EPS) * g_ref[...])


def _mlp(h2, w1, w2, x1, gain, gt2, tm, tf):
    s_len, d = x1.shape
    dff = w1.shape[1]
    return pl.pallas_call(
        _mlp_kernel,
        grid=(s_len // tm, dff // tf),
        in_specs=[pl.BlockSpec((tm, d), lambda i, f: (i, 0)),
                  pl.BlockSpec((d, tf), lambda i, f: (0, f)),
                  pl.BlockSpec((tf, d), lambda i, f: (f, 0)),
                  pl.BlockSpec((tm, d), lambda i, f: (i, 0)),
                  pl.BlockSpec((1, d), lambda i, f: (0, 0)),
                  pl.BlockSpec((1, d), lambda i, f: (0, 0))],
        out_specs=pl.BlockSpec((tm, d), lambda i, f: (i, 0)),
        out_shape=jax.ShapeDtypeStruct((s_len, d), F32),
        scratch_shapes=[pltpu.VMEM((tm, d), F32)],
        compiler_params=pltpu.CompilerParams(
            dimension_semantics=("arbitrary", "arbitrary"),
            vmem_limit_bytes=_vmem(2 * (tm * d * 2 + 2 * d * tf * 2 + 2 * tm * d * 4)
                                   + tm * d * 4 + 3 * tm * tf * 4 + 4 * 1024 * 1024)),
        name="mlp",
    )(h2, w1, w2, x1, gain, gt2)


def _rope_tables(s_len, n_ctx):
    rows = s_len // GRID_W
    n_freq = ATT_HEAD_DIM // 4
    inv_freq = ROPE_BASE ** (-jnp.arange(n_freq, dtype=F32) / n_freq)
    ang_r = jnp.arange(rows, dtype=F32)[:, None] * inv_freq[None, :]
    ang_c = jnp.arange(GRID_W, dtype=F32)[:, None] * inv_freq[None, :]
    expand_r = lambda t: jnp.repeat(t, GRID_W, axis=0)
    expand_c = lambda t: jnp.tile(t, (rows, 1))
    cos_r, sin_r = expand_r(jnp.cos(ang_r)), expand_r(jnp.sin(ang_r))
    cos_c, sin_c = expand_c(jnp.cos(ang_c)), expand_c(jnp.sin(ang_c))
    cos = jnp.concatenate([cos_r, cos_r, cos_c, cos_c], axis=-1)
    sin = jnp.concatenate([-sin_r, sin_r, -sin_c, sin_c], axis=-1)
    cos = jnp.concatenate([cos, jnp.ones((n_ctx, ATT_HEAD_DIM), F32)], axis=0)
    sin = jnp.concatenate([sin, jnp.zeros((n_ctx, ATT_HEAD_DIM), F32)], axis=0)
    return cos, sin


def _block_diag2(w):
    _, k, n = w.shape
    z = jnp.zeros((k, n), w.dtype)
    return jnp.concatenate([jnp.concatenate([w[0], z], axis=1), jnp.concatenate([z, w[1]], axis=1)], axis=0)


def _layer(x2, ctx2, c8, l, w_ada, b_ada, g_pre_mix, g_post_mix, g_pre_mlp, g_post_mlp, w_in, attn_sink,
           rwkv_mu_prev, rwkv_mu_next, rwkv_w0, rwkv_w2, rwkv_a0, rwkv_a2, rwkv_g2, rwkv_k_k, rwkv_k_a,
           rwkv_r_k, rwkv_ln_w, rwkv_ln_b, w_out, w_mlp_in, w_mlp_out):
    s_len, d = x2.shape
    n_ctx = ctx2.shape[0]
    row = lambda v: v.reshape(1, -1)

    mod = _ada(c8, w_ada[l], row(b_ada[l]))
    sh1, sc1, gt1, sh2, sc2, gt2 = [mod[:2, j * d:(j + 1) * d] for j in range(N_MOD)]

    tm_rows = 256
    h = _norm_mod(x2, ctx2, row(g_pre_mix[l]), sc1.reshape(2, 1, d), sh1.reshape(2, 1, d), tm_rows)

    cos, sin = _rope_tables(s_len, n_ctx)
    m_all = s_len + n_ctx
    tm_proj = 768 if m_all % 768 == 0 else 256
    qkv = _proj(h, w_in[l][:, :ATT_IN].astype(BF16), cos, sin, tm_proj, ATT_IN, N_Q_HEADS + N_KV_HEADS)
    rwkv_in = w_in.shape[2] - ATT_IN
    slab = _proj(h, w_in[l][:, ATT_IN:].astype(BF16), cos, sin, tm_proj, rwkv_in // 3, 0)

    att = _attention(qkv, attn_sink[l], s_len, n_ctx)

    head = jnp.arange(RWKV_WIDTH) // RWKV_HEAD_DIM
    seg = (head[:, None] == jnp.arange(LANES)[None, :]).astype(BF16)
    expand = seg.T
    streams = _rwkv_prep(
        slab, row(rwkv_mu_prev[l]), row(rwkv_mu_next[l]),
        _block_diag2(rwkv_w2[l]).astype(BF16), rwkv_w0[l].reshape(1, -1),
        _block_diag2(rwkv_a2[l]).astype(BF16), rwkv_a0[l].reshape(1, -1),
        rwkv_g2[l].astype(BF16), row(rwkv_k_k[l]), row(rwkv_k_a[l]), rwkv_r_k[l].reshape(1, -1),
        seg, expand, s_len, n_ctx, tm_rows)
    r, v, kk, g, bonus, lwf, lwb, kdf, kdb, bf, bb = streams
    yf, yb, (w_out_b, w_mlp_in_b, w_mlp_out_b) = _rwkv_scan(
        r, v, kk, lwf, lwb, kdf, kdb, bf, bb, s_len, n_ctx, (w_out[l], w_mlp_in[l], w_mlp_out[l]))

    x1, h2 = _out_proj(att, yf, yb, g, bonus, x2, w_out_b,
                       row(rwkv_ln_w[l]), row(rwkv_ln_b[l]), seg, expand,
                       row(g_post_mix[l]), gt1[0:1], row(g_pre_mlp[l]), sc2[0:1], sh2[0:1], 512)
    return _mlp(h2, w_mlp_in_b, w_mlp_out_b, x1, row(g_post_mlp[l]), gt2[0:1], 512, 1024)


def kernel(x, c, ctx, c_ctx, w_ada, b_ada, g_pre_mix, g_post_mix, g_pre_mlp, g_post_mlp, w_in, attn_sink,
           rwkv_mu_prev, rwkv_mu_next, rwkv_w0, rwkv_w2, rwkv_a0, rwkv_a2, rwkv_g2, rwkv_k_k, rwkv_k_a,
           rwkv_r_k, rwkv_ln_w, rwkv_ln_b, w_out, w_mlp_in, w_mlp_out):
    b, s_len, d = x.shape
    depth = w_ada.shape[0]
    assert b == 1 and depth == 1, "single sample, single layer (context stream is not carried to a next layer)"
    c8 = jnp.zeros((8, d), F32).at[0].set(c[0]).at[1].set(c_ctx)
    out = _layer(x[0], ctx[0], c8, 0, w_ada, b_ada, g_pre_mix, g_post_mix, g_pre_mlp, g_post_mlp, w_in,
                 attn_sink, rwkv_mu_prev, rwkv_mu_next, rwkv_w0, rwkv_w2, rwkv_a0, rwkv_a2, rwkv_g2,
                 rwkv_k_k, rwkv_k_a, rwkv_r_k, rwkv_ln_w, rwkv_ln_b, w_out, w_mlp_in, w_mlp_out)
    return out[None]
```

```python
import functools

import jax
import jax.numpy as jnp
from jax import lax
from jax.experimental import pallas as pl
from jax.experimental.pallas import tpu as pltpu

F32 = jnp.float32
BF16 = jnp.bfloat16

GRID_W = 64
ATT_HEAD_DIM = 128
N_Q_HEADS = 8
N_KV_HEADS = 2
GQA_GROUP = N_Q_HEADS // N_KV_HEADS
ATT_WIDTH = N_Q_HEADS * ATT_HEAD_DIM
KV_WIDTH = N_KV_HEADS * ATT_HEAD_DIM
ATT_IN = ATT_WIDTH + 2 * KV_WIDTH
WINDOW = 128
BLOCK = 128
ROPE_BASE = 10000.0
RWKV_HEAD_DIM = 64
RWKV_WIDTH = 1024
N_RWKV_HEADS = RWKV_WIDTH // RWKV_HEAD_DIM
LORA_W = 64
LORA_A = 64
LORA_G = 128
N_MOD = 6
NORM_EPS = 1e-6
GN_EPS = 64e-5
DECAY_SCALE = 0.6065306597126334

LANES = 128
CHUNK = 64
PAIR = 2 * RWKV_HEAD_DIM
N_PAIRS = RWKV_WIDTH // PAIR
PREP_OUT_DTYPES = (BF16, BF16, BF16, F32, F32, F32, F32, BF16, BF16, BF16, BF16)
OUT_SUB_BLOCKS = 4
VMEM_CAP = 56 * 1024 * 1024


def _vmem(nbytes):
    return int(min(VMEM_CAP, max(16 * 1024 * 1024, nbytes)))


def _dot(a, b):
    return jnp.dot(a, b, preferred_element_type=F32)


def _dot_nt(a, b):
    return lax.dot_general(a, b, (((1,), (1,)), ((), ())), preferred_element_type=F32)


def _dot_tn(a, b):
    return lax.dot_general(a, b, (((0,), (0,)), ((), ())), preferred_element_type=F32)


def _bdot(a, b):
    return _dot(a.astype(BF16), b.astype(BF16))


def _bdot_nt(a, b):
    return _dot_nt(a.astype(BF16), b.astype(BF16))


def _bdot_tn(a, b):
    return _dot_tn(a.astype(BF16), b.astype(BF16))


def _split_dot(x, m):
    hi = x.astype(BF16)
    lo = (x - hi.astype(F32)).astype(BF16)
    return _dot(hi, m) + _dot(lo, m)


def _head_sum(x, seg, expand):
    return _split_dot(_split_dot(x, seg), expand)


def _ada_kernel(c_ref, w_ref, b_ref, o_ref):
    c = c_ref[...]
    s = c * jax.nn.sigmoid(c)
    o_ref[...] = _bdot(s, w_ref[...]) + b_ref[...]


def _ada(c8, w_ada, b_ada):
    d, n = w_ada.shape
    tn = 1024
    return pl.pallas_call(
        _ada_kernel,
        grid=(n // tn,),
        in_specs=[pl.BlockSpec((8, d), lambda j: (0, 0)),
                  pl.BlockSpec((d, tn), lambda j: (0, j)),
                  pl.BlockSpec((1, tn), lambda j: (0, j))],
        out_specs=pl.BlockSpec((8, tn), lambda j: (0, j)),
        out_shape=jax.ShapeDtypeStruct((8, n), F32),
        compiler_params=pltpu.CompilerParams(
            dimension_semantics=("arbitrary",),
            vmem_limit_bytes=_vmem(2 * d * tn * 4 + 4 * 1024 * 1024)),
        name="ada",
    )(c8, w_ada, b_ada)


def _norm_mod_kernel(x_ref, c_ref, g_ref, sc_ref, sh_ref, o_ref, *, n_lat_tiles):
    def emit(src_ref):
        x = src_ref[...]
        ms = jnp.mean(x * x, axis=-1, keepdims=True)
        y = x * lax.rsqrt(ms + NORM_EPS) * g_ref[...]
        o_ref[...] = (y * (1.0 + sc_ref[0]) + sh_ref[0]).astype(BF16)

    i = pl.program_id(0)
    pl.when(i < n_lat_tiles)(lambda: emit(x_ref))
    pl.when(i >= n_lat_tiles)(lambda: emit(c_ref))


def _norm_mod(x2, ctx2, gain, sc2, sh2, tm):
    s_len, d = x2.shape
    n_lat_tiles = s_len // tm
    n_tiles = n_lat_tiles + ctx2.shape[0] // tm
    sel = lambda i: (jnp.where(i >= n_lat_tiles, 1, 0), 0, 0)
    return pl.pallas_call(
        functools.partial(_norm_mod_kernel, n_lat_tiles=n_lat_tiles),
        grid=(n_tiles,),
        in_specs=[pl.BlockSpec((tm, d), lambda i: (jnp.minimum(i, n_lat_tiles - 1), 0)),
                  pl.BlockSpec((tm, d), lambda i: (jnp.maximum(i - n_lat_tiles, 0), 0)),
                  pl.BlockSpec((1, d), lambda i: (0, 0)),
                  pl.BlockSpec((1, 1, d), sel),
                  pl.BlockSpec((1, 1, d), sel)],
        out_specs=pl.BlockSpec((tm, d), lambda i: (i, 0)),
        out_shape=jax.ShapeDtypeStruct((n_tiles * tm, d), BF16),
        compiler_params=pltpu.CompilerParams(
            dimension_semantics=("arbitrary",),
            vmem_limit_bytes=_vmem(2 * tm * d * 10 + 4 * 1024 * 1024)),
        name="norm_mod",
    )(x2, ctx2, gain, sc2, sh2)


def _rope_heads(acc, cos, sin, n_heads):
    lane = lax.broadcasted_iota(jnp.int32, (1, ATT_HEAD_DIM), 1)
    first = (lane % 64) < 32
    outs = []
    for h in range(n_heads):
        xh = acc[:, h * ATT_HEAD_DIM:(h + 1) * ATT_HEAD_DIM]
        partner = jnp.where(first, pltpu.roll(xh, 96, 1), pltpu.roll(xh, 32, 1))
        outs.append(xh * cos + partner * sin)
    return outs


def _proj_kernel(a_ref, w_ref, cos_ref, sin_ref, o_ref, *, rope_heads):
    acc = _dot(a_ref[...], w_ref[...])
    if rope_heads:
        roped = _rope_heads(acc, cos_ref[...], sin_ref[...], rope_heads)
        for h, r in enumerate(roped):
            o_ref[:, h * ATT_HEAD_DIM:(h + 1) * ATT_HEAD_DIM] = r
        o_ref[:, rope_heads * ATT_HEAD_DIM:] = acc[:, rope_heads * ATT_HEAD_DIM:]
    else:
        o_ref[...] = acc


def _proj(a, w, cos, sin, tm, tn, rope_heads):
    m, k = a.shape
    n = w.shape[1]
    return pl.pallas_call(
        functools.partial(_proj_kernel, rope_heads=rope_heads),
        grid=(m // tm, n // tn),
        in_specs=[pl.BlockSpec((tm, k), lambda i, j: (i, 0)),
                  pl.BlockSpec((k, tn), lambda i, j: (0, j)),
                  pl.BlockSpec((tm, ATT_HEAD_DIM), lambda i, j: (i, 0)),
                  pl.BlockSpec((tm, ATT_HEAD_DIM), lambda i, j: (i, 0))],
        out_specs=pl.BlockSpec((tm, tn), lambda i, j: (i, j)),
        out_shape=jax.ShapeDtypeStruct((m, n), F32),
        compiler_params=pltpu.CompilerParams(
            dimension_semantics=("arbitrary", "arbitrary"),
            vmem_limit_bytes=_vmem(2 * (tm * k * 2 + k * tn * 2 + tm * tn * 4 + 2 * tm * 512)
                                   + tm * tn * 8 + 4 * 1024 * 1024)),
        name="proj_qkv" if rope_heads else "proj_slab",
    )(a, w, cos, sin)


def _attn_kernel(sink_ref, q_ref, kp_ref, kc_ref, kn_ref, vp_ref, vc_ref, vn_ref,
                 kx_ref, vx_ref, o_ref, *, nb, n_ctx):
    i = pl.program_id(0)
    scale = ATT_HEAD_DIM ** -0.5
    rows = GQA_GROUP * BLOCK
    span = 3 * BLOCK
    ncol = span + n_ctx
    row = lax.broadcasted_iota(jnp.int32, (rows, ncol), 0) % BLOCK
    col = lax.broadcasted_iota(jnp.int32, (rows, ncol), 1)
    rel = col - BLOCK - row
    lo = jnp.where(i > 0, 0, BLOCK)
    hi = jnp.where(i < nb - 1, span, 2 * BLOCK)
    band = (jnp.abs(rel) <= WINDOW) & (col >= lo) & (col < hi)
    valid = band | (col >= span)
    heads = range(N_KV_HEADS)
    cols = [slice(hk * ATT_HEAD_DIM, (hk + 1) * ATT_HEAD_DIM) for hk in heads]
    q = [(jnp.concatenate(
        [q_ref[:, (hk * GQA_GROUP + g) * ATT_HEAD_DIM:(hk * GQA_GROUP + g + 1) * ATT_HEAD_DIM]
         for g in range(GQA_GROUP)], axis=0) * scale).astype(BF16) for hk in heads]
    k_all = [jnp.concatenate([kp_ref[:, ks], kc_ref[:, ks], kn_ref[:, ks], kx_ref[:, ks]], axis=0).astype(BF16)
             for ks in cols]
    v_all = [jnp.concatenate([vp_ref[:, ks], vc_ref[:, ks], vn_ref[:, ks], vx_ref[:, ks]], axis=0).astype(BF16)
             for ks in cols]
    sink = [jnp.concatenate([jnp.full((BLOCK, 1), sink_ref[hk * GQA_GROUP + g], F32) for g in range(GQA_GROUP)],
                            axis=0) for hk in heads]
    s = [jnp.where(valid, _dot_nt(qh, kh), -jnp.inf) for qh, kh in zip(q, k_all)]
    m = [jnp.maximum(jnp.max(sh, axis=-1, keepdims=True), sk) for sh, sk in zip(s, sink)]
    e = [jnp.exp(sh - mh) for sh, mh in zip(s, m)]
    denom = [jnp.sum(eh, axis=-1, keepdims=True) + jnp.exp(sk - mh) for eh, sk, mh in zip(e, sink, m)]
    o = [_dot(eh.astype(BF16), vh) * (1.0 / dh) for eh, vh, dh in zip(e, v_all, denom)]
    for hk in heads:
        for g in range(GQA_GROUP):
            h = hk * GQA_GROUP + g
            o_ref[:, h * ATT_HEAD_DIM:(h + 1) * ATT_HEAD_DIM] = o[hk][g * BLOCK:(g + 1) * BLOCK].astype(BF16)


def _attention(qkv, sink, s_len, n_ctx):
    nb = s_len // BLOCK
    kcol = ATT_WIDTH // KV_WIDTH
    vcol = kcol + 1
    ctx_blk = s_len // n_ctx
    prev = lambda i: jnp.maximum(i - 1, 0)
    nxt = lambda i: jnp.minimum(i + 1, nb - 1)
    kv = lambda rowf, c: pl.BlockSpec((BLOCK, KV_WIDTH), lambda i: (rowf(i), c))
    same = lambda i: i
    return pl.pallas_call(
        functools.partial(_attn_kernel, nb=nb, n_ctx=n_ctx),
        grid=(nb,),
        in_specs=[pl.BlockSpec(memory_space=pltpu.SMEM),
                  pl.BlockSpec((BLOCK, ATT_WIDTH), lambda i: (i, 0)),
                  kv(prev, kcol), kv(same, kcol), kv(nxt, kcol),
                  kv(prev, vcol), kv(same, vcol), kv(nxt, vcol),
                  pl.BlockSpec((n_ctx, KV_WIDTH), lambda i: (ctx_blk, kcol)),
                  pl.BlockSpec((n_ctx, KV_WIDTH), lambda i: (ctx_blk, vcol))],
        out_specs=pl.BlockSpec((BLOCK, ATT_WIDTH), lambda i: (i, 0)),
        out_shape=jax.ShapeDtypeStruct((s_len, ATT_WIDTH), BF16),
        compiler_params=pltpu.CompilerParams(
            dimension_semantics=("arbitrary",),
            vmem_limit_bytes=_vmem(32 * 1024 * 1024)),
        name="attention",
    )(sink, qkv, qkv, qkv, qkv, qkv, qkv, qkv, qkv, qkv)


def _prep_kernel(x_ref, xp_ref, xn_ref, mup_ref, mun_ref, w2_ref, w0_ref, a2_ref, a0_ref, g2_ref,
                 kk_ref, ka_ref, rk_ref, seg_ref, exp_ref,
                 r_o, v_o, kk_o, g_o, bon_o, lwf_o, lwb_o, kdf_o, kdb_o, bf_o, bb_o,
                 *, tm, start_tiles, end_tiles):
    i = pl.program_id(0)
    r_w = RWKV_WIDTH
    x = x_ref[...]
    is_start = functools.reduce(jnp.logical_or, [i == t for t in start_tiles])
    is_end = functools.reduce(jnp.logical_or, [i == t for t in end_tiles])
    keep_prev = jnp.full((1, 1), jnp.where(is_start, 0, 1), jnp.int32) == 1
    keep_next = jnp.full((1, 1), jnp.where(is_end, 0, 1), jnp.int32) == 1
    prev_row = jnp.where(keep_prev, xp_ref[7:8, :], 0.0)
    next_row = jnp.where(keep_next, xn_ref[0:1, :], 0.0)
    xp = pltpu.roll(x, 1, 0)
    xn = pltpu.roll(x, tm - 1, 0)
    row8 = lax.broadcasted_iota(jnp.int32, (8, 1), 0)
    xp = jnp.concatenate([jnp.where(row8 == 0, prev_row, xp[:8]), xp[8:]], axis=0)
    xn = jnp.concatenate([xn[:tm - 8], jnp.where(row8 == 7, next_row, xn[tm - 8:])], axis=0)
    mup = mup_ref[...]
    mun = mun_ref[...]
    s = x * (1.0 - mup - mun) + xp * mup + xn * mun

    r = s[:, :r_w]
    k = s[:, r_w:2 * r_w]
    v = s[:, 2 * r_w:3 * r_w]
    o1 = 3 * r_w
    o2 = o1 + 2 * LORA_W
    o3 = o2 + 2 * LORA_A
    w_lo = s[:, o1:o2]
    a_lo = s[:, o2:o3]
    g_lo = s[:, o3:]

    zw = w0_ref[...] + _bdot(jnp.tanh(w_lo), w2_ref[...])
    lw = -DECAY_SCALE * jax.nn.sigmoid(zw)
    a = jax.nn.sigmoid(a0_ref[...] + _bdot(a_lo, a2_ref[...]))
    g = _bdot(jax.nn.sigmoid(g_lo), g2_ref[...])

    kkf = k * kk_ref[...]
    ssq = _head_sum(kkf * kkf, seg_ref[...], exp_ref[...])
    kk = kkf * jnp.minimum(lax.rsqrt(ssq), 1e12)
    ka = ka_ref[...]
    kd_f = k * (1.0 + (a[:, :r_w] - 1.0) * ka)
    kd_b = k * (1.0 + (a[:, r_w:] - 1.0) * ka)
    k_bonus = 0.5 * (kd_f + kd_b)
    bonus = _head_sum(r * k_bonus * rk_ref[...], seg_ref[...], exp_ref[...]) * v

    r_o[...] = r.astype(r_o.dtype)
    v_o[...] = v.astype(v_o.dtype)
    kk_o[...] = kk.astype(kk_o.dtype)
    g_o[...] = g
    bon_o[...] = bonus
    lwf_o[...] = lw[:, :r_w]
    lwb_o[...] = lw[:, r_w:]
    kdf_o[...] = kd_f.astype(kdf_o.dtype)
    kdb_o[...] = kd_b.astype(kdb_o.dtype)
    bf_o[...] = (kk * a[:, :r_w]).astype(bf_o.dtype)
    bb_o[...] = (kk * a[:, r_w:]).astype(bb_o.dtype)


def _rwkv_prep(slab, mup, mun, w2blk, w0cat, a2blk, a0cat, g2, k_k, k_a, r_k, seg, expand, s_len, n_ctx, tm):
    m, width = slab.shape
    nt = m // tm
    hb = tm // 8
    nblk8 = m // 8
    start_tiles = (0, s_len // tm)
    end_tiles = (s_len // tm - 1, nt - 1)
    full = lambda arr: pl.BlockSpec(arr.shape, lambda i: (0,) * arr.ndim)
    consts = (mup, mun, w2blk, w0cat, a2blk, a0cat, g2, k_k, k_a, r_k, seg, expand)
    out_spec = pl.BlockSpec((tm, RWKV_WIDTH), lambda i: (i, 0))
    n_out = len(PREP_OUT_DTYPES)
    return pl.pallas_call(
        functools.partial(_prep_kernel, tm=tm, start_tiles=start_tiles, end_tiles=end_tiles),
        grid=(nt,),
        in_specs=[pl.BlockSpec((tm, width), lambda i: (i, 0)),
                  pl.BlockSpec((8, width), lambda i: (jnp.maximum(i * hb - 1, 0), 0)),
                  pl.BlockSpec((8, width), lambda i: (jnp.minimum((i + 1) * hb, nblk8 - 1), 0))]
                 + [full(c) for c in consts],
        out_specs=[out_spec] * n_out,
        out_shape=[jax.ShapeDtypeStruct((m, RWKV_WIDTH), dt) for dt in PREP_OUT_DTYPES],
        compiler_params=pltpu.CompilerParams(
            dimension_semantics=("arbitrary",),
            vmem_limit_bytes=_vmem(2 * tm * width * 4 + 2 * n_out * tm * RWKV_WIDTH * 4
                                   + 12 * tm * RWKV_WIDTH * 4 + 8 * 1024 * 1024)),
        name="rwkv_prep",
    )(slab, slab, slab, *consts)


def _stack(x, m0):
    return jnp.concatenate([jnp.where(m0, x, 0.0), jnp.where(m0, 0.0, x)], axis=0)


def _scan_streams(r_ref, v_ref, kk_ref, lw_ref, kd_ref, b_ref, reverse):
    L = CHUNK
    lw = lw_ref[...]
    ti = lax.broadcasted_iota(jnp.int32, (L, L), 0)
    tj = lax.broadcasted_iota(jnp.int32, (L, L), 1)
    tri = jnp.where((ti <= tj) if reverse else (ti >= tj), 1.0, 0.0).astype(BF16)
    lw_hi = lw.astype(BF16)
    lw_lo = (lw - lw_hi.astype(F32)).astype(BF16)
    c = _dot(tri, lw_hi) + _dot(tri, lw_lo)
    p_cum = jnp.exp(c)
    p_inv = jnp.exp(-c)
    p_prev = jnp.exp(c - lw)
    last = 0 if reverse else L - 1
    p_end = p_cum[last:last + 1, :]
    b_t = b_ref[...] * p_inv
    k_t = kd_ref[...] * p_inv
    gi = lax.broadcasted_iota(jnp.int32, (2 * L, 2 * L), 0)
    gj = lax.broadcasted_iota(jnp.int32, (2 * L, 2 * L), 1)
    same = (gi // L) == (gj // L)
    strict = same & ((gi < gj) if reverse else (gi > gj))
    incl = same & ((gi <= gj) if reverse else (gi >= gj))
    return dict(a=-kk_ref[...] * p_prev, r=r_ref[...] * p_cum, b=b_t, k=k_t, bh=b_t * p_end, kh=k_t * p_end,
                v=v_ref[...], p_end=p_end, strict=strict, incl=incl)


def _scan_kernel(rf, vf, kkf, lwf, kdf, bf, rb, vb, kkb, lwb, kdb, bb, *rest, n_cast):
    cast_in, (yf, yb), cast_out, s_ref = rest[:n_cast], rest[n_cast:n_cast + 2], rest[n_cast + 2:-1], rest[-1]

    @pl.when(pl.program_id(0) == 0)
    def _():
        s_ref[...] = jnp.zeros_like(s_ref)

    for src, dst in zip(cast_in, cast_out):
        dst[...] = src[...].astype(dst.dtype)

    L = CHUNK
    streams = (_scan_streams(rf, vf, kkf, lwf, kdf, bf, False), _scan_streams(rb, vb, kkb, lwb, kdb, bb, True))
    y_refs = (yf, yb)
    m0 = lax.broadcasted_iota(jnp.int32, (L, PAIR), 1) < RWKV_HEAD_DIM
    chains = [(d, p) for d in range(2) for p in range(N_PAIRS)]

    def stacked(name, d, p):
        return _stack(streams[d][name][:, p * PAIR:(p + 1) * PAIR], m0)

    ar = [jnp.concatenate([stacked("a", d, p), stacked("r", d, p)], axis=0).astype(BF16) for d, p in chains]
    bk = [jnp.concatenate([stacked("b", d, p), stacked("k", d, p)], axis=0).astype(BF16) for d, p in chains]
    v_s = [stacked("v", d, p).astype(BF16) for d, p in chains]
    state = [s_ref[d, p] for d, p in chains]
    g = [_dot_nt(x, z) for x, z in zip(ar, bk)]
    xs = [_dot_nt(x, s.astype(BF16)) for x, s in zip(ar, state)]
    a_ab, a_kv, a_rb = [], [], []
    for (d, _), gc in zip(chains, g):
        strict, incl = streams[d]["strict"], streams[d]["incl"]
        a_ab.append(jnp.where(strict, gc[:2 * L, :2 * L], 0.0).astype(BF16))
        a_kv.append(jnp.concatenate([jnp.where(strict, gc[:2 * L, 2 * L:], 0.0),
                                     jnp.where(incl, gc[2 * L:, 2 * L:], 0.0)], axis=0).astype(BF16))
        a_rb.append(jnp.where(incl, gc[2 * L:, :2 * L], 0.0).astype(BF16))
    av = [_dot(a, v) for a, v in zip(a_kv, v_s)]
    w = [x[:2 * L] + y[:2 * L] for x, y in zip(xs, av)]
    am = a_ab
    w = [wc + _dot(a, wc.astype(BF16)) for a, wc in zip(am, w)]
    for _ in range(5):
        am = [_dot(a, a).astype(BF16) for a in am]
        w = [wc + _dot(a, wc.astype(BF16)) for a, wc in zip(am, w)]
    sa = [wc.astype(BF16) for wc in w]
    ys = [x[2 * L:] + y[2 * L:] + _dot(a, s) for x, y, a, s in zip(xs, av, a_rb, sa)]
    for (d, p), yc in zip(chains, ys):
        y_refs[d][:, p * PAIR:(p + 1) * PAIR] = yc[:L] + yc[L:]
    for (d, p), s_old, sac, vc in zip(chains, state, sa, v_s):
        bhkh = jnp.concatenate([stacked("bh", d, p), stacked("kh", d, p)], axis=0).astype(BF16)
        upd = _dot_tn(jnp.concatenate([sac, vc], axis=0), bhkh)
        s_ref[d, p] = s_old * streams[d]["p_end"][:, p * PAIR:(p + 1) * PAIR] + upd


def _cast_slices(w, l, n_steps):
    _, n_rows, n_cols = w.shape
    n_blk = 1
    while n_blk * 2 <= n_steps and n_rows % (n_blk * 2) == 0 and (n_rows // (n_blk * 2)) % 16 == 0:
        n_blk *= 2
    rows = n_rows // n_blk
    step = lambda i: jnp.minimum(i, n_blk - 1)
    return (pl.BlockSpec((None, rows, n_cols), lambda i: (l, step(i), 0)),
            pl.BlockSpec((rows, n_cols), lambda i: (step(i), 0)),
            jax.ShapeDtypeStruct((n_rows, n_cols), BF16))


def _rwkv_scan(r, v, kk, lwf, lwb, kdf, kdb, bf, bb, s_len, n_ctx, cast_weights, l):
    m = r.shape[0]
    nc = m // CHUNK
    nc_lat = s_len // CHUNK
    nc_ctx = n_ctx // CHUNK
    fwd = lambda i: (jnp.where(i < nc_ctx, nc_lat + i, i - nc_ctx), 0)
    bwd = lambda i: (nc - 1 - i, 0)
    fs = pl.BlockSpec((CHUNK, RWKV_WIDTH), fwd)
    bs = pl.BlockSpec((CHUNK, RWKV_WIDTH), bwd)
    cast_in, cast_out, cast_shapes = zip(*[_cast_slices(w, l, nc) for w in cast_weights])
    cast_bytes = sum(spec.block_shape[0] * spec.block_shape[1] * 6 for spec in cast_out)
    outs = pl.pallas_call(
        functools.partial(_scan_kernel, n_cast=len(cast_weights)),
        grid=(nc,),
        in_specs=[fs] * 6 + [bs] * 6 + list(cast_in),
        out_specs=[fs, bs] + list(cast_out),
        out_shape=[jax.ShapeDtypeStruct((m, RWKV_WIDTH), F32)] * 2 + list(cast_shapes),
        scratch_shapes=[pltpu.VMEM((2, N_PAIRS, PAIR, PAIR), F32)],
        compiler_params=pltpu.CompilerParams(
            dimension_semantics=("arbitrary",),
            vmem_limit_bytes=_vmem(2 * 14 * CHUNK * RWKV_WIDTH * 4 + 2 * N_PAIRS * PAIR * PAIR * 4
                                   + 2 * cast_bytes + 24 * 1024 * 1024)),
        name="rwkv_scan",
    )(r, v, kk, lwf, kdf, bf, r, v, kk, lwb, kdb, bb, *cast_weights)
    return outs[0], outs[1], outs[2:]


def _out_kernel(att_ref, yf_ref, yb_ref, g_ref, bon_ref, x_ref, wa_ref, wr_ref, lnw_ref, lnb_ref,
                seg_ref, exp_ref, gpm_ref, gt1_ref, gmlp_ref, sc2_ref, sh2_ref, x1_ref, h2_ref):
    mix_att = _dot(att_ref[...], wa_ref[...])
    inv_n = 1.0 / RWKV_HEAD_DIM
    seg, expand = seg_ref[...], exp_ref[...]
    rows = att_ref.shape[0] // OUT_SUB_BLOCKS
    blocks = [pl.ds(n * rows, rows) for n in range(OUT_SUB_BLOCKS)]
    y = [yf_ref[b, :] + yb_ref[b, :] for b in blocks]
    s1 = [_bdot(v, seg) for v in y]
    yc = [v - _split_dot(s, expand) * inv_n for v, s in zip(y, s1)]
    s2 = [_bdot(c * c, seg) for c in yc]
    var = [_bdot(s, expand) * inv_n for s in s2]
    rec = [((c * lax.rsqrt(vr + GN_EPS) * lnw_ref[...] + lnb_ref[...]) + bon_ref[b, :]) * g_ref[b, :]
           for c, vr, b in zip(yc, var, blocks)]
    mix = mix_att + _bdot(jnp.concatenate(rec, axis=0), wr_ref[...])
    ms = jnp.mean(mix * mix, axis=-1, keepdims=True)
    x1 = x_ref[...] + gt1_ref[...] * (mix * lax.rsqrt(ms + NORM_EPS) * gpm_ref[...])
    x1_ref[...] = x1
    ms2 = jnp.mean(x1 * x1, axis=-1, keepdims=True)
    h2 = x1 * lax.rsqrt(ms2 + NORM_EPS) * gmlp_ref[...]
    h2_ref[...] = (h2 * (1.0 + sc2_ref[...]) + sh2_ref[...]).astype(BF16)


def _out_proj(att, yf, yb, g, bonus, x, w_out, lnw, lnb, seg, expand, gpm, gt1, gmlp, sc2, sh2, tm):
    s_len, d = x.shape
    assert ATT_WIDTH == RWKV_WIDTH and w_out.shape == (ATT_WIDTH + RWKV_WIDTH, d)
    row = lambda w: pl.BlockSpec((tm, w), lambda i: (i, 0))
    full = lambda arr: pl.BlockSpec(arr.shape, lambda i: (0,) * arr.ndim)
    w_half = lambda n: pl.BlockSpec((ATT_WIDTH, d), lambda i: (n, 0))
    consts = (lnw, lnb, seg, expand, gpm, gt1, gmlp, sc2, sh2)
    return pl.pallas_call(
        _out_kernel,
        grid=(s_len // tm,),
        in_specs=[row(ATT_WIDTH)] + [row(RWKV_WIDTH)] * 4 + [row(d), w_half(0), w_half(1)]
                 + [full(c) for c in consts],
        out_specs=[row(d), row(d)],
        out_shape=[jax.ShapeDtypeStruct((s_len, d), F32), jax.ShapeDtypeStruct((s_len, d), BF16)],
        compiler_params=pltpu.CompilerParams(
            dimension_semantics=("arbitrary",),
            vmem_limit_bytes=_vmem(2 * tm * (ATT_WIDTH * 2 + 4 * RWKV_WIDTH * 4 + d * 4 + d * 4 + d * 2)
                                   + 2 * 2 * d * d + 8 * tm * d * 4 + 4 * 1024 * 1024)),
        name="out_proj",
    )(att, yf, yb, g, bonus, x, w_out, w_out, *consts)


def _mlp_kernel(h_ref, w1_ref, w2_ref, x1_ref, g_ref, gt_ref, o_ref, acc_ref):
    f = pl.program_id(1)

    @pl.when(f == 0)
    def _():
        acc_ref[...] = jnp.zeros_like(acc_ref)

    u = jnp.maximum(_dot(h_ref[...], w1_ref[...]), 0.0)
    acc_ref[...] += _dot((u * u).astype(BF16), w2_ref[...])

    @pl.when(f == pl.num_programs(1) - 1)
    def _():
        ff = acc_ref[...]
        ms = jnp.mean(ff * ff, axis=-1, keepdims=True)
        o_ref[...] = x1_ref[...] + gt_ref[...] * (ff * lax.rsqrt(ms + NORM_EPS) * g_ref[...])


def _mlp(h2, w1, w2, x1, gain, gt2, tm, tf):
    s_len, d = x1.shape
    dff = w1.shape[1]
    return pl.pallas_call(
        _mlp_kernel,
        grid=(s_len // tm, dff // tf),
        in_specs=[pl.BlockSpec((tm, d), lambda i, f: (i, 0)),
                  pl.BlockSpec((d, tf), lambda i, f: (0, f)),
                  pl.BlockSpec((tf, d), lambda i, f: (f, 0)),
                  pl.BlockSpec((tm, d), lambda i, f: (i, 0)),
                  pl.BlockSpec((1, d), lambda i, f: (0, 0)),
                  pl.BlockSpec((1, d), lambda i, f: (0, 0))],
        out_specs=pl.BlockSpec((tm, d), lambda i, f: (i, 0)),
        out_shape=jax.ShapeDtypeStruct((s_len, d), F32),
        scratch_shapes=[pltpu.VMEM((tm, d), F32)],
        compiler_params=pltpu.CompilerParams(
            dimension_semantics=("arbitrary", "arbitrary"),
            vmem_limit_bytes=_vmem(2 * (tm * d * 2 + 2 * d * tf * 2 + 2 * tm * d * 4)
                                   + tm * d * 4 + 3 * tm * tf * 4 + 4 * 1024 * 1024)),
        name="mlp",
    )(h2, w1, w2, x1, gain, gt2)


def _rope_tables(s_len, n_ctx):
    rows = s_len // GRID_W
    n_freq = ATT_HEAD_DIM // 4
    inv_freq = ROPE_BASE ** (-jnp.arange(n_freq, dtype=F32) / n_freq)
    ang_r = jnp.arange(rows, dtype=F32)[:, None] * inv_freq[None, :]
    ang_c = jnp.arange(GRID_W, dtype=F32)[:, None] * inv_freq[None, :]
    expand_r = lambda t: jnp.repeat(t, GRID_W, axis=0)
    expand_c = lambda t: jnp.tile(t, (rows, 1))
    cos_r, sin_r = expand_r(jnp.cos(ang_r)), expand_r(jnp.sin(ang_r))
    cos_c, sin_c = expand_c(jnp.cos(ang_c)), expand_c(jnp.sin(ang_c))
    cos = jnp.concatenate([cos_r, cos_r, cos_c, cos_c], axis=-1)
    sin = jnp.concatenate([-sin_r, sin_r, -sin_c, sin_c], axis=-1)
    cos = jnp.concatenate([cos, jnp.ones((n_ctx, ATT_HEAD_DIM), F32)], axis=0)
    sin = jnp.concatenate([sin, jnp.zeros((n_ctx, ATT_HEAD_DIM), F32)], axis=0)
    return cos, sin


def _block_diag2(w):
    _, k, n = w.shape
    z = jnp.zeros((k, n), w.dtype)
    return jnp.concatenate([jnp.concatenate([w[0], z], axis=1), jnp.concatenate([z, w[1]], axis=1)], axis=0)


def _layer(x2, ctx2, c8, l, w_ada, b_ada, g_pre_mix, g_post_mix, g_pre_mlp, g_post_mlp, w_in, attn_sink,
           rwkv_mu_prev, rwkv_mu_next, rwkv_w0, rwkv_w2, rwkv_a0, rwkv_a2, rwkv_g2, rwkv_k_k, rwkv_k_a,
           rwkv_r_k, rwkv_ln_w, rwkv_ln_b, w_out, w_mlp_in, w_mlp_out):
    s_len, d = x2.shape
    n_ctx = ctx2.shape[0]
    row = lambda v: v.reshape(1, -1)

    mod = _ada(c8, w_ada[l], row(b_ada[l]))
    sh1, sc1, gt1, sh2, sc2, gt2 = [mod[:2, j * d:(j + 1) * d] for j in range(N_MOD)]

    tm_rows = 256
    h = _norm_mod(x2, ctx2, row(g_pre_mix[l]), sc1.reshape(2, 1, d), sh1.reshape(2, 1, d), tm_rows)

    cos, sin = _rope_tables(s_len, n_ctx)
    m_all = s_len + n_ctx
    tm_proj = 768 if m_all % 768 == 0 else 256
    qkv = _proj(h, w_in[l][:, :ATT_IN].astype(BF16), cos, sin, tm_proj, ATT_IN, N_Q_HEADS + N_KV_HEADS)
    rwkv_in = w_in.shape[2] - ATT_IN
    slab = _proj(h, w_in[l][:, ATT_IN:].astype(BF16), cos, sin, tm_proj, rwkv_in // 3, 0)

    att = _attention(qkv, attn_sink[l], s_len, n_ctx)

    head = jnp.arange(RWKV_WIDTH) // RWKV_HEAD_DIM
    seg = (head[:, None] == jnp.arange(LANES)[None, :]).astype(BF16)
    expand = seg.T
    streams = _rwkv_prep(
        slab, row(rwkv_mu_prev[l]), row(rwkv_mu_next[l]),
        _block_diag2(rwkv_w2[l]).astype(BF16), rwkv_w0[l].reshape(1, -1),
        _block_diag2(rwkv_a2[l]).astype(BF16), rwkv_a0[l].reshape(1, -1),
        rwkv_g2[l].astype(BF16), row(rwkv_k_k[l]), row(rwkv_k_a[l]), rwkv_r_k[l].reshape(1, -1),
        seg, expand, s_len, n_ctx, tm_rows)
    r, v, kk, g, bonus, lwf, lwb, kdf, kdb, bf, bb = streams
    yf, yb, (w_out_b, w_mlp_in_b, w_mlp_out_b) = _rwkv_scan(
        r, v, kk, lwf, lwb, kdf, kdb, bf, bb, s_len, n_ctx, (w_out, w_mlp_in, w_mlp_out), l)

    x1, h2 = _out_proj(att, yf, yb, g, bonus, x2, w_out_b,
                       row(rwkv_ln_w[l]), row(rwkv_ln_b[l]), seg, expand,
                       row(g_post_mix[l]), gt1[0:1], row(g_pre_mlp[l]), sc2[0:1], sh2[0:1], 512)
    return _mlp(h2, w_mlp_in_b, w_mlp_out_b, x1, row(g_post_mlp[l]), gt2[0:1], 512, 1024)


def kernel(x, c, ctx, c_ctx, w_ada, b_ada, g_pre_mix, g_post_mix, g_pre_mlp, g_post_mlp, w_in, attn_sink,
           rwkv_mu_prev, rwkv_mu_next, rwkv_w0, rwkv_w2, rwkv_a0, rwkv_a2, rwkv_g2, rwkv_k_k, rwkv_k_a,
           rwkv_r_k, rwkv_ln_w, rwkv_ln_b, w_out, w_mlp_in, w_mlp_out):
    b, s_len, d = x.shape
    depth = w_ada.shape[0]
    assert b == 1 and depth == 1, "single sample, single layer (context stream is not carried to a next layer)"
    c8 = jnp.zeros((8, d), F32).at[0].set(c[0]).at[1].set(c_ctx)
    out = _layer(x[0], ctx[0], c8, 0, w_ada, b_ada, g_pre_mix, g_post_mix, g_pre_mlp, g_post_mlp, w_in,
                 attn_sink, rwkv_mu_prev, rwkv_mu_next, rwkv_w0, rwkv_w2, rwkv_a0, rwkv_a2, rwkv_g2,
                 rwkv_k_k, rwkv_k_a, rwkv_r_k, rwkv_ln_w, rwkv_ln_b, w_out, w_mlp_in, w_mlp_out)
    return out[None]
```

```python
import functools

import jax
import jax.numpy as jnp
from jax import lax
from jax.experimental import pallas as pl
from jax.experimental.pallas import tpu as pltpu

F32 = jnp.float32
BF16 = jnp.bfloat16

GRID_W = 64
ATT_HEAD_DIM = 128
N_Q_HEADS = 8
N_KV_HEADS = 2
GQA_GROUP = N_Q_HEADS // N_KV_HEADS
ATT_WIDTH = N_Q_HEADS * ATT_HEAD_DIM
KV_WIDTH = N_KV_HEADS * ATT_HEAD_DIM
ATT_IN = ATT_WIDTH + 2 * KV_WIDTH
WINDOW = 128
BLOCK = 128
ROPE_BASE = 10000.0
RWKV_HEAD_DIM = 64
RWKV_WIDTH = 1024
N_RWKV_HEADS = RWKV_WIDTH // RWKV_HEAD_DIM
LORA_W = 64
LORA_A = 64
LORA_G = 128
N_MOD = 6
NORM_EPS = 1e-6
GN_EPS = 64e-5
DECAY_SCALE = 0.6065306597126334

LANES = 128
CHUNK = 64
PAIR = 2 * RWKV_HEAD_DIM
N_PAIRS = RWKV_WIDTH // PAIR
PREP_OUT_DTYPES = (BF16, BF16, BF16, F32, F32, F32, F32, BF16, BF16, BF16, BF16)
OUT_SUB_BLOCKS = 4
VMEM_CAP = 56 * 1024 * 1024


def _vmem(nbytes):
    return int(min(VMEM_CAP, max(16 * 1024 * 1024, nbytes)))


def _dot(a, b):
    return jnp.dot(a, b, preferred_element_type=F32)


def _dot_nt(a, b):
    return lax.dot_general(a, b, (((1,), (1,)), ((), ())), preferred_element_type=F32)


def _dot_tn(a, b):
    return lax.dot_general(a, b, (((0,), (0,)), ((), ())), preferred_element_type=F32)


def _bdot(a, b):
    return _dot(a.astype(BF16), b.astype(BF16))


def _bdot_nt(a, b):
    return _dot_nt(a.astype(BF16), b.astype(BF16))


def _bdot_tn(a, b):
    return _dot_tn(a.astype(BF16), b.astype(BF16))


def _split_dot(x, m):
    hi = x.astype(BF16)
    lo = (x - hi.astype(F32)).astype(BF16)
    return _dot(hi, m) + _dot(lo, m)


def _head_sum(x, seg, expand):
    return _split_dot(_split_dot(x, seg), expand)


def _ada_kernel(c_ref, w_ref, b_ref, o_ref):
    c = c_ref[...]
    s = c * jax.nn.sigmoid(c)
    o_ref[...] = _bdot(s, w_ref[...]) + b_ref[...]


def _ada(c8, w_ada, b_ada):
    d, n = w_ada.shape
    tn = 1024
    return pl.pallas_call(
        _ada_kernel,
        grid=(n // tn,),
        in_specs=[pl.BlockSpec((8, d), lambda j: (0, 0)),
                  pl.BlockSpec((d, tn), lambda j: (0, j)),
                  pl.BlockSpec((1, tn), lambda j: (0, j))],
        out_specs=pl.BlockSpec((8, tn), lambda j: (0, j)),
        out_shape=jax.ShapeDtypeStruct((8, n), F32),
        compiler_params=pltpu.CompilerParams(
            dimension_semantics=("arbitrary",),
            vmem_limit_bytes=_vmem(2 * d * tn * 4 + 4 * 1024 * 1024)),
        name="ada",
    )(c8, w_ada, b_ada)


def _norm_mod_kernel(x_ref, c_ref, g_ref, sc_ref, sh_ref, o_ref, *, n_lat_tiles):
    def emit(src_ref):
        x = src_ref[...]
        ms = jnp.mean(x * x, axis=-1, keepdims=True)
        y = x * lax.rsqrt(ms + NORM_EPS) * g_ref[...]
        o_ref[...] = (y * (1.0 + sc_ref[0]) + sh_ref[0]).astype(BF16)

    i = pl.program_id(0)
    pl.when(i < n_lat_tiles)(lambda: emit(x_ref))
    pl.when(i >= n_lat_tiles)(lambda: emit(c_ref))


def _norm_mod(x2, ctx2, gain, sc2, sh2, tm):
    s_len, d = x2.shape
    n_lat_tiles = s_len // tm
    n_tiles = n_lat_tiles + ctx2.shape[0] // tm
    sel = lambda i: (jnp.where(i >= n_lat_tiles, 1, 0), 0, 0)
    return pl.pallas_call(
        functools.partial(_norm_mod_kernel, n_lat_tiles=n_lat_tiles),
        grid=(n_tiles,),
        in_specs=[pl.BlockSpec((tm, d), lambda i: (jnp.minimum(i, n_lat_tiles - 1), 0)),
                  pl.BlockSpec((tm, d), lambda i: (jnp.maximum(i - n_lat_tiles, 0), 0)),
                  pl.BlockSpec((1, d), lambda i: (0, 0)),
                  pl.BlockSpec((1, 1, d), sel),
                  pl.BlockSpec((1, 1, d), sel)],
        out_specs=pl.BlockSpec((tm, d), lambda i: (i, 0)),
        out_shape=jax.ShapeDtypeStruct((n_tiles * tm, d), BF16),
        compiler_params=pltpu.CompilerParams(
            dimension_semantics=("arbitrary",),
            vmem_limit_bytes=_vmem(2 * tm * d * 10 + 4 * 1024 * 1024)),
        name="norm_mod",
    )(x2, ctx2, gain, sc2, sh2)


def _rope_heads(acc, cos, sin, n_heads):
    lane = lax.broadcasted_iota(jnp.int32, (1, ATT_HEAD_DIM), 1)
    first = (lane % 64) < 32
    outs = []
    for h in range(n_heads):
        xh = acc[:, h * ATT_HEAD_DIM:(h + 1) * ATT_HEAD_DIM]
        partner = jnp.where(first, pltpu.roll(xh, 96, 1), pltpu.roll(xh, 32, 1))
        outs.append(xh * cos + partner * sin)
    return outs


def _proj_kernel(a_ref, w_ref, rcos_ref, rsin_ref, ccos_ref, csin_ref, o_ref, *, rope_heads, s_len):
    acc = _dot(a_ref[...], w_ref[...])
    tm = a_ref.shape[0]
    i, j = pl.program_id(0), pl.program_id(1)
    row = i * tm + lax.broadcasted_iota(jnp.int32, (tm, 1), 0)
    rotate = row < jnp.where(j == 0, s_len, 0)
    cos = jnp.where(rotate, (rcos_ref[...] + ccos_ref[...][None]).reshape(tm, ATT_HEAD_DIM), 1.0)
    sin = jnp.where(rotate, (rsin_ref[...] + csin_ref[...][None]).reshape(tm, ATT_HEAD_DIM), 0.0)
    for h, r in enumerate(_rope_heads(acc, cos, sin, rope_heads)):
        o_ref[:, h * ATT_HEAD_DIM:(h + 1) * ATT_HEAD_DIM] = r
    o_ref[:, rope_heads * ATT_HEAD_DIM:] = acc[:, rope_heads * ATT_HEAD_DIM:]


def _proj(a, w, rope_tables, s_len, tm, tn, rope_heads):
    m, k = a.shape
    n = w.shape[1]
    n_tiles = m // tm
    assert rope_heads * ATT_HEAD_DIM <= tn and n % tn == 0 and tm % GRID_W == 0
    g = tm // GRID_W
    row_spec = pl.BlockSpec((g, 1, ATT_HEAD_DIM), lambda i, j: (i, 0, 0))
    col_spec = pl.BlockSpec((GRID_W, ATT_HEAD_DIM), lambda i, j: (0, 0))
    return pl.pallas_call(
        functools.partial(_proj_kernel, rope_heads=rope_heads, s_len=s_len),
        grid=(n_tiles, n // tn),
        in_specs=[pl.BlockSpec((tm, k), lambda i, j: (i, 0)),
                  pl.BlockSpec((k, tn), lambda i, j: (0, j)),
                  row_spec, row_spec, col_spec, col_spec],
        out_specs=pl.BlockSpec((tm, tn), lambda i, j: (i, j)),
        out_shape=jax.ShapeDtypeStruct((m, n), F32),
        compiler_params=pltpu.CompilerParams(
            dimension_semantics=("arbitrary", "arbitrary"),
            vmem_limit_bytes=_vmem(2 * (tm * k * 2 + k * tn * 2 + tm * tn * 4 + 2 * tm * 512)
                                   + tm * tn * 8 + 4 * 1024 * 1024)),
        name="proj_in",
    )(a, w, *rope_tables)


def _attn_kernel(sink_ref, q_ref, kp_ref, kc_ref, kn_ref, vp_ref, vc_ref, vn_ref,
                 kx_ref, vx_ref, o_ref, *, nb, n_ctx):
    i = pl.program_id(0)
    scale = ATT_HEAD_DIM ** -0.5
    rows = GQA_GROUP * BLOCK
    span = 3 * BLOCK
    ncol = span + n_ctx
    row = lax.broadcasted_iota(jnp.int32, (rows, ncol), 0) % BLOCK
    col = lax.broadcasted_iota(jnp.int32, (rows, ncol), 1)
    rel = col - BLOCK - row
    lo = jnp.where(i > 0, 0, BLOCK)
    hi = jnp.where(i < nb - 1, span, 2 * BLOCK)
    band = (jnp.abs(rel) <= WINDOW) & (col >= lo) & (col < hi)
    valid = band | (col >= span)
    heads = range(N_KV_HEADS)
    cols = [slice(hk * ATT_HEAD_DIM, (hk + 1) * ATT_HEAD_DIM) for hk in heads]
    q = [(jnp.concatenate(
        [q_ref[:, (hk * GQA_GROUP + g) * ATT_HEAD_DIM:(hk * GQA_GROUP + g + 1) * ATT_HEAD_DIM]
         for g in range(GQA_GROUP)], axis=0) * scale).astype(BF16) for hk in heads]
    k_all = [jnp.concatenate([kp_ref[:, ks], kc_ref[:, ks], kn_ref[:, ks], kx_ref[:, ks]], axis=0).astype(BF16)
             for ks in cols]
    v_all = [jnp.concatenate([vp_ref[:, ks], vc_ref[:, ks], vn_ref[:, ks], vx_ref[:, ks]], axis=0).astype(BF16)
             for ks in cols]
    sink = [jnp.concatenate([jnp.full((BLOCK, 1), sink_ref[hk * GQA_GROUP + g], F32) for g in range(GQA_GROUP)],
                            axis=0) for hk in heads]
    s = [jnp.where(valid, _dot_nt(qh, kh), -jnp.inf) for qh, kh in zip(q, k_all)]
    m = [jnp.maximum(jnp.max(sh, axis=-1, keepdims=True), sk) for sh, sk in zip(s, sink)]
    e = [jnp.exp(sh - mh) for sh, mh in zip(s, m)]
    denom = [jnp.sum(eh, axis=-1, keepdims=True) + jnp.exp(sk - mh) for eh, sk, mh in zip(e, sink, m)]
    o = [_dot(eh.astype(BF16), vh) * (1.0 / dh) for eh, vh, dh in zip(e, v_all, denom)]
    for hk in heads:
        for g in range(GQA_GROUP):
            h = hk * GQA_GROUP + g
            o_ref[:, h * ATT_HEAD_DIM:(h + 1) * ATT_HEAD_DIM] = o[hk][g * BLOCK:(g + 1) * BLOCK].astype(BF16)


def _attention(qkv, sink, s_len, n_ctx):
    nb = s_len // BLOCK
    kcol = ATT_WIDTH // KV_WIDTH
    vcol = kcol + 1
    ctx_blk = s_len // n_ctx
    prev = lambda i: jnp.maximum(i - 1, 0)
    nxt = lambda i: jnp.minimum(i + 1, nb - 1)
    kv = lambda rowf, c: pl.BlockSpec((BLOCK, KV_WIDTH), lambda i: (rowf(i), c))
    same = lambda i: i
    return pl.pallas_call(
        functools.partial(_attn_kernel, nb=nb, n_ctx=n_ctx),
        grid=(nb,),
        in_specs=[pl.BlockSpec(memory_space=pltpu.SMEM),
                  pl.BlockSpec((BLOCK, ATT_WIDTH), lambda i: (i, 0)),
                  kv(prev, kcol), kv(same, kcol), kv(nxt, kcol),
                  kv(prev, vcol), kv(same, vcol), kv(nxt, vcol),
                  pl.BlockSpec((n_ctx, KV_WIDTH), lambda i: (ctx_blk, kcol)),
                  pl.BlockSpec((n_ctx, KV_WIDTH), lambda i: (ctx_blk, vcol))],
        out_specs=pl.BlockSpec((BLOCK, ATT_WIDTH), lambda i: (i, 0)),
        out_shape=jax.ShapeDtypeStruct((s_len, ATT_WIDTH), BF16),
        compiler_params=pltpu.CompilerParams(
            dimension_semantics=("arbitrary",),
            vmem_limit_bytes=_vmem(32 * 1024 * 1024)),
        name="attention",
    )(sink, qkv, qkv, qkv, qkv, qkv, qkv, qkv, qkv, qkv)


def _prep_kernel(x_ref, xp_ref, xn_ref, mup_ref, mun_ref, w2_ref, w0_ref, a2_ref, a0_ref, g2_ref,
                 kk_ref, ka_ref, rk_ref, seg_ref, exp_ref,
                 r_o, v_o, kk_o, g_o, bon_o, lwf_o, lwb_o, kdf_o, kdb_o, bf_o, bb_o,
                 *, tm, start_tiles, end_tiles):
    i = pl.program_id(0)
    r_w = RWKV_WIDTH
    x = x_ref[:, ATT_IN:]
    is_start = functools.reduce(jnp.logical_or, [i == t for t in start_tiles])
    is_end = functools.reduce(jnp.logical_or, [i == t for t in end_tiles])
    keep_prev = jnp.full((1, 1), jnp.where(is_start, 0, 1), jnp.int32) == 1
    keep_next = jnp.full((1, 1), jnp.where(is_end, 0, 1), jnp.int32) == 1
    prev_row = jnp.where(keep_prev, xp_ref[7:8, ATT_IN:], 0.0)
    next_row = jnp.where(keep_next, xn_ref[0:1, ATT_IN:], 0.0)
    xp = pltpu.roll(x, 1, 0)
    xn = pltpu.roll(x, tm - 1, 0)
    row8 = lax.broadcasted_iota(jnp.int32, (8, 1), 0)
    xp = jnp.concatenate([jnp.where(row8 == 0, prev_row, xp[:8]), xp[8:]], axis=0)
    xn = jnp.concatenate([xn[:tm - 8], jnp.where(row8 == 7, next_row, xn[tm - 8:])], axis=0)
    mup = mup_ref[...]
    mun = mun_ref[...]
    s = x * (1.0 - mup - mun) + xp * mup + xn * mun

    r = s[:, :r_w]
    k = s[:, r_w:2 * r_w]
    v = s[:, 2 * r_w:3 * r_w]
    o1 = 3 * r_w
    o2 = o1 + 2 * LORA_W
    o3 = o2 + 2 * LORA_A
    w_lo = s[:, o1:o2]
    a_lo = s[:, o2:o3]
    g_lo = s[:, o3:]

    zw = w0_ref[...] + _bdot(jnp.tanh(w_lo), w2_ref[...])
    lw = -DECAY_SCALE * jax.nn.sigmoid(zw)
    a = jax.nn.sigmoid(a0_ref[...] + _bdot(a_lo, a2_ref[...]))
    g = _bdot(jax.nn.sigmoid(g_lo), g2_ref[...])

    kkf = k * kk_ref[...]
    ssq = _head_sum(kkf * kkf, seg_ref[...], exp_ref[...])
    kk = kkf * jnp.minimum(lax.rsqrt(ssq), 1e12)
    ka = ka_ref[...]
    kd_f = k * (1.0 + (a[:, :r_w] - 1.0) * ka)
    kd_b = k * (1.0 + (a[:, r_w:] - 1.0) * ka)
    k_bonus = 0.5 * (kd_f + kd_b)
    bonus = _head_sum(r * k_bonus * rk_ref[...], seg_ref[...], exp_ref[...]) * v

    r_o[...] = r.astype(r_o.dtype)
    v_o[...] = v.astype(v_o.dtype)
    kk_o[...] = kk.astype(kk_o.dtype)
    g_o[...] = g
    bon_o[...] = bonus
    lwf_o[...] = lw[:, :r_w]
    lwb_o[...] = lw[:, r_w:]
    kdf_o[...] = kd_f.astype(kdf_o.dtype)
    kdb_o[...] = kd_b.astype(kdb_o.dtype)
    bf_o[...] = (kk * a[:, :r_w]).astype(bf_o.dtype)
    bb_o[...] = (kk * a[:, r_w:]).astype(bb_o.dtype)


def _rwkv_prep(slab, mup, mun, w2blk, w0cat, a2blk, a0cat, g2, k_k, k_a, r_k, seg, expand, s_len, n_ctx, tm):
    m, width = slab.shape
    nt = m // tm
    hb = tm // 8
    nblk8 = m // 8
    start_tiles = (0, s_len // tm)
    end_tiles = (s_len // tm - 1, nt - 1)
    full = lambda arr: pl.BlockSpec(arr.shape, lambda i: (0,) * arr.ndim)
    consts = (mup, mun, w2blk, w0cat, a2blk, a0cat, g2, k_k, k_a, r_k, seg, expand)
    out_spec = pl.BlockSpec((tm, RWKV_WIDTH), lambda i: (i, 0))
    n_out = len(PREP_OUT_DTYPES)
    return pl.pallas_call(
        functools.partial(_prep_kernel, tm=tm, start_tiles=start_tiles, end_tiles=end_tiles),
        grid=(nt,),
        in_specs=[pl.BlockSpec((tm, width), lambda i: (i, 0)),
                  pl.BlockSpec((8, width), lambda i: (jnp.maximum(i * hb - 1, 0), 0)),
                  pl.BlockSpec((8, width), lambda i: (jnp.minimum((i + 1) * hb, nblk8 - 1), 0))]
                 + [full(c) for c in consts],
        out_specs=[out_spec] * n_out,
        out_shape=[jax.ShapeDtypeStruct((m, RWKV_WIDTH), dt) for dt in PREP_OUT_DTYPES],
        compiler_params=pltpu.CompilerParams(
            dimension_semantics=("arbitrary",),
            vmem_limit_bytes=_vmem(2 * tm * width * 4 + 2 * n_out * tm * RWKV_WIDTH * 4
                                   + 12 * tm * RWKV_WIDTH * 4 + 8 * 1024 * 1024)),
        name="rwkv_prep",
    )(slab, slab, slab, *consts)


def _stack(x, m0):
    return jnp.concatenate([jnp.where(m0, x, 0.0), jnp.where(m0, 0.0, x)], axis=0)


def _scan_streams(r_ref, v_ref, kk_ref, lw_ref, kd_ref, b_ref, reverse):
    L = CHUNK
    lw = lw_ref[...]
    ti = lax.broadcasted_iota(jnp.int32, (L, L), 0)
    tj = lax.broadcasted_iota(jnp.int32, (L, L), 1)
    tri = jnp.where((ti <= tj) if reverse else (ti >= tj), 1.0, 0.0).astype(BF16)
    lw_hi = lw.astype(BF16)
    lw_lo = (lw - lw_hi.astype(F32)).astype(BF16)
    c = _dot(tri, lw_hi) + _dot(tri, lw_lo)
    p_cum = jnp.exp(c)
    p_inv = jnp.exp(-c)
    p_prev = jnp.exp(c - lw)
    last = 0 if reverse else L - 1
    p_end = p_cum[last:last + 1, :]
    b_t = b_ref[...] * p_inv
    k_t = kd_ref[...] * p_inv
    gi = lax.broadcasted_iota(jnp.int32, (2 * L, 2 * L), 0)
    gj = lax.broadcasted_iota(jnp.int32, (2 * L, 2 * L), 1)
    same = (gi // L) == (gj // L)
    strict = same & ((gi < gj) if reverse else (gi > gj))
    incl = same & ((gi <= gj) if reverse else (gi >= gj))
    return dict(a=-kk_ref[...] * p_prev, r=r_ref[...] * p_cum, b=b_t, k=k_t, bh=b_t * p_end, kh=k_t * p_end,
                v=v_ref[...], p_end=p_end, strict=strict, incl=incl)


def _scan_kernel(rf, vf, kkf, lwf, kdf, bf, rb, vb, kkb, lwb, kdb, bb, *rest, n_cast):
    cast_in, (yf, yb), cast_out, s_ref = rest[:n_cast], rest[n_cast:n_cast + 2], rest[n_cast + 2:-1], rest[-1]

    @pl.when(pl.program_id(0) == 0)
    def _():
        s_ref[...] = jnp.zeros_like(s_ref)

    for src, dst in zip(cast_in, cast_out):
        dst[...] = src[...].astype(dst.dtype)

    L = CHUNK
    streams = (_scan_streams(rf, vf, kkf, lwf, kdf, bf, False), _scan_streams(rb, vb, kkb, lwb, kdb, bb, True))
    y_refs = (yf, yb)
    m0 = lax.broadcasted_iota(jnp.int32, (L, PAIR), 1) < RWKV_HEAD_DIM
    chains = [(d, p) for d in range(2) for p in range(N_PAIRS)]

    def stacked(name, d, p):
        return _stack(streams[d][name][:, p * PAIR:(p + 1) * PAIR], m0)

    ar = [jnp.concatenate([stacked("a", d, p), stacked("r", d, p)], axis=0).astype(BF16) for d, p in chains]
    bk = [jnp.concatenate([stacked("b", d, p), stacked("k", d, p)], axis=0).astype(BF16) for d, p in chains]
    v_s = [stacked("v", d, p).astype(BF16) for d, p in chains]
    state = [s_ref[d, p] for d, p in chains]
    g = [_dot_nt(x, z) for x, z in zip(ar, bk)]
    xs = [_dot_nt(x, s.astype(BF16)) for x, s in zip(ar, state)]
    a_ab, a_kv, a_rb = [], [], []
    for (d, _), gc in zip(chains, g):
        strict, incl = streams[d]["strict"], streams[d]["incl"]
        a_ab.append(jnp.where(strict, gc[:2 * L, :2 * L], 0.0).astype(BF16))
        a_kv.append(jnp.concatenate([jnp.where(strict, gc[:2 * L, 2 * L:], 0.0),
                                     jnp.where(incl, gc[2 * L:, 2 * L:], 0.0)], axis=0).astype(BF16))
        a_rb.append(jnp.where(incl, gc[2 * L:, :2 * L], 0.0).astype(BF16))
    av = [_dot(a, v) for a, v in zip(a_kv, v_s)]
    w = [x[:2 * L] + y[:2 * L] for x, y in zip(xs, av)]
    am = a_ab
    w = [wc + _dot(a, wc.astype(BF16)) for a, wc in zip(am, w)]
    for _ in range(5):
        am = [_dot(a, a).astype(BF16) for a in am]
        w = [wc + _dot(a, wc.astype(BF16)) for a, wc in zip(am, w)]
    sa = [wc.astype(BF16) for wc in w]
    ys = [x[2 * L:] + y[2 * L:] + _dot(a, s) for x, y, a, s in zip(xs, av, a_rb, sa)]
    for (d, p), yc in zip(chains, ys):
        y_refs[d][:, p * PAIR:(p + 1) * PAIR] = yc[:L] + yc[L:]
    for (d, p), s_old, sac, vc in zip(chains, state, sa, v_s):
        bhkh = jnp.concatenate([stacked("bh", d, p), stacked("kh", d, p)], axis=0).astype(BF16)
        upd = _dot_tn(jnp.concatenate([sac, vc], axis=0), bhkh)
        s_ref[d, p] = s_old * streams[d]["p_end"][:, p * PAIR:(p + 1) * PAIR] + upd


def _cast_slices(w, l, n_steps):
    _, n_rows, n_cols = w.shape
    n_blk = 1
    while n_blk * 2 <= n_steps and n_rows % (n_blk * 2) == 0 and (n_rows // (n_blk * 2)) % 16 == 0:
        n_blk *= 2
    rows = n_rows // n_blk
    step = lambda i: jnp.minimum(i, n_blk - 1)
    return (pl.BlockSpec((None, rows, n_cols), lambda i: (l, step(i), 0)),
            pl.BlockSpec((rows, n_cols), lambda i: (step(i), 0)),
            jax.ShapeDtypeStruct((n_rows, n_cols), BF16))


def _rwkv_scan(r, v, kk, lwf, lwb, kdf, kdb, bf, bb, s_len, n_ctx, cast_weights, l):
    m = r.shape[0]
    nc = m // CHUNK
    nc_lat = s_len // CHUNK
    nc_ctx = n_ctx // CHUNK
    fwd = lambda i: (jnp.where(i < nc_ctx, nc_lat + i, i - nc_ctx), 0)
    bwd = lambda i: (nc - 1 - i, 0)
    fs = pl.BlockSpec((CHUNK, RWKV_WIDTH), fwd)
    bs = pl.BlockSpec((CHUNK, RWKV_WIDTH), bwd)
    cast_in, cast_out, cast_shapes = zip(*[_cast_slices(w, l, nc) for w in cast_weights])
    cast_bytes = sum(spec.block_shape[0] * spec.block_shape[1] * 6 for spec in cast_out)
    outs = pl.pallas_call(
        functools.partial(_scan_kernel, n_cast=len(cast_weights)),
        grid=(nc,),
        in_specs=[fs] * 6 + [bs] * 6 + list(cast_in),
        out_specs=[fs, bs] + list(cast_out),
        out_shape=[jax.ShapeDtypeStruct((m, RWKV_WIDTH), F32)] * 2 + list(cast_shapes),
        scratch_shapes=[pltpu.VMEM((2, N_PAIRS, PAIR, PAIR), F32)],
        compiler_params=pltpu.CompilerParams(
            dimension_semantics=("arbitrary",),
            vmem_limit_bytes=_vmem(2 * 14 * CHUNK * RWKV_WIDTH * 4 + 2 * N_PAIRS * PAIR * PAIR * 4
                                   + 2 * cast_bytes + 24 * 1024 * 1024)),
        name="rwkv_scan",
    )(r, v, kk, lwf, kdf, bf, r, v, kk, lwb, kdb, bb, *cast_weights)
    return outs[0], outs[1], outs[2:]


def _out_kernel(att_ref, yf_ref, yb_ref, g_ref, bon_ref, x_ref, wa_ref, wr_ref, lnw_ref, lnb_ref,
                seg_ref, exp_ref, gpm_ref, gt1_ref, gmlp_ref, sc2_ref, sh2_ref, x1_ref, h2_ref):
    mix_att = _dot(att_ref[...], wa_ref[...])
    inv_n = 1.0 / RWKV_HEAD_DIM
    seg, expand = seg_ref[...], exp_ref[...]
    rows = att_ref.shape[0] // OUT_SUB_BLOCKS
    blocks = [pl.ds(n * rows, rows) for n in range(OUT_SUB_BLOCKS)]
    y = [yf_ref[b, :] + yb_ref[b, :] for b in blocks]
    s1 = [_bdot(v, seg) for v in y]
    yc = [v - _split_dot(s, expand) * inv_n for v, s in zip(y, s1)]
    s2 = [_bdot(c * c, seg) for c in yc]
    var = [_bdot(s, expand) * inv_n for s in s2]
    rec = [((c * lax.rsqrt(vr + GN_EPS) * lnw_ref[...] + lnb_ref[...]) + bon_ref[b, :]) * g_ref[b, :]
           for c, vr, b in zip(yc, var, blocks)]
    mix = mix_att + _bdot(jnp.concatenate(rec, axis=0), wr_ref[...])
    ms = jnp.mean(mix * mix, axis=-1, keepdims=True)
    x1 = x_ref[...] + gt1_ref[...] * (mix * lax.rsqrt(ms + NORM_EPS) * gpm_ref[...])
    x1_ref[...] = x1
    ms2 = jnp.mean(x1 * x1, axis=-1, keepdims=True)
    h2 = x1 * lax.rsqrt(ms2 + NORM_EPS) * gmlp_ref[...]
    h2_ref[...] = (h2 * (1.0 + sc2_ref[...]) + sh2_ref[...]).astype(BF16)


def _out_proj(att, yf, yb, g, bonus, x, w_out, lnw, lnb, seg, expand, gpm, gt1, gmlp, sc2, sh2, tm):
    s_len, d = x.shape
    assert ATT_WIDTH == RWKV_WIDTH and w_out.shape == (ATT_WIDTH + RWKV_WIDTH, d)
    row = lambda w: pl.BlockSpec((tm, w), lambda i: (i, 0))
    full = lambda arr: pl.BlockSpec(arr.shape, lambda i: (0,) * arr.ndim)
    w_half = lambda n: pl.BlockSpec((ATT_WIDTH, d), lambda i: (n, 0))
    consts = (lnw, lnb, seg, expand, gpm, gt1, gmlp, sc2, sh2)
    return pl.pallas_call(
        _out_kernel,
        grid=(s_len // tm,),
        in_specs=[row(ATT_WIDTH)] + [row(RWKV_WIDTH)] * 4 + [row(d), w_half(0), w_half(1)]
                 + [full(c) for c in consts],
        out_specs=[row(d), row(d)],
        out_shape=[jax.ShapeDtypeStruct((s_len, d), F32), jax.ShapeDtypeStruct((s_len, d), BF16)],
        compiler_params=pltpu.CompilerParams(
            dimension_semantics=("arbitrary",),
            vmem_limit_bytes=_vmem(2 * tm * (ATT_WIDTH * 2 + 4 * RWKV_WIDTH * 4 + d * 4 + d * 4 + d * 2)
                                   + 2 * 2 * d * d + 8 * tm * d * 4 + 4 * 1024 * 1024)),
        name="out_proj",
    )(att, yf, yb, g, bonus, x, w_out, w_out, *consts)


def _mlp_kernel(h_ref, w1_ref, w2_ref, x1_ref, g_ref, gt_ref, o_ref, acc_ref):
    f = pl.program_id(1)

    @pl.when(f == 0)
    def _():
        acc_ref[...] = jnp.zeros_like(acc_ref)

    u = jnp.maximum(_dot(h_ref[...], w1_ref[...]), 0.0)
    acc_ref[...] += _dot((u * u).astype(BF16), w2_ref[...])

    @pl.when(f == pl.num_programs(1) - 1)
    def _():
        ff = acc_ref[...]
        ms = jnp.mean(ff * ff, axis=-1, keepdims=True)
        o_ref[...] = x1_ref[...] + gt_ref[...] * (ff * lax.rsqrt(ms + NORM_EPS) * g_ref[...])


def _mlp(h2, w1, w2, x1, gain, gt2, tm, tf):
    s_len, d = x1.shape
    dff = w1.shape[1]
    return pl.pallas_call(
        _mlp_kernel,
        grid=(s_len // tm, dff // tf),
        in_specs=[pl.BlockSpec((tm, d), lambda i, f: (i, 0)),
                  pl.BlockSpec((d, tf), lambda i, f: (0, f)),
                  pl.BlockSpec((tf, d), lambda i, f: (f, 0)),
                  pl.BlockSpec((tm, d), lambda i, f: (i, 0)),
                  pl.BlockSpec((1, d), lambda i, f: (0, 0)),
                  pl.BlockSpec((1, d), lambda i, f: (0, 0))],
        out_specs=pl.BlockSpec((tm, d), lambda i, f: (i, 0)),
        out_shape=jax.ShapeDtypeStruct((s_len, d), F32),
        scratch_shapes=[pltpu.VMEM((tm, d), F32)],
        compiler_params=pltpu.CompilerParams(
            dimension_semantics=("arbitrary", "arbitrary"),
            vmem_limit_bytes=_vmem(2 * (tm * d * 2 + 2 * d * tf * 2 + 2 * tm * d * 4)
                                   + tm * d * 4 + 3 * tm * tf * 4 + 4 * 1024 * 1024)),
        name="mlp",
    )(h2, w1, w2, x1, gain, gt2)


def _rope_tables(s_len, rows_padded):
    rows = s_len // GRID_W
    n_freq = ATT_HEAD_DIM // 4
    inv_freq = ROPE_BASE ** (-jnp.arange(n_freq, dtype=F32) / n_freq)
    ang_r = jnp.arange(rows, dtype=F32)[:, None] * inv_freq[None, :]
    ang_c = jnp.arange(GRID_W, dtype=F32)[:, None] * inv_freq[None, :]
    zr = jnp.zeros((rows, 2 * n_freq), F32)
    zc = jnp.zeros((GRID_W, 2 * n_freq), F32)
    pad = lambda t: jnp.pad(t, ((0, rows_padded - rows), (0, 0)))[:, None, :]
    row_cos = pad(jnp.concatenate([jnp.cos(ang_r), jnp.cos(ang_r), zr], axis=-1))
    row_sin = pad(jnp.concatenate([-jnp.sin(ang_r), jnp.sin(ang_r), zr], axis=-1))
    col_cos = jnp.concatenate([zc, jnp.cos(ang_c), jnp.cos(ang_c)], axis=-1)
    col_sin = jnp.concatenate([zc, -jnp.sin(ang_c), jnp.sin(ang_c)], axis=-1)
    return row_cos, row_sin, col_cos, col_sin


def _block_diag2(w):
    _, k, n = w.shape
    z = jnp.zeros((k, n), w.dtype)
    return jnp.concatenate([jnp.concatenate([w[0], z], axis=1), jnp.concatenate([z, w[1]], axis=1)], axis=0)


def _layer(x2, ctx2, c8, l, w_ada, b_ada, g_pre_mix, g_post_mix, g_pre_mlp, g_post_mlp, w_in, attn_sink,
           rwkv_mu_prev, rwkv_mu_next, rwkv_w0, rwkv_w2, rwkv_a0, rwkv_a2, rwkv_g2, rwkv_k_k, rwkv_k_a,
           rwkv_r_k, rwkv_ln_w, rwkv_ln_b, w_out, w_mlp_in, w_mlp_out):
    s_len, d = x2.shape
    n_ctx = ctx2.shape[0]
    row = lambda v: v.reshape(1, -1)

    mod = _ada(c8, w_ada[l], row(b_ada[l]))
    sh1, sc1, gt1, sh2, sc2, gt2 = [mod[:2, j * d:(j + 1) * d] for j in range(N_MOD)]

    tm_rows = 256
    h = _norm_mod(x2, ctx2, row(g_pre_mix[l]), sc1.reshape(2, 1, d), sh1.reshape(2, 1, d), tm_rows)

    m_all = s_len + n_ctx
    tm_proj = 768 if m_all % 768 == 0 else 256
    rope_tables = _rope_tables(s_len, (m_all // tm_proj) * (tm_proj // GRID_W))
    p_in = _proj(h, w_in[l].astype(BF16), rope_tables, s_len, tm_proj, w_in.shape[2] // 3,
                 N_Q_HEADS + N_KV_HEADS)

    att = _attention(p_in, attn_sink[l], s_len, n_ctx)

    head = jnp.arange(RWKV_WIDTH) // RWKV_HEAD_DIM
    seg = (head[:, None] == jnp.arange(LANES)[None, :]).astype(BF16)
    expand = seg.T
    streams = _rwkv_prep(
        p_in, row(rwkv_mu_prev[l]), row(rwkv_mu_next[l]),
        _block_diag2(rwkv_w2[l]).astype(BF16), rwkv_w0[l].reshape(1, -1),
        _block_diag2(rwkv_a2[l]).astype(BF16), rwkv_a0[l].reshape(1, -1),
        rwkv_g2[l].astype(BF16), row(rwkv_k_k[l]), row(rwkv_k_a[l]), rwkv_r_k[l].reshape(1, -1),
        seg, expand, s_len, n_ctx, tm_rows)
    r, v, kk, g, bonus, lwf, lwb, kdf, kdb, bf, bb = streams
    yf, yb, (w_out_b, w_mlp_in_b, w_mlp_out_b) = _rwkv_scan(
        r, v, kk, lwf, lwb, kdf, kdb, bf, bb, s_len, n_ctx, (w_out, w_mlp_in, w_mlp_out), l)

    x1, h2 = _out_proj(att, yf, yb, g, bonus, x2, w_out_b,
                       row(rwkv_ln_w[l]), row(rwkv_ln_b[l]), seg, expand,
                       row(g_post_mix[l]), gt1[0:1], row(g_pre_mlp[l]), sc2[0:1], sh2[0:1], 512)
    return _mlp(h2, w_mlp_in_b, w_mlp_out_b, x1, row(g_post_mlp[l]), gt2[0:1], 512, 1024)


def kernel(x, c, ctx, c_ctx, w_ada, b_ada, g_pre_mix, g_post_mix, g_pre_mlp, g_post_mlp, w_in, attn_sink,
           rwkv_mu_prev, rwkv_mu_next, rwkv_w0, rwkv_w2, rwkv_a0, rwkv_a2, rwkv_g2, rwkv_k_k, rwkv_k_a,
           rwkv_r_k, rwkv_ln_w, rwkv_ln_b, w_out, w_mlp_in, w_mlp_out):
    b, s_len, d = x.shape
    depth = w_ada.shape[0]
    assert b == 1 and depth == 1, "single sample, single layer (context stream is not carried to a next layer)"
    c8 = jnp.zeros((8, d), F32).at[0].set(c[0]).at[1].set(c_ctx)
    out = _layer(x[0], ctx[0], c8, 0, w_ada, b_ada, g_pre_mix, g_post_mix, g_pre_mlp, g_post_mlp, w_in,
                 attn_sink, rwkv_mu_prev, rwkv_mu_next, rwkv_w0, rwkv_w2, rwkv_a0, rwkv_a2, rwkv_g2,
                 rwkv_k_k, rwkv_k_a, rwkv_r_k, rwkv_ln_w, rwkv_ln_b, w_out, w_mlp_in, w_mlp_out)
    return out[None]
```

```python
import functools

import jax
import jax.numpy as jnp
from jax import lax
from jax.experimental import pallas as pl
from jax.experimental.pallas import tpu as pltpu

F32 = jnp.float32
BF16 = jnp.bfloat16

GRID_W = 64
ATT_HEAD_DIM = 128
N_Q_HEADS = 8
N_KV_HEADS = 2
GQA_GROUP = N_Q_HEADS // N_KV_HEADS
ATT_WIDTH = N_Q_HEADS * ATT_HEAD_DIM
KV_WIDTH = N_KV_HEADS * ATT_HEAD_DIM
ATT_IN = ATT_WIDTH + 2 * KV_WIDTH
WINDOW = 128
BLOCK = 128
ROPE_BASE = 10000.0
RWKV_HEAD_DIM = 64
RWKV_WIDTH = 1024
N_RWKV_HEADS = RWKV_WIDTH // RWKV_HEAD_DIM
LORA_W = 64
LORA_A = 64
LORA_G = 128
N_MOD = 6
NORM_EPS = 1e-6
GN_EPS = 64e-5
DECAY_SCALE = 0.6065306597126334

LANES = 128
CHUNK = 64
PAIR = 2 * RWKV_HEAD_DIM
N_PAIRS = RWKV_WIDTH // PAIR
PREP_OUT_DTYPES = (BF16, BF16, BF16, BF16, BF16, F32, F32, BF16, BF16, BF16, BF16)
HALO_ROWS = 16
OUT_SUB_BLOCKS = 4
VMEM_CAP = 56 * 1024 * 1024


def _vmem(nbytes):
    return int(min(VMEM_CAP, max(16 * 1024 * 1024, nbytes)))


def _dot(a, b):
    return jnp.dot(a, b, preferred_element_type=F32)


def _dot_nt(a, b):
    return lax.dot_general(a, b, (((1,), (1,)), ((), ())), preferred_element_type=F32)


def _dot_tn(a, b):
    return lax.dot_general(a, b, (((0,), (0,)), ((), ())), preferred_element_type=F32)


def _bdot(a, b):
    return _dot(a.astype(BF16), b.astype(BF16))


def _bdot_nt(a, b):
    return _dot_nt(a.astype(BF16), b.astype(BF16))


def _bdot_tn(a, b):
    return _dot_tn(a.astype(BF16), b.astype(BF16))


def _split_dot(x, m):
    hi = x.astype(BF16)
    lo = (x - hi.astype(F32)).astype(BF16)
    return _dot(hi, m) + _dot(lo, m)


def _head_sum(x, seg, expand):
    return _split_dot(_split_dot(x, seg), expand)


def _ada_kernel(c_ref, w_ref, b_ref, o_ref):
    c = c_ref[...]
    s = c * jax.nn.sigmoid(c)
    o_ref[...] = _bdot(s, w_ref[...]) + b_ref[...]


def _ada(c8, w_ada, b_ada):
    d, n = w_ada.shape
    tn = 1024
    return pl.pallas_call(
        _ada_kernel,
        grid=(n // tn,),
        in_specs=[pl.BlockSpec((8, d), lambda j: (0, 0)),
                  pl.BlockSpec((d, tn), lambda j: (0, j)),
                  pl.BlockSpec((1, tn), lambda j: (0, j))],
        out_specs=pl.BlockSpec((8, tn), lambda j: (0, j)),
        out_shape=jax.ShapeDtypeStruct((8, n), F32),
        compiler_params=pltpu.CompilerParams(
            dimension_semantics=("arbitrary",),
            vmem_limit_bytes=_vmem(2 * d * tn * 4 + 4 * 1024 * 1024)),
        name="ada",
    )(c8, w_ada, b_ada)


def _norm_mod_kernel(x_ref, c_ref, g_ref, sc_ref, sh_ref, o_ref, *, n_lat_tiles):
    def emit(src_ref):
        x = src_ref[...]
        ms = jnp.mean(x * x, axis=-1, keepdims=True)
        y = x * lax.rsqrt(ms + NORM_EPS) * g_ref[...]
        o_ref[...] = (y * (1.0 + sc_ref[0]) + sh_ref[0]).astype(BF16)

    i = pl.program_id(0)
    pl.when(i < n_lat_tiles)(lambda: emit(x_ref))
    pl.when(i >= n_lat_tiles)(lambda: emit(c_ref))


def _norm_mod(x2, ctx2, gain, sc2, sh2, tm):
    s_len, d = x2.shape
    n_lat_tiles = s_len // tm
    n_tiles = n_lat_tiles + ctx2.shape[0] // tm
    sel = lambda i: (jnp.where(i >= n_lat_tiles, 1, 0), 0, 0)
    return pl.pallas_call(
        functools.partial(_norm_mod_kernel, n_lat_tiles=n_lat_tiles),
        grid=(n_tiles,),
        in_specs=[pl.BlockSpec((tm, d), lambda i: (jnp.minimum(i, n_lat_tiles - 1), 0)),
                  pl.BlockSpec((tm, d), lambda i: (jnp.maximum(i - n_lat_tiles, 0), 0)),
                  pl.BlockSpec((1, d), lambda i: (0, 0)),
                  pl.BlockSpec((1, 1, d), sel),
                  pl.BlockSpec((1, 1, d), sel)],
        out_specs=pl.BlockSpec((tm, d), lambda i: (i, 0)),
        out_shape=jax.ShapeDtypeStruct((n_tiles * tm, d), BF16),
        compiler_params=pltpu.CompilerParams(
            dimension_semantics=("arbitrary",),
            vmem_limit_bytes=_vmem(2 * tm * d * 10 + 4 * 1024 * 1024)),
        name="norm_mod",
    )(x2, ctx2, gain, sc2, sh2)


def _rope_heads(acc, cos, sin, n_heads):
    lane = lax.broadcasted_iota(jnp.int32, (1, ATT_HEAD_DIM), 1)
    first = (lane % 64) < 32
    outs = []
    for h in range(n_heads):
        xh = acc[:, h * ATT_HEAD_DIM:(h + 1) * ATT_HEAD_DIM]
        partner = jnp.where(first, pltpu.roll(xh, 96, 1), pltpu.roll(xh, 32, 1))
        outs.append(xh * cos + partner * sin)
    return outs


def _proj_kernel(a_ref, w_ref, rcos_ref, rsin_ref, ccos_ref, csin_ref, o_ref, *, rope_heads, s_len):
    acc = _dot(a_ref[...], w_ref[...])
    tm = a_ref.shape[0]
    i, j = pl.program_id(0), pl.program_id(1)
    row = i * tm + lax.broadcasted_iota(jnp.int32, (tm, 1), 0)
    rotate = row < jnp.where(j == 0, s_len, 0)
    cos = jnp.where(rotate, (rcos_ref[...] + ccos_ref[...][None]).reshape(tm, ATT_HEAD_DIM), 1.0)
    sin = jnp.where(rotate, (rsin_ref[...] + csin_ref[...][None]).reshape(tm, ATT_HEAD_DIM), 0.0)
    for h, r in enumerate(_rope_heads(acc, cos, sin, rope_heads)):
        o_ref[:, h * ATT_HEAD_DIM:(h + 1) * ATT_HEAD_DIM] = r.astype(o_ref.dtype)
    o_ref[:, rope_heads * ATT_HEAD_DIM:] = acc[:, rope_heads * ATT_HEAD_DIM:].astype(o_ref.dtype)


def _proj(a, w, rope_tables, s_len, tm, tn, rope_heads):
    m, k = a.shape
    n = w.shape[1]
    n_tiles = m // tm
    assert rope_heads * ATT_HEAD_DIM <= tn and n % tn == 0 and tm % GRID_W == 0
    g = tm // GRID_W
    row_spec = pl.BlockSpec((g, 1, ATT_HEAD_DIM), lambda i, j: (i, 0, 0))
    col_spec = pl.BlockSpec((GRID_W, ATT_HEAD_DIM), lambda i, j: (0, 0))
    return pl.pallas_call(
        functools.partial(_proj_kernel, rope_heads=rope_heads, s_len=s_len),
        grid=(n_tiles, n // tn),
        in_specs=[pl.BlockSpec((tm, k), lambda i, j: (i, 0)),
                  pl.BlockSpec((k, tn), lambda i, j: (0, j)),
                  row_spec, row_spec, col_spec, col_spec],
        out_specs=pl.BlockSpec((tm, tn), lambda i, j: (i, j)),
        out_shape=jax.ShapeDtypeStruct((m, n), BF16),
        compiler_params=pltpu.CompilerParams(
            dimension_semantics=("arbitrary", "arbitrary"),
            vmem_limit_bytes=_vmem(2 * (tm * k * 2 + k * tn * 2 + tm * tn * 2 + 2 * tm * 512)
                                   + tm * tn * 8 + 4 * 1024 * 1024)),
        name="proj_in",
    )(a, w, *rope_tables)


def _attn_kernel(sink_ref, q_ref, kp_ref, kc_ref, kn_ref, vp_ref, vc_ref, vn_ref,
                 kx_ref, vx_ref, o_ref, *, nb, n_ctx):
    i = pl.program_id(0)
    scale = ATT_HEAD_DIM ** -0.5
    rows = GQA_GROUP * BLOCK
    span = 3 * BLOCK
    ncol = span + n_ctx
    row = lax.broadcasted_iota(jnp.int32, (rows, ncol), 0) % BLOCK
    col = lax.broadcasted_iota(jnp.int32, (rows, ncol), 1)
    rel = col - BLOCK - row
    lo = jnp.where(i > 0, 0, BLOCK)
    hi = jnp.where(i < nb - 1, span, 2 * BLOCK)
    band = (jnp.abs(rel) <= WINDOW) & (col >= lo) & (col < hi)
    valid = band | (col >= span)
    heads = range(N_KV_HEADS)
    cols = [slice(hk * ATT_HEAD_DIM, (hk + 1) * ATT_HEAD_DIM) for hk in heads]
    q = [(jnp.concatenate(
        [q_ref[:, (hk * GQA_GROUP + g) * ATT_HEAD_DIM:(hk * GQA_GROUP + g + 1) * ATT_HEAD_DIM]
         for g in range(GQA_GROUP)], axis=0).astype(F32) * scale).astype(BF16) for hk in heads]
    k_all = [jnp.concatenate([kp_ref[:, ks], kc_ref[:, ks], kn_ref[:, ks], kx_ref[:, ks]], axis=0).astype(BF16)
             for ks in cols]
    v_all = [jnp.concatenate([vp_ref[:, ks], vc_ref[:, ks], vn_ref[:, ks], vx_ref[:, ks]], axis=0).astype(BF16)
             for ks in cols]
    sink = [jnp.concatenate([jnp.full((BLOCK, 1), sink_ref[hk * GQA_GROUP + g], F32) for g in range(GQA_GROUP)],
                            axis=0) for hk in heads]
    s = [jnp.where(valid, _dot_nt(qh, kh), -jnp.inf) for qh, kh in zip(q, k_all)]
    m = [jnp.maximum(jnp.max(sh, axis=-1, keepdims=True), sk) for sh, sk in zip(s, sink)]
    e = [jnp.exp(sh - mh) for sh, mh in zip(s, m)]
    denom = [jnp.sum(eh, axis=-1, keepdims=True) + jnp.exp(sk - mh) for eh, sk, mh in zip(e, sink, m)]
    o = [_dot(eh.astype(BF16), vh) * (1.0 / dh) for eh, vh, dh in zip(e, v_all, denom)]
    for hk in heads:
        for g in range(GQA_GROUP):
            h = hk * GQA_GROUP + g
            o_ref[:, h * ATT_HEAD_DIM:(h + 1) * ATT_HEAD_DIM] = o[hk][g * BLOCK:(g + 1) * BLOCK].astype(BF16)


def _attention(qkv, sink, s_len, n_ctx):
    nb = s_len // BLOCK
    kcol = ATT_WIDTH // KV_WIDTH
    vcol = kcol + 1
    ctx_blk = s_len // n_ctx
    prev = lambda i: jnp.maximum(i - 1, 0)
    nxt = lambda i: jnp.minimum(i + 1, nb - 1)
    kv = lambda rowf, c: pl.BlockSpec((BLOCK, KV_WIDTH), lambda i: (rowf(i), c))
    same = lambda i: i
    return pl.pallas_call(
        functools.partial(_attn_kernel, nb=nb, n_ctx=n_ctx),
        grid=(nb,),
        in_specs=[pl.BlockSpec(memory_space=pltpu.SMEM),
                  pl.BlockSpec((BLOCK, ATT_WIDTH), lambda i: (i, 0)),
                  kv(prev, kcol), kv(same, kcol), kv(nxt, kcol),
                  kv(prev, vcol), kv(same, vcol), kv(nxt, vcol),
                  pl.BlockSpec((n_ctx, KV_WIDTH), lambda i: (ctx_blk, kcol)),
                  pl.BlockSpec((n_ctx, KV_WIDTH), lambda i: (ctx_blk, vcol))],
        out_specs=pl.BlockSpec((BLOCK, ATT_WIDTH), lambda i: (i, 0)),
        out_shape=jax.ShapeDtypeStruct((s_len, ATT_WIDTH), BF16),
        compiler_params=pltpu.CompilerParams(
            dimension_semantics=("arbitrary",),
            vmem_limit_bytes=_vmem(32 * 1024 * 1024)),
        name="attention",
    )(sink, qkv, qkv, qkv, qkv, qkv, qkv, qkv, qkv, qkv)


def _prep_kernel(x_ref, xp_ref, xn_ref, mup_ref, mun_ref, w2_ref, w0_ref, a2_ref, a0_ref, g2_ref,
                 kk_ref, ka_ref, rk_ref, seg_ref, exp_ref,
                 r_o, v_o, kk_o, g_o, bon_o, lwf_o, lwb_o, kdf_o, kdb_o, bf_o, bb_o,
                 *, tm, start_tiles, end_tiles):
    i = pl.program_id(0)
    r_w = RWKV_WIDTH
    x = x_ref[:, ATT_IN:].astype(F32)
    is_start = functools.reduce(jnp.logical_or, [i == t for t in start_tiles])
    is_end = functools.reduce(jnp.logical_or, [i == t for t in end_tiles])
    keep_prev = jnp.full((1, 1), jnp.where(is_start, 0, 1), jnp.int32) == 1
    keep_next = jnp.full((1, 1), jnp.where(is_end, 0, 1), jnp.int32) == 1
    prev_row = jnp.where(keep_prev, xp_ref[HALO_ROWS - 1:HALO_ROWS, ATT_IN:].astype(F32), 0.0)
    next_row = jnp.where(keep_next, xn_ref[0:1, ATT_IN:].astype(F32), 0.0)
    xp = pltpu.roll(x, 1, 0)
    xn = pltpu.roll(x, tm - 1, 0)
    row8 = lax.broadcasted_iota(jnp.int32, (8, 1), 0)
    xp = jnp.concatenate([jnp.where(row8 == 0, prev_row, xp[:8]), xp[8:]], axis=0)
    xn = jnp.concatenate([xn[:tm - 8], jnp.where(row8 == 7, next_row, xn[tm - 8:])], axis=0)
    mup = mup_ref[...]
    mun = mun_ref[...]
    s = x * (1.0 - mup - mun) + xp * mup + xn * mun

    r = s[:, :r_w]
    k = s[:, r_w:2 * r_w]
    v = s[:, 2 * r_w:3 * r_w]
    o1 = 3 * r_w
    o2 = o1 + 2 * LORA_W
    o3 = o2 + 2 * LORA_A
    w_lo = s[:, o1:o2]
    a_lo = s[:, o2:o3]
    g_lo = s[:, o3:]

    zw = w0_ref[...] + _bdot(jnp.tanh(w_lo), w2_ref[...])
    lw = -DECAY_SCALE * jax.nn.sigmoid(zw)
    a = jax.nn.sigmoid(a0_ref[...] + _bdot(a_lo, a2_ref[...]))
    g = _bdot(jax.nn.sigmoid(g_lo), g2_ref[...])

    kkf = k * kk_ref[...]
    ssq = _head_sum(kkf * kkf, seg_ref[...], exp_ref[...])
    kk = kkf * jnp.minimum(lax.rsqrt(ssq), 1e12)
    ka = ka_ref[...]
    kd_f = k * (1.0 + (a[:, :r_w] - 1.0) * ka)
    kd_b = k * (1.0 + (a[:, r_w:] - 1.0) * ka)
    k_bonus = 0.5 * (kd_f + kd_b)
    bonus = _head_sum(r * k_bonus * rk_ref[...], seg_ref[...], exp_ref[...]) * v

    r_o[...] = r.astype(r_o.dtype)
    v_o[...] = v.astype(v_o.dtype)
    kk_o[...] = kk.astype(kk_o.dtype)
    g_o[...] = g.astype(g_o.dtype)
    bon_o[...] = bonus.astype(bon_o.dtype)
    lwf_o[...] = lw[:, :r_w]
    lwb_o[...] = lw[:, r_w:]
    kdf_o[...] = kd_f.astype(kdf_o.dtype)
    kdb_o[...] = kd_b.astype(kdb_o.dtype)
    bf_o[...] = (kk * a[:, :r_w]).astype(bf_o.dtype)
    bb_o[...] = (kk * a[:, r_w:]).astype(bb_o.dtype)


def _rwkv_prep(slab, mup, mun, w2blk, w0cat, a2blk, a0cat, g2, k_k, k_a, r_k, seg, expand, s_len, n_ctx, tm):
    m, width = slab.shape
    nt = m // tm
    hb = tm // HALO_ROWS
    nblk8 = m // HALO_ROWS
    start_tiles = (0, s_len // tm)
    end_tiles = (s_len // tm - 1, nt - 1)
    full = lambda arr: pl.BlockSpec(arr.shape, lambda i: (0,) * arr.ndim)
    consts = (mup, mun, w2blk, w0cat, a2blk, a0cat, g2, k_k, k_a, r_k, seg, expand)
    out_spec = pl.BlockSpec((tm, RWKV_WIDTH), lambda i: (i, 0))
    n_out = len(PREP_OUT_DTYPES)
    return pl.pallas_call(
        functools.partial(_prep_kernel, tm=tm, start_tiles=start_tiles, end_tiles=end_tiles),
        grid=(nt,),
        in_specs=[pl.BlockSpec((tm, width), lambda i: (i, 0)),
                  pl.BlockSpec((HALO_ROWS, width), lambda i: (jnp.maximum(i * hb - 1, 0), 0)),
                  pl.BlockSpec((HALO_ROWS, width), lambda i: (jnp.minimum((i + 1) * hb, nblk8 - 1), 0))]
                 + [full(c) for c in consts],
        out_specs=[out_spec] * n_out,
        out_shape=[jax.ShapeDtypeStruct((m, RWKV_WIDTH), dt) for dt in PREP_OUT_DTYPES],
        compiler_params=pltpu.CompilerParams(
            dimension_semantics=("arbitrary",),
            vmem_limit_bytes=_vmem(2 * tm * width * 4 + 2 * n_out * tm * RWKV_WIDTH * 4
                                   + 12 * tm * RWKV_WIDTH * 4 + 8 * 1024 * 1024)),
        name="rwkv_prep",
    )(slab, slab, slab, *consts)


def _stack(x, m0):
    return jnp.concatenate([jnp.where(m0, x, 0.0), jnp.where(m0, 0.0, x)], axis=0)


def _scan_streams(r_ref, v_ref, kk_ref, lw_ref, kd_ref, b_ref, reverse):
    L = CHUNK
    lw = lw_ref[...]
    ti = lax.broadcasted_iota(jnp.int32, (L, L), 0)
    tj = lax.broadcasted_iota(jnp.int32, (L, L), 1)
    tri = jnp.where((ti <= tj) if reverse else (ti >= tj), 1.0, 0.0).astype(BF16)
    lw_hi = lw.astype(BF16)
    lw_lo = (lw - lw_hi.astype(F32)).astype(BF16)
    c = _dot(tri, lw_hi) + _dot(tri, lw_lo)
    p_cum = jnp.exp(c)
    p_inv = jnp.exp(-c)
    p_prev = jnp.exp(c - lw)
    last = 0 if reverse else L - 1
    p_end = p_cum[last:last + 1, :]
    b_t = b_ref[...] * p_inv
    k_t = kd_ref[...] * p_inv
    gi = lax.broadcasted_iota(jnp.int32, (2 * L, 2 * L), 0)
    gj = lax.broadcasted_iota(jnp.int32, (2 * L, 2 * L), 1)
    same = (gi // L) == (gj // L)
    strict = same & ((gi < gj) if reverse else (gi > gj))
    incl = same & ((gi <= gj) if reverse else (gi >= gj))
    return dict(a=-kk_ref[...] * p_prev, r=r_ref[...] * p_cum, b=b_t, k=k_t, bh=b_t * p_end, kh=k_t * p_end,
                v=v_ref[...], p_end=p_end, strict=strict, incl=incl)


def _scan_kernel(rf, vf, kkf, lwf, kdf, bf, rb, vb, kkb, lwb, kdb, bb, *rest, n_cast):
    cast_in, (yf, yb), cast_out, s_ref = rest[:n_cast], rest[n_cast:n_cast + 2], rest[n_cast + 2:-1], rest[-1]

    @pl.when(pl.program_id(0) == 0)
    def _():
        s_ref[...] = jnp.zeros_like(s_ref)

    for src, dst in zip(cast_in, cast_out):
        dst[...] = src[...].astype(dst.dtype)

    L = CHUNK
    streams = (_scan_streams(rf, vf, kkf, lwf, kdf, bf, False), _scan_streams(rb, vb, kkb, lwb, kdb, bb, True))
    y_refs = (yf, yb)
    m0 = lax.broadcasted_iota(jnp.int32, (L, PAIR), 1) < RWKV_HEAD_DIM
    chains = [(d, p) for d in range(2) for p in range(N_PAIRS)]

    def stacked(name, d, p):
        return _stack(streams[d][name][:, p * PAIR:(p + 1) * PAIR], m0)

    ar = [jnp.concatenate([stacked("a", d, p), stacked("r", d, p)], axis=0).astype(BF16) for d, p in chains]
    bk = [jnp.concatenate([stacked("b", d, p), stacked("k", d, p)], axis=0).astype(BF16) for d, p in chains]
    v_s = [stacked("v", d, p).astype(BF16) for d, p in chains]
    state = [s_ref[d, p] for d, p in chains]
    g = [_dot_nt(x, z) for x, z in zip(ar, bk)]
    xs = [_dot_nt(x, s.astype(BF16)) for x, s in zip(ar, state)]
    a_ab, a_kv, a_rb = [], [], []
    for (d, _), gc in zip(chains, g):
        strict, incl = streams[d]["strict"], streams[d]["incl"]
        a_ab.append(jnp.where(strict, gc[:2 * L, :2 * L], 0.0).astype(BF16))
        a_kv.append(jnp.concatenate([jnp.where(strict, gc[:2 * L, 2 * L:], 0.0),
                                     jnp.where(incl, gc[2 * L:, 2 * L:], 0.0)], axis=0).astype(BF16))
        a_rb.append(jnp.where(incl, gc[2 * L:, :2 * L], 0.0).astype(BF16))
    av = [_dot(a, v) for a, v in zip(a_kv, v_s)]
    w = [x[:2 * L] + y[:2 * L] for x, y in zip(xs, av)]
    am = a_ab
    w = [wc + _dot(a, wc.astype(BF16)) for a, wc in zip(am, w)]
    for _ in range(5):
        am = [_dot(a, a).astype(BF16) for a in am]
        w = [wc + _dot(a, wc.astype(BF16)) for a, wc in zip(am, w)]
    sa = [wc.astype(BF16) for wc in w]
    ys = [x[2 * L:] + y[2 * L:] + _dot(a, s) for x, y, a, s in zip(xs, av, a_rb, sa)]
    for (d, p), yc in zip(chains, ys):
        y_refs[d][:, p * PAIR:(p + 1) * PAIR] = yc[:L] + yc[L:]
    for (d, p), s_old, sac, vc in zip(chains, state, sa, v_s):
        bhkh = jnp.concatenate([stacked("bh", d, p), stacked("kh", d, p)], axis=0).astype(BF16)
        upd = _dot_tn(jnp.concatenate([sac, vc], axis=0), bhkh)
        s_ref[d, p] = s_old * streams[d]["p_end"][:, p * PAIR:(p + 1) * PAIR] + upd


def _cast_slices(w, l, n_steps):
    _, n_rows, n_cols = w.shape
    n_blk = 1
    while n_blk * 2 <= n_steps and n_rows % (n_blk * 2) == 0 and (n_rows // (n_blk * 2)) % 16 == 0:
        n_blk *= 2
    rows = n_rows // n_blk
    step = lambda i: jnp.minimum(i, n_blk - 1)
    return (pl.BlockSpec((None, rows, n_cols), lambda i: (l, step(i), 0)),
            pl.BlockSpec((rows, n_cols), lambda i: (step(i), 0)),
            jax.ShapeDtypeStruct((n_rows, n_cols), BF16))


def _rwkv_scan(r, v, kk, lwf, lwb, kdf, kdb, bf, bb, s_len, n_ctx, cast_weights, l):
    m = r.shape[0]
    nc = m // CHUNK
    nc_lat = s_len // CHUNK
    nc_ctx = n_ctx // CHUNK
    fwd = lambda i: (jnp.where(i < nc_ctx, nc_lat + i, i - nc_ctx), 0)
    bwd = lambda i: (nc - 1 - i, 0)
    fs = pl.BlockSpec((CHUNK, RWKV_WIDTH), fwd)
    bs = pl.BlockSpec((CHUNK, RWKV_WIDTH), bwd)
    cast_in, cast_out, cast_shapes = zip(*[_cast_slices(w, l, nc) for w in cast_weights])
    cast_bytes = sum(spec.block_shape[0] * spec.block_shape[1] * 6 for spec in cast_out)
    outs = pl.pallas_call(
        functools.partial(_scan_kernel, n_cast=len(cast_weights)),
        grid=(nc,),
        in_specs=[fs] * 6 + [bs] * 6 + list(cast_in),
        out_specs=[fs, bs] + list(cast_out),
        out_shape=[jax.ShapeDtypeStruct((m, RWKV_WIDTH), F32)] * 2 + list(cast_shapes),
        scratch_shapes=[pltpu.VMEM((2, N_PAIRS, PAIR, PAIR), F32)],
        compiler_params=pltpu.CompilerParams(
            dimension_semantics=("arbitrary",),
            vmem_limit_bytes=_vmem(2 * 14 * CHUNK * RWKV_WIDTH * 4 + 2 * N_PAIRS * PAIR * PAIR * 4
                                   + 2 * cast_bytes + 24 * 1024 * 1024)),
        name="rwkv_scan",
    )(r, v, kk, lwf, kdf, bf, r, v, kk, lwb, kdb, bb, *cast_weights)
    return outs[0], outs[1], outs[2:]


def _out_kernel(att_ref, yf_ref, yb_ref, g_ref, bon_ref, x_ref, wa_ref, wr_ref, lnw_ref, lnb_ref,
                seg_ref, exp_ref, gpm_ref, gt1_ref, gmlp_ref, sc2_ref, sh2_ref, x1_ref, h2_ref):
    mix_att = _dot(att_ref[...], wa_ref[...])
    inv_n = 1.0 / RWKV_HEAD_DIM
    seg, expand = seg_ref[...], exp_ref[...]
    rows = att_ref.shape[0] // OUT_SUB_BLOCKS
    blocks = [pl.ds(n * rows, rows) for n in range(OUT_SUB_BLOCKS)]
    y = [yf_ref[b, :] + yb_ref[b, :] for b in blocks]
    s1 = [_bdot(v, seg) for v in y]
    yc = [v - _split_dot(s, expand) * inv_n for v, s in zip(y, s1)]
    s2 = [_bdot(c * c, seg) for c in yc]
    var = [_bdot(s, expand) * inv_n for s in s2]
    rec = [((c * lax.rsqrt(vr + GN_EPS) * lnw_ref[...] + lnb_ref[...]) + bon_ref[b, :]) * g_ref[b, :]
           for c, vr, b in zip(yc, var, blocks)]
    mix = mix_att + _bdot(jnp.concatenate(rec, axis=0), wr_ref[...])
    ms = jnp.mean(mix * mix, axis=-1, keepdims=True)
    x1 = x_ref[...] + gt1_ref[...] * (mix * lax.rsqrt(ms + NORM_EPS) * gpm_ref[...])
    x1_ref[...] = x1
    ms2 = jnp.mean(x1 * x1, axis=-1, keepdims=True)
    h2 = x1 * lax.rsqrt(ms2 + NORM_EPS) * gmlp_ref[...]
    h2_ref[...] = (h2 * (1.0 + sc2_ref[...]) + sh2_ref[...]).astype(BF16)


def _out_proj(att, yf, yb, g, bonus, x, w_out, lnw, lnb, seg, expand, gpm, gt1, gmlp, sc2, sh2, tm):
    s_len, d = x.shape
    assert ATT_WIDTH == RWKV_WIDTH and w_out.shape == (ATT_WIDTH + RWKV_WIDTH, d)
    row = lambda w: pl.BlockSpec((tm, w), lambda i: (i, 0))
    full = lambda arr: pl.BlockSpec(arr.shape, lambda i: (0,) * arr.ndim)
    w_half = lambda n: pl.BlockSpec((ATT_WIDTH, d), lambda i: (n, 0))
    consts = (lnw, lnb, seg, expand, gpm, gt1, gmlp, sc2, sh2)
    return pl.pallas_call(
        _out_kernel,
        grid=(s_len // tm,),
        in_specs=[row(ATT_WIDTH)] + [row(RWKV_WIDTH)] * 4 + [row(d), w_half(0), w_half(1)]
                 + [full(c) for c in consts],
        out_specs=[row(d), row(d)],
        out_shape=[jax.ShapeDtypeStruct((s_len, d), F32), jax.ShapeDtypeStruct((s_len, d), BF16)],
        compiler_params=pltpu.CompilerParams(
            dimension_semantics=("arbitrary",),
            vmem_limit_bytes=_vmem(2 * tm * (ATT_WIDTH * 2 + 4 * RWKV_WIDTH * 4 + d * 4 + d * 4 + d * 2)
                                   + 2 * 2 * d * d + 8 * tm * d * 4 + 4 * 1024 * 1024)),
        name="out_proj",
    )(att, yf, yb, g, bonus, x, w_out, w_out, *consts)


def _mlp_kernel(h_ref, w1_ref, w2_ref, x1_ref, g_ref, gt_ref, o_ref, acc_ref):
    f = pl.program_id(1)

    @pl.when(f == 0)
    def _():
        acc_ref[...] = jnp.zeros_like(acc_ref)

    u = jnp.maximum(_dot(h_ref[...], w1_ref[...]), 0.0)
    acc_ref[...] += _dot((u * u).astype(BF16), w2_ref[...])

    @pl.when(f == pl.num_programs(1) - 1)
    def _():
        ff = acc_ref[...]
        ms = jnp.mean(ff * ff, axis=-1, keepdims=True)
        o_ref[...] = x1_ref[...] + gt_ref[...] * (ff * lax.rsqrt(ms + NORM_EPS) * g_ref[...])


def _mlp(h2, w1, w2, x1, gain, gt2, tm, tf):
    s_len, d = x1.shape
    dff = w1.shape[1]
    return pl.pallas_call(
        _mlp_kernel,
        grid=(s_len // tm, dff // tf),
        in_specs=[pl.BlockSpec((tm, d), lambda i, f: (i, 0)),
                  pl.BlockSpec((d, tf), lambda i, f: (0, f)),
                  pl.BlockSpec((tf, d), lambda i, f: (f, 0)),
                  pl.BlockSpec((tm, d), lambda i, f: (i, 0)),
                  pl.BlockSpec((1, d), lambda i, f: (0, 0)),
                  pl.BlockSpec((1, d), lambda i, f: (0, 0))],
        out_specs=pl.BlockSpec((tm, d), lambda i, f: (i, 0)),
        out_shape=jax.ShapeDtypeStruct((s_len, d), F32),
        scratch_shapes=[pltpu.VMEM((tm, d), F32)],
        compiler_params=pltpu.CompilerParams(
            dimension_semantics=("arbitrary", "arbitrary"),
            vmem_limit_bytes=_vmem(2 * (tm * d * 2 + 2 * d * tf * 2 + 2 * tm * d * 4)
                                   + tm * d * 4 + 3 * tm * tf * 4 + 4 * 1024 * 1024)),
        name="mlp",
    )(h2, w1, w2, x1, gain, gt2)


def _rope_tables(s_len, rows_padded):
    rows = s_len // GRID_W
    n_freq = ATT_HEAD_DIM // 4
    inv_freq = ROPE_BASE ** (-jnp.arange(n_freq, dtype=F32) / n_freq)
    ang_r = jnp.arange(rows, dtype=F32)[:, None] * inv_freq[None, :]
    ang_c = jnp.arange(GRID_W, dtype=F32)[:, None] * inv_freq[None, :]
    zr = jnp.zeros((rows, 2 * n_freq), F32)
    zc = jnp.zeros((GRID_W, 2 * n_freq), F32)
    pad = lambda t: jnp.pad(t, ((0, rows_padded - rows), (0, 0)))[:, None, :]
    row_cos = pad(jnp.concatenate([jnp.cos(ang_r), jnp.cos(ang_r), zr], axis=-1))
    row_sin = pad(jnp.concatenate([-jnp.sin(ang_r), jnp.sin(ang_r), zr], axis=-1))
    col_cos = jnp.concatenate([zc, jnp.cos(ang_c), jnp.cos(ang_c)], axis=-1)
    col_sin = jnp.concatenate([zc, -jnp.sin(ang_c), jnp.sin(ang_c)], axis=-1)
    return row_cos, row_sin, col_cos, col_sin


def _block_diag2(w):
    _, k, n = w.shape
    z = jnp.zeros((k, n), w.dtype)
    return jnp.concatenate([jnp.concatenate([w[0], z], axis=1), jnp.concatenate([z, w[1]], axis=1)], axis=0)


def _layer(x2, ctx2, c8, l, w_ada, b_ada, g_pre_mix, g_post_mix, g_pre_mlp, g_post_mlp, w_in, attn_sink,
           rwkv_mu_prev, rwkv_mu_next, rwkv_w0, rwkv_w2, rwkv_a0, rwkv_a2, rwkv_g2, rwkv_k_k, rwkv_k_a,
           rwkv_r_k, rwkv_ln_w, rwkv_ln_b, w_out, w_mlp_in, w_mlp_out):
    s_len, d = x2.shape
    n_ctx = ctx2.shape[0]
    row = lambda v: v.reshape(1, -1)

    mod = _ada(c8, w_ada[l], row(b_ada[l]))
    sh1, sc1, gt1, sh2, sc2, gt2 = [mod[:2, j * d:(j + 1) * d] for j in range(N_MOD)]

    tm_rows = 256
    h = _norm_mod(x2, ctx2, row(g_pre_mix[l]), sc1.reshape(2, 1, d), sh1.reshape(2, 1, d), tm_rows)

    m_all = s_len + n_ctx
    tm_proj = 768 if m_all % 768 == 0 else 256
    rope_tables = _rope_tables(s_len, (m_all // tm_proj) * (tm_proj // GRID_W))
    p_in = _proj(h, w_in[l].astype(BF16), rope_tables, s_len, tm_proj, w_in.shape[2] // 3,
                 N_Q_HEADS + N_KV_HEADS)

    att = _attention(p_in, attn_sink[l], s_len, n_ctx)

    head = jnp.arange(RWKV_WIDTH) // RWKV_HEAD_DIM
    seg = (head[:, None] == jnp.arange(LANES)[None, :]).astype(BF16)
    expand = seg.T
    streams = _rwkv_prep(
        p_in, row(rwkv_mu_prev[l]), row(rwkv_mu_next[l]),
        _block_diag2(rwkv_w2[l]).astype(BF16), rwkv_w0[l].reshape(1, -1),
        _block_diag2(rwkv_a2[l]).astype(BF16), rwkv_a0[l].reshape(1, -1),
        rwkv_g2[l].astype(BF16), row(rwkv_k_k[l]), row(rwkv_k_a[l]), rwkv_r_k[l].reshape(1, -1),
        seg, expand, s_len, n_ctx, tm_rows)
    r, v, kk, g, bonus, lwf, lwb, kdf, kdb, bf, bb = streams
    yf, yb, (w_out_b, w_mlp_in_b, w_mlp_out_b) = _rwkv_scan(
        r, v, kk, lwf, lwb, kdf, kdb, bf, bb, s_len, n_ctx, (w_out, w_mlp_in, w_mlp_out), l)

    x1, h2 = _out_proj(att, yf, yb, g, bonus, x2, w_out_b,
                       row(rwkv_ln_w[l]), row(rwkv_ln_b[l]), seg, expand,
                       row(g_post_mix[l]), gt1[0:1], row(g_pre_mlp[l]), sc2[0:1], sh2[0:1], 512)
    return _mlp(h2, w_mlp_in_b, w_mlp_out_b, x1, row(g_post_mlp[l]), gt2[0:1], 512, 1024)


def kernel(x, c, ctx, c_ctx, w_ada, b_ada, g_pre_mix, g_post_mix, g_pre_mlp, g_post_mlp, w_in, attn_sink,
           rwkv_mu_prev, rwkv_mu_next, rwkv_w0, rwkv_w2, rwkv_a0, rwkv_a2, rwkv_g2, rwkv_k_k, rwkv_k_a,
           rwkv_r_k, rwkv_ln_w, rwkv_ln_b, w_out, w_mlp_in, w_mlp_out):
    b, s_len, d = x.shape
    depth = w_ada.shape[0]
    assert b == 1 and depth == 1, "single sample, single layer (context stream is not carried to a next layer)"
    c8 = jnp.zeros((8, d), F32).at[0].set(c[0]).at[1].set(c_ctx)
    out = _layer(x[0], ctx[0], c8, 0, w_ada, b_ada, g_pre_mix, g_post_mix, g_pre_mlp, g_post_mlp, w_in,
                 attn_sink, rwkv_mu_prev, rwkv_mu_next, rwkv_w0, rwkv_w2, rwkv_a0, rwkv_a2, rwkv_g2,
                 rwkv_k_k, rwkv_k_a, rwkv_r_k, rwkv_ln_w, rwkv_ln_b, w_out, w_mlp_in, w_mlp_out)
    return out[None]
```

```python
import functools

import jax
import jax.numpy as jnp
from jax import lax
from jax.experimental import pallas as pl
from jax.experimental.pallas import tpu as pltpu

F32 = jnp.float32
BF16 = jnp.bfloat16

GRID_W = 64
ATT_HEAD_DIM = 128
N_Q_HEADS = 8
N_KV_HEADS = 2
GQA_GROUP = N_Q_HEADS // N_KV_HEADS
ATT_WIDTH = N_Q_HEADS * ATT_HEAD_DIM
KV_WIDTH = N_KV_HEADS * ATT_HEAD_DIM
ATT_IN = ATT_WIDTH + 2 * KV_WIDTH
WINDOW = 128
BLOCK = 128
ROPE_BASE = 10000.0
RWKV_HEAD_DIM = 64
RWKV_WIDTH = 1024
N_RWKV_HEADS = RWKV_WIDTH // RWKV_HEAD_DIM
LORA_W = 64
LORA_A = 64
LORA_G = 128
N_MOD = 6
NORM_EPS = 1e-6
GN_EPS = 64e-5
DECAY_SCALE = 0.6065306597126334

LANES = 128
CHUNK = 64
PAIR = 2 * RWKV_HEAD_DIM
N_PAIRS = RWKV_WIDTH // PAIR
PREP_OUT_DTYPES = (BF16, BF16, BF16, BF16, BF16, F32, F32, BF16, BF16, BF16, BF16)
HALO_ROWS = 16
OUT_SUB_BLOCKS = 4
VMEM_CAP = 56 * 1024 * 1024


def _vmem(nbytes):
    return int(min(VMEM_CAP, max(16 * 1024 * 1024, nbytes)))


def _dot(a, b):
    return jnp.dot(a, b, preferred_element_type=F32)


def _dot_nt(a, b):
    return lax.dot_general(a, b, (((1,), (1,)), ((), ())), preferred_element_type=F32)


def _dot_tn(a, b):
    return lax.dot_general(a, b, (((0,), (0,)), ((), ())), preferred_element_type=F32)


def _bdot(a, b):
    return _dot(a.astype(BF16), b.astype(BF16))


def _bdot_nt(a, b):
    return _dot_nt(a.astype(BF16), b.astype(BF16))


def _bdot_tn(a, b):
    return _dot_tn(a.astype(BF16), b.astype(BF16))


def _split_dot(x, m):
    hi = x.astype(BF16)
    lo = (x - hi.astype(F32)).astype(BF16)
    return _dot(hi, m) + _dot(lo, m)


def _sigmoid(z):
    return 0.5 * jnp.tanh(0.5 * z) + 0.5


def _ada_kernel(c_ref, w_ref, b_ref, o_ref):
    c = c_ref[...]
    s = c * jax.nn.sigmoid(c)
    o_ref[...] = _bdot(s, w_ref[...]) + b_ref[...]


def _ada(c8, w_ada, b_ada):
    d, n = w_ada.shape
    tn = 1024
    return pl.pallas_call(
        _ada_kernel,
        grid=(n // tn,),
        in_specs=[pl.BlockSpec((8, d), lambda j: (0, 0)),
                  pl.BlockSpec((d, tn), lambda j: (0, j)),
                  pl.BlockSpec((1, tn), lambda j: (0, j))],
        out_specs=pl.BlockSpec((8, tn), lambda j: (0, j)),
        out_shape=jax.ShapeDtypeStruct((8, n), F32),
        compiler_params=pltpu.CompilerParams(
            dimension_semantics=("arbitrary",),
            vmem_limit_bytes=_vmem(2 * d * tn * 4 + 4 * 1024 * 1024)),
        name="ada",
    )(c8, w_ada, b_ada)


def _norm_mod_kernel(x_ref, c_ref, g_ref, sc_ref, sh_ref, o_ref, *, n_lat_tiles):
    def emit(src_ref):
        x = src_ref[...]
        ms = jnp.mean(x * x, axis=-1, keepdims=True)
        y = x * lax.rsqrt(ms + NORM_EPS) * g_ref[...]
        o_ref[...] = (y * (1.0 + sc_ref[0]) + sh_ref[0]).astype(BF16)

    i = pl.program_id(0)
    pl.when(i < n_lat_tiles)(lambda: emit(x_ref))
    pl.when(i >= n_lat_tiles)(lambda: emit(c_ref))


def _norm_mod(x2, ctx2, gain, sc2, sh2, tm):
    s_len, d = x2.shape
    n_lat_tiles = s_len // tm
    n_tiles = n_lat_tiles + ctx2.shape[0] // tm
    sel = lambda i: (jnp.where(i >= n_lat_tiles, 1, 0), 0, 0)
    return pl.pallas_call(
        functools.partial(_norm_mod_kernel, n_lat_tiles=n_lat_tiles),
        grid=(n_tiles,),
        in_specs=[pl.BlockSpec((tm, d), lambda i: (jnp.minimum(i, n_lat_tiles - 1), 0)),
                  pl.BlockSpec((tm, d), lambda i: (jnp.maximum(i - n_lat_tiles, 0), 0)),
                  pl.BlockSpec((1, d), lambda i: (0, 0)),
                  pl.BlockSpec((1, 1, d), sel),
                  pl.BlockSpec((1, 1, d), sel)],
        out_specs=pl.BlockSpec((tm, d), lambda i: (i, 0)),
        out_shape=jax.ShapeDtypeStruct((n_tiles * tm, d), BF16),
        compiler_params=pltpu.CompilerParams(
            dimension_semantics=("arbitrary",),
            vmem_limit_bytes=_vmem(2 * tm * d * 10 + 4 * 1024 * 1024)),
        name="norm_mod",
    )(x2, ctx2, gain, sc2, sh2)


def _rope_heads(acc, cos, sin, n_heads):
    lane = lax.broadcasted_iota(jnp.int32, (1, ATT_HEAD_DIM), 1)
    first = (lane % 64) < 32
    outs = []
    for h in range(n_heads):
        xh = acc[:, h * ATT_HEAD_DIM:(h + 1) * ATT_HEAD_DIM]
        partner = jnp.where(first, pltpu.roll(xh, 96, 1), pltpu.roll(xh, 32, 1))
        outs.append(xh * cos + partner * sin)
    return outs


def _proj_kernel(a_ref, w_ref, rcos_ref, rsin_ref, ccos_ref, csin_ref, o_ref, *, rope_heads, s_len):
    acc = _dot(a_ref[...], w_ref[...])
    tm = a_ref.shape[0]
    i, j = pl.program_id(0), pl.program_id(1)
    row = i * tm + lax.broadcasted_iota(jnp.int32, (tm, 1), 0)
    rotate = row < jnp.where(j == 0, s_len, 0)
    cos = jnp.where(rotate, (rcos_ref[...] + ccos_ref[...][None]).reshape(tm, ATT_HEAD_DIM), 1.0)
    sin = jnp.where(rotate, (rsin_ref[...] + csin_ref[...][None]).reshape(tm, ATT_HEAD_DIM), 0.0)
    for h, r in enumerate(_rope_heads(acc, cos, sin, rope_heads)):
        o_ref[:, h * ATT_HEAD_DIM:(h + 1) * ATT_HEAD_DIM] = r.astype(o_ref.dtype)
    o_ref[:, rope_heads * ATT_HEAD_DIM:] = acc[:, rope_heads * ATT_HEAD_DIM:].astype(o_ref.dtype)


def _proj(a, w, rope_tables, s_len, tm, tn, rope_heads):
    m, k = a.shape
    n = w.shape[1]
    n_tiles = m // tm
    assert rope_heads * ATT_HEAD_DIM <= tn and n % tn == 0 and tm % GRID_W == 0
    g = tm // GRID_W
    row_spec = pl.BlockSpec((g, 1, ATT_HEAD_DIM), lambda i, j: (i, 0, 0))
    col_spec = pl.BlockSpec((GRID_W, ATT_HEAD_DIM), lambda i, j: (0, 0))
    return pl.pallas_call(
        functools.partial(_proj_kernel, rope_heads=rope_heads, s_len=s_len),
        grid=(n_tiles, n // tn),
        in_specs=[pl.BlockSpec((tm, k), lambda i, j: (i, 0)),
                  pl.BlockSpec((k, tn), lambda i, j: (0, j)),
                  row_spec, row_spec, col_spec, col_spec],
        out_specs=pl.BlockSpec((tm, tn), lambda i, j: (i, j)),
        out_shape=jax.ShapeDtypeStruct((m, n), BF16),
        compiler_params=pltpu.CompilerParams(
            dimension_semantics=("arbitrary", "arbitrary"),
            vmem_limit_bytes=_vmem(2 * (tm * k * 2 + k * tn * 2 + tm * tn * 2 + 2 * tm * 512)
                                   + tm * tn * 8 + 4 * 1024 * 1024)),
        name="proj_in",
    )(a, w, *rope_tables)


def _attn_kernel(sink_ref, q_ref, kp_ref, kc_ref, kn_ref, vp_ref, vc_ref, vn_ref,
                 kx_ref, vx_ref, o_ref, *, nb, n_ctx):
    i = pl.program_id(0)
    scale = ATT_HEAD_DIM ** -0.5
    rows = GQA_GROUP * BLOCK
    span = 3 * BLOCK
    ncol = span + n_ctx
    row = lax.broadcasted_iota(jnp.int32, (rows, ncol), 0) % BLOCK
    col = lax.broadcasted_iota(jnp.int32, (rows, ncol), 1)
    rel = col - BLOCK - row
    lo = jnp.where(i > 0, 0, BLOCK)
    hi = jnp.where(i < nb - 1, span, 2 * BLOCK)
    band = (jnp.abs(rel) <= WINDOW) & (col >= lo) & (col < hi)
    valid = band | (col >= span)
    heads = range(N_KV_HEADS)
    cols = [slice(hk * ATT_HEAD_DIM, (hk + 1) * ATT_HEAD_DIM) for hk in heads]
    q = [(jnp.concatenate(
        [q_ref[:, (hk * GQA_GROUP + g) * ATT_HEAD_DIM:(hk * GQA_GROUP + g + 1) * ATT_HEAD_DIM]
         for g in range(GQA_GROUP)], axis=0).astype(F32) * scale).astype(BF16) for hk in heads]
    k_all = [jnp.concatenate([kp_ref[:, ks], kc_ref[:, ks], kn_ref[:, ks], kx_ref[:, ks]], axis=0).astype(BF16)
             for ks in cols]
    v_all = [jnp.concatenate([vp_ref[:, ks], vc_ref[:, ks], vn_ref[:, ks], vx_ref[:, ks]], axis=0).astype(BF16)
             for ks in cols]
    sink = [jnp.concatenate([jnp.full((BLOCK, 1), sink_ref[hk * GQA_GROUP + g], F32) for g in range(GQA_GROUP)],
                            axis=0) for hk in heads]
    s = [jnp.where(valid, _dot_nt(qh, kh), -jnp.inf) for qh, kh in zip(q, k_all)]
    m = [jnp.maximum(jnp.max(sh, axis=-1, keepdims=True), sk) for sh, sk in zip(s, sink)]
    e = [jnp.exp(sh - mh) for sh, mh in zip(s, m)]
    denom = [jnp.sum(eh, axis=-1, keepdims=True) + jnp.exp(sk - mh) for eh, sk, mh in zip(e, sink, m)]
    o = [_dot(eh.astype(BF16), vh) * (1.0 / dh) for eh, vh, dh in zip(e, v_all, denom)]
    for hk in heads:
        for g in range(GQA_GROUP):
            h = hk * GQA_GROUP + g
            o_ref[:, h * ATT_HEAD_DIM:(h + 1) * ATT_HEAD_DIM] = o[hk][g * BLOCK:(g + 1) * BLOCK].astype(BF16)


def _attention(qkv, sink, s_len, n_ctx):
    nb = s_len // BLOCK
    kcol = ATT_WIDTH // KV_WIDTH
    vcol = kcol + 1
    ctx_blk = s_len // n_ctx
    prev = lambda i: jnp.maximum(i - 1, 0)
    nxt = lambda i: jnp.minimum(i + 1, nb - 1)
    kv = lambda rowf, c: pl.BlockSpec((BLOCK, KV_WIDTH), lambda i: (rowf(i), c))
    same = lambda i: i
    return pl.pallas_call(
        functools.partial(_attn_kernel, nb=nb, n_ctx=n_ctx),
        grid=(nb,),
        in_specs=[pl.BlockSpec(memory_space=pltpu.SMEM),
                  pl.BlockSpec((BLOCK, ATT_WIDTH), lambda i: (i, 0)),
                  kv(prev, kcol), kv(same, kcol), kv(nxt, kcol),
                  kv(prev, vcol), kv(same, vcol), kv(nxt, vcol),
                  pl.BlockSpec((n_ctx, KV_WIDTH), lambda i: (ctx_blk, kcol)),
                  pl.BlockSpec((n_ctx, KV_WIDTH), lambda i: (ctx_blk, vcol))],
        out_specs=pl.BlockSpec((BLOCK, ATT_WIDTH), lambda i: (i, 0)),
        out_shape=jax.ShapeDtypeStruct((s_len, ATT_WIDTH), BF16),
        compiler_params=pltpu.CompilerParams(
            dimension_semantics=("arbitrary",),
            vmem_limit_bytes=_vmem(32 * 1024 * 1024)),
        name="attention",
    )(sink, qkv, qkv, qkv, qkv, qkv, qkv, qkv, qkv, qkv)


def _prep_kernel(x_ref, xp_ref, xn_ref, mup_ref, mun_ref, w2_ref, w0_ref, a2_ref, a0_ref, g2_ref,
                 kk_ref, ka_ref, rk_ref, seg_ref, exp_ref,
                 r_o, v_o, kk_o, g_o, bon_o, lwf_o, lwb_o, kdf_o, kdb_o, bf_o, bb_o,
                 *, tm, start_tiles, end_tiles):
    i = pl.program_id(0)
    r_w = RWKV_WIDTH
    x = x_ref[:, ATT_IN:].astype(F32)
    is_start = functools.reduce(jnp.logical_or, [i == t for t in start_tiles])
    is_end = functools.reduce(jnp.logical_or, [i == t for t in end_tiles])
    keep_prev = jnp.full((1, 1), jnp.where(is_start, 0, 1), jnp.int32) == 1
    keep_next = jnp.full((1, 1), jnp.where(is_end, 0, 1), jnp.int32) == 1
    prev_row = jnp.where(keep_prev, xp_ref[HALO_ROWS - 1:HALO_ROWS, ATT_IN:].astype(F32), 0.0)
    next_row = jnp.where(keep_next, xn_ref[0:1, ATT_IN:].astype(F32), 0.0)
    xp = pltpu.roll(x, 1, 0)
    xn = pltpu.roll(x, tm - 1, 0)
    row8 = lax.broadcasted_iota(jnp.int32, (8, 1), 0)
    xp = jnp.concatenate([jnp.where(row8 == 0, prev_row, xp[:8]), xp[8:]], axis=0)
    xn = jnp.concatenate([xn[:tm - 8], jnp.where(row8 == 7, next_row, xn[tm - 8:])], axis=0)
    mup = mup_ref[...]
    mun = mun_ref[...]
    s = x * (1.0 - mup - mun) + xp * mup + xn * mun

    r = s[:, :r_w]
    k = s[:, r_w:2 * r_w]
    v = s[:, 2 * r_w:3 * r_w]
    o1 = 3 * r_w
    o2 = o1 + 2 * LORA_W
    o3 = o2 + 2 * LORA_A
    w_lo = s[:, o1:o2]
    a_lo = s[:, o2:o3]
    g_lo = s[:, o3:]

    tw = jnp.tanh(w0_ref[...] + _bdot(jnp.tanh(w_lo), w2_ref[...]))
    ta = jnp.tanh(a0_ref[...] + _bdot(a_lo, a2_ref[...]))
    lw = (-0.5 * DECAY_SCALE) * tw - 0.5 * DECAY_SCALE
    g = _bdot(_sigmoid(g_lo), g2_ref[...])

    kkf = k * kk_ref[...]
    ssq = _split_dot(_bdot(kkf * kkf, seg_ref[...]), exp_ref[...])
    kk = kkf * jnp.minimum(lax.rsqrt(ssq), 1e12)
    kkh = k * (0.5 * ka_ref[...])
    k1 = k - kkh
    kd_f = k1 + kkh * ta[:, :r_w]
    kd_b = k1 + kkh * ta[:, r_w:]
    k_bonus = 0.5 * (kd_f + kd_b)
    bonus = _split_dot(_bdot(r * k_bonus * rk_ref[...], seg_ref[...]), exp_ref[...]) * v
    kk_half = 0.5 * kk

    r_o[...] = r.astype(r_o.dtype)
    v_o[...] = v.astype(v_o.dtype)
    kk_o[...] = kk.astype(kk_o.dtype)
    g_o[...] = g.astype(g_o.dtype)
    bon_o[...] = bonus.astype(bon_o.dtype)
    lwf_o[...] = lw[:, :r_w]
    lwb_o[...] = lw[:, r_w:]
    kdf_o[...] = kd_f.astype(kdf_o.dtype)
    kdb_o[...] = kd_b.astype(kdb_o.dtype)
    bf_o[...] = (kk_half + kk_half * ta[:, :r_w]).astype(bf_o.dtype)
    bb_o[...] = (kk_half + kk_half * ta[:, r_w:]).astype(bb_o.dtype)


def _rwkv_prep(slab, mup, mun, w2blk, w0cat, a2blk, a0cat, g2, k_k, k_a, r_k, seg, expand, s_len, n_ctx, tm):
    m, width = slab.shape
    nt = m // tm
    hb = tm // HALO_ROWS
    nblk8 = m // HALO_ROWS
    start_tiles = (0, s_len // tm)
    end_tiles = (s_len // tm - 1, nt - 1)
    full = lambda arr: pl.BlockSpec(arr.shape, lambda i: (0,) * arr.ndim)
    consts = (mup, mun, w2blk, w0cat, a2blk, a0cat, g2, k_k, k_a, r_k, seg, expand)
    out_spec = pl.BlockSpec((tm, RWKV_WIDTH), lambda i: (i, 0))
    n_out = len(PREP_OUT_DTYPES)
    return pl.pallas_call(
        functools.partial(_prep_kernel, tm=tm, start_tiles=start_tiles, end_tiles=end_tiles),
        grid=(nt,),
        in_specs=[pl.BlockSpec((tm, width), lambda i: (i, 0)),
                  pl.BlockSpec((HALO_ROWS, width), lambda i: (jnp.maximum(i * hb - 1, 0), 0)),
                  pl.BlockSpec((HALO_ROWS, width), lambda i: (jnp.minimum((i + 1) * hb, nblk8 - 1), 0))]
                 + [full(c) for c in consts],
        out_specs=[out_spec] * n_out,
        out_shape=[jax.ShapeDtypeStruct((m, RWKV_WIDTH), dt) for dt in PREP_OUT_DTYPES],
        compiler_params=pltpu.CompilerParams(
            dimension_semantics=("arbitrary",),
            vmem_limit_bytes=_vmem(2 * tm * width * 4 + 2 * n_out * tm * RWKV_WIDTH * 4
                                   + 12 * tm * RWKV_WIDTH * 4 + 8 * 1024 * 1024)),
        name="rwkv_prep",
    )(slab, slab, slab, *consts)


def _stack(x, m0):
    return jnp.concatenate([jnp.where(m0, x, 0.0), jnp.where(m0, 0.0, x)], axis=0)


def _scan_streams(r_ref, v_ref, kk_ref, lw_ref, kd_ref, b_ref, reverse):
    L = CHUNK
    lw = lw_ref[...]
    ti = lax.broadcasted_iota(jnp.int32, (L, L), 0)
    tj = lax.broadcasted_iota(jnp.int32, (L, L), 1)
    tri = jnp.where((ti <= tj) if reverse else (ti >= tj), 1.0, 0.0).astype(BF16)
    lw_hi = lw.astype(BF16)
    lw_lo = (lw - lw_hi.astype(F32)).astype(BF16)
    c = _dot(tri, lw_hi) + _dot(tri, lw_lo)
    p_cum = jnp.exp(c)
    p_inv = jnp.exp(-c)
    p_prev = jnp.exp(c - lw)
    last = 0 if reverse else L - 1
    p_end = p_cum[last:last + 1, :]
    b_t = b_ref[...] * p_inv
    k_t = kd_ref[...] * p_inv
    gi = lax.broadcasted_iota(jnp.int32, (2 * L, 2 * L), 0)
    gj = lax.broadcasted_iota(jnp.int32, (2 * L, 2 * L), 1)
    same = (gi // L) == (gj // L)
    strict = same & ((gi < gj) if reverse else (gi > gj))
    incl = same & ((gi <= gj) if reverse else (gi >= gj))
    return dict(a=-kk_ref[...] * p_prev, r=r_ref[...] * p_cum, b=b_t, k=k_t, bh=b_t * p_end, kh=k_t * p_end,
                v=v_ref[...], p_end=p_end, strict=strict, incl=incl)


def _scan_kernel(rf, vf, kkf, lwf, kdf, bf, rb, vb, kkb, lwb, kdb, bb, *rest, n_cast):
    cast_in, (yf, yb), cast_out, s_ref = rest[:n_cast], rest[n_cast:n_cast + 2], rest[n_cast + 2:-1], rest[-1]

    @pl.when(pl.program_id(0) == 0)
    def _():
        s_ref[...] = jnp.zeros_like(s_ref)

    for src, dst in zip(cast_in, cast_out):
        dst[...] = src[...].astype(dst.dtype)

    L = CHUNK
    streams = (_scan_streams(rf, vf, kkf, lwf, kdf, bf, False), _scan_streams(rb, vb, kkb, lwb, kdb, bb, True))
    y_refs = (yf, yb)
    m0 = lax.broadcasted_iota(jnp.int32, (L, PAIR), 1) < RWKV_HEAD_DIM
    chains = [(d, p) for d in range(2) for p in range(N_PAIRS)]

    def stacked(name, d, p):
        return _stack(streams[d][name][:, p * PAIR:(p + 1) * PAIR], m0)

    ar = [jnp.concatenate([stacked("a", d, p), stacked("r", d, p)], axis=0).astype(BF16) for d, p in chains]
    bk = [jnp.concatenate([stacked("b", d, p), stacked("k", d, p)], axis=0).astype(BF16) for d, p in chains]
    v_s = [stacked("v", d, p).astype(BF16) for d, p in chains]
    state = [s_ref[d, p] for d, p in chains]
    g = [_dot_nt(x, z) for x, z in zip(ar, bk)]
    xs = [_dot_nt(x, s.astype(BF16)) for x, s in zip(ar, state)]
    a_ab, a_kv, a_rb = [], [], []
    for (d, _), gc in zip(chains, g):
        strict, incl = streams[d]["strict"], streams[d]["incl"]
        a_ab.append(jnp.where(strict, gc[:2 * L, :2 * L], 0.0).astype(BF16))
        a_kv.append(jnp.concatenate([jnp.where(strict, gc[:2 * L, 2 * L:], 0.0),
                                     jnp.where(incl, gc[2 * L:, 2 * L:], 0.0)], axis=0).astype(BF16))
        a_rb.append(jnp.where(incl, gc[2 * L:, :2 * L], 0.0).astype(BF16))
    av = [_dot(a, v) for a, v in zip(a_kv, v_s)]
    w = [x[:2 * L] + y[:2 * L] for x, y in zip(xs, av)]
    am = a_ab
    w = [wc + _dot(a, wc.astype(BF16)) for a, wc in zip(am, w)]
    for _ in range(5):
        am = [_dot(a, a).astype(BF16) for a in am]
        w = [wc + _dot(a, wc.astype(BF16)) for a, wc in zip(am, w)]
    sa = [wc.astype(BF16) for wc in w]
    ys = [x[2 * L:] + y[2 * L:] + _dot(a, s) for x, y, a, s in zip(xs, av, a_rb, sa)]
    for (d, p), yc in zip(chains, ys):
        y_refs[d][:, p * PAIR:(p + 1) * PAIR] = yc[:L] + yc[L:]
    for (d, p), s_old, sac, vc in zip(chains, state, sa, v_s):
        bhkh = jnp.concatenate([stacked("bh", d, p), stacked("kh", d, p)], axis=0).astype(BF16)
        upd = _dot_tn(jnp.concatenate([sac, vc], axis=0), bhkh)
        s_ref[d, p] = s_old * streams[d]["p_end"][:, p * PAIR:(p + 1) * PAIR] + upd


def _cast_slices(w, l, n_steps):
    _, n_rows, n_cols = w.shape
    n_blk = 1
    while n_blk * 2 <= n_steps and n_rows % (n_blk * 2) == 0 and (n_rows // (n_blk * 2)) % 16 == 0:
        n_blk *= 2
    rows = n_rows // n_blk
    step = lambda i: jnp.minimum(i, n_blk - 1)
    return (pl.BlockSpec((None, rows, n_cols), lambda i: (l, step(i), 0)),
            pl.BlockSpec((rows, n_cols), lambda i: (step(i), 0)),
            jax.ShapeDtypeStruct((n_rows, n_cols), BF16))


def _rwkv_scan(r, v, kk, lwf, lwb, kdf, kdb, bf, bb, s_len, n_ctx, cast_weights, l):
    m = r.shape[0]
    nc = m // CHUNK
    nc_lat = s_len // CHUNK
    nc_ctx = n_ctx // CHUNK
    fwd = lambda i: (jnp.where(i < nc_ctx, nc_lat + i, i - nc_ctx), 0)
    bwd = lambda i: (nc - 1 - i, 0)
    fs = pl.BlockSpec((CHUNK, RWKV_WIDTH), fwd)
    bs = pl.BlockSpec((CHUNK, RWKV_WIDTH), bwd)
    cast_in, cast_out, cast_shapes = zip(*[_cast_slices(w, l, nc) for w in cast_weights])
    cast_bytes = sum(spec.block_shape[0] * spec.block_shape[1] * 6 for spec in cast_out)
    outs = pl.pallas_call(
        functools.partial(_scan_kernel, n_cast=len(cast_weights)),
        grid=(nc,),
        in_specs=[fs] * 6 + [bs] * 6 + list(cast_in),
        out_specs=[fs, bs] + list(cast_out),
        out_shape=[jax.ShapeDtypeStruct((m, RWKV_WIDTH), F32)] * 2 + list(cast_shapes),
        scratch_shapes=[pltpu.VMEM((2, N_PAIRS, PAIR, PAIR), F32)],
        compiler_params=pltpu.CompilerParams(
            dimension_semantics=("arbitrary",),
            vmem_limit_bytes=_vmem(2 * 14 * CHUNK * RWKV_WIDTH * 4 + 2 * N_PAIRS * PAIR * PAIR * 4
                                   + 2 * cast_bytes + 24 * 1024 * 1024)),
        name="rwkv_scan",
    )(r, v, kk, lwf, kdf, bf, r, v, kk, lwb, kdb, bb, *cast_weights)
    return outs[0], outs[1], outs[2:]


def _out_kernel(att_ref, yf_ref, yb_ref, g_ref, bon_ref, x_ref, wa_ref, wr_ref, lnw_ref, lnb_ref,
                seg_ref, exp_ref, gpm_ref, gt1_ref, gmlp_ref, sc2_ref, sh2_ref, x1_ref, h2_ref):
    mix_att = _dot(att_ref[...], wa_ref[...])
    inv_n = 1.0 / RWKV_HEAD_DIM
    seg, expand = seg_ref[...], exp_ref[...]
    rows = att_ref.shape[0] // OUT_SUB_BLOCKS
    blocks = [pl.ds(n * rows, rows) for n in range(OUT_SUB_BLOCKS)]
    y = [yf_ref[b, :] + yb_ref[b, :] for b in blocks]
    s1 = [_bdot(v, seg) for v in y]
    yc = [v - _split_dot(s, expand) * inv_n for v, s in zip(y, s1)]
    s2 = [_bdot(c * c, seg) for c in yc]
    var = [_bdot(s, expand) * inv_n for s in s2]
    rec = [((c * lax.rsqrt(vr + GN_EPS) * lnw_ref[...] + lnb_ref[...]) + bon_ref[b, :]) * g_ref[b, :]
           for c, vr, b in zip(yc, var, blocks)]
    mix = mix_att + _bdot(jnp.concatenate(rec, axis=0), wr_ref[...])
    ms = jnp.mean(mix * mix, axis=-1, keepdims=True)
    x1 = x_ref[...] + gt1_ref[...] * (mix * lax.rsqrt(ms + NORM_EPS) * gpm_ref[...])
    x1_ref[...] = x1
    ms2 = jnp.mean(x1 * x1, axis=-1, keepdims=True)
    h2 = x1 * lax.rsqrt(ms2 + NORM_EPS) * gmlp_ref[...]
    h2_ref[...] = (h2 * (1.0 + sc2_ref[...]) + sh2_ref[...]).astype(BF16)


def _out_proj(att, yf, yb, g, bonus, x, w_out, lnw, lnb, seg, expand, gpm, gt1, gmlp, sc2, sh2, tm):
    s_len, d = x.shape
    assert ATT_WIDTH == RWKV_WIDTH and w_out.shape == (ATT_WIDTH + RWKV_WIDTH, d)
    row = lambda w: pl.BlockSpec((tm, w), lambda i: (i, 0))
    full = lambda arr: pl.BlockSpec(arr.shape, lambda i: (0,) * arr.ndim)
    w_half = lambda n: pl.BlockSpec((ATT_WIDTH, d), lambda i: (n, 0))
    consts = (lnw, lnb, seg, expand, gpm, gt1, gmlp, sc2, sh2)
    return pl.pallas_call(
        _out_kernel,
        grid=(s_len // tm,),
        in_specs=[row(ATT_WIDTH)] + [row(RWKV_WIDTH)] * 4 + [row(d), w_half(0), w_half(1)]
                 + [full(c) for c in consts],
        out_specs=[row(d), row(d)],
        out_shape=[jax.ShapeDtypeStruct((s_len, d), F32), jax.ShapeDtypeStruct((s_len, d), BF16)],
        compiler_params=pltpu.CompilerParams(
            dimension_semantics=("arbitrary",),
            vmem_limit_bytes=_vmem(2 * tm * (ATT_WIDTH * 2 + 4 * RWKV_WIDTH * 4 + d * 4 + d * 4 + d * 2)
                                   + 2 * 2 * d * d + 8 * tm * d * 4 + 4 * 1024 * 1024)),
        name="out_proj",
    )(att, yf, yb, g, bonus, x, w_out, w_out, *consts)


def _mlp_kernel(h_ref, w1_ref, w2_ref, x1_ref, g_ref, gt_ref, o_ref, acc_ref):
    f = pl.program_id(1)

    @pl.when(f == 0)
    def _():
        acc_ref[...] = jnp.zeros_like(acc_ref)

    u = jnp.maximum(_dot(h_ref[...], w1_ref[...]), 0.0)
    acc_ref[...] += _dot((u * u).astype(BF16), w2_ref[...])

    @pl.when(f == pl.num_programs(1) - 1)
    def _():
        ff = acc_ref[...]
        ms = jnp.mean(ff * ff, axis=-1, keepdims=True)
        o_ref[...] = x1_ref[...] + gt_ref[...] * (ff * lax.rsqrt(ms + NORM_EPS) * g_ref[...])


def _mlp(h2, w1, w2, x1, gain, gt2, tm, tf):
    s_len, d = x1.shape
    dff = w1.shape[1]
    return pl.pallas_call(
        _mlp_kernel,
        grid=(s_len // tm, dff // tf),
        in_specs=[pl.BlockSpec((tm, d), lambda i, f: (i, 0)),
                  pl.BlockSpec((d, tf), lambda i, f: (0, f)),
                  pl.BlockSpec((tf, d), lambda i, f: (f, 0)),
                  pl.BlockSpec((tm, d), lambda i, f: (i, 0)),
                  pl.BlockSpec((1, d), lambda i, f: (0, 0)),
                  pl.BlockSpec((1, d), lambda i, f: (0, 0))],
        out_specs=pl.BlockSpec((tm, d), lambda i, f: (i, 0)),
        out_shape=jax.ShapeDtypeStruct((s_len, d), F32),
        scratch_shapes=[pltpu.VMEM((tm, d), F32)],
        compiler_params=pltpu.CompilerParams(
            dimension_semantics=("arbitrary", "arbitrary"),
            vmem_limit_bytes=_vmem(2 * (tm * d * 2 + 2 * d * tf * 2 + 2 * tm * d * 4)
                                   + tm * d * 4 + 3 * tm * tf * 4 + 4 * 1024 * 1024)),
        name="mlp",
    )(h2, w1, w2, x1, gain, gt2)


def _rope_tables(s_len, rows_padded):
    rows = s_len // GRID_W
    n_freq = ATT_HEAD_DIM // 4
    inv_freq = ROPE_BASE ** (-jnp.arange(n_freq, dtype=F32) / n_freq)
    ang_r = jnp.arange(rows, dtype=F32)[:, None] * inv_freq[None, :]
    ang_c = jnp.arange(GRID_W, dtype=F32)[:, None] * inv_freq[None, :]
    zr = jnp.zeros((rows, 2 * n_freq), F32)
    zc = jnp.zeros((GRID_W, 2 * n_freq), F32)
    pad = lambda t: jnp.pad(t, ((0, rows_padded - rows), (0, 0)))[:, None, :]
    row_cos = pad(jnp.concatenate([jnp.cos(ang_r), jnp.cos(ang_r), zr], axis=-1))
    row_sin = pad(jnp.concatenate([-jnp.sin(ang_r), jnp.sin(ang_r), zr], axis=-1))
    col_cos = jnp.concatenate([zc, jnp.cos(ang_c), jnp.cos(ang_c)], axis=-1)
    col_sin = jnp.concatenate([zc, -jnp.sin(ang_c), jnp.sin(ang_c)], axis=-1)
    return row_cos, row_sin, col_cos, col_sin


def _block_diag2(w):
    _, k, n = w.shape
    z = jnp.zeros((k, n), w.dtype)
    return jnp.concatenate([jnp.concatenate([w[0], z], axis=1), jnp.concatenate([z, w[1]], axis=1)], axis=0)


def _layer(x2, ctx2, c8, l, w_ada, b_ada, g_pre_mix, g_post_mix, g_pre_mlp, g_post_mlp, w_in, attn_sink,
           rwkv_mu_prev, rwkv_mu_next, rwkv_w0, rwkv_w2, rwkv_a0, rwkv_a2, rwkv_g2, rwkv_k_k, rwkv_k_a,
           rwkv_r_k, rwkv_ln_w, rwkv_ln_b, w_out, w_mlp_in, w_mlp_out):
    s_len, d = x2.shape
    n_ctx = ctx2.shape[0]
    row = lambda v: v.reshape(1, -1)

    mod = _ada(c8, w_ada[l], row(b_ada[l]))
    sh1, sc1, gt1, sh2, sc2, gt2 = [mod[:2, j * d:(j + 1) * d] for j in range(N_MOD)]

    tm_rows = 256
    h = _norm_mod(x2, ctx2, row(g_pre_mix[l]), sc1.reshape(2, 1, d), sh1.reshape(2, 1, d), tm_rows)

    m_all = s_len + n_ctx
    tm_proj = 768 if m_all % 768 == 0 else 256
    rope_tables = _rope_tables(s_len, (m_all // tm_proj) * (tm_proj // GRID_W))
    p_in = _proj(h, w_in[l].astype(BF16), rope_tables, s_len, tm_proj, w_in.shape[2] // 3,
                 N_Q_HEADS + N_KV_HEADS)

    att = _attention(p_in, attn_sink[l], s_len, n_ctx)

    head = jnp.arange(RWKV_WIDTH) // RWKV_HEAD_DIM
    seg = (head[:, None] == jnp.arange(LANES)[None, :]).astype(BF16)
    expand = seg.T
    streams = _rwkv_prep(
        p_in, row(rwkv_mu_prev[l]), row(rwkv_mu_next[l]),
        (0.5 * _block_diag2(rwkv_w2[l])).astype(BF16), 0.5 * rwkv_w0[l].reshape(1, -1),
        (0.5 * _block_diag2(rwkv_a2[l])).astype(BF16), 0.5 * rwkv_a0[l].reshape(1, -1),
        rwkv_g2[l].astype(BF16), row(rwkv_k_k[l]), row(rwkv_k_a[l]), rwkv_r_k[l].reshape(1, -1),
        seg, expand, s_len, n_ctx, tm_rows)
    r, v, kk, g, bonus, lwf, lwb, kdf, kdb, bf, bb = streams
    yf, yb, (w_out_b, w_mlp_in_b, w_mlp_out_b) = _rwkv_scan(
        r, v, kk, lwf, lwb, kdf, kdb, bf, bb, s_len, n_ctx, (w_out, w_mlp_in, w_mlp_out), l)

    x1, h2 = _out_proj(att, yf, yb, g, bonus, x2, w_out_b,
                       row(rwkv_ln_w[l]), row(rwkv_ln_b[l]), seg, expand,
                       row(g_post_mix[l]), gt1[0:1], row(g_pre_mlp[l]), sc2[0:1], sh2[0:1], 512)
    return _mlp(h2, w_mlp_in_b, w_mlp_out_b, x1, row(g_post_mlp[l]), gt2[0:1], 512, 1024)


def kernel(x, c, ctx, c_ctx, w_ada, b_ada, g_pre_mix, g_post_mix, g_pre_mlp, g_post_mlp, w_in, attn_sink,
           rwkv_mu_prev, rwkv_mu_next, rwkv_w0, rwkv_w2, rwkv_a0, rwkv_a2, rwkv_g2, rwkv_k_k, rwkv_k_a,
           rwkv_r_k, rwkv_ln_w, rwkv_ln_b, w_out, w_mlp_in, w_mlp_out):
    b, s_len, d = x.shape
    depth = w_ada.shape[0]
    assert b == 1 and depth == 1, "single sample, single layer (context stream is not carried to a next layer)"
    c8 = jnp.zeros((8, d), F32).at[0].set(c[0]).at[1].set(c_ctx)
    out = _layer(x[0], ctx[0], c8, 0, w_ada, b_ada, g_pre_mix, g_post_mix, g_pre_mlp, g_post_mlp, w_in,
                 attn_sink, rwkv_mu_prev, rwkv_mu_next, rwkv_w0, rwkv_w2, rwkv_a0, rwkv_a2, rwkv_g2,
                 rwkv_k_k, rwkv_k_a, rwkv_r_k, rwkv_ln_w, rwkv_ln_b, w_out, w_mlp_in, w_mlp_out)
    return out[None]
```

```python
import functools

import jax
import jax.numpy as jnp
from jax import lax
from jax.experimental import pallas as pl
from jax.experimental.pallas import tpu as pltpu

F32 = jnp.float32
BF16 = jnp.bfloat16

GRID_W = 64
ATT_HEAD_DIM = 128
N_Q_HEADS = 8
N_KV_HEADS = 2
GQA_GROUP = N_Q_HEADS // N_KV_HEADS
ATT_WIDTH = N_Q_HEADS * ATT_HEAD_DIM
KV_WIDTH = N_KV_HEADS * ATT_HEAD_DIM
ATT_IN = ATT_WIDTH + 2 * KV_WIDTH
WINDOW = 128
BLOCK = 128
ROPE_BASE = 10000.0
RWKV_HEAD_DIM = 64
RWKV_WIDTH = 1024
N_RWKV_HEADS = RWKV_WIDTH // RWKV_HEAD_DIM
LORA_W = 64
LORA_A = 64
LORA_G = 128
N_MOD = 6
NORM_EPS = 1e-6
GN_EPS = 64e-5
DECAY_SCALE = 0.6065306597126334

LANES = 128
CHUNK = 64
PAIR = 2 * RWKV_HEAD_DIM
N_PAIRS = RWKV_WIDTH // PAIR
PREP_OUT_DTYPES = (BF16, BF16, BF16, BF16, BF16, F32, F32, BF16, BF16, BF16, BF16)
ATTN_BLOCKS_PER_STEP = 4
HALO_ROWS = 16
OUT_SUB_BLOCKS = 4
VMEM_CAP = 56 * 1024 * 1024


def _vmem(nbytes):
    return int(min(VMEM_CAP, max(16 * 1024 * 1024, nbytes)))


def _dot(a, b):
    return jnp.dot(a, b, preferred_element_type=F32)


def _dot_nt(a, b):
    return lax.dot_general(a, b, (((1,), (1,)), ((), ())), preferred_element_type=F32)


def _dot_tn(a, b):
    return lax.dot_general(a, b, (((0,), (0,)), ((), ())), preferred_element_type=F32)


def _bdot(a, b):
    return _dot(a.astype(BF16), b.astype(BF16))


def _bdot_nt(a, b):
    return _dot_nt(a.astype(BF16), b.astype(BF16))


def _bdot_tn(a, b):
    return _dot_tn(a.astype(BF16), b.astype(BF16))


def _split_dot(x, m):
    hi = x.astype(BF16)
    lo = (x - hi.astype(F32)).astype(BF16)
    return _dot(hi, m) + _dot(lo, m)


def _sigmoid(z):
    return 0.5 * jnp.tanh(0.5 * z) + 0.5


def _ada_kernel(c_ref, w_ref, b_ref, o_ref):
    c = c_ref[...]
    s = c * jax.nn.sigmoid(c)
    o_ref[...] = _bdot(s, w_ref[...]) + b_ref[...]


def _ada(c8, w_ada, b_ada):
    d, n = w_ada.shape
    tn = 1024
    return pl.pallas_call(
        _ada_kernel,
        grid=(n // tn,),
        in_specs=[pl.BlockSpec((8, d), lambda j: (0, 0)),
                  pl.BlockSpec((d, tn), lambda j: (0, j)),
                  pl.BlockSpec((1, tn), lambda j: (0, j))],
        out_specs=pl.BlockSpec((8, tn), lambda j: (0, j)),
        out_shape=jax.ShapeDtypeStruct((8, n), F32),
        compiler_params=pltpu.CompilerParams(
            dimension_semantics=("arbitrary",),
            vmem_limit_bytes=_vmem(2 * d * tn * 4 + 4 * 1024 * 1024)),
        name="ada",
    )(c8, w_ada, b_ada)


def _norm_mod_kernel(x_ref, c_ref, g_ref, sc_ref, sh_ref, o_ref, *, n_lat_tiles):
    def emit(src_ref):
        x = src_ref[...]
        ms = jnp.mean(x * x, axis=-1, keepdims=True)
        y = x * lax.rsqrt(ms + NORM_EPS) * g_ref[...]
        o_ref[...] = (y * (1.0 + sc_ref[0]) + sh_ref[0]).astype(BF16)

    i = pl.program_id(0)
    pl.when(i < n_lat_tiles)(lambda: emit(x_ref))
    pl.when(i >= n_lat_tiles)(lambda: emit(c_ref))


def _norm_mod(x2, ctx2, gain, sc2, sh2, tm):
    s_len, d = x2.shape
    n_lat_tiles = s_len // tm
    n_tiles = n_lat_tiles + ctx2.shape[0] // tm
    sel = lambda i: (jnp.where(i >= n_lat_tiles, 1, 0), 0, 0)
    return pl.pallas_call(
        functools.partial(_norm_mod_kernel, n_lat_tiles=n_lat_tiles),
        grid=(n_tiles,),
        in_specs=[pl.BlockSpec((tm, d), lambda i: (jnp.minimum(i, n_lat_tiles - 1), 0)),
                  pl.BlockSpec((tm, d), lambda i: (jnp.maximum(i - n_lat_tiles, 0), 0)),
                  pl.BlockSpec((1, d), lambda i: (0, 0)),
                  pl.BlockSpec((1, 1, d), sel),
                  pl.BlockSpec((1, 1, d), sel)],
        out_specs=pl.BlockSpec((tm, d), lambda i: (i, 0)),
        out_shape=jax.ShapeDtypeStruct((n_tiles * tm, d), BF16),
        compiler_params=pltpu.CompilerParams(
            dimension_semantics=("arbitrary",),
            vmem_limit_bytes=_vmem(2 * tm * d * 10 + 4 * 1024 * 1024)),
        name="norm_mod",
    )(x2, ctx2, gain, sc2, sh2)


def _rope_heads(acc, cos, sin, n_heads):
    lane = lax.broadcasted_iota(jnp.int32, (1, ATT_HEAD_DIM), 1)
    first = (lane % 64) < 32
    outs = []
    for h in range(n_heads):
        xh = acc[:, h * ATT_HEAD_DIM:(h + 1) * ATT_HEAD_DIM]
        partner = jnp.where(first, pltpu.roll(xh, 96, 1), pltpu.roll(xh, 32, 1))
        outs.append(xh * cos + partner * sin)
    return outs


def _proj_kernel(a_ref, w_ref, rcos_ref, rsin_ref, ccos_ref, csin_ref, o_ref, *, rope_heads, s_len):
    acc = _dot(a_ref[...], w_ref[...])
    tm = a_ref.shape[0]
    i, j = pl.program_id(0), pl.program_id(1)
    row = i * tm + lax.broadcasted_iota(jnp.int32, (tm, 1), 0)
    rotate = row < jnp.where(j == 0, s_len, 0)
    cos = jnp.where(rotate, (rcos_ref[...] + ccos_ref[...][None]).reshape(tm, ATT_HEAD_DIM), 1.0)
    sin = jnp.where(rotate, (rsin_ref[...] + csin_ref[...][None]).reshape(tm, ATT_HEAD_DIM), 0.0)
    for h, r in enumerate(_rope_heads(acc, cos, sin, rope_heads)):
        o_ref[:, h * ATT_HEAD_DIM:(h + 1) * ATT_HEAD_DIM] = r.astype(o_ref.dtype)
    o_ref[:, rope_heads * ATT_HEAD_DIM:] = acc[:, rope_heads * ATT_HEAD_DIM:].astype(o_ref.dtype)


def _proj(a, w, rope_tables, s_len, tm, tn, rope_heads):
    m, k = a.shape
    n = w.shape[1]
    n_tiles = m // tm
    assert rope_heads * ATT_HEAD_DIM <= tn and n % tn == 0 and tm % GRID_W == 0
    g = tm // GRID_W
    row_spec = pl.BlockSpec((g, 1, ATT_HEAD_DIM), lambda i, j: (i, 0, 0))
    col_spec = pl.BlockSpec((GRID_W, ATT_HEAD_DIM), lambda i, j: (0, 0))
    return pl.pallas_call(
        functools.partial(_proj_kernel, rope_heads=rope_heads, s_len=s_len),
        grid=(n_tiles, n // tn),
        in_specs=[pl.BlockSpec((tm, k), lambda i, j: (i, 0)),
                  pl.BlockSpec((k, tn), lambda i, j: (0, j)),
                  row_spec, row_spec, col_spec, col_spec],
        out_specs=pl.BlockSpec((tm, tn), lambda i, j: (i, j)),
        out_shape=jax.ShapeDtypeStruct((m, n), BF16),
        compiler_params=pltpu.CompilerParams(
            dimension_semantics=("arbitrary", "arbitrary"),
            vmem_limit_bytes=_vmem(2 * (tm * k * 2 + k * tn * 2 + tm * tn * 2 + 2 * tm * 512)
                                   + tm * tn * 8 + 4 * 1024 * 1024)),
        name="proj_in",
    )(a, w, *rope_tables)


def _attn_kernel(sink_ref, q_ref, *rest, nb, n_ctx):
    nq = ATTN_BLOCKS_PER_STEP
    k_refs, v_refs = rest[:nq + 2], rest[nq + 2:2 * (nq + 2)]
    kx_ref, vx_ref, o_ref = rest[2 * (nq + 2):]
    i = pl.program_id(0)
    scale = ATT_HEAD_DIM ** -0.5
    rows = GQA_GROUP * BLOCK
    span = 3 * BLOCK
    ncol = span + n_ctx
    row = lax.broadcasted_iota(jnp.int32, (rows, ncol), 0) % BLOCK
    col = lax.broadcasted_iota(jnp.int32, (rows, ncol), 1)
    in_window = jnp.abs(col - BLOCK - row) <= WINDOW
    chains = [(t, hk) for t in range(nq) for hk in range(N_KV_HEADS)]
    valid = []
    for t in range(nq):
        blk = nq * i + t
        lo = jnp.where(blk > 0, 0, BLOCK)
        hi = jnp.where(blk < nb - 1, span, 2 * BLOCK)
        valid.append((in_window & (col >= lo) & (col < hi)) | (col >= span))
    ks = lambda hk: slice(hk * ATT_HEAD_DIM, (hk + 1) * ATT_HEAD_DIM)
    q = [(jnp.concatenate(
        [q_ref[t * BLOCK:(t + 1) * BLOCK, (hk * GQA_GROUP + g) * ATT_HEAD_DIM:(hk * GQA_GROUP + g + 1) * ATT_HEAD_DIM]
         for g in range(GQA_GROUP)], axis=0).astype(F32) * scale).astype(BF16) for t, hk in chains]
    k_all = [jnp.concatenate([k_refs[t][:, ks(hk)], k_refs[t + 1][:, ks(hk)], k_refs[t + 2][:, ks(hk)],
                              kx_ref[:, ks(hk)]], axis=0).astype(BF16) for t, hk in chains]
    v_all = [jnp.concatenate([v_refs[t][:, ks(hk)], v_refs[t + 1][:, ks(hk)], v_refs[t + 2][:, ks(hk)],
                              vx_ref[:, ks(hk)]], axis=0).astype(BF16) for t, hk in chains]
    sink = [jnp.concatenate([jnp.full((BLOCK, 1), sink_ref[hk * GQA_GROUP + g], F32) for g in range(GQA_GROUP)],
                            axis=0) for _, hk in chains]
    s = [jnp.where(valid[t], _dot_nt(qh, kh), -jnp.inf) for (t, _), qh, kh in zip(chains, q, k_all)]
    m = [jnp.maximum(jnp.max(sh, axis=-1, keepdims=True), sk) for sh, sk in zip(s, sink)]
    e = [jnp.exp(sh - mh) for sh, mh in zip(s, m)]
    denom = [jnp.sum(eh, axis=-1, keepdims=True) + jnp.exp(sk - mh) for eh, sk, mh in zip(e, sink, m)]
    o = [_dot(eh.astype(BF16), vh) * (1.0 / dh) for eh, vh, dh in zip(e, v_all, denom)]
    for (t, hk), oc in zip(chains, o):
        for g in range(GQA_GROUP):
            h = hk * GQA_GROUP + g
            o_ref[t * BLOCK:(t + 1) * BLOCK, h * ATT_HEAD_DIM:(h + 1) * ATT_HEAD_DIM] = (
                oc[g * BLOCK:(g + 1) * BLOCK].astype(BF16))


def _attention(qkv, sink, s_len, n_ctx):
    nq = ATTN_BLOCKS_PER_STEP
    nb = s_len // BLOCK
    assert nb % nq == 0 and s_len % n_ctx == 0
    kcol = ATT_WIDTH // KV_WIDTH
    vcol = kcol + 1
    ctx_blk = s_len // n_ctx
    kv = lambda u, c: pl.BlockSpec((BLOCK, KV_WIDTH), lambda i: (jnp.clip(nq * i - 1 + u, 0, nb - 1), c))
    return pl.pallas_call(
        functools.partial(_attn_kernel, nb=nb, n_ctx=n_ctx),
        grid=(nb // nq,),
        in_specs=[pl.BlockSpec(memory_space=pltpu.SMEM),
                  pl.BlockSpec((nq * BLOCK, ATT_WIDTH), lambda i: (i, 0))]
                 + [kv(u, kcol) for u in range(nq + 2)] + [kv(u, vcol) for u in range(nq + 2)]
                 + [pl.BlockSpec((n_ctx, KV_WIDTH), lambda i: (ctx_blk, kcol)),
                    pl.BlockSpec((n_ctx, KV_WIDTH), lambda i: (ctx_blk, vcol))],
        out_specs=pl.BlockSpec((nq * BLOCK, ATT_WIDTH), lambda i: (i, 0)),
        out_shape=jax.ShapeDtypeStruct((s_len, ATT_WIDTH), BF16),
        compiler_params=pltpu.CompilerParams(
            dimension_semantics=("arbitrary",),
            vmem_limit_bytes=_vmem(40 * 1024 * 1024)),
        name="attention",
    )(sink, qkv, *([qkv] * (2 * (nq + 2) + 2)))


def _prep_kernel(x_ref, xp_ref, xn_ref, mup_ref, mun_ref, w2_ref, w0_ref, a2_ref, a0_ref, g2_ref,
                 kk_ref, ka_ref, rk_ref, seg_ref, exp_ref,
                 r_o, v_o, kk_o, g_o, bon_o, lwf_o, lwb_o, kdf_o, kdb_o, bf_o, bb_o,
                 *, tm, start_tiles, end_tiles):
    i = pl.program_id(0)
    r_w = RWKV_WIDTH
    x = x_ref[:, ATT_IN:].astype(F32)
    is_start = functools.reduce(jnp.logical_or, [i == t for t in start_tiles])
    is_end = functools.reduce(jnp.logical_or, [i == t for t in end_tiles])
    keep_prev = jnp.full((1, 1), jnp.where(is_start, 0, 1), jnp.int32) == 1
    keep_next = jnp.full((1, 1), jnp.where(is_end, 0, 1), jnp.int32) == 1
    prev_row = jnp.where(keep_prev, xp_ref[HALO_ROWS - 1:HALO_ROWS, ATT_IN:].astype(F32), 0.0)
    next_row = jnp.where(keep_next, xn_ref[0:1, ATT_IN:].astype(F32), 0.0)
    xp = pltpu.roll(x, 1, 0)
    xn = pltpu.roll(x, tm - 1, 0)
    row8 = lax.broadcasted_iota(jnp.int32, (8, 1), 0)
    xp = jnp.concatenate([jnp.where(row8 == 0, prev_row, xp[:8]), xp[8:]], axis=0)
    xn = jnp.concatenate([xn[:tm - 8], jnp.where(row8 == 7, next_row, xn[tm - 8:])], axis=0)
    mup = mup_ref[...]
    mun = mun_ref[...]
    s = x * (1.0 - mup - mun) + xp * mup + xn * mun

    r = s[:, :r_w]
    k = s[:, r_w:2 * r_w]
    v = s[:, 2 * r_w:3 * r_w]
    o1 = 3 * r_w
    o2 = o1 + 2 * LORA_W
    o3 = o2 + 2 * LORA_A
    w_lo = s[:, o1:o2]
    a_lo = s[:, o2:o3]
    g_lo = s[:, o3:]

    tw = jnp.tanh(w0_ref[...] + _bdot(jnp.tanh(w_lo), w2_ref[...]))
    ta = jnp.tanh(a0_ref[...] + _bdot(a_lo, a2_ref[...]))
    lw = (-0.5 * DECAY_SCALE) * tw - 0.5 * DECAY_SCALE
    g = _bdot(_sigmoid(g_lo), g2_ref[...])

    kkf = k * kk_ref[...]
    ssq = _split_dot(_bdot(kkf * kkf, seg_ref[...]), exp_ref[...])
    kk = kkf * jnp.minimum(lax.rsqrt(ssq), 1e12)
    kkh = k * (0.5 * ka_ref[...])
    k1 = k - kkh
    kd_f = k1 + kkh * ta[:, :r_w]
    kd_b = k1 + kkh * ta[:, r_w:]
    k_bonus = 0.5 * (kd_f + kd_b)
    bonus = _split_dot(_bdot(r * k_bonus * rk_ref[...], seg_ref[...]), exp_ref[...]) * v
    kk_half = 0.5 * kk

    r_o[...] = r.astype(r_o.dtype)
    v_o[...] = v.astype(v_o.dtype)
    kk_o[...] = kk.astype(kk_o.dtype)
    g_o[...] = g.astype(g_o.dtype)
    bon_o[...] = bonus.astype(bon_o.dtype)
    lwf_o[...] = lw[:, :r_w]
    lwb_o[...] = lw[:, r_w:]
    kdf_o[...] = kd_f.astype(kdf_o.dtype)
    kdb_o[...] = kd_b.astype(kdb_o.dtype)
    bf_o[...] = (kk_half + kk_half * ta[:, :r_w]).astype(bf_o.dtype)
    bb_o[...] = (kk_half + kk_half * ta[:, r_w:]).astype(bb_o.dtype)


def _rwkv_prep(slab, mup, mun, w2blk, w0cat, a2blk, a0cat, g2, k_k, k_a, r_k, seg, expand, s_len, n_ctx, tm):
    m, width = slab.shape
    nt = m // tm
    hb = tm // HALO_ROWS
    nblk8 = m // HALO_ROWS
    start_tiles = (0, s_len // tm)
    end_tiles = (s_len // tm - 1, nt - 1)
    full = lambda arr: pl.BlockSpec(arr.shape, lambda i: (0,) * arr.ndim)
    consts = (mup, mun, w2blk, w0cat, a2blk, a0cat, g2, k_k, k_a, r_k, seg, expand)
    out_spec = pl.BlockSpec((tm, RWKV_WIDTH), lambda i: (i, 0))
    n_out = len(PREP_OUT_DTYPES)
    return pl.pallas_call(
        functools.partial(_prep_kernel, tm=tm, start_tiles=start_tiles, end_tiles=end_tiles),
        grid=(nt,),
        in_specs=[pl.BlockSpec((tm, width), lambda i: (i, 0)),
                  pl.BlockSpec((HALO_ROWS, width), lambda i: (jnp.maximum(i * hb - 1, 0), 0)),
                  pl.BlockSpec((HALO_ROWS, width), lambda i: (jnp.minimum((i + 1) * hb, nblk8 - 1), 0))]
                 + [full(c) for c in consts],
        out_specs=[out_spec] * n_out,
        out_shape=[jax.ShapeDtypeStruct((m, RWKV_WIDTH), dt) for dt in PREP_OUT_DTYPES],
        compiler_params=pltpu.CompilerParams(
            dimension_semantics=("arbitrary",),
            vmem_limit_bytes=_vmem(2 * tm * width * 4 + 2 * n_out * tm * RWKV_WIDTH * 4
                                   + 12 * tm * RWKV_WIDTH * 4 + 8 * 1024 * 1024)),
        name="rwkv_prep",
    )(slab, slab, slab, *consts)


def _stack(x, m0):
    return jnp.concatenate([jnp.where(m0, x, 0.0), jnp.where(m0, 0.0, x)], axis=0)


def _scan_streams(r_ref, v_ref, kk_ref, lw_ref, kd_ref, b_ref, reverse):
    L = CHUNK
    lw = lw_ref[...]
    ti = lax.broadcasted_iota(jnp.int32, (L, L), 0)
    tj = lax.broadcasted_iota(jnp.int32, (L, L), 1)
    tri = jnp.where((ti <= tj) if reverse else (ti >= tj), 1.0, 0.0).astype(BF16)
    lw_hi = lw.astype(BF16)
    lw_lo = (lw - lw_hi.astype(F32)).astype(BF16)
    c = _dot(tri, lw_hi) + _dot(tri, lw_lo)
    p_cum = jnp.exp(c)
    p_inv = jnp.exp(-c)
    p_prev = jnp.exp(c - lw)
    last = 0 if reverse else L - 1
    p_end = p_cum[last:last + 1, :]
    b_t = b_ref[...] * p_inv
    k_t = kd_ref[...] * p_inv
    gi = lax.broadcasted_iota(jnp.int32, (2 * L, 2 * L), 0)
    gj = lax.broadcasted_iota(jnp.int32, (2 * L, 2 * L), 1)
    same = (gi // L) == (gj // L)
    strict = same & ((gi < gj) if reverse else (gi > gj))
    incl = same & ((gi <= gj) if reverse else (gi >= gj))
    return dict(a=-kk_ref[...] * p_prev, r=r_ref[...] * p_cum, b=b_t, k=k_t, bh=b_t * p_end, kh=k_t * p_end,
                v=v_ref[...], p_end=p_end, strict=strict, incl=incl)


def _scan_kernel(rf, vf, kkf, lwf, kdf, bf, rb, vb, kkb, lwb, kdb, bb, *rest, n_cast):
    cast_in, (yf, yb), cast_out, s_ref = rest[:n_cast], rest[n_cast:n_cast + 2], rest[n_cast + 2:-1], rest[-1]

    @pl.when(pl.program_id(0) == 0)
    def _():
        s_ref[...] = jnp.zeros_like(s_ref)

    for src, dst in zip(cast_in, cast_out):
        dst[...] = src[...].astype(dst.dtype)

    L = CHUNK
    streams = (_scan_streams(rf, vf, kkf, lwf, kdf, bf, False), _scan_streams(rb, vb, kkb, lwb, kdb, bb, True))
    y_refs = (yf, yb)
    m0 = lax.broadcasted_iota(jnp.int32, (L, PAIR), 1) < RWKV_HEAD_DIM
    chains = [(d, p) for d in range(2) for p in range(N_PAIRS)]

    def stacked(name, d, p):
        return _stack(streams[d][name][:, p * PAIR:(p + 1) * PAIR], m0)

    ar = [jnp.concatenate([stacked("a", d, p), stacked("r", d, p)], axis=0).astype(BF16) for d, p in chains]
    bk = [jnp.concatenate([stacked("b", d, p), stacked("k", d, p)], axis=0).astype(BF16) for d, p in chains]
    v_s = [stacked("v", d, p).astype(BF16) for d, p in chains]
    state = [s_ref[d, p] for d, p in chains]
    g = [_dot_nt(x, z) for x, z in zip(ar, bk)]
    xs = [_dot_nt(x, s.astype(BF16)) for x, s in zip(ar, state)]
    a_ab, a_kv, a_rb = [], [], []
    for (d, _), gc in zip(chains, g):
        strict, incl = streams[d]["strict"], streams[d]["incl"]
        a_ab.append(jnp.where(strict, gc[:2 * L, :2 * L], 0.0).astype(BF16))
        a_kv.append(jnp.concatenate([jnp.where(strict, gc[:2 * L, 2 * L:], 0.0),
                                     jnp.where(incl, gc[2 * L:, 2 * L:], 0.0)], axis=0).astype(BF16))
        a_rb.append(jnp.where(incl, gc[2 * L:, :2 * L], 0.0).astype(BF16))
    av = [_dot(a, v) for a, v in zip(a_kv, v_s)]
    w = [x[:2 * L] + y[:2 * L] for x, y in zip(xs, av)]
    am = a_ab
    w = [wc + _dot(a, wc.astype(BF16)) for a, wc in zip(am, w)]
    for _ in range(5):
        am = [_dot(a, a).astype(BF16) for a in am]
        w = [wc + _dot(a, wc.astype(BF16)) for a, wc in zip(am, w)]
    sa = [wc.astype(BF16) for wc in w]
    ys = [x[2 * L:] + y[2 * L:] + _dot(a, s) for x, y, a, s in zip(xs, av, a_rb, sa)]
    for (d, p), yc in zip(chains, ys):
        y_refs[d][:, p * PAIR:(p + 1) * PAIR] = yc[:L] + yc[L:]
    for (d, p), s_old, sac, vc in zip(chains, state, sa, v_s):
        bhkh = jnp.concatenate([stacked("bh", d, p), stacked("kh", d, p)], axis=0).astype(BF16)
        upd = _dot_tn(jnp.concatenate([sac, vc], axis=0), bhkh)
        s_ref[d, p] = s_old * streams[d]["p_end"][:, p * PAIR:(p + 1) * PAIR] + upd


def _cast_slices(w, l, n_steps):
    _, n_rows, n_cols = w.shape
    n_blk = 1
    while n_blk * 2 <= n_steps and n_rows % (n_blk * 2) == 0 and (n_rows // (n_blk * 2)) % 16 == 0:
        n_blk *= 2
    rows = n_rows // n_blk
    step = lambda i: jnp.minimum(i, n_blk - 1)
    return (pl.BlockSpec((None, rows, n_cols), lambda i: (l, step(i), 0)),
            pl.BlockSpec((rows, n_cols), lambda i: (step(i), 0)),
            jax.ShapeDtypeStruct((n_rows, n_cols), BF16))


def _rwkv_scan(r, v, kk, lwf, lwb, kdf, kdb, bf, bb, s_len, n_ctx, cast_weights, l):
    m = r.shape[0]
    nc = m // CHUNK
    nc_lat = s_len // CHUNK
    nc_ctx = n_ctx // CHUNK
    fwd = lambda i: (jnp.where(i < nc_ctx, nc_lat + i, i - nc_ctx), 0)
    bwd = lambda i: (nc - 1 - i, 0)
    fs = pl.BlockSpec((CHUNK, RWKV_WIDTH), fwd)
    bs = pl.BlockSpec((CHUNK, RWKV_WIDTH), bwd)
    cast_in, cast_out, cast_shapes = zip(*[_cast_slices(w, l, nc) for w in cast_weights])
    cast_bytes = sum(spec.block_shape[0] * spec.block_shape[1] * 6 for spec in cast_out)
    outs = pl.pallas_call(
        functools.partial(_scan_kernel, n_cast=len(cast_weights)),
        grid=(nc,),
        in_specs=[fs] * 6 + [bs] * 6 + list(cast_in),
        out_specs=[fs, bs] + list(cast_out),
        out_shape=[jax.ShapeDtypeStruct((m, RWKV_WIDTH), F32)] * 2 + list(cast_shapes),
        scratch_shapes=[pltpu.VMEM((2, N_PAIRS, PAIR, PAIR), F32)],
        compiler_params=pltpu.CompilerParams(
            dimension_semantics=("arbitrary",),
            vmem_limit_bytes=_vmem(2 * 14 * CHUNK * RWKV_WIDTH * 4 + 2 * N_PAIRS * PAIR * PAIR * 4
                                   + 2 * cast_bytes + 24 * 1024 * 1024)),
        name="rwkv_scan",
    )(r, v, kk, lwf, kdf, bf, r, v, kk, lwb, kdb, bb, *cast_weights)
    return outs[0], outs[1], outs[2:]


def _out_kernel(att_ref, yf_ref, yb_ref, g_ref, bon_ref, x_ref, wa_ref, wr_ref, lnw_ref, lnb_ref,
                seg_ref, exp_ref, gpm_ref, gt1_ref, gmlp_ref, sc2_ref, sh2_ref, x1_ref, h2_ref):
    mix_att = _dot(att_ref[...], wa_ref[...])
    inv_n = 1.0 / RWKV_HEAD_DIM
    seg, expand = seg_ref[...], exp_ref[...]
    rows = att_ref.shape[0] // OUT_SUB_BLOCKS
    blocks = [pl.ds(n * rows, rows) for n in range(OUT_SUB_BLOCKS)]
    y = [yf_ref[b, :] + yb_ref[b, :] for b in blocks]
    s1 = [_bdot(v, seg) for v in y]
    yc = [v - _split_dot(s, expand) * inv_n for v, s in zip(y, s1)]
    s2 = [_bdot(c * c, seg) for c in yc]
    var = [_bdot(s, expand) * inv_n for s in s2]
    rec = [((c * lax.rsqrt(vr + GN_EPS) * lnw_ref[...] + lnb_ref[...]) + bon_ref[b, :]) * g_ref[b, :]
           for c, vr, b in zip(yc, var, blocks)]
    mix = mix_att + _bdot(jnp.concatenate(rec, axis=0), wr_ref[...])
    ms = jnp.mean(mix * mix, axis=-1, keepdims=True)
    x1 = x_ref[...] + gt1_ref[...] * (mix * lax.rsqrt(ms + NORM_EPS) * gpm_ref[...])
    x1_ref[...] = x1
    ms2 = jnp.mean(x1 * x1, axis=-1, keepdims=True)
    h2 = x1 * lax.rsqrt(ms2 + NORM_EPS) * gmlp_ref[...]
    h2_ref[...] = (h2 * (1.0 + sc2_ref[...]) + sh2_ref[...]).astype(BF16)


def _out_proj(att, yf, yb, g, bonus, x, w_out, lnw, lnb, seg, expand, gpm, gt1, gmlp, sc2, sh2, tm):
    s_len, d = x.shape
    assert ATT_WIDTH == RWKV_WIDTH and w_out.shape == (ATT_WIDTH + RWKV_WIDTH, d)
    row = lambda w: pl.BlockSpec((tm, w), lambda i: (i, 0))
    full = lambda arr: pl.BlockSpec(arr.shape, lambda i: (0,) * arr.ndim)
    w_half = lambda n: pl.BlockSpec((ATT_WIDTH, d), lambda i: (n, 0))
    consts = (lnw, lnb, seg, expand, gpm, gt1, gmlp, sc2, sh2)
    return pl.pallas_call(
        _out_kernel,
        grid=(s_len // tm,),
        in_specs=[row(ATT_WIDTH)] + [row(RWKV_WIDTH)] * 4 + [row(d), w_half(0), w_half(1)]
                 + [full(c) for c in consts],
        out_specs=[row(d), row(d)],
        out_shape=[jax.ShapeDtypeStruct((s_len, d), F32), jax.ShapeDtypeStruct((s_len, d), BF16)],
        compiler_params=pltpu.CompilerParams(
            dimension_semantics=("arbitrary",),
            vmem_limit_bytes=_vmem(2 * tm * (ATT_WIDTH * 2 + 4 * RWKV_WIDTH * 4 + d * 4 + d * 4 + d * 2)
                                   + 2 * 2 * d * d + 8 * tm * d * 4 + 4 * 1024 * 1024)),
        name="out_proj",
    )(att, yf, yb, g, bonus, x, w_out, w_out, *consts)


def _mlp_kernel(h_ref, w1_ref, w2_ref, x1_ref, g_ref, gt_ref, o_ref, acc_ref):
    f = pl.program_id(1)

    @pl.when(f == 0)
    def _():
        acc_ref[...] = jnp.zeros_like(acc_ref)

    u = jnp.maximum(_dot(h_ref[...], w1_ref[...]), 0.0)
    acc_ref[...] += _dot((u * u).astype(BF16), w2_ref[...])

    @pl.when(f == pl.num_programs(1) - 1)
    def _():
        ff = acc_ref[...]
        ms = jnp.mean(ff * ff, axis=-1, keepdims=True)
        o_ref[...] = x1_ref[...] + gt_ref[...] * (ff * lax.rsqrt(ms + NORM_EPS) * g_ref[...])


def _mlp(h2, w1, w2, x1, gain, gt2, tm, tf):
    s_len, d = x1.shape
    dff = w1.shape[1]
    return pl.pallas_call(
        _mlp_kernel,
        grid=(s_len // tm, dff // tf),
        in_specs=[pl.BlockSpec((tm, d), lambda i, f: (i, 0)),
                  pl.BlockSpec((d, tf), lambda i, f: (0, f)),
                  pl.BlockSpec((tf, d), lambda i, f: (f, 0)),
                  pl.BlockSpec((tm, d), lambda i, f: (i, 0)),
                  pl.BlockSpec((1, d), lambda i, f: (0, 0)),
                  pl.BlockSpec((1, d), lambda i, f: (0, 0))],
        out_specs=pl.BlockSpec((tm, d), lambda i, f: (i, 0)),
        out_shape=jax.ShapeDtypeStruct((s_len, d), F32),
        scratch_shapes=[pltpu.VMEM((tm, d), F32)],
        compiler_params=pltpu.CompilerParams(
            dimension_semantics=("arbitrary", "arbitrary"),
            vmem_limit_bytes=_vmem(2 * (tm * d * 2 + 2 * d * tf * 2 + 2 * tm * d * 4)
                                   + tm * d * 4 + 3 * tm * tf * 4 + 4 * 1024 * 1024)),
        name="mlp",
    )(h2, w1, w2, x1, gain, gt2)


def _rope_tables(s_len, rows_padded):
    rows = s_len // GRID_W
    n_freq = ATT_HEAD_DIM // 4
    inv_freq = ROPE_BASE ** (-jnp.arange(n_freq, dtype=F32) / n_freq)
    ang_r = jnp.arange(rows, dtype=F32)[:, None] * inv_freq[None, :]
    ang_c = jnp.arange(GRID_W, dtype=F32)[:, None] * inv_freq[None, :]
    zr = jnp.zeros((rows, 2 * n_freq), F32)
    zc = jnp.zeros((GRID_W, 2 * n_freq), F32)
    pad = lambda t: jnp.pad(t, ((0, rows_padded - rows), (0, 0)))[:, None, :]
    row_cos = pad(jnp.concatenate([jnp.cos(ang_r), jnp.cos(ang_r), zr], axis=-1))
    row_sin = pad(jnp.concatenate([-jnp.sin(ang_r), jnp.sin(ang_r), zr], axis=-1))
    col_cos = jnp.concatenate([zc, jnp.cos(ang_c), jnp.cos(ang_c)], axis=-1)
    col_sin = jnp.concatenate([zc, -jnp.sin(ang_c), jnp.sin(ang_c)], axis=-1)
    return row_cos, row_sin, col_cos, col_sin


def _block_diag2(w):
    _, k, n = w.shape
    z = jnp.zeros((k, n), w.dtype)
    return jnp.concatenate([jnp.concatenate([w[0], z], axis=1), jnp.concatenate([z, w[1]], axis=1)], axis=0)


def _layer(x2, ctx2, c8, l, w_ada, b_ada, g_pre_mix, g_post_mix, g_pre_mlp, g_post_mlp, w_in, attn_sink,
           rwkv_mu_prev, rwkv_mu_next, rwkv_w0, rwkv_w2, rwkv_a0, rwkv_a2, rwkv_g2, rwkv_k_k, rwkv_k_a,
           rwkv_r_k, rwkv_ln_w, rwkv_ln_b, w_out, w_mlp_in, w_mlp_out):
    s_len, d = x2.shape
    n_ctx = ctx2.shape[0]
    row = lambda v: v.reshape(1, -1)

    mod = _ada(c8, w_ada[l], row(b_ada[l]))
    sh1, sc1, gt1, sh2, sc2, gt2 = [mod[:2, j * d:(j + 1) * d] for j in range(N_MOD)]

    tm_rows = 256
    h = _norm_mod(x2, ctx2, row(g_pre_mix[l]), sc1.reshape(2, 1, d), sh1.reshape(2, 1, d), tm_rows)

    m_all = s_len + n_ctx
    tm_proj = 768 if m_all % 768 == 0 else 256
    rope_tables = _rope_tables(s_len, (m_all // tm_proj) * (tm_proj // GRID_W))
    p_in = _proj(h, w_in[l].astype(BF16), rope_tables, s_len, tm_proj, w_in.shape[2] // 3,
                 N_Q_HEADS + N_KV_HEADS)

    att = _attention(p_in, attn_sink[l], s_len, n_ctx)

    head = jnp.arange(RWKV_WIDTH) // RWKV_HEAD_DIM
    seg = (head[:, None] == jnp.arange(LANES)[None, :]).astype(BF16)
    expand = seg.T
    streams = _rwkv_prep(
        p_in, row(rwkv_mu_prev[l]), row(rwkv_mu_next[l]),
        (0.5 * _block_diag2(rwkv_w2[l])).astype(BF16), 0.5 * rwkv_w0[l].reshape(1, -1),
        (0.5 * _block_diag2(rwkv_a2[l])).astype(BF16), 0.5 * rwkv_a0[l].reshape(1, -1),
        rwkv_g2[l].astype(BF16), row(rwkv_k_k[l]), row(rwkv_k_a[l]), rwkv_r_k[l].reshape(1, -1),
        seg, expand, s_len, n_ctx, tm_rows)
    r, v, kk, g, bonus, lwf, lwb, kdf, kdb, bf, bb = streams
    yf, yb, (w_out_b, w_mlp_in_b, w_mlp_out_b) = _rwkv_scan(
        r, v, kk, lwf, lwb, kdf, kdb, bf, bb, s_len, n_ctx, (w_out, w_mlp_in, w_mlp_out), l)

    x1, h2 = _out_proj(att, yf, yb, g, bonus, x2, w_out_b,
                       row(rwkv_ln_w[l]), row(rwkv_ln_b[l]), seg, expand,
                       row(g_post_mix[l]), gt1[0:1], row(g_pre_mlp[l]), sc2[0:1], sh2[0:1], 512)
    return _mlp(h2, w_mlp_in_b, w_mlp_out_b, x1, row(g_post_mlp[l]), gt2[0:1], 512, 1024)


def kernel(x, c, ctx, c_ctx, w_ada, b_ada, g_pre_mix, g_post_mix, g_pre_mlp, g_post_mlp, w_in, attn_sink,
           rwkv_mu_prev, rwkv_mu_next, rwkv_w0, rwkv_w2, rwkv_a0, rwkv_a2, rwkv_g2, rwkv_k_k, rwkv_k_a,
           rwkv_r_k, rwkv_ln_w, rwkv_ln_b, w_out, w_mlp_in, w_mlp_out):
    b, s_len, d = x.shape
    depth = w_ada.shape[0]
    assert b == 1 and depth == 1, "single sample, single layer (context stream is not carried to a next layer)"
    c8 = jnp.zeros((8, d), F32).at[0].set(c[0]).at[1].set(c_ctx)
    out = _layer(x[0], ctx[0], c8, 0, w_ada, b_ada, g_pre_mix, g_post_mix, g_pre_mlp, g_post_mlp, w_in,
                 attn_sink, rwkv_mu_prev, rwkv_mu_next, rwkv_w0, rwkv_w2, rwkv_a0, rwkv_a2, rwkv_g2,
                 rwkv_k_k, rwkv_k_a, rwkv_r_k, rwkv_ln_w, rwkv_ln_b, w_out, w_mlp_in, w_mlp_out)
    return out[None]
```

```python
import functools

import jax
import jax.numpy as jnp
from jax import lax
from jax.experimental import pallas as pl
from jax.experimental.pallas import tpu as pltpu

F32 = jnp.float32
BF16 = jnp.bfloat16

GRID_W = 64
ATT_HEAD_DIM = 128
N_Q_HEADS = 8
N_KV_HEADS = 2
GQA_GROUP = N_Q_HEADS // N_KV_HEADS
ATT_WIDTH = N_Q_HEADS * ATT_HEAD_DIM
KV_WIDTH = N_KV_HEADS * ATT_HEAD_DIM
ATT_IN = ATT_WIDTH + 2 * KV_WIDTH
WINDOW = 128
BLOCK = 128
ROPE_BASE = 10000.0
RWKV_HEAD_DIM = 64
RWKV_WIDTH = 1024
N_RWKV_HEADS = RWKV_WIDTH // RWKV_HEAD_DIM
LORA_W = 64
LORA_A = 64
LORA_G = 128
N_MOD = 6
NORM_EPS = 1e-6
GN_EPS = 64e-5
DECAY_SCALE = 0.6065306597126334

LANES = 128
CHUNK = 64
PAIR = 2 * RWKV_HEAD_DIM
N_PAIRS = RWKV_WIDTH // PAIR
PREP_OUT_DTYPES = (BF16, BF16, BF16, BF16, BF16, F32, F32, BF16, BF16, BF16, BF16)
ATTN_BLOCKS_PER_STEP = 4
HALO_ROWS = 16
OUT_SUB_BLOCKS = 4
VMEM_CAP = 56 * 1024 * 1024


def _vmem(nbytes):
    return int(min(VMEM_CAP, max(16 * 1024 * 1024, nbytes)))


def _dot(a, b):
    return jnp.dot(a, b, preferred_element_type=F32)


def _dot_nt(a, b):
    return lax.dot_general(a, b, (((1,), (1,)), ((), ())), preferred_element_type=F32)


def _dot_tn(a, b):
    return lax.dot_general(a, b, (((0,), (0,)), ((), ())), preferred_element_type=F32)


def _bdot(a, b):
    return _dot(a.astype(BF16), b.astype(BF16))


def _bdot_nt(a, b):
    return _dot_nt(a.astype(BF16), b.astype(BF16))


def _bdot_tn(a, b):
    return _dot_tn(a.astype(BF16), b.astype(BF16))


def _split_dot(x, m):
    hi = x.astype(BF16)
    lo = (x - hi.astype(F32)).astype(BF16)
    return _dot(hi, m) + _dot(lo, m)


def _sigmoid(z):
    return 0.5 * jnp.tanh(0.5 * z) + 0.5


def _ada_kernel(c_ref, w_ref, b_ref, o_ref):
    c = c_ref[...]
    s = c * jax.nn.sigmoid(c)
    o_ref[...] = _bdot(s, w_ref[...]) + b_ref[...]


def _ada(c8, w_ada, b_ada):
    d, n = w_ada.shape
    tn = 1024
    return pl.pallas_call(
        _ada_kernel,
        grid=(n // tn,),
        in_specs=[pl.BlockSpec((8, d), lambda j: (0, 0)),
                  pl.BlockSpec((d, tn), lambda j: (0, j)),
                  pl.BlockSpec((1, tn), lambda j: (0, j))],
        out_specs=pl.BlockSpec((8, tn), lambda j: (0, j)),
        out_shape=jax.ShapeDtypeStruct((8, n), F32),
        compiler_params=pltpu.CompilerParams(
            dimension_semantics=("arbitrary",),
            vmem_limit_bytes=_vmem(2 * d * tn * 4 + 4 * 1024 * 1024)),
        name="ada",
    )(c8, w_ada, b_ada)


def _norm_mod_kernel(x_ref, c_ref, g_ref, sc_ref, sh_ref, o_ref, *, n_lat_tiles):
    def emit(src_ref):
        x = src_ref[...]
        ms = jnp.mean(x * x, axis=-1, keepdims=True)
        y = x * lax.rsqrt(ms + NORM_EPS) * g_ref[...]
        o_ref[...] = (y * (1.0 + sc_ref[0]) + sh_ref[0]).astype(BF16)

    i = pl.program_id(0)
    pl.when(i < n_lat_tiles)(lambda: emit(x_ref))
    pl.when(i >= n_lat_tiles)(lambda: emit(c_ref))


def _norm_mod(x2, ctx2, gain, sc2, sh2, tm):
    s_len, d = x2.shape
    n_lat_tiles = s_len // tm
    n_tiles = n_lat_tiles + ctx2.shape[0] // tm
    sel = lambda i: (jnp.where(i >= n_lat_tiles, 1, 0), 0, 0)
    return pl.pallas_call(
        functools.partial(_norm_mod_kernel, n_lat_tiles=n_lat_tiles),
        grid=(n_tiles,),
        in_specs=[pl.BlockSpec((tm, d), lambda i: (jnp.minimum(i, n_lat_tiles - 1), 0)),
                  pl.BlockSpec((tm, d), lambda i: (jnp.maximum(i - n_lat_tiles, 0), 0)),
                  pl.BlockSpec((1, d), lambda i: (0, 0)),
                  pl.BlockSpec((1, 1, d), sel),
                  pl.BlockSpec((1, 1, d), sel)],
        out_specs=pl.BlockSpec((tm, d), lambda i: (i, 0)),
        out_shape=jax.ShapeDtypeStruct((n_tiles * tm, d), BF16),
        compiler_params=pltpu.CompilerParams(
            dimension_semantics=("arbitrary",),
            vmem_limit_bytes=_vmem(2 * tm * d * 10 + 4 * 1024 * 1024)),
        name="norm_mod",
    )(x2, ctx2, gain, sc2, sh2)


def _rope_heads(acc, cos, sin, n_heads):
    lane = lax.broadcasted_iota(jnp.int32, (1, ATT_HEAD_DIM), 1)
    first = (lane % 64) < 32
    outs = []
    for h in range(n_heads):
        xh = acc[:, h * ATT_HEAD_DIM:(h + 1) * ATT_HEAD_DIM]
        partner = jnp.where(first, pltpu.roll(xh, 96, 1), pltpu.roll(xh, 32, 1))
        outs.append(xh * cos + partner * sin)
    return outs


def _proj_kernel(a_ref, w_ref, rcos_ref, rsin_ref, ccos_ref, csin_ref, o_ref, *, rope_heads, s_len):
    acc = _dot(a_ref[...], w_ref[...])
    tm = a_ref.shape[0]
    i, j = pl.program_id(0), pl.program_id(1)
    row = i * tm + lax.broadcasted_iota(jnp.int32, (tm, 1), 0)
    rotate = row < jnp.where(j == 0, s_len, 0)
    cos = jnp.where(rotate, (rcos_ref[...] + ccos_ref[...][None]).reshape(tm, ATT_HEAD_DIM), 1.0)
    sin = jnp.where(rotate, (rsin_ref[...] + csin_ref[...][None]).reshape(tm, ATT_HEAD_DIM), 0.0)
    for h, r in enumerate(_rope_heads(acc, cos, sin, rope_heads)):
        o_ref[:, h * ATT_HEAD_DIM:(h + 1) * ATT_HEAD_DIM] = r.astype(o_ref.dtype)
    o_ref[:, rope_heads * ATT_HEAD_DIM:] = acc[:, rope_heads * ATT_HEAD_DIM:].astype(o_ref.dtype)


def _proj(a, w, rope_tables, s_len, tm, tn, rope_heads):
    m, k = a.shape
    n = w.shape[1]
    n_tiles = m // tm
    assert rope_heads * ATT_HEAD_DIM <= tn and n % tn == 0 and tm % GRID_W == 0
    g = tm // GRID_W
    row_spec = pl.BlockSpec((g, 1, ATT_HEAD_DIM), lambda i, j: (i, 0, 0))
    col_spec = pl.BlockSpec((GRID_W, ATT_HEAD_DIM), lambda i, j: (0, 0))
    return pl.pallas_call(
        functools.partial(_proj_kernel, rope_heads=rope_heads, s_len=s_len),
        grid=(n_tiles, n // tn),
        in_specs=[pl.BlockSpec((tm, k), lambda i, j: (i, 0)),
                  pl.BlockSpec((k, tn), lambda i, j: (0, j)),
                  row_spec, row_spec, col_spec, col_spec],
        out_specs=pl.BlockSpec((tm, tn), lambda i, j: (i, j)),
        out_shape=jax.ShapeDtypeStruct((m, n), BF16),
        compiler_params=pltpu.CompilerParams(
            dimension_semantics=("arbitrary", "arbitrary"),
            vmem_limit_bytes=_vmem(2 * (tm * k * 2 + k * tn * 2 + tm * tn * 2 + 2 * tm * 512)
                                   + tm * tn * 8 + 4 * 1024 * 1024)),
        name="proj_in",
    )(a, w, *rope_tables)


def _attn_kernel(sink_ref, q_ref, *rest, nb, n_ctx):
    nq = ATTN_BLOCKS_PER_STEP
    k_refs, v_refs = rest[:nq + 2], rest[nq + 2:2 * (nq + 2)]
    kx_ref, vx_ref, o_ref = rest[2 * (nq + 2):]
    i = pl.program_id(0)
    scale = ATT_HEAD_DIM ** -0.5
    rows = GQA_GROUP * BLOCK
    span = 3 * BLOCK
    ncol = span + n_ctx
    row = lax.broadcasted_iota(jnp.int32, (rows, ncol), 0) % BLOCK
    col = lax.broadcasted_iota(jnp.int32, (rows, ncol), 1)
    in_window = jnp.abs(col - BLOCK - row) <= WINDOW
    chains = [(t, hk) for t in range(nq) for hk in range(N_KV_HEADS)]
    valid = []
    for t in range(nq):
        blk = nq * i + t
        lo = jnp.where(blk > 0, 0, BLOCK)
        hi = jnp.where(blk < nb - 1, span, 2 * BLOCK)
        valid.append((in_window & (col >= lo) & (col < hi)) | (col >= span))
    ks = lambda hk: slice(hk * ATT_HEAD_DIM, (hk + 1) * ATT_HEAD_DIM)
    q = [(jnp.concatenate(
        [q_ref[t * BLOCK:(t + 1) * BLOCK, (hk * GQA_GROUP + g) * ATT_HEAD_DIM:(hk * GQA_GROUP + g + 1) * ATT_HEAD_DIM]
         for g in range(GQA_GROUP)], axis=0).astype(F32) * scale).astype(BF16) for t, hk in chains]
    k_all = [jnp.concatenate([k_refs[t][:, ks(hk)], k_refs[t + 1][:, ks(hk)], k_refs[t + 2][:, ks(hk)],
                              kx_ref[:, ks(hk)]], axis=0).astype(BF16) for t, hk in chains]
    v_all = [jnp.concatenate([v_refs[t][:, ks(hk)], v_refs[t + 1][:, ks(hk)], v_refs[t + 2][:, ks(hk)],
                              vx_ref[:, ks(hk)]], axis=0).astype(BF16) for t, hk in chains]
    sink = [jnp.concatenate([jnp.full((BLOCK, 1), sink_ref[hk * GQA_GROUP + g], F32) for g in range(GQA_GROUP)],
                            axis=0) for _, hk in chains]
    s = [jnp.where(valid[t], _dot_nt(qh, kh), -jnp.inf) for (t, _), qh, kh in zip(chains, q, k_all)]
    m = [jnp.maximum(jnp.max(sh, axis=-1, keepdims=True), sk) for sh, sk in zip(s, sink)]
    e = [jnp.exp(sh - mh) for sh, mh in zip(s, m)]
    denom = [jnp.sum(eh, axis=-1, keepdims=True) + jnp.exp(sk - mh) for eh, sk, mh in zip(e, sink, m)]
    o = [_dot(eh.astype(BF16), vh) * (1.0 / dh) for eh, vh, dh in zip(e, v_all, denom)]
    for (t, hk), oc in zip(chains, o):
        for g in range(GQA_GROUP):
            h = hk * GQA_GROUP + g
            o_ref[t * BLOCK:(t + 1) * BLOCK, h * ATT_HEAD_DIM:(h + 1) * ATT_HEAD_DIM] = (
                oc[g * BLOCK:(g + 1) * BLOCK].astype(BF16))


def _attention(qkv, sink, s_len, n_ctx):
    nq = ATTN_BLOCKS_PER_STEP
    nb = s_len // BLOCK
    assert nb % nq == 0 and s_len % n_ctx == 0
    kcol = ATT_WIDTH // KV_WIDTH
    vcol = kcol + 1
    ctx_blk = s_len // n_ctx
    kv = lambda u, c: pl.BlockSpec((BLOCK, KV_WIDTH), lambda i: (jnp.clip(nq * i - 1 + u, 0, nb - 1), c))
    return pl.pallas_call(
        functools.partial(_attn_kernel, nb=nb, n_ctx=n_ctx),
        grid=(nb // nq,),
        in_specs=[pl.BlockSpec(memory_space=pltpu.SMEM),
                  pl.BlockSpec((nq * BLOCK, ATT_WIDTH), lambda i: (i, 0))]
                 + [kv(u, kcol) for u in range(nq + 2)] + [kv(u, vcol) for u in range(nq + 2)]
                 + [pl.BlockSpec((n_ctx, KV_WIDTH), lambda i: (ctx_blk, kcol)),
                    pl.BlockSpec((n_ctx, KV_WIDTH), lambda i: (ctx_blk, vcol))],
        out_specs=pl.BlockSpec((nq * BLOCK, ATT_WIDTH), lambda i: (i, 0)),
        out_shape=jax.ShapeDtypeStruct((s_len, ATT_WIDTH), BF16),
        compiler_params=pltpu.CompilerParams(
            dimension_semantics=("arbitrary",),
            vmem_limit_bytes=_vmem(40 * 1024 * 1024)),
        name="attention",
    )(sink, qkv, *([qkv] * (2 * (nq + 2) + 2)))


def _prep_kernel(x_ref, xp_ref, xn_ref, mup_ref, mun_ref, w2_ref, w0_ref, a2_ref, a0_ref, g2_ref,
                 kk_ref, ka_ref, rk_ref, seg_ref, exp_ref,
                 r_o, v_o, kk_o, g_o, bon_o, lwf_o, lwb_o, kdf_o, kdb_o, bf_o, bb_o,
                 *, tm, start_tiles, end_tiles):
    i = pl.program_id(0)
    r_w = RWKV_WIDTH
    x = x_ref[:, ATT_IN:].astype(F32)
    is_start = functools.reduce(jnp.logical_or, [i == t for t in start_tiles])
    is_end = functools.reduce(jnp.logical_or, [i == t for t in end_tiles])
    keep_prev = jnp.full((1, 1), jnp.where(is_start, 0, 1), jnp.int32) == 1
    keep_next = jnp.full((1, 1), jnp.where(is_end, 0, 1), jnp.int32) == 1
    prev_row = jnp.where(keep_prev, xp_ref[HALO_ROWS - 1:HALO_ROWS, ATT_IN:].astype(F32), 0.0)
    next_row = jnp.where(keep_next, xn_ref[0:1, ATT_IN:].astype(F32), 0.0)
    xp = pltpu.roll(x, 1, 0)
    xn = pltpu.roll(x, tm - 1, 0)
    row8 = lax.broadcasted_iota(jnp.int32, (8, 1), 0)
    xp = jnp.concatenate([jnp.where(row8 == 0, prev_row, xp[:8]), xp[8:]], axis=0)
    xn = jnp.concatenate([xn[:tm - 8], jnp.where(row8 == 7, next_row, xn[tm - 8:])], axis=0)
    mup = mup_ref[...]
    mun = mun_ref[...]
    s = x * (1.0 - mup - mun) + xp * mup + xn * mun

    r = s[:, :r_w]
    k = s[:, r_w:2 * r_w]
    v = s[:, 2 * r_w:3 * r_w]
    o1 = 3 * r_w
    o2 = o1 + 2 * LORA_W
    o3 = o2 + 2 * LORA_A
    w_lo = s[:, o1:o2]
    a_lo = s[:, o2:o3]
    g_lo = s[:, o3:]

    tw = jnp.tanh(w0_ref[...] + _bdot(jnp.tanh(w_lo), w2_ref[...]))
    ta = jnp.tanh(a0_ref[...] + _bdot(a_lo, a2_ref[...]))
    lw = (-0.5 * DECAY_SCALE) * tw - 0.5 * DECAY_SCALE
    g = _bdot(_sigmoid(g_lo), g2_ref[...])

    kkf = k * kk_ref[...]
    ssq = _split_dot(_bdot(kkf * kkf, seg_ref[...]), exp_ref[...])
    kk = kkf * jnp.minimum(lax.rsqrt(ssq), 1e12)
    kkh = k * (0.5 * ka_ref[...])
    k1 = k - kkh
    kd_f = k1 + kkh * ta[:, :r_w]
    kd_b = k1 + kkh * ta[:, r_w:]
    k_bonus = 0.5 * (kd_f + kd_b)
    bonus = _split_dot(_bdot(r * k_bonus * rk_ref[...], seg_ref[...]), exp_ref[...]) * v
    kk_half = 0.5 * kk

    r_o[...] = r.astype(r_o.dtype)
    v_o[...] = v.astype(v_o.dtype)
    kk_o[...] = kk.astype(kk_o.dtype)
    g_o[...] = g.astype(g_o.dtype)
    bon_o[...] = bonus.astype(bon_o.dtype)
    lwf_o[...] = lw[:, :r_w]
    lwb_o[...] = lw[:, r_w:]
    kdf_o[...] = kd_f.astype(kdf_o.dtype)
    kdb_o[...] = kd_b.astype(kdb_o.dtype)
    bf_o[...] = (kk_half + kk_half * ta[:, :r_w]).astype(bf_o.dtype)
    bb_o[...] = (kk_half + kk_half * ta[:, r_w:]).astype(bb_o.dtype)


def _rwkv_prep(slab, mup, mun, w2blk, w0cat, a2blk, a0cat, g2, k_k, k_a, r_k, seg, expand, s_len, n_ctx, tm):
    m, width = slab.shape
    nt = m // tm
    hb = tm // HALO_ROWS
    nblk8 = m // HALO_ROWS
    start_tiles = (0, s_len // tm)
    end_tiles = (s_len // tm - 1, nt - 1)
    full = lambda arr: pl.BlockSpec(arr.shape, lambda i: (0,) * arr.ndim)
    consts = (mup, mun, w2blk, w0cat, a2blk, a0cat, g2, k_k, k_a, r_k, seg, expand)
    out_spec = pl.BlockSpec((tm, RWKV_WIDTH), lambda i: (i, 0))
    n_out = len(PREP_OUT_DTYPES)
    return pl.pallas_call(
        functools.partial(_prep_kernel, tm=tm, start_tiles=start_tiles, end_tiles=end_tiles),
        grid=(nt,),
        in_specs=[pl.BlockSpec((tm, width), lambda i: (i, 0)),
                  pl.BlockSpec((HALO_ROWS, width), lambda i: (jnp.maximum(i * hb - 1, 0), 0)),
                  pl.BlockSpec((HALO_ROWS, width), lambda i: (jnp.minimum((i + 1) * hb, nblk8 - 1), 0))]
                 + [full(c) for c in consts],
        out_specs=[out_spec] * n_out,
        out_shape=[jax.ShapeDtypeStruct((m, RWKV_WIDTH), dt) for dt in PREP_OUT_DTYPES],
        compiler_params=pltpu.CompilerParams(
            dimension_semantics=("arbitrary",),
            vmem_limit_bytes=_vmem(2 * tm * width * 4 + 2 * n_out * tm * RWKV_WIDTH * 4
                                   + 12 * tm * RWKV_WIDTH * 4 + 8 * 1024 * 1024)),
        name="rwkv_prep",
    )(slab, slab, slab, *consts)


def _stack(x, m0):
    return jnp.concatenate([jnp.where(m0, x, 0.0), jnp.where(m0, 0.0, x)], axis=0)


def _scan_streams(r_ref, v_ref, kk_ref, lw_ref, kd_ref, b_ref, reverse):
    L = CHUNK
    lw = lw_ref[...]
    ti = lax.broadcasted_iota(jnp.int32, (L, L), 0)
    tj = lax.broadcasted_iota(jnp.int32, (L, L), 1)
    tri = jnp.where((ti <= tj) if reverse else (ti >= tj), 1.0, 0.0).astype(BF16)
    lw_hi = lw.astype(BF16)
    lw_lo = (lw - lw_hi.astype(F32)).astype(BF16)
    c = _dot(tri, lw_hi) + _dot(tri, lw_lo)
    p_cum = jnp.exp(c)
    p_inv = jnp.exp(-c)
    p_prev = jnp.exp(c - lw)
    last = 0 if reverse else L - 1
    p_end = p_cum[last:last + 1, :]
    b_t = b_ref[...] * p_inv
    k_t = kd_ref[...] * p_inv
    gi = lax.broadcasted_iota(jnp.int32, (2 * L, 2 * L), 0)
    gj = lax.broadcasted_iota(jnp.int32, (2 * L, 2 * L), 1)
    same = (gi // L) == (gj // L)
    strict = same & ((gi < gj) if reverse else (gi > gj))
    incl = same & ((gi <= gj) if reverse else (gi >= gj))
    return dict(a=-kk_ref[...] * p_prev, r=r_ref[...] * p_cum, b=b_t, k=k_t, bh=b_t * p_end, kh=k_t * p_end,
                v=v_ref[...], p_end=p_end, strict=strict, incl=incl)


def _scan_kernel(rf, vf, kkf, lwf, kdf, bf, rb, vb, kkb, lwb, kdb, bb, *rest, n_cast):
    cast_in, (yf, yb), cast_out, s_ref = rest[:n_cast], rest[n_cast:n_cast + 2], rest[n_cast + 2:-1], rest[-1]

    @pl.when(pl.program_id(0) == 0)
    def _():
        s_ref[...] = jnp.zeros_like(s_ref)

    for src, dst in zip(cast_in, cast_out):
        dst[...] = src[...].astype(dst.dtype)

    L = CHUNK
    streams = (_scan_streams(rf, vf, kkf, lwf, kdf, bf, False), _scan_streams(rb, vb, kkb, lwb, kdb, bb, True))
    y_refs = (yf, yb)
    m0 = lax.broadcasted_iota(jnp.int32, (L, PAIR), 1) < RWKV_HEAD_DIM
    chains = [(d, p) for d in range(2) for p in range(N_PAIRS)]

    def side(name, d, p):
        return streams[d][name][:, p * PAIR:(p + 1) * PAIR]

    def stacked(name, d, p):
        return _stack(side(name, d, p), m0)

    head0_rows = (lax.broadcasted_iota(jnp.int32, (4 * L, PAIR), 0) // L) % 2 == 0
    same_head = ((lax.broadcasted_iota(jnp.int32, (PAIR, PAIR), 0) // RWKV_HEAD_DIM)
                 == (lax.broadcasted_iota(jnp.int32, (PAIR, PAIR), 1) // RWKV_HEAD_DIM))
    ar = [jnp.concatenate([stacked("a", d, p), stacked("r", d, p)], axis=0).astype(BF16) for d, p in chains]
    bk = [jnp.concatenate([side("b", d, p), side("k", d, p)], axis=0).astype(BF16) for d, p in chains]
    v_s = [stacked("v", d, p).astype(BF16) for d, p in chains]
    state = [s_ref[d, p] for d, p in chains]
    g = [_dot_nt(x, z) for x, z in zip(ar, bk)]
    xs = [_dot_nt(x, s.astype(BF16)) for x, s in zip(ar, state)]
    a_ab, a_kv, a_rb = [], [], []
    for (d, _), gc in zip(chains, g):
        strict, incl = streams[d]["strict"], streams[d]["incl"]
        gr = pltpu.roll(gc, RWKV_HEAD_DIM, 1)
        vs_b = jnp.where(head0_rows, gc, gr)
        vs_k = jnp.where(head0_rows, gr, gc)
        a_ab.append(jnp.where(strict, vs_b[:2 * L], 0.0).astype(BF16))
        a_kv.append(jnp.concatenate([jnp.where(strict, vs_k[:2 * L], 0.0),
                                     jnp.where(incl, vs_k[2 * L:], 0.0)], axis=0).astype(BF16))
        a_rb.append(jnp.where(incl, vs_b[2 * L:], 0.0).astype(BF16))
    av = [_dot(a, v) for a, v in zip(a_kv, v_s)]
    w = [x[:2 * L] + y[:2 * L] for x, y in zip(xs, av)]
    am = a_ab
    w = [wc + _dot(a, wc.astype(BF16)) for a, wc in zip(am, w)]
    for _ in range(5):
        am = [_dot(a, a).astype(BF16) for a in am]
        w = [wc + _dot(a, wc.astype(BF16)) for a, wc in zip(am, w)]
    sa = [wc.astype(BF16) for wc in w]
    ys = [x[2 * L:] + y[2 * L:] + _dot(a, s) for x, y, a, s in zip(xs, av, a_rb, sa)]
    for (d, p), yc in zip(chains, ys):
        y_refs[d][:, p * PAIR:(p + 1) * PAIR] = yc[:L] + yc[L:]
    for (d, p), s_old, wc in zip(chains, state, w):
        sa_v = jnp.concatenate([wc[:L] + wc[L:], side("v", d, p)], axis=0).astype(BF16)
        bhkh = jnp.concatenate([side("bh", d, p), side("kh", d, p)], axis=0).astype(BF16)
        upd = jnp.where(same_head, _dot_tn(sa_v, bhkh), 0.0)
        s_ref[d, p] = s_old * side("p_end", d, p) + upd


def _cast_slices(w, l, n_steps):
    _, n_rows, n_cols = w.shape
    n_blk = 1
    while n_blk * 2 <= n_steps and n_rows % (n_blk * 2) == 0 and (n_rows // (n_blk * 2)) % 16 == 0:
        n_blk *= 2
    rows = n_rows // n_blk
    step = lambda i: jnp.minimum(i, n_blk - 1)
    return (pl.BlockSpec((None, rows, n_cols), lambda i: (l, step(i), 0)),
            pl.BlockSpec((rows, n_cols), lambda i: (step(i), 0)),
            jax.ShapeDtypeStruct((n_rows, n_cols), BF16))


def _rwkv_scan(r, v, kk, lwf, lwb, kdf, kdb, bf, bb, s_len, n_ctx, cast_weights, l):
    m = r.shape[0]
    nc = m // CHUNK
    nc_lat = s_len // CHUNK
    nc_ctx = n_ctx // CHUNK
    fwd = lambda i: (jnp.where(i < nc_ctx, nc_lat + i, i - nc_ctx), 0)
    bwd = lambda i: (nc - 1 - i, 0)
    fs = pl.BlockSpec((CHUNK, RWKV_WIDTH), fwd)
    bs = pl.BlockSpec((CHUNK, RWKV_WIDTH), bwd)
    cast_in, cast_out, cast_shapes = zip(*[_cast_slices(w, l, nc) for w in cast_weights])
    cast_bytes = sum(spec.block_shape[0] * spec.block_shape[1] * 6 for spec in cast_out)
    outs = pl.pallas_call(
        functools.partial(_scan_kernel, n_cast=len(cast_weights)),
        grid=(nc,),
        in_specs=[fs] * 6 + [bs] * 6 + list(cast_in),
        out_specs=[fs, bs] + list(cast_out),
        out_shape=[jax.ShapeDtypeStruct((m, RWKV_WIDTH), F32)] * 2 + list(cast_shapes),
        scratch_shapes=[pltpu.VMEM((2, N_PAIRS, PAIR, PAIR), F32)],
        compiler_params=pltpu.CompilerParams(
            dimension_semantics=("arbitrary",),
            vmem_limit_bytes=_vmem(2 * 14 * CHUNK * RWKV_WIDTH * 4 + 2 * N_PAIRS * PAIR * PAIR * 4
                                   + 2 * cast_bytes + 24 * 1024 * 1024)),
        name="rwkv_scan",
    )(r, v, kk, lwf, kdf, bf, r, v, kk, lwb, kdb, bb, *cast_weights)
    return outs[0], outs[1], outs[2:]


def _out_kernel(att_ref, yf_ref, yb_ref, g_ref, bon_ref, x_ref, wa_ref, wr_ref, lnw_ref, lnb_ref,
                seg_ref, exp_ref, gpm_ref, gt1_ref, gmlp_ref, sc2_ref, sh2_ref, x1_ref, h2_ref):
    mix_att = _dot(att_ref[...], wa_ref[...])
    inv_n = 1.0 / RWKV_HEAD_DIM
    seg, expand = seg_ref[...], exp_ref[...]
    rows = att_ref.shape[0] // OUT_SUB_BLOCKS
    blocks = [pl.ds(n * rows, rows) for n in range(OUT_SUB_BLOCKS)]
    y = [yf_ref[b, :] + yb_ref[b, :] for b in blocks]
    s1 = [_bdot(v, seg) for v in y]
    yc = [v - _split_dot(s, expand) * inv_n for v, s in zip(y, s1)]
    s2 = [_bdot(c * c, seg) for c in yc]
    var = [_bdot(s, expand) * inv_n for s in s2]
    rec = [((c * lax.rsqrt(vr + GN_EPS) * lnw_ref[...] + lnb_ref[...]) + bon_ref[b, :]) * g_ref[b, :]
           for c, vr, b in zip(yc, var, blocks)]
    mix = mix_att + _bdot(jnp.concatenate(rec, axis=0), wr_ref[...])
    ms = jnp.mean(mix * mix, axis=-1, keepdims=True)
    x1 = x_ref[...] + gt1_ref[...] * (mix * lax.rsqrt(ms + NORM_EPS) * gpm_ref[...])
    x1_ref[...] = x1
    ms2 = jnp.mean(x1 * x1, axis=-1, keepdims=True)
    h2 = x1 * lax.rsqrt(ms2 + NORM_EPS) * gmlp_ref[...]
    h2_ref[...] = (h2 * (1.0 + sc2_ref[...]) + sh2_ref[...]).astype(BF16)


def _out_proj(att, yf, yb, g, bonus, x, w_out, lnw, lnb, seg, expand, gpm, gt1, gmlp, sc2, sh2, tm):
    s_len, d = x.shape
    assert ATT_WIDTH == RWKV_WIDTH and w_out.shape == (ATT_WIDTH + RWKV_WIDTH, d)
    row = lambda w: pl.BlockSpec((tm, w), lambda i: (i, 0))
    full = lambda arr: pl.BlockSpec(arr.shape, lambda i: (0,) * arr.ndim)
    w_half = lambda n: pl.BlockSpec((ATT_WIDTH, d), lambda i: (n, 0))
    consts = (lnw, lnb, seg, expand, gpm, gt1, gmlp, sc2, sh2)
    return pl.pallas_call(
        _out_kernel,
        grid=(s_len // tm,),
        in_specs=[row(ATT_WIDTH)] + [row(RWKV_WIDTH)] * 4 + [row(d), w_half(0), w_half(1)]
                 + [full(c) for c in consts],
        out_specs=[row(d), row(d)],
        out_shape=[jax.ShapeDtypeStruct((s_len, d), F32), jax.ShapeDtypeStruct((s_len, d), BF16)],
        compiler_params=pltpu.CompilerParams(
            dimension_semantics=("arbitrary",),
            vmem_limit_bytes=_vmem(2 * tm * (ATT_WIDTH * 2 + 4 * RWKV_WIDTH * 4 + d * 4 + d * 4 + d * 2)
                                   + 2 * 2 * d * d + 8 * tm * d * 4 + 4 * 1024 * 1024)),
        name="out_proj",
    )(att, yf, yb, g, bonus, x, w_out, w_out, *consts)


def _mlp_kernel(h_ref, w1_ref, w2_ref, x1_ref, g_ref, gt_ref, o_ref, acc_ref):
    f = pl.program_id(1)

    @pl.when(f == 0)
    def _():
        acc_ref[...] = jnp.zeros_like(acc_ref)

    u = jnp.maximum(_dot(h_ref[...], w1_ref[...]), 0.0)
    acc_ref[...] += _dot((u * u).astype(BF16), w2_ref[...])

    @pl.when(f == pl.num_programs(1) - 1)
    def _():
        ff = acc_ref[...]
        ms = jnp.mean(ff * ff, axis=-1, keepdims=True)
        o_ref[...] = x1_ref[...] + gt_ref[...] * (ff * lax.rsqrt(ms + NORM_EPS) * g_ref[...])


def _mlp(h2, w1, w2, x1, gain, gt2, tm, tf):
    s_len, d = x1.shape
    dff = w1.shape[1]
    return pl.pallas_call(
        _mlp_kernel,
        grid=(s_len // tm, dff // tf),
        in_specs=[pl.BlockSpec((tm, d), lambda i, f: (i, 0)),
                  pl.BlockSpec((d, tf), lambda i, f: (0, f)),
                  pl.BlockSpec((tf, d), lambda i, f: (f, 0)),
                  pl.BlockSpec((tm, d), lambda i, f: (i, 0)),
                  pl.BlockSpec((1, d), lambda i, f: (0, 0)),
                  pl.BlockSpec((1, d), lambda i, f: (0, 0))],
        out_specs=pl.BlockSpec((tm, d), lambda i, f: (i, 0)),
        out_shape=jax.ShapeDtypeStruct((s_len, d), F32),
        scratch_shapes=[pltpu.VMEM((tm, d), F32)],
        compiler_params=pltpu.CompilerParams(
            dimension_semantics=("arbitrary", "arbitrary"),
            vmem_limit_bytes=_vmem(2 * (tm * d * 2 + 2 * d * tf * 2 + 2 * tm * d * 4)
                                   + tm * d * 4 + 3 * tm * tf * 4 + 4 * 1024 * 1024)),
        name="mlp",
    )(h2, w1, w2, x1, gain, gt2)


def _rope_tables(s_len, rows_padded):
    rows = s_len // GRID_W
    n_freq = ATT_HEAD_DIM // 4
    inv_freq = ROPE_BASE ** (-jnp.arange(n_freq, dtype=F32) / n_freq)
    ang_r = jnp.arange(rows, dtype=F32)[:, None] * inv_freq[None, :]
    ang_c = jnp.arange(GRID_W, dtype=F32)[:, None] * inv_freq[None, :]
    zr = jnp.zeros((rows, 2 * n_freq), F32)
    zc = jnp.zeros((GRID_W, 2 * n_freq), F32)
    pad = lambda t: jnp.pad(t, ((0, rows_padded - rows), (0, 0)))[:, None, :]
    row_cos = pad(jnp.concatenate([jnp.cos(ang_r), jnp.cos(ang_r), zr], axis=-1))
    row_sin = pad(jnp.concatenate([-jnp.sin(ang_r), jnp.sin(ang_r), zr], axis=-1))
    col_cos = jnp.concatenate([zc, jnp.cos(ang_c), jnp.cos(ang_c)], axis=-1)
    col_sin = jnp.concatenate([zc, -jnp.sin(ang_c), jnp.sin(ang_c)], axis=-1)
    return row_cos, row_sin, col_cos, col_sin


def _block_diag2(w):
    _, k, n = w.shape
    z = jnp.zeros((k, n), w.dtype)
    return jnp.concatenate([jnp.concatenate([w[0], z], axis=1), jnp.concatenate([z, w[1]], axis=1)], axis=0)


def _layer(x2, ctx2, c8, l, w_ada, b_ada, g_pre_mix, g_post_mix, g_pre_mlp, g_post_mlp, w_in, attn_sink,
           rwkv_mu_prev, rwkv_mu_next, rwkv_w0, rwkv_w2, rwkv_a0, rwkv_a2, rwkv_g2, rwkv_k_k, rwkv_k_a,
           rwkv_r_k, rwkv_ln_w, rwkv_ln_b, w_out, w_mlp_in, w_mlp_out):
    s_len, d = x2.shape
    n_ctx = ctx2.shape[0]
    row = lambda v: v.reshape(1, -1)

    mod = _ada(c8, w_ada[l], row(b_ada[l]))
    sh1, sc1, gt1, sh2, sc2, gt2 = [mod[:2, j * d:(j + 1) * d] for j in range(N_MOD)]

    tm_rows = 256
    h = _norm_mod(x2, ctx2, row(g_pre_mix[l]), sc1.reshape(2, 1, d), sh1.reshape(2, 1, d), tm_rows)

    m_all = s_len + n_ctx
    tm_proj = 768 if m_all % 768 == 0 else 256
    rope_tables = _rope_tables(s_len, (m_all // tm_proj) * (tm_proj // GRID_W))
    p_in = _proj(h, w_in[l].astype(BF16), rope_tables, s_len, tm_proj, w_in.shape[2] // 3,
                 N_Q_HEADS + N_KV_HEADS)

    att = _attention(p_in, attn_sink[l], s_len, n_ctx)

    head = jnp.arange(RWKV_WIDTH) // RWKV_HEAD_DIM
    seg = (head[:, None] == jnp.arange(LANES)[None, :]).astype(BF16)
    expand = seg.T
    streams = _rwkv_prep(
        p_in, row(rwkv_mu_prev[l]), row(rwkv_mu_next[l]),
        (0.5 * _block_diag2(rwkv_w2[l])).astype(BF16), 0.5 * rwkv_w0[l].reshape(1, -1),
        (0.5 * _block_diag2(rwkv_a2[l])).astype(BF16), 0.5 * rwkv_a0[l].reshape(1, -1),
        rwkv_g2[l].astype(BF16), row(rwkv_k_k[l]), row(rwkv_k_a[l]), rwkv_r_k[l].reshape(1, -1),
        seg, expand, s_len, n_ctx, tm_rows)
    r, v, kk, g, bonus, lwf, lwb, kdf, kdb, bf, bb = streams
    yf, yb, (w_out_b, w_mlp_in_b, w_mlp_out_b) = _rwkv_scan(
        r, v, kk, lwf, lwb, kdf, kdb, bf, bb, s_len, n_ctx, (w_out, w_mlp_in, w_mlp_out), l)

    x1, h2 = _out_proj(att, yf, yb, g, bonus, x2, w_out_b,
                       row(rwkv_ln_w[l]), row(rwkv_ln_b[l]), seg, expand,
                       row(g_post_mix[l]), gt1[0:1], row(g_pre_mlp[l]), sc2[0:1], sh2[0:1], 512)
    return _mlp(h2, w_mlp_in_b, w_mlp_out_b, x1, row(g_post_mlp[l]), gt2[0:1], 512, 1024)


def kernel(x, c, ctx, c_ctx, w_ada, b_ada, g_pre_mix, g_post_mix, g_pre_mlp, g_post_mlp, w_in, attn_sink,
           rwkv_mu_prev, rwkv_mu_next, rwkv_w0, rwkv_w2, rwkv_a0, rwkv_a2, rwkv_g2, rwkv_k_k, rwkv_k_a,
           rwkv_r_k, rwkv_ln_w, rwkv_ln_b, w_out, w_mlp_in, w_mlp_out):
    b, s_len, d = x.shape
    depth = w_ada.shape[0]
    assert b == 1 and depth == 1, "single sample, single layer (context stream is not carried to a next layer)"
    c8 = jnp.zeros((8, d), F32).at[0].set(c[0]).at[1].set(c_ctx)
    out = _layer(x[0], ctx[0], c8, 0, w_ada, b_ada, g_pre_mix, g_post_mix, g_pre_mlp, g_post_mlp, w_in,
                 attn_sink, rwkv_mu_prev, rwkv_mu_next, rwkv_w0, rwkv_w2, rwkv_a0, rwkv_a2, rwkv_g2,
                 rwkv_k_k, rwkv_k_a, rwkv_r_k, rwkv_ln_w, rwkv_ln_b, w_out, w_mlp_in, w_mlp_out)
    return out[None]
```

```python
import functools
from typing import NamedTuple

import jax
import jax.numpy as jnp
from jax import lax
from jax.experimental import pallas as pl
from jax.experimental.pallas import tpu as pltpu

F32 = jnp.float32
BF16 = jnp.bfloat16

GRID_W = 64
ATT_HEAD_DIM = 128
N_Q_HEADS = 8
N_KV_HEADS = 2
GQA_GROUP = N_Q_HEADS // N_KV_HEADS
ATT_WIDTH = N_Q_HEADS * ATT_HEAD_DIM
KV_WIDTH = N_KV_HEADS * ATT_HEAD_DIM
ATT_IN = ATT_WIDTH + 2 * KV_WIDTH
WINDOW = 128
BLOCK = 128
ROPE_BASE = 10000.0
RWKV_HEAD_DIM = 64
RWKV_WIDTH = 1024
N_RWKV_HEADS = RWKV_WIDTH // RWKV_HEAD_DIM
LORA_W = 64
LORA_A = 64
LORA_G = 128
N_MOD = 6
NORM_EPS = 1e-6
GN_EPS = 64e-5
DECAY_SCALE = 0.6065306597126334

LANES = 128
CHUNK = 64
PAIR = 2 * RWKV_HEAD_DIM
N_PAIRS = RWKV_WIDTH // PAIR
PREP_OUT_DTYPES = (BF16, BF16, BF16, BF16, BF16, F32, F32, BF16, BF16, BF16, BF16)
ATTN_BLOCKS_PER_STEP = 4
HALO_ROWS = 16
OUT_SUB_BLOCKS = 4
VMEM_CAP = 56 * 1024 * 1024


def _vmem(nbytes):
    return int(min(VMEM_CAP, max(16 * 1024 * 1024, nbytes)))


def _dot(a, b):
    return jnp.dot(a, b, preferred_element_type=F32)


def _dot_nt(a, b):
    return lax.dot_general(a, b, (((1,), (1,)), ((), ())), preferred_element_type=F32)


def _dot_tn(a, b):
    return lax.dot_general(a, b, (((0,), (0,)), ((), ())), preferred_element_type=F32)


def _bdot(a, b):
    return _dot(a.astype(BF16), b.astype(BF16))


def _bdot_nt(a, b):
    return _dot_nt(a.astype(BF16), b.astype(BF16))


def _bdot_tn(a, b):
    return _dot_tn(a.astype(BF16), b.astype(BF16))


def _split_dot(x, m):
    hi = x.astype(BF16)
    lo = (x - hi.astype(F32)).astype(BF16)
    return _dot(hi, m) + _dot(lo, m)


def _sigmoid(z):
    return 0.5 * jnp.tanh(0.5 * z) + 0.5


def _ada_kernel(c_ref, w_ref, b_ref, o_ref):
    c = c_ref[...]
    s = c * jax.nn.sigmoid(c)
    o_ref[...] = _bdot(s, w_ref[...]) + b_ref[...]


def _ada(c8, w_ada, b_ada):
    d, n = w_ada.shape
    tn = 1024
    return pl.pallas_call(
        _ada_kernel,
        grid=(n // tn,),
        in_specs=[pl.BlockSpec((8, d), lambda j: (0, 0)),
                  pl.BlockSpec((d, tn), lambda j: (0, j)),
                  pl.BlockSpec((1, tn), lambda j: (0, j))],
        out_specs=pl.BlockSpec((8, tn), lambda j: (0, j)),
        out_shape=jax.ShapeDtypeStruct((8, n), F32),
        compiler_params=pltpu.CompilerParams(
            dimension_semantics=("arbitrary",),
            vmem_limit_bytes=_vmem(2 * d * tn * 4 + 4 * 1024 * 1024)),
        name="ada",
    )(c8, w_ada, b_ada)


def _norm_mod_kernel(x_ref, c_ref, g_ref, sc_ref, sh_ref, o_ref, *, lat_last):
    def emit(x, who, rows):
        ms = jnp.mean(x * x, axis=-1, keepdims=True)
        y = (x * lax.rsqrt(ms + NORM_EPS)) * (g_ref[...] * (1.0 + sc_ref[who])) + sh_ref[who]
        o_ref[rows, :] = y.astype(BF16)

    i = pl.program_id(0)
    last = pl.num_programs(0) - 1

    @pl.when(i < last)
    def _():
        emit(x_ref[...], 0, slice(None))

    @pl.when(i == last)
    def _():
        if lat_last:
            emit(x_ref[:lat_last, :], 0, slice(0, lat_last))
        emit(c_ref[...], 1, slice(lat_last, None))


def _norm_mod(x2, ctx2, gain, sc2, sh2, tm):
    s_len, d = x2.shape
    n_ctx = ctx2.shape[0]
    n_tiles = (s_len + n_ctx) // tm
    lat_last = s_len - (n_tiles - 1) * tm
    assert n_tiles * tm == s_len + n_ctx and lat_last + n_ctx == tm and lat_last % 16 == 0
    both = pl.BlockSpec((2, 1, d), lambda i: (0, 0, 0))
    return pl.pallas_call(
        functools.partial(_norm_mod_kernel, lat_last=lat_last),
        grid=(n_tiles,),
        in_specs=[pl.BlockSpec((tm, d), lambda i: (jnp.minimum(i, (s_len - 1) // tm), 0)),
                  pl.BlockSpec((n_ctx, d), lambda i: (0, 0)),
                  pl.BlockSpec((1, d), lambda i: (0, 0)),
                  both, both],
        out_specs=pl.BlockSpec((tm, d), lambda i: (i, 0)),
        out_shape=jax.ShapeDtypeStruct((n_tiles * tm, d), BF16),
        compiler_params=pltpu.CompilerParams(
            dimension_semantics=("arbitrary",),
            vmem_limit_bytes=_vmem(2 * tm * d * 6 + 2 * n_ctx * d * 4 + 4 * tm * d * 4 + 4 * 1024 * 1024)),
        name="norm_mod",
    )(x2, ctx2, gain, sc2, sh2)


def _rope_heads(acc, cos, sin, n_heads):
    lane = lax.broadcasted_iota(jnp.int32, (1, ATT_HEAD_DIM), 1)
    first = (lane % 64) < 32
    outs = []
    for h in range(n_heads):
        xh = acc[:, h * ATT_HEAD_DIM:(h + 1) * ATT_HEAD_DIM]
        partner = jnp.where(first, pltpu.roll(xh, 96, 1), pltpu.roll(xh, 32, 1))
        outs.append(xh * cos + partner * sin)
    return outs


def _proj_kernel(a_ref, w_ref, rcos_ref, rsin_ref, ccos_ref, csin_ref, o_ref, *, rope_heads, s_len):
    acc = _dot(a_ref[...], w_ref[...])
    tm = a_ref.shape[0]
    i, j = pl.program_id(0), pl.program_id(1)
    row = i * tm + lax.broadcasted_iota(jnp.int32, (tm, 1), 0)
    rotate = row < jnp.where(j == 0, s_len, 0)
    cos = jnp.where(rotate, (rcos_ref[...] + ccos_ref[...][None]).reshape(tm, ATT_HEAD_DIM), 1.0)
    sin = jnp.where(rotate, (rsin_ref[...] + csin_ref[...][None]).reshape(tm, ATT_HEAD_DIM), 0.0)
    for h, r in enumerate(_rope_heads(acc, cos, sin, rope_heads)):
        o_ref[:, h * ATT_HEAD_DIM:(h + 1) * ATT_HEAD_DIM] = r.astype(o_ref.dtype)
    o_ref[:, rope_heads * ATT_HEAD_DIM:] = acc[:, rope_heads * ATT_HEAD_DIM:].astype(o_ref.dtype)


def _proj(a, w, rope_tables, s_len, tm, tn, rope_heads):
    m, k = a.shape
    n = w.shape[1]
    n_tiles = m // tm
    assert rope_heads * ATT_HEAD_DIM <= tn and n % tn == 0 and tm % GRID_W == 0
    g = tm // GRID_W
    row_spec = pl.BlockSpec((g, 1, ATT_HEAD_DIM), lambda i, j: (i, 0, 0))
    col_spec = pl.BlockSpec((GRID_W, ATT_HEAD_DIM), lambda i, j: (0, 0))
    return pl.pallas_call(
        functools.partial(_proj_kernel, rope_heads=rope_heads, s_len=s_len),
        grid=(n_tiles, n // tn),
        in_specs=[pl.BlockSpec((tm, k), lambda i, j: (i, 0)),
                  pl.BlockSpec((k, tn), lambda i, j: (0, j)),
                  row_spec, row_spec, col_spec, col_spec],
        out_specs=pl.BlockSpec((tm, tn), lambda i, j: (i, j)),
        out_shape=jax.ShapeDtypeStruct((m, n), BF16),
        compiler_params=pltpu.CompilerParams(
            dimension_semantics=("arbitrary", "arbitrary"),
            vmem_limit_bytes=_vmem(2 * (tm * k * 2 + k * tn * 2 + tm * tn * 2 + 2 * tm * 512)
                                   + tm * tn * 8 + 4 * 1024 * 1024)),
        name="proj_in",
    )(a, w, *rope_tables)


def _attn_kernel(sink_ref, q_ref, *rest, nb, n_ctx):
    nq = ATTN_BLOCKS_PER_STEP
    k_refs, v_refs = rest[:nq + 2], rest[nq + 2:2 * (nq + 2)]
    kx_ref, vx_ref, o_ref = rest[2 * (nq + 2):]
    i = pl.program_id(0)
    scale = ATT_HEAD_DIM ** -0.5
    rows = GQA_GROUP * BLOCK
    span = 3 * BLOCK
    ncol = span + n_ctx
    row = lax.broadcasted_iota(jnp.int32, (rows, ncol), 0) % BLOCK
    col = lax.broadcasted_iota(jnp.int32, (rows, ncol), 1)
    in_window = jnp.abs(col - BLOCK - row) <= WINDOW
    chains = [(t, hk) for t in range(nq) for hk in range(N_KV_HEADS)]
    valid = []
    for t in range(nq):
        blk = nq * i + t
        lo = jnp.where(blk > 0, 0, BLOCK)
        hi = jnp.where(blk < nb - 1, span, 2 * BLOCK)
        valid.append((in_window & (col >= lo) & (col < hi)) | (col >= span))
    ks = lambda hk: slice(hk * ATT_HEAD_DIM, (hk + 1) * ATT_HEAD_DIM)
    q = [(jnp.concatenate(
        [q_ref[t * BLOCK:(t + 1) * BLOCK, (hk * GQA_GROUP + g) * ATT_HEAD_DIM:(hk * GQA_GROUP + g + 1) * ATT_HEAD_DIM]
         for g in range(GQA_GROUP)], axis=0).astype(F32) * scale).astype(BF16) for t, hk in chains]
    k_all = [jnp.concatenate([k_refs[t][:, ks(hk)], k_refs[t + 1][:, ks(hk)], k_refs[t + 2][:, ks(hk)],
                              kx_ref[:, ks(hk)]], axis=0).astype(BF16) for t, hk in chains]
    v_all = [jnp.concatenate([v_refs[t][:, ks(hk)], v_refs[t + 1][:, ks(hk)], v_refs[t + 2][:, ks(hk)],
                              vx_ref[:, ks(hk)]], axis=0).astype(BF16) for t, hk in chains]
    sink = [jnp.concatenate([jnp.full((BLOCK, 1), sink_ref[hk * GQA_GROUP + g], F32) for g in range(GQA_GROUP)],
                            axis=0) for _, hk in chains]
    s = [jnp.where(valid[t], _dot_nt(qh, kh), -jnp.inf) for (t, _), qh, kh in zip(chains, q, k_all)]
    m = [jnp.maximum(jnp.max(sh, axis=-1, keepdims=True), sk) for sh, sk in zip(s, sink)]
    e = [jnp.exp(sh - mh) for sh, mh in zip(s, m)]
    denom = [jnp.sum(eh, axis=-1, keepdims=True) + jnp.exp(sk - mh) for eh, sk, mh in zip(e, sink, m)]
    o = [_dot(eh.astype(BF16), vh) * (1.0 / dh) for eh, vh, dh in zip(e, v_all, denom)]
    for (t, hk), oc in zip(chains, o):
        for g in range(GQA_GROUP):
            h = hk * GQA_GROUP + g
            o_ref[t * BLOCK:(t + 1) * BLOCK, h * ATT_HEAD_DIM:(h + 1) * ATT_HEAD_DIM] = (
                oc[g * BLOCK:(g + 1) * BLOCK].astype(BF16))


def _attention(qkv, sink, s_len, n_ctx):
    nq = ATTN_BLOCKS_PER_STEP
    nb = s_len // BLOCK
    assert nb % nq == 0 and s_len % n_ctx == 0
    kcol = ATT_WIDTH // KV_WIDTH
    vcol = kcol + 1
    ctx_blk = s_len // n_ctx
    kv = lambda u, c: pl.BlockSpec((BLOCK, KV_WIDTH), lambda i: (jnp.clip(nq * i - 1 + u, 0, nb - 1), c))
    return pl.pallas_call(
        functools.partial(_attn_kernel, nb=nb, n_ctx=n_ctx),
        grid=(nb // nq,),
        in_specs=[pl.BlockSpec(memory_space=pltpu.SMEM),
                  pl.BlockSpec((nq * BLOCK, ATT_WIDTH), lambda i: (i, 0))]
                 + [kv(u, kcol) for u in range(nq + 2)] + [kv(u, vcol) for u in range(nq + 2)]
                 + [pl.BlockSpec((n_ctx, KV_WIDTH), lambda i: (ctx_blk, kcol)),
                    pl.BlockSpec((n_ctx, KV_WIDTH), lambda i: (ctx_blk, vcol))],
        out_specs=pl.BlockSpec((nq * BLOCK, ATT_WIDTH), lambda i: (i, 0)),
        out_shape=jax.ShapeDtypeStruct((s_len, ATT_WIDTH), BF16),
        compiler_params=pltpu.CompilerParams(
            dimension_semantics=("arbitrary",),
            vmem_limit_bytes=_vmem(40 * 1024 * 1024)),
        name="attention",
    )(sink, qkv, *([qkv] * (2 * (nq + 2) + 2)))


def _prep_kernel(x_ref, xp_ref, xn_ref, mup_ref, mun_ref, w2_ref, w0_ref, a2_ref, a0_ref, g2_ref,
                 kk_ref, ka_ref, rk_ref, seg_ref, exp_ref,
                 r_o, v_o, kk_o, g_o, bon_o, lwf_o, lwb_o, kdf_o, kdb_o, bf_o, bb_o,
                 *, tm, start_tiles, end_tiles):
    i = pl.program_id(0)
    r_w = RWKV_WIDTH
    x = x_ref[:, ATT_IN:].astype(F32)
    is_start = functools.reduce(jnp.logical_or, [i == t for t in start_tiles])
    is_end = functools.reduce(jnp.logical_or, [i == t for t in end_tiles])
    keep_prev = jnp.full((1, 1), jnp.where(is_start, 0, 1), jnp.int32) == 1
    keep_next = jnp.full((1, 1), jnp.where(is_end, 0, 1), jnp.int32) == 1
    prev_row = jnp.where(keep_prev, xp_ref[HALO_ROWS - 1:HALO_ROWS, ATT_IN:].astype(F32), 0.0)
    next_row = jnp.where(keep_next, xn_ref[0:1, ATT_IN:].astype(F32), 0.0)
    xp = pltpu.roll(x, 1, 0)
    xn = pltpu.roll(x, tm - 1, 0)
    row8 = lax.broadcasted_iota(jnp.int32, (8, 1), 0)
    xp = jnp.concatenate([jnp.where(row8 == 0, prev_row, xp[:8]), xp[8:]], axis=0)
    xn = jnp.concatenate([xn[:tm - 8], jnp.where(row8 == 7, next_row, xn[tm - 8:])], axis=0)
    mup = mup_ref[...]
    mun = mun_ref[...]
    s = x * (1.0 - mup - mun) + xp * mup + xn * mun

    r = s[:, :r_w]
    k = s[:, r_w:2 * r_w]
    v = s[:, 2 * r_w:3 * r_w]
    o1 = 3 * r_w
    o2 = o1 + 2 * LORA_W
    o3 = o2 + 2 * LORA_A
    w_lo = s[:, o1:o2]
    a_lo = s[:, o2:o3]
    g_lo = s[:, o3:]

    tw = jnp.tanh(w0_ref[...] + _bdot(jnp.tanh(w_lo), w2_ref[...]))
    ta = jnp.tanh(a0_ref[...] + _bdot(a_lo, a2_ref[...]))
    lw = (-0.5 * DECAY_SCALE) * tw - 0.5 * DECAY_SCALE
    g = _bdot(_sigmoid(g_lo), g2_ref[...])

    kkf = k * kk_ref[...]
    ssq = _split_dot(_bdot(kkf * kkf, seg_ref[...]), exp_ref[...])
    kk = kkf * jnp.minimum(lax.rsqrt(ssq), 1e12)
    kkh = k * (0.5 * ka_ref[...])
    k1 = k - kkh
    kd_f = k1 + kkh * ta[:, :r_w]
    kd_b = k1 + kkh * ta[:, r_w:]
    k_bonus = 0.5 * (kd_f + kd_b)
    bonus = _split_dot(_bdot(r * k_bonus * rk_ref[...], seg_ref[...]), exp_ref[...]) * v
    kk_half = 0.5 * kk

    r_o[...] = r.astype(r_o.dtype)
    v_o[...] = v.astype(v_o.dtype)
    kk_o[...] = kk.astype(kk_o.dtype)
    g_o[...] = g.astype(g_o.dtype)
    bon_o[...] = bonus.astype(bon_o.dtype)
    lwf_o[...] = lw[:, :r_w]
    lwb_o[...] = lw[:, r_w:]
    kdf_o[...] = kd_f.astype(kdf_o.dtype)
    kdb_o[...] = kd_b.astype(kdb_o.dtype)
    bf_o[...] = (kk_half + kk_half * ta[:, :r_w]).astype(bf_o.dtype)
    bb_o[...] = (kk_half + kk_half * ta[:, r_w:]).astype(bb_o.dtype)


def _rwkv_prep(slab, mup, mun, w2blk, w0cat, a2blk, a0cat, g2, k_k, k_a, r_k, seg, expand, s_len, n_ctx, tm):
    m, width = slab.shape
    nt = m // tm
    hb = tm // HALO_ROWS
    nblk8 = m // HALO_ROWS
    start_tiles = (0, s_len // tm)
    end_tiles = (s_len // tm - 1, nt - 1)
    full = lambda arr: pl.BlockSpec(arr.shape, lambda i: (0,) * arr.ndim)
    consts = (mup, mun, w2blk, w0cat, a2blk, a0cat, g2, k_k, k_a, r_k, seg, expand)
    out_spec = pl.BlockSpec((tm, RWKV_WIDTH), lambda i: (i, 0))
    n_out = len(PREP_OUT_DTYPES)
    return pl.pallas_call(
        functools.partial(_prep_kernel, tm=tm, start_tiles=start_tiles, end_tiles=end_tiles),
        grid=(nt,),
        in_specs=[pl.BlockSpec((tm, width), lambda i: (i, 0)),
                  pl.BlockSpec((HALO_ROWS, width), lambda i: (jnp.maximum(i * hb - 1, 0), 0)),
                  pl.BlockSpec((HALO_ROWS, width), lambda i: (jnp.minimum((i + 1) * hb, nblk8 - 1), 0))]
                 + [full(c) for c in consts],
        out_specs=[out_spec] * n_out,
        out_shape=[jax.ShapeDtypeStruct((m, RWKV_WIDTH), dt) for dt in PREP_OUT_DTYPES],
        compiler_params=pltpu.CompilerParams(
            dimension_semantics=("arbitrary",),
            vmem_limit_bytes=_vmem(2 * tm * width * 4 + 2 * n_out * tm * RWKV_WIDTH * 4
                                   + 12 * tm * RWKV_WIDTH * 4 + 8 * 1024 * 1024)),
        name="rwkv_prep",
    )(slab, slab, slab, *consts)


def _stack(x, m0):
    return jnp.concatenate([jnp.where(m0, x, 0.0), jnp.where(m0, 0.0, x)], axis=0)


def _scan_streams(r_ref, v_ref, kk_ref, lw_ref, kd_ref, b_ref, reverse):
    L = CHUNK
    lw = lw_ref[...]
    ti = lax.broadcasted_iota(jnp.int32, (L, L), 0)
    tj = lax.broadcasted_iota(jnp.int32, (L, L), 1)
    tri = jnp.where((ti <= tj) if reverse else (ti >= tj), 1.0, 0.0).astype(BF16)
    lw_hi = lw.astype(BF16)
    lw_lo = (lw - lw_hi.astype(F32)).astype(BF16)
    c = _dot(tri, lw_hi) + _dot(tri, lw_lo)
    p_cum = jnp.exp(c)
    p_inv = jnp.exp(-c)
    p_prev = jnp.exp(c - lw)
    last = 0 if reverse else L - 1
    p_end = p_cum[last:last + 1, :]
    b_t = b_ref[...] * p_inv
    k_t = kd_ref[...] * p_inv
    gi = lax.broadcasted_iota(jnp.int32, (2 * L, 2 * L), 0)
    gj = lax.broadcasted_iota(jnp.int32, (2 * L, 2 * L), 1)
    same = (gi // L) == (gj // L)
    strict = same & ((gi < gj) if reverse else (gi > gj))
    incl = same & ((gi <= gj) if reverse else (gi >= gj))
    return dict(a=-kk_ref[...] * p_prev, r=r_ref[...] * p_cum, b=b_t, k=k_t, bh=b_t * p_end, kh=k_t * p_end,
                v=v_ref[...], p_end=p_end, strict=strict, incl=incl)


def _scan_kernel(rf, vf, kkf, lwf, kdf, bf, rb, vb, kkb, lwb, kdb, bb, *rest, n_cast):
    cast_in, (yf, yb), cast_out, s_ref = rest[:n_cast], rest[n_cast:n_cast + 2], rest[n_cast + 2:-1], rest[-1]

    @pl.when(pl.program_id(0) == 0)
    def _():
        s_ref[...] = jnp.zeros_like(s_ref)

    for src, dst in zip(cast_in, cast_out):
        dst[...] = src[...].astype(dst.dtype)

    L = CHUNK
    streams = (_scan_streams(rf, vf, kkf, lwf, kdf, bf, False), _scan_streams(rb, vb, kkb, lwb, kdb, bb, True))
    y_refs = (yf, yb)
    m0 = lax.broadcasted_iota(jnp.int32, (L, PAIR), 1) < RWKV_HEAD_DIM
    chains = [(d, p) for d in range(2) for p in range(N_PAIRS)]

    def side(name, d, p):
        return streams[d][name][:, p * PAIR:(p + 1) * PAIR]

    def stacked(name, d, p):
        return _stack(side(name, d, p), m0)

    head0_rows = (lax.broadcasted_iota(jnp.int32, (4 * L, PAIR), 0) // L) % 2 == 0
    same_head = ((lax.broadcasted_iota(jnp.int32, (PAIR, PAIR), 0) // RWKV_HEAD_DIM)
                 == (lax.broadcasted_iota(jnp.int32, (PAIR, PAIR), 1) // RWKV_HEAD_DIM))
    ar = [jnp.concatenate([stacked("a", d, p), stacked("r", d, p)], axis=0).astype(BF16) for d, p in chains]
    bk = [jnp.concatenate([side("b", d, p), side("k", d, p)], axis=0).astype(BF16) for d, p in chains]
    v_s = [stacked("v", d, p).astype(BF16) for d, p in chains]
    state = [s_ref[d, p] for d, p in chains]
    g = [_dot_nt(x, z) for x, z in zip(ar, bk)]
    xs = [_dot_nt(jnp.concatenate([side("a", d, p), side("r", d, p)], axis=0).astype(BF16), s.astype(BF16))
          for (d, p), s in zip(chains, state)]
    a_ab, a_kv, a_rb = [], [], []
    for (d, _), gc in zip(chains, g):
        strict, incl = streams[d]["strict"], streams[d]["incl"]
        gr = pltpu.roll(gc, RWKV_HEAD_DIM, 1)
        vs_b = jnp.where(head0_rows, gc, gr)
        vs_k = jnp.where(head0_rows, gr, gc)
        a_ab.append(jnp.where(strict, vs_b[:2 * L], 0.0).astype(BF16))
        a_kv.append(jnp.concatenate([jnp.where(strict, vs_k[:2 * L], 0.0),
                                     jnp.where(incl, vs_k[2 * L:], 0.0)], axis=0).astype(BF16))
        a_rb.append(jnp.where(incl, vs_b[2 * L:], 0.0).astype(BF16))
    av = [_dot(a, v) for a, v in zip(a_kv, v_s)]
    w = [_stack(x[:L], m0) + y[:2 * L] for x, y in zip(xs, av)]
    am = a_ab
    w = [wc + _dot(a, wc.astype(BF16)) for a, wc in zip(am, w)]
    for _ in range(5):
        am = [_dot(a, a).astype(BF16) for a in am]
        w = [wc + _dot(a, wc.astype(BF16)) for a, wc in zip(am, w)]
    sa = [wc.astype(BF16) for wc in w]
    ys = [y[2 * L:] + _dot(a, s) for y, a, s in zip(av, a_rb, sa)]
    for (d, p), x, yc in zip(chains, xs, ys):
        y_refs[d][:, p * PAIR:(p + 1) * PAIR] = x[L:] + yc[:L] + yc[L:]
    for (d, p), s_old, wc in zip(chains, state, w):
        sa_v = jnp.concatenate([wc[:L] + wc[L:], side("v", d, p)], axis=0).astype(BF16)
        bhkh = jnp.concatenate([side("bh", d, p), side("kh", d, p)], axis=0).astype(BF16)
        upd = jnp.where(same_head, _dot_tn(sa_v, bhkh), 0.0)
        s_ref[d, p] = s_old * side("p_end", d, p) + upd


def _cast_slices(w, l, n_steps):
    _, n_rows, n_cols = w.shape
    n_blk = 1
    while n_blk * 2 <= n_steps and n_rows % (n_blk * 2) == 0 and (n_rows // (n_blk * 2)) % 16 == 0:
        n_blk *= 2
    rows = n_rows // n_blk
    step = lambda i: jnp.minimum(i, n_blk - 1)
    return (pl.BlockSpec((None, rows, n_cols), lambda i: (l, step(i), 0)),
            pl.BlockSpec((rows, n_cols), lambda i: (step(i), 0)),
            jax.ShapeDtypeStruct((n_rows, n_cols), BF16))


def _rwkv_scan(r, v, kk, lwf, lwb, kdf, kdb, bf, bb, s_len, n_ctx, cast_weights, l):
    m = r.shape[0]
    nc = m // CHUNK
    nc_lat = s_len // CHUNK
    nc_ctx = n_ctx // CHUNK
    fwd = lambda i: (jnp.where(i < nc_ctx, nc_lat + i, i - nc_ctx), 0)
    bwd = lambda i: (nc - 1 - i, 0)
    fs = pl.BlockSpec((CHUNK, RWKV_WIDTH), fwd)
    bs = pl.BlockSpec((CHUNK, RWKV_WIDTH), bwd)
    cast_in, cast_out, cast_shapes = zip(*[_cast_slices(w, l, nc) for w in cast_weights])
    cast_bytes = sum(spec.block_shape[0] * spec.block_shape[1] * 6 for spec in cast_out)
    outs = pl.pallas_call(
        functools.partial(_scan_kernel, n_cast=len(cast_weights)),
        grid=(nc,),
        in_specs=[fs] * 6 + [bs] * 6 + list(cast_in),
        out_specs=[fs, bs] + list(cast_out),
        out_shape=[jax.ShapeDtypeStruct((m, RWKV_WIDTH), F32)] * 2 + list(cast_shapes),
        scratch_shapes=[pltpu.VMEM((2, N_PAIRS, PAIR, PAIR), F32)],
        compiler_params=pltpu.CompilerParams(
            dimension_semantics=("arbitrary",),
            vmem_limit_bytes=_vmem(2 * 14 * CHUNK * RWKV_WIDTH * 4 + 2 * N_PAIRS * PAIR * PAIR * 4
                                   + 2 * cast_bytes + 24 * 1024 * 1024)),
        name="rwkv_scan",
    )(r, v, kk, lwf, kdf, bf, r, v, kk, lwb, kdb, bb, *cast_weights)
    return outs[0], outs[1], outs[2:]


def _out_kernel(att_ref, yf_ref, yb_ref, g_ref, bon_ref, x_ref, wa_ref, wr_ref, lnw_ref, lnb_ref,
                seg_ref, exp_ref, gpm_ref, gt1_ref, gmlp_ref, sc2_ref, sh2_ref, x1_ref, h2_ref):
    mix_att = _dot(att_ref[...], wa_ref[...])
    inv_n = 1.0 / RWKV_HEAD_DIM
    seg, expand = seg_ref[...], exp_ref[...]
    rows = att_ref.shape[0] // OUT_SUB_BLOCKS
    blocks = [pl.ds(n * rows, rows) for n in range(OUT_SUB_BLOCKS)]
    y = [yf_ref[b, :] + yb_ref[b, :] for b in blocks]
    s1 = [_bdot(v, seg) for v in y]
    yc = [v - _split_dot(s, expand) * inv_n for v, s in zip(y, s1)]
    s2 = [_bdot(c * c, seg) for c in yc]
    var = [_bdot(s, expand) * inv_n for s in s2]
    rec = [((c * lax.rsqrt(vr + GN_EPS) * lnw_ref[...] + lnb_ref[...]) + bon_ref[b, :]) * g_ref[b, :]
           for c, vr, b in zip(yc, var, blocks)]
    mix = mix_att + _bdot(jnp.concatenate(rec, axis=0), wr_ref[...])
    ms = jnp.mean(mix * mix, axis=-1, keepdims=True)
    x1 = x_ref[...] + (mix * lax.rsqrt(ms + NORM_EPS)) * (gt1_ref[...] * gpm_ref[...])
    x1_ref[...] = x1
    ms2 = jnp.mean(x1 * x1, axis=-1, keepdims=True)
    h2 = (x1 * lax.rsqrt(ms2 + NORM_EPS)) * (gmlp_ref[...] * (1.0 + sc2_ref[...])) + sh2_ref[...]
    h2_ref[...] = h2.astype(BF16)


def _out_proj(att, yf, yb, g, bonus, x, w_out, lnw, lnb, seg, expand, gpm, gt1, gmlp, sc2, sh2, tm):
    s_len, d = x.shape
    assert ATT_WIDTH == RWKV_WIDTH and w_out.shape == (ATT_WIDTH + RWKV_WIDTH, d)
    row = lambda w: pl.BlockSpec((tm, w), lambda i: (i, 0))
    full = lambda arr: pl.BlockSpec(arr.shape, lambda i: (0,) * arr.ndim)
    w_half = lambda n: pl.BlockSpec((ATT_WIDTH, d), lambda i: (n, 0))
    consts = (lnw, lnb, seg, expand, gpm, gt1, gmlp, sc2, sh2)
    return pl.pallas_call(
        _out_kernel,
        grid=(s_len // tm,),
        in_specs=[row(ATT_WIDTH)] + [row(RWKV_WIDTH)] * 4 + [row(d), w_half(0), w_half(1)]
                 + [full(c) for c in consts],
        out_specs=[row(d), row(d)],
        out_shape=[jax.ShapeDtypeStruct((s_len, d), F32), jax.ShapeDtypeStruct((s_len, d), BF16)],
        compiler_params=pltpu.CompilerParams(
            dimension_semantics=("arbitrary",),
            vmem_limit_bytes=_vmem(2 * tm * (ATT_WIDTH * 2 + 4 * RWKV_WIDTH * 4 + d * 4 + d * 4 + d * 2)
                                   + 2 * 2 * d * d + 8 * tm * d * 4 + 4 * 1024 * 1024)),
        name="out_proj",
    )(att, yf, yb, g, bonus, x, w_out, w_out, *consts)


def _mlp_kernel(h_ref, w1_ref, w2_ref, x1_ref, g_ref, gt_ref, o_ref, acc_ref):
    f = pl.program_id(1)

    @pl.when(f == 0)
    def _():
        acc_ref[...] = jnp.zeros_like(acc_ref)

    u = jnp.maximum(_dot(h_ref[...], w1_ref[...]), 0.0)
    acc_ref[...] += _dot((u * u).astype(BF16), w2_ref[...])

    @pl.when(f == pl.num_programs(1) - 1)
    def _():
        ff = acc_ref[...]
        ms = jnp.mean(ff * ff, axis=-1, keepdims=True)
        o_ref[...] = x1_ref[...] + (ff * lax.rsqrt(ms + NORM_EPS)) * (gt_ref[...] * g_ref[...])


def _mlp(h2, w1, w2, x1, gain, gt2, tm, tf):
    s_len, d = x1.shape
    dff = w1.shape[1]
    return pl.pallas_call(
        _mlp_kernel,
        grid=(s_len // tm, dff // tf),
        in_specs=[pl.BlockSpec((tm, d), lambda i, f: (i, 0)),
                  pl.BlockSpec((d, tf), lambda i, f: (0, f)),
                  pl.BlockSpec((tf, d), lambda i, f: (f, 0)),
                  pl.BlockSpec((tm, d), lambda i, f: (i, 0)),
                  pl.BlockSpec((1, d), lambda i, f: (0, 0)),
                  pl.BlockSpec((1, d), lambda i, f: (0, 0))],
        out_specs=pl.BlockSpec((tm, d), lambda i, f: (i, 0)),
        out_shape=jax.ShapeDtypeStruct((s_len, d), F32),
        scratch_shapes=[pltpu.VMEM((tm, d), F32)],
        compiler_params=pltpu.CompilerParams(
            dimension_semantics=("arbitrary", "arbitrary"),
            vmem_limit_bytes=_vmem(2 * (tm * d * 2 + 2 * d * tf * 2 + 2 * tm * d * 4)
                                   + tm * d * 4 + 3 * tm * tf * 4 + 4 * 1024 * 1024)),
        name="mlp",
    )(h2, w1, w2, x1, gain, gt2)


def _rope_tables(s_len, rows_padded):
    rows = s_len // GRID_W
    n_freq = ATT_HEAD_DIM // 4
    inv_freq = ROPE_BASE ** (-jnp.arange(n_freq, dtype=F32) / n_freq)
    ang_r = jnp.arange(rows, dtype=F32)[:, None] * inv_freq[None, :]
    ang_c = jnp.arange(GRID_W, dtype=F32)[:, None] * inv_freq[None, :]
    zr = jnp.zeros((rows, 2 * n_freq), F32)
    zc = jnp.zeros((GRID_W, 2 * n_freq), F32)
    pad = lambda t: jnp.pad(t, ((0, rows_padded - rows), (0, 0)))[:, None, :]
    row_cos = pad(jnp.concatenate([jnp.cos(ang_r), jnp.cos(ang_r), zr], axis=-1))
    row_sin = pad(jnp.concatenate([-jnp.sin(ang_r), jnp.sin(ang_r), zr], axis=-1))
    col_cos = jnp.concatenate([zc, jnp.cos(ang_c), jnp.cos(ang_c)], axis=-1)
    col_sin = jnp.concatenate([zc, -jnp.sin(ang_c), jnp.sin(ang_c)], axis=-1)
    return row_cos, row_sin, col_cos, col_sin


class _Tiles(NamedTuple):
    cat_rows: int
    prep_rows: int
    out_rows: int
    mlp_rows: int
    mlp_ff: int


def _tile_sizes(s_len, n_ctx):
    cat_rows = 768 if (s_len + n_ctx) % 768 == 0 and s_len % 768 + n_ctx == 768 else n_ctx
    return _Tiles(cat_rows=cat_rows, prep_rows=256, out_rows=512, mlp_rows=512, mlp_ff=1024)


def _block_diag2(w):
    _, k, n = w.shape
    z = jnp.zeros((k, n), w.dtype)
    return jnp.concatenate([jnp.concatenate([w[0], z], axis=1), jnp.concatenate([z, w[1]], axis=1)], axis=0)


def _layer(x2, ctx2, c8, l, w_ada, b_ada, g_pre_mix, g_post_mix, g_pre_mlp, g_post_mlp, w_in, attn_sink,
           rwkv_mu_prev, rwkv_mu_next, rwkv_w0, rwkv_w2, rwkv_a0, rwkv_a2, rwkv_g2, rwkv_k_k, rwkv_k_a,
           rwkv_r_k, rwkv_ln_w, rwkv_ln_b, w_out, w_mlp_in, w_mlp_out):
    s_len, d = x2.shape
    n_ctx = ctx2.shape[0]
    row = lambda v: v.reshape(1, -1)

    mod = _ada(c8, w_ada[l], row(b_ada[l]))
    sh1, sc1, gt1, sh2, sc2, gt2 = [mod[:2, j * d:(j + 1) * d] for j in range(N_MOD)]

    tiles = _tile_sizes(s_len, n_ctx)
    h = _norm_mod(x2, ctx2, row(g_pre_mix[l]), sc1.reshape(2, 1, d), sh1.reshape(2, 1, d), tiles.cat_rows)

    rope_tables = _rope_tables(s_len, ((s_len + n_ctx) // tiles.cat_rows) * (tiles.cat_rows // GRID_W))
    p_in = _proj(h, w_in[l].astype(BF16), rope_tables, s_len, tiles.cat_rows, w_in.shape[2] // 3,
                 N_Q_HEADS + N_KV_HEADS)

    att = _attention(p_in, attn_sink[l], s_len, n_ctx)

    head = jnp.arange(RWKV_WIDTH) // RWKV_HEAD_DIM
    seg = (head[:, None] == jnp.arange(LANES)[None, :]).astype(BF16)
    expand = seg.T
    streams = _rwkv_prep(
        p_in, row(rwkv_mu_prev[l]), row(rwkv_mu_next[l]),
        (0.5 * _block_diag2(rwkv_w2[l])).astype(BF16), 0.5 * rwkv_w0[l].reshape(1, -1),
        (0.5 * _block_diag2(rwkv_a2[l])).astype(BF16), 0.5 * rwkv_a0[l].reshape(1, -1),
        rwkv_g2[l].astype(BF16), row(rwkv_k_k[l]), row(rwkv_k_a[l]), rwkv_r_k[l].reshape(1, -1),
        seg, expand, s_len, n_ctx, tiles.prep_rows)
    r, v, kk, g, bonus, lwf, lwb, kdf, kdb, bf, bb = streams
    yf, yb, (w_out_b, w_mlp_in_b, w_mlp_out_b) = _rwkv_scan(
        r, v, kk, lwf, lwb, kdf, kdb, bf, bb, s_len, n_ctx, (w_out, w_mlp_in, w_mlp_out), l)

    x1, h2 = _out_proj(att, yf, yb, g, bonus, x2, w_out_b,
                       row(rwkv_ln_w[l]), row(rwkv_ln_b[l]), seg, expand,
                       row(g_post_mix[l]), gt1[0:1], row(g_pre_mlp[l]), sc2[0:1], sh2[0:1], tiles.out_rows)
    return _mlp(h2, w_mlp_in_b, w_mlp_out_b, x1, row(g_post_mlp[l]), gt2[0:1], tiles.mlp_rows, tiles.mlp_ff)


def kernel(x, c, ctx, c_ctx, w_ada, b_ada, g_pre_mix, g_post_mix, g_pre_mlp, g_post_mlp, w_in, attn_sink,
           rwkv_mu_prev, rwkv_mu_next, rwkv_w0, rwkv_w2, rwkv_a0, rwkv_a2, rwkv_g2, rwkv_k_k, rwkv_k_a,
           rwkv_r_k, rwkv_ln_w, rwkv_ln_b, w_out, w_mlp_in, w_mlp_out):
    b, s_len, d = x.shape
    depth = w_ada.shape[0]
    assert b == 1 and depth == 1, "single sample, single layer (context stream is not carried to a next layer)"
    c8 = jnp.zeros((8, d), F32).at[0].set(c[0]).at[1].set(c_ctx)
    out = _layer(x[0], ctx[0], c8, 0, w_ada, b_ada, g_pre_mix, g_post_mix, g_pre_mlp, g_post_mlp, w_in,
                 attn_sink, rwkv_mu_prev, rwkv_mu_next, rwkv_w0, rwkv_w2, rwkv_a0, rwkv_a2, rwkv_g2,
                 rwkv_k_k, rwkv_k_a, rwkv_r_k, rwkv_ln_w, rwkv_ln_b, w_out, w_mlp_in, w_mlp_out)
    return out[None]
```

```python
import functools
from typing import NamedTuple

import jax
import jax.numpy as jnp
from jax import lax
from jax.experimental import pallas as pl
from jax.experimental.pallas import tpu as pltpu

F32 = jnp.float32
BF16 = jnp.bfloat16

GRID_W = 64
ATT_HEAD_DIM = 128
N_Q_HEADS = 8
N_KV_HEADS = 2
GQA_GROUP = N_Q_HEADS // N_KV_HEADS
ATT_WIDTH = N_Q_HEADS * ATT_HEAD_DIM
KV_WIDTH = N_KV_HEADS * ATT_HEAD_DIM
ATT_IN = ATT_WIDTH + 2 * KV_WIDTH
WINDOW = 128
BLOCK = 128
ROPE_BASE = 10000.0
RWKV_HEAD_DIM = 64
RWKV_WIDTH = 1024
N_RWKV_HEADS = RWKV_WIDTH // RWKV_HEAD_DIM
LORA_W = 64
LORA_A = 64
LORA_G = 128
N_MOD = 6
NORM_EPS = 1e-6
GN_EPS = 64e-5
DECAY_SCALE = 0.6065306597126334

LANES = 128
CHUNK = 64
PAIR = 2 * RWKV_HEAD_DIM
N_PAIRS = RWKV_WIDTH // PAIR
PREP_OUT_DTYPES = (BF16, BF16, BF16, BF16, BF16, F32, F32, BF16, BF16, BF16, BF16)
ATTN_BLOCKS_PER_STEP = 4
HALO_ROWS = 16
OUT_SUB_BLOCKS = 4
VMEM_CAP = 56 * 1024 * 1024


def _vmem(nbytes):
    return int(min(VMEM_CAP, max(16 * 1024 * 1024, nbytes)))


def _dot(a, b):
    return jnp.dot(a, b, preferred_element_type=F32)


def _dot_nt(a, b):
    return lax.dot_general(a, b, (((1,), (1,)), ((), ())), preferred_element_type=F32)


def _dot_tn(a, b):
    return lax.dot_general(a, b, (((0,), (0,)), ((), ())), preferred_element_type=F32)


def _bdot(a, b):
    return _dot(a.astype(BF16), b.astype(BF16))


def _bdot_nt(a, b):
    return _dot_nt(a.astype(BF16), b.astype(BF16))


def _bdot_tn(a, b):
    return _dot_tn(a.astype(BF16), b.astype(BF16))


def _split_dot(x, m):
    hi = x.astype(BF16)
    lo = (x - hi.astype(F32)).astype(BF16)
    return _dot(hi, m) + _dot(lo, m)


def _sigmoid(z):
    return 0.5 * jnp.tanh(0.5 * z) + 0.5


def _ada_kernel(c_ref, w_ref, b_ref, o_ref):
    c = c_ref[...]
    s = c * jax.nn.sigmoid(c)
    o_ref[...] = _bdot(s, w_ref[...]) + b_ref[...]


def _ada(c8, w_ada, b_ada):
    d, n = w_ada.shape
    tn = 1024
    return pl.pallas_call(
        _ada_kernel,
        grid=(n // tn,),
        in_specs=[pl.BlockSpec((8, d), lambda j: (0, 0)),
                  pl.BlockSpec((d, tn), lambda j: (0, j)),
                  pl.BlockSpec((1, tn), lambda j: (0, j))],
        out_specs=pl.BlockSpec((8, tn), lambda j: (0, j)),
        out_shape=jax.ShapeDtypeStruct((8, n), F32),
        compiler_params=pltpu.CompilerParams(
            dimension_semantics=("arbitrary",),
            vmem_limit_bytes=_vmem(2 * d * tn * 4 + 4 * 1024 * 1024)),
        name="ada",
    )(c8, w_ada, b_ada)


def _norm_mod_kernel(x_ref, c_ref, g_ref, sc_ref, sh_ref, o_ref, *, lat_last):
    def emit(x, who, rows):
        ms = jnp.mean(x * x, axis=-1, keepdims=True)
        y = (x * lax.rsqrt(ms + NORM_EPS)) * (g_ref[...] * (1.0 + sc_ref[who])) + sh_ref[who]
        o_ref[rows, :] = y.astype(BF16)

    i = pl.program_id(0)
    last = pl.num_programs(0) - 1

    @pl.when(i < last)
    def _():
        emit(x_ref[...], 0, slice(None))

    @pl.when(i == last)
    def _():
        if lat_last:
            emit(x_ref[:lat_last, :], 0, slice(0, lat_last))
        emit(c_ref[...], 1, slice(lat_last, None))


def _norm_mod(x2, ctx2, gain, sc2, sh2, tm):
    s_len, d = x2.shape
    n_ctx = ctx2.shape[0]
    n_tiles = (s_len + n_ctx) // tm
    lat_last = s_len - (n_tiles - 1) * tm
    assert n_tiles * tm == s_len + n_ctx and lat_last + n_ctx == tm and lat_last % 16 == 0
    both = pl.BlockSpec((2, 1, d), lambda i: (0, 0, 0))
    return pl.pallas_call(
        functools.partial(_norm_mod_kernel, lat_last=lat_last),
        grid=(n_tiles,),
        in_specs=[pl.BlockSpec((tm, d), lambda i: (jnp.minimum(i, (s_len - 1) // tm), 0)),
                  pl.BlockSpec((n_ctx, d), lambda i: (0, 0)),
                  pl.BlockSpec((1, d), lambda i: (0, 0)),
                  both, both],
        out_specs=pl.BlockSpec((tm, d), lambda i: (i, 0)),
        out_shape=jax.ShapeDtypeStruct((n_tiles * tm, d), BF16),
        compiler_params=pltpu.CompilerParams(
            dimension_semantics=("arbitrary",),
            vmem_limit_bytes=_vmem(2 * tm * d * 6 + 2 * n_ctx * d * 4 + 4 * tm * d * 4 + 4 * 1024 * 1024)),
        name="norm_mod",
    )(x2, ctx2, gain, sc2, sh2)


def _rope_heads(acc, cos, sin, n_heads):
    lane = lax.broadcasted_iota(jnp.int32, (1, ATT_HEAD_DIM), 1)
    first = (lane % 64) < 32
    outs = []
    for h in range(n_heads):
        xh = acc[:, h * ATT_HEAD_DIM:(h + 1) * ATT_HEAD_DIM]
        partner = jnp.where(first, pltpu.roll(xh, 96, 1), pltpu.roll(xh, 32, 1))
        outs.append(xh * cos + partner * sin)
    return outs


def _proj_kernel(a_ref, w_ref, rcos_ref, rsin_ref, ccos_ref, csin_ref, o_ref, *, rope_heads, s_len):
    acc = _dot(a_ref[...], w_ref[...])
    tm = a_ref.shape[0]
    i, j = pl.program_id(0), pl.program_id(1)
    row = i * tm + lax.broadcasted_iota(jnp.int32, (tm, 1), 0)
    rotate = row < jnp.where(j == 0, s_len, 0)
    cos = jnp.where(rotate, (rcos_ref[...] + ccos_ref[...][None]).reshape(tm, ATT_HEAD_DIM), 1.0)
    sin = jnp.where(rotate, (rsin_ref[...] + csin_ref[...][None]).reshape(tm, ATT_HEAD_DIM), 0.0)
    for h, r in enumerate(_rope_heads(acc, cos, sin, rope_heads)):
        o_ref[:, h * ATT_HEAD_DIM:(h + 1) * ATT_HEAD_DIM] = r.astype(o_ref.dtype)
    o_ref[:, rope_heads * ATT_HEAD_DIM:] = acc[:, rope_heads * ATT_HEAD_DIM:].astype(o_ref.dtype)


def _proj(a, w, rope_tables, s_len, tm, tn, rope_heads):
    m, k = a.shape
    n = w.shape[1]
    n_tiles = m // tm
    assert rope_heads * ATT_HEAD_DIM <= tn and n % tn == 0 and tm % GRID_W == 0
    g = tm // GRID_W
    row_spec = pl.BlockSpec((g, 1, ATT_HEAD_DIM), lambda i, j: (i, 0, 0))
    col_spec = pl.BlockSpec((GRID_W, ATT_HEAD_DIM), lambda i, j: (0, 0))
    return pl.pallas_call(
        functools.partial(_proj_kernel, rope_heads=rope_heads, s_len=s_len),
        grid=(n_tiles, n // tn),
        in_specs=[pl.BlockSpec((tm, k), lambda i, j: (i, 0)),
                  pl.BlockSpec((k, tn), lambda i, j: (0, j)),
                  row_spec, row_spec, col_spec, col_spec],
        out_specs=pl.BlockSpec((tm, tn), lambda i, j: (i, j)),
        out_shape=jax.ShapeDtypeStruct((m, n), BF16),
        compiler_params=pltpu.CompilerParams(
            dimension_semantics=("arbitrary", "arbitrary"),
            vmem_limit_bytes=_vmem(2 * (tm * k * 2 + k * tn * 2 + tm * tn * 2 + 2 * tm * 512)
                                   + tm * tn * 8 + 4 * 1024 * 1024)),
        name="proj_in",
    )(a, w, *rope_tables)


def _attn_kernel(sink_ref, q_ref, *rest, nb, n_ctx):
    nq = ATTN_BLOCKS_PER_STEP
    k_refs, v_refs = rest[:nq + 2], rest[nq + 2:2 * (nq + 2)]
    kx_ref, vx_ref, o_ref = rest[2 * (nq + 2):]
    i = pl.program_id(0)
    scale = ATT_HEAD_DIM ** -0.5
    rows = GQA_GROUP * BLOCK
    span = 3 * BLOCK
    ncol = span + n_ctx
    row = lax.broadcasted_iota(jnp.int32, (rows, ncol), 0) % BLOCK
    col = lax.broadcasted_iota(jnp.int32, (rows, ncol), 1)
    in_window = jnp.abs(col - BLOCK - row) <= WINDOW
    chains = [(t, hk) for t in range(nq) for hk in range(N_KV_HEADS)]
    valid = []
    for t in range(nq):
        blk = nq * i + t
        lo = jnp.where(blk > 0, 0, BLOCK)
        hi = jnp.where(blk < nb - 1, span, 2 * BLOCK)
        valid.append((in_window & (col >= lo) & (col < hi)) | (col >= span))
    ks = lambda hk: slice(hk * ATT_HEAD_DIM, (hk + 1) * ATT_HEAD_DIM)
    q = [(jnp.concatenate(
        [q_ref[t * BLOCK:(t + 1) * BLOCK, (hk * GQA_GROUP + g) * ATT_HEAD_DIM:(hk * GQA_GROUP + g + 1) * ATT_HEAD_DIM]
         for g in range(GQA_GROUP)], axis=0).astype(F32) * scale).astype(BF16) for t, hk in chains]
    k_all = [jnp.concatenate([k_refs[t][:, ks(hk)], k_refs[t + 1][:, ks(hk)], k_refs[t + 2][:, ks(hk)],
                              kx_ref[:, ks(hk)]], axis=0).astype(BF16) for t, hk in chains]
    v_all = [jnp.concatenate([v_refs[t][:, ks(hk)], v_refs[t + 1][:, ks(hk)], v_refs[t + 2][:, ks(hk)],
                              vx_ref[:, ks(hk)]], axis=0).astype(BF16) for t, hk in chains]
    sink = [jnp.concatenate([jnp.full((BLOCK, 1), sink_ref[hk * GQA_GROUP + g], F32) for g in range(GQA_GROUP)],
                            axis=0) for _, hk in chains]
    s = [jnp.where(valid[t], _dot_nt(qh, kh), -jnp.inf) for (t, _), qh, kh in zip(chains, q, k_all)]
    m = [jnp.maximum(jnp.max(sh, axis=-1, keepdims=True), sk) for sh, sk in zip(s, sink)]
    e = [jnp.exp(sh - mh) for sh, mh in zip(s, m)]
    denom = [jnp.sum(eh, axis=-1, keepdims=True) + jnp.exp(sk - mh) for eh, sk, mh in zip(e, sink, m)]
    o = [_dot(eh.astype(BF16), vh) * (1.0 / dh) for eh, vh, dh in zip(e, v_all, denom)]
    for (t, hk), oc in zip(chains, o):
        for g in range(GQA_GROUP):
            h = hk * GQA_GROUP + g
            o_ref[t * BLOCK:(t + 1) * BLOCK, h * ATT_HEAD_DIM:(h + 1) * ATT_HEAD_DIM] = (
                oc[g * BLOCK:(g + 1) * BLOCK].astype(BF16))


def _attention(qkv, sink, s_len, n_ctx):
    nq = ATTN_BLOCKS_PER_STEP
    nb = s_len // BLOCK
    assert nb % nq == 0 and s_len % n_ctx == 0
    kcol = ATT_WIDTH // KV_WIDTH
    vcol = kcol + 1
    ctx_blk = s_len // n_ctx
    kv = lambda u, c: pl.BlockSpec((BLOCK, KV_WIDTH), lambda i: (jnp.clip(nq * i - 1 + u, 0, nb - 1), c))
    return pl.pallas_call(
        functools.partial(_attn_kernel, nb=nb, n_ctx=n_ctx),
        grid=(nb // nq,),
        in_specs=[pl.BlockSpec(memory_space=pltpu.SMEM),
                  pl.BlockSpec((nq * BLOCK, ATT_WIDTH), lambda i: (i, 0))]
                 + [kv(u, kcol) for u in range(nq + 2)] + [kv(u, vcol) for u in range(nq + 2)]
                 + [pl.BlockSpec((n_ctx, KV_WIDTH), lambda i: (ctx_blk, kcol)),
                    pl.BlockSpec((n_ctx, KV_WIDTH), lambda i: (ctx_blk, vcol))],
        out_specs=pl.BlockSpec((nq * BLOCK, ATT_WIDTH), lambda i: (i, 0)),
        out_shape=jax.ShapeDtypeStruct((s_len, ATT_WIDTH), BF16),
        compiler_params=pltpu.CompilerParams(
            dimension_semantics=("arbitrary",),
            vmem_limit_bytes=_vmem(40 * 1024 * 1024)),
        name="attention",
    )(sink, qkv, *([qkv] * (2 * (nq + 2) + 2)))


def _prep_kernel(x_ref, xp_ref, xn_ref, mup_ref, mun_ref, w2_ref, w0_ref, a2_ref, a0_ref, g2_ref,
                 kk_ref, ka_ref, rk_ref, seg_ref, exp_ref,
                 r_o, v_o, kk_o, g_o, bon_o, lwf_o, lwb_o, kdf_o, kdb_o, bf_o, bb_o,
                 *, tm, start_tiles, end_tiles):
    i = pl.program_id(0)
    r_w = RWKV_WIDTH
    x = x_ref[:, ATT_IN:].astype(F32)
    is_start = functools.reduce(jnp.logical_or, [i == t for t in start_tiles])
    is_end = functools.reduce(jnp.logical_or, [i == t for t in end_tiles])
    keep_prev = jnp.full((1, 1), jnp.where(is_start, 0, 1), jnp.int32) == 1
    keep_next = jnp.full((1, 1), jnp.where(is_end, 0, 1), jnp.int32) == 1
    prev_row = jnp.where(keep_prev, xp_ref[HALO_ROWS - 1:HALO_ROWS, ATT_IN:].astype(F32), 0.0)
    next_row = jnp.where(keep_next, xn_ref[0:1, ATT_IN:].astype(F32), 0.0)
    xp = pltpu.roll(x, 1, 0)
    xn = pltpu.roll(x, tm - 1, 0)
    row8 = lax.broadcasted_iota(jnp.int32, (8, 1), 0)
    xp = jnp.concatenate([jnp.where(row8 == 0, prev_row, xp[:8]), xp[8:]], axis=0)
    xn = jnp.concatenate([xn[:tm - 8], jnp.where(row8 == 7, next_row, xn[tm - 8:])], axis=0)
    mup = mup_ref[...]
    mun = mun_ref[...]
    s = x * (1.0 - mup - mun) + xp * mup + xn * mun

    r = s[:, :r_w]
    k = s[:, r_w:2 * r_w]
    v = s[:, 2 * r_w:3 * r_w]
    o1 = 3 * r_w
    o2 = o1 + 2 * LORA_W
    o3 = o2 + 2 * LORA_A
    w_lo = s[:, o1:o2]
    a_lo = s[:, o2:o3]
    g_lo = s[:, o3:]

    tw = jnp.tanh(w0_ref[...] + _bdot(jnp.tanh(w_lo), w2_ref[...]))
    ta = jnp.tanh(a0_ref[...] + _bdot(a_lo, a2_ref[...]))
    lw = (-0.5 * DECAY_SCALE) * tw - 0.5 * DECAY_SCALE
    g = _bdot(_sigmoid(g_lo), g2_ref[...])

    kkf = k * kk_ref[...]
    ssq = _split_dot(_bdot(kkf * kkf, seg_ref[...]), exp_ref[...])
    kk = kkf * jnp.minimum(lax.rsqrt(ssq), 1e12)
    kkh = k * (0.5 * ka_ref[...])
    k1 = k - kkh
    kd_f = k1 + kkh * ta[:, :r_w]
    kd_b = k1 + kkh * ta[:, r_w:]
    k_bonus = 0.5 * (kd_f + kd_b)
    bonus = _split_dot(_bdot(r * k_bonus * rk_ref[...], seg_ref[...]), exp_ref[...]) * v
    kk_half = 0.5 * kk

    r_o[...] = r.astype(r_o.dtype)
    v_o[...] = v.astype(v_o.dtype)
    kk_o[...] = kk.astype(kk_o.dtype)
    g_o[...] = g.astype(g_o.dtype)
    bon_o[...] = bonus.astype(bon_o.dtype)
    lwf_o[...] = lw[:, :r_w]
    lwb_o[...] = lw[:, r_w:]
    kdf_o[...] = kd_f.astype(kdf_o.dtype)
    kdb_o[...] = kd_b.astype(kdb_o.dtype)
    bf_o[...] = (kk_half + kk_half * ta[:, :r_w]).astype(bf_o.dtype)
    bb_o[...] = (kk_half + kk_half * ta[:, r_w:]).astype(bb_o.dtype)


def _rwkv_prep(slab, mup, mun, w2blk, w0cat, a2blk, a0cat, g2, k_k, k_a, r_k, seg, expand, s_len, n_ctx, tm):
    m, width = slab.shape
    nt = m // tm
    hb = tm // HALO_ROWS
    nblk8 = m // HALO_ROWS
    start_tiles = (0, s_len // tm)
    end_tiles = (s_len // tm - 1, nt - 1)
    full = lambda arr: pl.BlockSpec(arr.shape, lambda i: (0,) * arr.ndim)
    consts = (mup, mun, w2blk, w0cat, a2blk, a0cat, g2, k_k, k_a, r_k, seg, expand)
    out_spec = pl.BlockSpec((tm, RWKV_WIDTH), lambda i: (i, 0))
    n_out = len(PREP_OUT_DTYPES)
    return pl.pallas_call(
        functools.partial(_prep_kernel, tm=tm, start_tiles=start_tiles, end_tiles=end_tiles),
        grid=(nt,),
        in_specs=[pl.BlockSpec((tm, width), lambda i: (i, 0)),
                  pl.BlockSpec((HALO_ROWS, width), lambda i: (jnp.maximum(i * hb - 1, 0), 0)),
                  pl.BlockSpec((HALO_ROWS, width), lambda i: (jnp.minimum((i + 1) * hb, nblk8 - 1), 0))]
                 + [full(c) for c in consts],
        out_specs=[out_spec] * n_out,
        out_shape=[jax.ShapeDtypeStruct((m, RWKV_WIDTH), dt) for dt in PREP_OUT_DTYPES],
        compiler_params=pltpu.CompilerParams(
            dimension_semantics=("arbitrary",),
            vmem_limit_bytes=_vmem(2 * tm * width * 4 + 2 * n_out * tm * RWKV_WIDTH * 4
                                   + 12 * tm * RWKV_WIDTH * 4 + 8 * 1024 * 1024)),
        name="rwkv_prep",
    )(slab, slab, slab, *consts)


def _stack(x, m0):
    return jnp.concatenate([jnp.where(m0, x, 0.0), jnp.where(m0, 0.0, x)], axis=0)


def _scan_streams(r_ref, v_ref, kk_ref, lw_ref, kd_ref, b_ref, reverse):
    L = CHUNK
    lw = lw_ref[...]
    ti = lax.broadcasted_iota(jnp.int32, (L, L), 0)
    tj = lax.broadcasted_iota(jnp.int32, (L, L), 1)
    tri = jnp.where((ti <= tj) if reverse else (ti >= tj), 1.0, 0.0).astype(BF16)
    lw_hi = lw.astype(BF16)
    lw_lo = (lw - lw_hi.astype(F32)).astype(BF16)
    c = _dot(tri, lw_hi) + _dot(tri, lw_lo)
    p_cum = jnp.exp(c)
    p_inv = jnp.exp(-c)
    p_prev = jnp.exp(c - lw)
    last = 0 if reverse else L - 1
    p_end = p_cum[last:last + 1, :]
    b_t = b_ref[...] * p_inv
    k_t = kd_ref[...] * p_inv
    gi = lax.broadcasted_iota(jnp.int32, (2 * L, 2 * L), 0)
    gj = lax.broadcasted_iota(jnp.int32, (2 * L, 2 * L), 1)
    same = (gi // L) == (gj // L)
    strict = same & ((gi < gj) if reverse else (gi > gj))
    incl = same & ((gi <= gj) if reverse else (gi >= gj))
    return dict(a=-kk_ref[...] * p_prev, r=r_ref[...] * p_cum, b=b_t, k=k_t, bh=b_t * p_end, kh=k_t * p_end,
                v=v_ref[...], p_end=p_end, strict=strict, incl=incl)


def _scan_kernel(rf, vf, kkf, lwf, kdf, bf, rb, vb, kkb, lwb, kdb, bb, *rest, n_cast):
    cast_in, (yf, yb), cast_out, s_ref = rest[:n_cast], rest[n_cast:n_cast + 2], rest[n_cast + 2:-1], rest[-1]

    @pl.when(pl.program_id(0) == 0)
    def _():
        s_ref[...] = jnp.zeros_like(s_ref)

    for src, dst in zip(cast_in, cast_out):
        dst[...] = src[...].astype(dst.dtype)

    L = CHUNK
    streams = (_scan_streams(rf, vf, kkf, lwf, kdf, bf, False), _scan_streams(rb, vb, kkb, lwb, kdb, bb, True))
    y_refs = (yf, yb)
    m0 = lax.broadcasted_iota(jnp.int32, (L, PAIR), 1) < RWKV_HEAD_DIM
    chains = [(d, p) for d in range(2) for p in range(N_PAIRS)]

    def side(name, d, p):
        return streams[d][name][:, p * PAIR:(p + 1) * PAIR]

    def stacked(name, d, p):
        return _stack(side(name, d, p), m0)

    head0_rows = (lax.broadcasted_iota(jnp.int32, (4 * L, PAIR), 0) // L) % 2 == 0
    same_head = ((lax.broadcasted_iota(jnp.int32, (PAIR, PAIR), 0) // RWKV_HEAD_DIM)
                 == (lax.broadcasted_iota(jnp.int32, (PAIR, PAIR), 1) // RWKV_HEAD_DIM))
    ar = [jnp.concatenate([stacked("a", d, p), stacked("r", d, p)], axis=0).astype(BF16) for d, p in chains]
    bk = [jnp.concatenate([side("b", d, p), side("k", d, p)], axis=0).astype(BF16) for d, p in chains]
    v_s = [stacked("v", d, p).astype(BF16) for d, p in chains]
    state = [s_ref[d, p] for d, p in chains]
    g = [_dot_nt(x, z) for x, z in zip(ar, bk)]
    xs = [_dot_nt(x, s.astype(BF16)) for x, s in zip(ar, state)]
    a_ab, a_kv, a_rb = [], [], []
    for (d, _), gc in zip(chains, g):
        strict, incl = streams[d]["strict"], streams[d]["incl"]
        gr = pltpu.roll(gc, RWKV_HEAD_DIM, 1)
        vs_b = jnp.where(head0_rows, gc, gr)
        vs_k = jnp.where(head0_rows, gr, gc)
        a_ab.append(jnp.where(strict, vs_b[:2 * L], 0.0).astype(BF16))
        a_kv.append(jnp.concatenate([jnp.where(strict, vs_k[:2 * L], 0.0),
                                     jnp.where(incl, vs_k[2 * L:], 0.0)], axis=0).astype(BF16))
        a_rb.append(jnp.where(incl, vs_b[2 * L:], 0.0).astype(BF16))
    av = [_dot(a, v) for a, v in zip(a_kv, v_s)]
    w = [x[:2 * L] + y[:2 * L] for x, y in zip(xs, av)]
    am = a_ab
    w = [wc + _dot(a, wc.astype(BF16)) for a, wc in zip(am, w)]
    for _ in range(5):
        am = [_dot(a, a).astype(BF16) for a in am]
        w = [wc + _dot(a, wc.astype(BF16)) for a, wc in zip(am, w)]
    sa = [wc.astype(BF16) for wc in w]
    ys = [x[2 * L:] + y[2 * L:] + _dot(a, s) for x, y, a, s in zip(xs, av, a_rb, sa)]
    for (d, p), yc in zip(chains, ys):
        y_refs[d][:, p * PAIR:(p + 1) * PAIR] = yc[:L] + yc[L:]
    for (d, p), s_old, wc in zip(chains, state, w):
        sa_v = jnp.concatenate([wc[:L] + wc[L:], side("v", d, p)], axis=0).astype(BF16)
        bhkh = jnp.concatenate([side("bh", d, p), side("kh", d, p)], axis=0).astype(BF16)
        upd = jnp.where(same_head, _dot_tn(sa_v, bhkh), 0.0)
        s_ref[d, p] = s_old * side("p_end", d, p) + upd


def _cast_slices(w, l, n_steps):
    _, n_rows, n_cols = w.shape
    n_blk = 1
    while n_blk * 2 <= n_steps and n_rows % (n_blk * 2) == 0 and (n_rows // (n_blk * 2)) % 16 == 0:
        n_blk *= 2
    rows = n_rows // n_blk
    step = lambda i: jnp.minimum(i, n_blk - 1)
    return (pl.BlockSpec((None, rows, n_cols), lambda i: (l, step(i), 0)),
            pl.BlockSpec((rows, n_cols), lambda i: (step(i), 0)),
            jax.ShapeDtypeStruct((n_rows, n_cols), BF16))


def _rwkv_scan(r, v, kk, lwf, lwb, kdf, kdb, bf, bb, s_len, n_ctx, cast_weights, l):
    m = r.shape[0]
    nc = m // CHUNK
    nc_lat = s_len // CHUNK
    nc_ctx = n_ctx // CHUNK
    fwd = lambda i: (jnp.where(i < nc_ctx, nc_lat + i, i - nc_ctx), 0)
    bwd = lambda i: (nc - 1 - i, 0)
    fs = pl.BlockSpec((CHUNK, RWKV_WIDTH), fwd)
    bs = pl.BlockSpec((CHUNK, RWKV_WIDTH), bwd)
    cast_in, cast_out, cast_shapes = zip(*[_cast_slices(w, l, nc) for w in cast_weights])
    cast_bytes = sum(spec.block_shape[0] * spec.block_shape[1] * 6 for spec in cast_out)
    outs = pl.pallas_call(
        functools.partial(_scan_kernel, n_cast=len(cast_weights)),
        grid=(nc,),
        in_specs=[fs] * 6 + [bs] * 6 + list(cast_in),
        out_specs=[fs, bs] + list(cast_out),
        out_shape=[jax.ShapeDtypeStruct((m, RWKV_WIDTH), F32)] * 2 + list(cast_shapes),
        scratch_shapes=[pltpu.VMEM((2, N_PAIRS, PAIR, PAIR), F32)],
        compiler_params=pltpu.CompilerParams(
            dimension_semantics=("arbitrary",),
            vmem_limit_bytes=_vmem(2 * 14 * CHUNK * RWKV_WIDTH * 4 + 2 * N_PAIRS * PAIR * PAIR * 4
                                   + 2 * cast_bytes + 24 * 1024 * 1024)),
        name="rwkv_scan",
    )(r, v, kk, lwf, kdf, bf, r, v, kk, lwb, kdb, bb, *cast_weights)
    return outs[0], outs[1], outs[2:]


def _out_kernel(att_ref, yf_ref, yb_ref, g_ref, bon_ref, x_ref, wa_ref, wr_ref, lnw_ref, lnb_ref,
                seg_ref, exp_ref, gpm_ref, gt1_ref, gmlp_ref, sc2_ref, sh2_ref, x1_ref, h2_ref):
    mix_att = _dot(att_ref[...], wa_ref[...])
    inv_n = 1.0 / RWKV_HEAD_DIM
    seg, expand = seg_ref[...], exp_ref[...]
    rows = att_ref.shape[0] // OUT_SUB_BLOCKS
    blocks = [pl.ds(n * rows, rows) for n in range(OUT_SUB_BLOCKS)]
    y = [yf_ref[b, :] + yb_ref[b, :] for b in blocks]
    s1 = [_bdot(v, seg) for v in y]
    yc = [v - _split_dot(s, expand) * inv_n for v, s in zip(y, s1)]
    s2 = [_bdot(c * c, seg) for c in yc]
    var = [_bdot(s, expand) * inv_n for s in s2]
    rec = [((c * lax.rsqrt(vr + GN_EPS) * lnw_ref[...] + lnb_ref[...]) + bon_ref[b, :]) * g_ref[b, :]
           for c, vr, b in zip(yc, var, blocks)]
    mix = mix_att + _bdot(jnp.concatenate(rec, axis=0), wr_ref[...])
    ms = jnp.mean(mix * mix, axis=-1, keepdims=True)
    x1 = x_ref[...] + (mix * lax.rsqrt(ms + NORM_EPS)) * (gt1_ref[...] * gpm_ref[...])
    x1_ref[...] = x1
    ms2 = jnp.mean(x1 * x1, axis=-1, keepdims=True)
    h2 = (x1 * lax.rsqrt(ms2 + NORM_EPS)) * (gmlp_ref[...] * (1.0 + sc2_ref[...])) + sh2_ref[...]
    h2_ref[...] = h2.astype(BF16)


def _out_proj(att, yf, yb, g, bonus, x, w_out, lnw, lnb, seg, expand, gpm, gt1, gmlp, sc2, sh2, tm):
    s_len, d = x.shape
    assert ATT_WIDTH == RWKV_WIDTH and w_out.shape == (ATT_WIDTH + RWKV_WIDTH, d)
    row = lambda w: pl.BlockSpec((tm, w), lambda i: (i, 0))
    full = lambda arr: pl.BlockSpec(arr.shape, lambda i: (0,) * arr.ndim)
    w_half = lambda n: pl.BlockSpec((ATT_WIDTH, d), lambda i: (n, 0))
    consts = (lnw, lnb, seg, expand, gpm, gt1, gmlp, sc2, sh2)
    return pl.pallas_call(
        _out_kernel,
        grid=(s_len // tm,),
        in_specs=[row(ATT_WIDTH)] + [row(RWKV_WIDTH)] * 4 + [row(d), w_half(0), w_half(1)]
                 + [full(c) for c in consts],
        out_specs=[row(d), row(d)],
        out_shape=[jax.ShapeDtypeStruct((s_len, d), F32), jax.ShapeDtypeStruct((s_len, d), BF16)],
        compiler_params=pltpu.CompilerParams(
            dimension_semantics=("arbitrary",),
            vmem_limit_bytes=_vmem(2 * tm * (ATT_WIDTH * 2 + 4 * RWKV_WIDTH * 4 + d * 4 + d * 4 + d * 2)
                                   + 2 * 2 * d * d + 8 * tm * d * 4 + 4 * 1024 * 1024)),
        name="out_proj",
    )(att, yf, yb, g, bonus, x, w_out, w_out, *consts)


def _mlp_kernel(h_ref, w1_ref, w2_ref, x1_ref, g_ref, gt_ref, o_ref, acc_ref):
    f = pl.program_id(1)

    @pl.when(f == 0)
    def _():
        acc_ref[...] = jnp.zeros_like(acc_ref)

    u = jnp.maximum(_dot(h_ref[...], w1_ref[...]), 0.0)
    acc_ref[...] += _dot((u * u).astype(BF16), w2_ref[...])

    @pl.when(f == pl.num_programs(1) - 1)
    def _():
        ff = acc_ref[...]
        ms = jnp.mean(ff * ff, axis=-1, keepdims=True)
        o_ref[...] = x1_ref[...] + (ff * lax.rsqrt(ms + NORM_EPS)) * (gt_ref[...] * g_ref[...])


def _mlp(h2, w1, w2, x1, gain, gt2, tm, tf):
    s_len, d = x1.shape
    dff = w1.shape[1]
    return pl.pallas_call(
        _mlp_kernel,
        grid=(s_len // tm, dff // tf),
        in_specs=[pl.BlockSpec((tm, d), lambda i, f: (i, 0)),
                  pl.BlockSpec((d, tf), lambda i, f: (0, f)),
                  pl.BlockSpec((tf, d), lambda i, f: (f, 0)),
                  pl.BlockSpec((tm, d), lambda i, f: (i, 0)),
                  pl.BlockSpec((1, d), lambda i, f: (0, 0)),
                  pl.BlockSpec((1, d), lambda i, f: (0, 0))],
        out_specs=pl.BlockSpec((tm, d), lambda i, f: (i, 0)),
        out_shape=jax.ShapeDtypeStruct((s_len, d), F32),
        scratch_shapes=[pltpu.VMEM((tm, d), F32)],
        compiler_params=pltpu.CompilerParams(
            dimension_semantics=("arbitrary", "arbitrary"),
            vmem_limit_bytes=_vmem(2 * (tm * d * 2 + 2 * d * tf * 2 + 2 * tm * d * 4)
                                   + tm * d * 4 + 3 * tm * tf * 4 + 4 * 1024 * 1024)),
        name="mlp",
    )(h2, w1, w2, x1, gain, gt2)


def _rope_tables(s_len, rows_padded):
    rows = s_len // GRID_W
    n_freq = ATT_HEAD_DIM // 4
    inv_freq = ROPE_BASE ** (-jnp.arange(n_freq, dtype=F32) / n_freq)
    ang_r = jnp.arange(rows, dtype=F32)[:, None] * inv_freq[None, :]
    ang_c = jnp.arange(GRID_W, dtype=F32)[:, None] * inv_freq[None, :]
    zr = jnp.zeros((rows, 2 * n_freq), F32)
    zc = jnp.zeros((GRID_W, 2 * n_freq), F32)
    pad = lambda t: jnp.pad(t, ((0, rows_padded - rows), (0, 0)))[:, None, :]
    row_cos = pad(jnp.concatenate([jnp.cos(ang_r), jnp.cos(ang_r), zr], axis=-1))
    row_sin = pad(jnp.concatenate([-jnp.sin(ang_r), jnp.sin(ang_r), zr], axis=-1))
    col_cos = jnp.concatenate([zc, jnp.cos(ang_c), jnp.cos(ang_c)], axis=-1)
    col_sin = jnp.concatenate([zc, -jnp.sin(ang_c), jnp.sin(ang_c)], axis=-1)
    return row_cos, row_sin, col_cos, col_sin


class _Tiles(NamedTuple):
    cat_rows: int
    prep_rows: int
    out_rows: int
    mlp_rows: int
    mlp_ff: int


def _tile_sizes(s_len, n_ctx):
    cat_rows = 768 if (s_len + n_ctx) % 768 == 0 and s_len % 768 + n_ctx == 768 else n_ctx
    return _Tiles(cat_rows=cat_rows, prep_rows=256, out_rows=512, mlp_rows=512, mlp_ff=1024)


def _block_diag2(w):
    _, k, n = w.shape
    z = jnp.zeros((k, n), w.dtype)
    return jnp.concatenate([jnp.concatenate([w[0], z], axis=1), jnp.concatenate([z, w[1]], axis=1)], axis=0)


def _layer(x2, ctx2, c8, l, w_ada, b_ada, g_pre_mix, g_post_mix, g_pre_mlp, g_post_mlp, w_in, attn_sink,
           rwkv_mu_prev, rwkv_mu_next, rwkv_w0, rwkv_w2, rwkv_a0, rwkv_a2, rwkv_g2, rwkv_k_k, rwkv_k_a,
           rwkv_r_k, rwkv_ln_w, rwkv_ln_b, w_out, w_mlp_in, w_mlp_out):
    s_len, d = x2.shape
    n_ctx = ctx2.shape[0]
    row = lambda v: v.reshape(1, -1)

    mod = _ada(c8, w_ada[l], row(b_ada[l]))
    sh1, sc1, gt1, sh2, sc2, gt2 = [mod[:2, j * d:(j + 1) * d] for j in range(N_MOD)]

    tiles = _tile_sizes(s_len, n_ctx)
    h = _norm_mod(x2, ctx2, row(g_pre_mix[l]), sc1.reshape(2, 1, d), sh1.reshape(2, 1, d), tiles.cat_rows)

    rope_tables = _rope_tables(s_len, ((s_len + n_ctx) // tiles.cat_rows) * (tiles.cat_rows // GRID_W))
    p_in = _proj(h, w_in[l].astype(BF16), rope_tables, s_len, tiles.cat_rows, w_in.shape[2] // 3,
                 N_Q_HEADS + N_KV_HEADS)

    att = _attention(p_in, attn_sink[l], s_len, n_ctx)

    head = jnp.arange(RWKV_WIDTH) // RWKV_HEAD_DIM
    seg = (head[:, None] == jnp.arange(LANES)[None, :]).astype(BF16)
    expand = seg.T
    streams = _rwkv_prep(
        p_in, row(rwkv_mu_prev[l]), row(rwkv_mu_next[l]),
        (0.5 * _block_diag2(rwkv_w2[l])).astype(BF16), 0.5 * rwkv_w0[l].reshape(1, -1),
        (0.5 * _block_diag2(rwkv_a2[l])).astype(BF16), 0.5 * rwkv_a0[l].reshape(1, -1),
        rwkv_g2[l].astype(BF16), row(rwkv_k_k[l]), row(rwkv_k_a[l]), rwkv_r_k[l].reshape(1, -1),
        seg, expand, s_len, n_ctx, tiles.prep_rows)
    r, v, kk, g, bonus, lwf, lwb, kdf, kdb, bf, bb = streams
    yf, yb, (w_out_b, w_mlp_in_b, w_mlp_out_b) = _rwkv_scan(
        r, v, kk, lwf, lwb, kdf, kdb, bf, bb, s_len, n_ctx, (w_out, w_mlp_in, w_mlp_out), l)

    x1, h2 = _out_proj(att, yf, yb, g, bonus, x2, w_out_b,
                       row(rwkv_ln_w[l]), row(rwkv_ln_b[l]), seg, expand,
                       row(g_post_mix[l]), gt1[0:1], row(g_pre_mlp[l]), sc2[0:1], sh2[0:1], tiles.out_rows)
    return _mlp(h2, w_mlp_in_b, w_mlp_out_b, x1, row(g_post_mlp[l]), gt2[0:1], tiles.mlp_rows, tiles.mlp_ff)


def kernel(x, c, ctx, c_ctx, w_ada, b_ada, g_pre_mix, g_post_mix, g_pre_mlp, g_post_mlp, w_in, attn_sink,
           rwkv_mu_prev, rwkv_mu_next, rwkv_w0, rwkv_w2, rwkv_a0, rwkv_a2, rwkv_g2, rwkv_k_k, rwkv_k_a,
           rwkv_r_k, rwkv_ln_w, rwkv_ln_b, w_out, w_mlp_in, w_mlp_out):
    b, s_len, d = x.shape
    depth = w_ada.shape[0]
    assert b == 1 and depth == 1, "single sample, single layer (context stream is not carried to a next layer)"
    c8 = jnp.zeros((8, d), F32).at[0].set(c[0]).at[1].set(c_ctx)
    out = _layer(x[0], ctx[0], c8, 0, w_ada, b_ada, g_pre_mix, g_post_mix, g_pre_mlp, g_post_mlp, w_in,
                 attn_sink, rwkv_mu_prev, rwkv_mu_next, rwkv_w0, rwkv_w2, rwkv_a0, rwkv_a2, rwkv_g2,
                 rwkv_k_k, rwkv_k_a, rwkv_r_k, rwkv_ln_w, rwkv_ln_b, w_out, w_mlp_in, w_mlp_out)
    return out[None]
```

```python
import functools
from typing import NamedTuple

import jax
import jax.numpy as jnp
from jax import lax
from jax.experimental import pallas as pl
from jax.experimental.pallas import tpu as pltpu

F32 = jnp.float32
BF16 = jnp.bfloat16

GRID_W = 64
ATT_HEAD_DIM = 128
N_Q_HEADS = 8
N_KV_HEADS = 2
GQA_GROUP = N_Q_HEADS // N_KV_HEADS
ATT_WIDTH = N_Q_HEADS * ATT_HEAD_DIM
KV_WIDTH = N_KV_HEADS * ATT_HEAD_DIM
ATT_IN = ATT_WIDTH + 2 * KV_WIDTH
WINDOW = 128
BLOCK = 128
ROPE_BASE = 10000.0
RWKV_HEAD_DIM = 64
RWKV_WIDTH = 1024
N_RWKV_HEADS = RWKV_WIDTH // RWKV_HEAD_DIM
LORA_W = 64
LORA_A = 64
LORA_G = 128
N_MOD = 6
NORM_EPS = 1e-6
GN_EPS = 64e-5
DECAY_SCALE = 0.6065306597126334

LANES = 128
CHUNK = 64
PAIR = 2 * RWKV_HEAD_DIM
N_PAIRS = RWKV_WIDTH // PAIR
PREP_OUT_DTYPES = (BF16, BF16, BF16, BF16, BF16, F32, F32, BF16, BF16, BF16, BF16)
ATTN_BLOCKS_PER_STEP = 8
HALO_ROWS = 16
OUT_SUB_BLOCKS = 4
VMEM_CAP = 56 * 1024 * 1024


def _vmem(nbytes):
    return int(min(VMEM_CAP, max(16 * 1024 * 1024, nbytes)))


def _dot(a, b):
    return jnp.dot(a, b, preferred_element_type=F32)


def _dot_nt(a, b):
    return lax.dot_general(a, b, (((1,), (1,)), ((), ())), preferred_element_type=F32)


def _dot_tn(a, b):
    return lax.dot_general(a, b, (((0,), (0,)), ((), ())), preferred_element_type=F32)


def _bdot(a, b):
    return _dot(a.astype(BF16), b.astype(BF16))


def _bdot_nt(a, b):
    return _dot_nt(a.astype(BF16), b.astype(BF16))


def _bdot_tn(a, b):
    return _dot_tn(a.astype(BF16), b.astype(BF16))


def _split_dot(x, m):
    hi = x.astype(BF16)
    lo = (x - hi.astype(F32)).astype(BF16)
    return _dot(hi, m) + _dot(lo, m)


def _sigmoid(z):
    return 0.5 * jnp.tanh(0.5 * z) + 0.5


def _ada_kernel(c_ref, w_ref, b_ref, o_ref):
    c = c_ref[...]
    s = c * jax.nn.sigmoid(c)
    o_ref[...] = _bdot(s, w_ref[...]) + b_ref[...]


def _ada(c8, w_ada, b_ada):
    d, n = w_ada.shape
    tn = 1024
    return pl.pallas_call(
        _ada_kernel,
        grid=(n // tn,),
        in_specs=[pl.BlockSpec((8, d), lambda j: (0, 0)),
                  pl.BlockSpec((d, tn), lambda j: (0, j)),
                  pl.BlockSpec((1, tn), lambda j: (0, j))],
        out_specs=pl.BlockSpec((8, tn), lambda j: (0, j)),
        out_shape=jax.ShapeDtypeStruct((8, n), F32),
        compiler_params=pltpu.CompilerParams(
            dimension_semantics=("arbitrary",),
            vmem_limit_bytes=_vmem(2 * d * tn * 4 + 4 * 1024 * 1024)),
        name="ada",
    )(c8, w_ada, b_ada)


def _norm_mod_kernel(x_ref, c_ref, g_ref, sc_ref, sh_ref, o_ref, *, lat_last):
    def emit(x, who, rows):
        ms = jnp.mean(x * x, axis=-1, keepdims=True)
        y = (x * lax.rsqrt(ms + NORM_EPS)) * (g_ref[...] * (1.0 + sc_ref[who])) + sh_ref[who]
        o_ref[rows, :] = y.astype(BF16)

    i = pl.program_id(0)
    last = pl.num_programs(0) - 1

    @pl.when(i < last)
    def _():
        emit(x_ref[...], 0, slice(None))

    @pl.when(i == last)
    def _():
        if lat_last:
            emit(x_ref[:lat_last, :], 0, slice(0, lat_last))
        emit(c_ref[...], 1, slice(lat_last, None))


def _norm_mod(x2, ctx2, gain, sc2, sh2, tm):
    s_len, d = x2.shape
    n_ctx = ctx2.shape[0]
    n_tiles = (s_len + n_ctx) // tm
    lat_last = s_len - (n_tiles - 1) * tm
    assert n_tiles * tm == s_len + n_ctx and lat_last + n_ctx == tm and lat_last % 16 == 0
    both = pl.BlockSpec((2, 1, d), lambda i: (0, 0, 0))
    return pl.pallas_call(
        functools.partial(_norm_mod_kernel, lat_last=lat_last),
        grid=(n_tiles,),
        in_specs=[pl.BlockSpec((tm, d), lambda i: (jnp.minimum(i, (s_len - 1) // tm), 0)),
                  pl.BlockSpec((n_ctx, d), lambda i: (0, 0)),
                  pl.BlockSpec((1, d), lambda i: (0, 0)),
                  both, both],
        out_specs=pl.BlockSpec((tm, d), lambda i: (i, 0)),
        out_shape=jax.ShapeDtypeStruct((n_tiles * tm, d), BF16),
        compiler_params=pltpu.CompilerParams(
            dimension_semantics=("arbitrary",),
            vmem_limit_bytes=_vmem(2 * tm * d * 6 + 2 * n_ctx * d * 4 + 4 * tm * d * 4 + 4 * 1024 * 1024)),
        name="norm_mod",
    )(x2, ctx2, gain, sc2, sh2)


def _rope_heads(acc, cos, sin, n_heads):
    lane = lax.broadcasted_iota(jnp.int32, (1, ATT_HEAD_DIM), 1)
    first = (lane % 64) < 32
    outs = []
    for h in range(n_heads):
        xh = acc[:, h * ATT_HEAD_DIM:(h + 1) * ATT_HEAD_DIM]
        partner = jnp.where(first, pltpu.roll(xh, 96, 1), pltpu.roll(xh, 32, 1))
        outs.append(xh * cos + partner * sin)
    return outs


def _proj_kernel(a_ref, w_ref, rcos_ref, rsin_ref, ccos_ref, csin_ref, o_ref, *, rope_heads, s_len):
    acc = _dot(a_ref[...], w_ref[...])
    tm = a_ref.shape[0]
    i, j = pl.program_id(0), pl.program_id(1)
    row = i * tm + lax.broadcasted_iota(jnp.int32, (tm, 1), 0)
    rotate = row < jnp.where(j == 0, s_len, 0)
    cos = jnp.where(rotate, (rcos_ref[...] + ccos_ref[...][None]).reshape(tm, ATT_HEAD_DIM), 1.0)
    sin = jnp.where(rotate, (rsin_ref[...] + csin_ref[...][None]).reshape(tm, ATT_HEAD_DIM), 0.0)
    for h, r in enumerate(_rope_heads(acc, cos, sin, rope_heads)):
        o_ref[:, h * ATT_HEAD_DIM:(h + 1) * ATT_HEAD_DIM] = r.astype(o_ref.dtype)
    o_ref[:, rope_heads * ATT_HEAD_DIM:] = acc[:, rope_heads * ATT_HEAD_DIM:].astype(o_ref.dtype)


def _proj(a, w, rope_tables, s_len, tm, tn, rope_heads):
    m, k = a.shape
    n = w.shape[1]
    n_tiles = m // tm
    assert rope_heads * ATT_HEAD_DIM <= tn and n % tn == 0 and tm % GRID_W == 0
    g = tm // GRID_W
    row_spec = pl.BlockSpec((g, 1, ATT_HEAD_DIM), lambda i, j: (i, 0, 0))
    col_spec = pl.BlockSpec((GRID_W, ATT_HEAD_DIM), lambda i, j: (0, 0))
    return pl.pallas_call(
        functools.partial(_proj_kernel, rope_heads=rope_heads, s_len=s_len),
        grid=(n_tiles, n // tn),
        in_specs=[pl.BlockSpec((tm, k), lambda i, j: (i, 0)),
                  pl.BlockSpec((k, tn), lambda i, j: (0, j)),
                  row_spec, row_spec, col_spec, col_spec],
        out_specs=pl.BlockSpec((tm, tn), lambda i, j: (i, j)),
        out_shape=jax.ShapeDtypeStruct((m, n), BF16),
        compiler_params=pltpu.CompilerParams(
            dimension_semantics=("arbitrary", "arbitrary"),
            vmem_limit_bytes=_vmem(2 * (tm * k * 2 + k * tn * 2 + tm * tn * 2 + 2 * tm * 512)
                                   + tm * tn * 8 + 4 * 1024 * 1024)),
        name="proj_in",
    )(a, w, *rope_tables)


def _attn_kernel(sink_ref, q_ref, *rest, nb, n_ctx):
    nq = ATTN_BLOCKS_PER_STEP
    k_refs, v_refs = rest[:nq + 2], rest[nq + 2:2 * (nq + 2)]
    kx_ref, vx_ref, o_ref = rest[2 * (nq + 2):]
    i = pl.program_id(0)
    scale = ATT_HEAD_DIM ** -0.5
    rows = GQA_GROUP * BLOCK
    span = 3 * BLOCK
    ncol = span + n_ctx
    row = lax.broadcasted_iota(jnp.int32, (rows, ncol), 0) % BLOCK
    col = lax.broadcasted_iota(jnp.int32, (rows, ncol), 1)
    in_window = jnp.abs(col - BLOCK - row) <= WINDOW
    chains = [(t, hk) for t in range(nq) for hk in range(N_KV_HEADS)]
    valid = []
    for t in range(nq):
        blk = nq * i + t
        lo = jnp.where(blk > 0, 0, BLOCK)
        hi = jnp.where(blk < nb - 1, span, 2 * BLOCK)
        valid.append((in_window & (col >= lo) & (col < hi)) | (col >= span))
    ks = lambda hk: slice(hk * ATT_HEAD_DIM, (hk + 1) * ATT_HEAD_DIM)
    q = [(jnp.concatenate(
        [q_ref[t * BLOCK:(t + 1) * BLOCK, (hk * GQA_GROUP + g) * ATT_HEAD_DIM:(hk * GQA_GROUP + g + 1) * ATT_HEAD_DIM]
         for g in range(GQA_GROUP)], axis=0).astype(F32) * scale).astype(BF16) for t, hk in chains]
    k_all = [jnp.concatenate([k_refs[t][:, ks(hk)], k_refs[t + 1][:, ks(hk)], k_refs[t + 2][:, ks(hk)],
                              kx_ref[:, ks(hk)]], axis=0).astype(BF16) for t, hk in chains]
    v_all = [jnp.concatenate([v_refs[t][:, ks(hk)], v_refs[t + 1][:, ks(hk)], v_refs[t + 2][:, ks(hk)],
                              vx_ref[:, ks(hk)]], axis=0).astype(BF16) for t, hk in chains]
    sink = [jnp.concatenate([jnp.full((BLOCK, 1), sink_ref[hk * GQA_GROUP + g], F32) for g in range(GQA_GROUP)],
                            axis=0) for _, hk in chains]
    s = [jnp.where(valid[t], _dot_nt(qh, kh), -jnp.inf) for (t, _), qh, kh in zip(chains, q, k_all)]
    m = [jnp.maximum(jnp.max(sh, axis=-1, keepdims=True), sk) for sh, sk in zip(s, sink)]
    e = [jnp.exp(sh - mh) for sh, mh in zip(s, m)]
    denom = [jnp.sum(eh, axis=-1, keepdims=True) + jnp.exp(sk - mh) for eh, sk, mh in zip(e, sink, m)]
    o = [_dot(eh.astype(BF16), vh) * (1.0 / dh) for eh, vh, dh in zip(e, v_all, denom)]
    for (t, hk), oc in zip(chains, o):
        for g in range(GQA_GROUP):
            h = hk * GQA_GROUP + g
            o_ref[t * BLOCK:(t + 1) * BLOCK, h * ATT_HEAD_DIM:(h + 1) * ATT_HEAD_DIM] = (
                oc[g * BLOCK:(g + 1) * BLOCK].astype(BF16))


def _attention(qkv, sink, s_len, n_ctx):
    nq = ATTN_BLOCKS_PER_STEP
    nb = s_len // BLOCK
    assert nb % nq == 0 and s_len % n_ctx == 0
    kcol = ATT_WIDTH // KV_WIDTH
    vcol = kcol + 1
    ctx_blk = s_len // n_ctx
    kv = lambda u, c: pl.BlockSpec((BLOCK, KV_WIDTH), lambda i: (jnp.clip(nq * i - 1 + u, 0, nb - 1), c))
    return pl.pallas_call(
        functools.partial(_attn_kernel, nb=nb, n_ctx=n_ctx),
        grid=(nb // nq,),
        in_specs=[pl.BlockSpec(memory_space=pltpu.SMEM),
                  pl.BlockSpec((nq * BLOCK, ATT_WIDTH), lambda i: (i, 0))]
                 + [kv(u, kcol) for u in range(nq + 2)] + [kv(u, vcol) for u in range(nq + 2)]
                 + [pl.BlockSpec((n_ctx, KV_WIDTH), lambda i: (ctx_blk, kcol)),
                    pl.BlockSpec((n_ctx, KV_WIDTH), lambda i: (ctx_blk, vcol))],
        out_specs=pl.BlockSpec((nq * BLOCK, ATT_WIDTH), lambda i: (i, 0)),
        out_shape=jax.ShapeDtypeStruct((s_len, ATT_WIDTH), BF16),
        compiler_params=pltpu.CompilerParams(
            dimension_semantics=("arbitrary",),
            vmem_limit_bytes=_vmem(40 * 1024 * 1024)),
        name="attention",
    )(sink, qkv, *([qkv] * (2 * (nq + 2) + 2)))


def _prep_kernel(x_ref, xp_ref, xn_ref, mup_ref, mun_ref, w2_ref, w0_ref, a2_ref, a0_ref, g2_ref,
                 kk_ref, ka_ref, rk_ref, seg_ref, exp_ref,
                 r_o, v_o, kk_o, g_o, bon_o, lwf_o, lwb_o, kdf_o, kdb_o, bf_o, bb_o,
                 *, tm, start_tiles, end_tiles):
    i = pl.program_id(0)
    r_w = RWKV_WIDTH
    x = x_ref[:, ATT_IN:].astype(F32)
    is_start = functools.reduce(jnp.logical_or, [i == t for t in start_tiles])
    is_end = functools.reduce(jnp.logical_or, [i == t for t in end_tiles])
    keep_prev = jnp.full((1, 1), jnp.where(is_start, 0, 1), jnp.int32) == 1
    keep_next = jnp.full((1, 1), jnp.where(is_end, 0, 1), jnp.int32) == 1
    prev_row = jnp.where(keep_prev, xp_ref[HALO_ROWS - 1:HALO_ROWS, ATT_IN:].astype(F32), 0.0)
    next_row = jnp.where(keep_next, xn_ref[0:1, ATT_IN:].astype(F32), 0.0)
    xp = pltpu.roll(x, 1, 0)
    xn = pltpu.roll(x, tm - 1, 0)
    row8 = lax.broadcasted_iota(jnp.int32, (8, 1), 0)
    xp = jnp.concatenate([jnp.where(row8 == 0, prev_row, xp[:8]), xp[8:]], axis=0)
    xn = jnp.concatenate([xn[:tm - 8], jnp.where(row8 == 7, next_row, xn[tm - 8:])], axis=0)
    mup = mup_ref[...]
    mun = mun_ref[...]
    s = x * (1.0 - mup - mun) + xp * mup + xn * mun

    r = s[:, :r_w]
    k = s[:, r_w:2 * r_w]
    v = s[:, 2 * r_w:3 * r_w]
    o1 = 3 * r_w
    o2 = o1 + 2 * LORA_W
    o3 = o2 + 2 * LORA_A
    w_lo = s[:, o1:o2]
    a_lo = s[:, o2:o3]
    g_lo = s[:, o3:]

    tw = jnp.tanh(w0_ref[...] + _bdot(jnp.tanh(w_lo), w2_ref[...]))
    ta = jnp.tanh(a0_ref[...] + _bdot(a_lo, a2_ref[...]))
    lw = (-0.5 * DECAY_SCALE) * tw - 0.5 * DECAY_SCALE
    g = _bdot(_sigmoid(g_lo), g2_ref[...])

    kkf = k * kk_ref[...]
    ssq = _split_dot(_bdot(kkf * kkf, seg_ref[...]), exp_ref[...])
    kk = kkf * jnp.minimum(lax.rsqrt(ssq), 1e12)
    kkh = k * (0.5 * ka_ref[...])
    k1 = k - kkh
    kd_f = k1 + kkh * ta[:, :r_w]
    kd_b = k1 + kkh * ta[:, r_w:]
    k_bonus = 0.5 * (kd_f + kd_b)
    bonus = _split_dot(_bdot(r * k_bonus * rk_ref[...], seg_ref[...]), exp_ref[...]) * v
    kk_half = 0.5 * kk

    r_o[...] = r.astype(r_o.dtype)
    v_o[...] = v.astype(v_o.dtype)
    kk_o[...] = kk.astype(kk_o.dtype)
    g_o[...] = g.astype(g_o.dtype)
    bon_o[...] = bonus.astype(bon_o.dtype)
    lwf_o[...] = lw[:, :r_w]
    lwb_o[...] = lw[:, r_w:]
    kdf_o[...] = kd_f.astype(kdf_o.dtype)
    kdb_o[...] = kd_b.astype(kdb_o.dtype)
    bf_o[...] = (kk_half + kk_half * ta[:, :r_w]).astype(bf_o.dtype)
    bb_o[...] = (kk_half + kk_half * ta[:, r_w:]).astype(bb_o.dtype)


def _rwkv_prep(slab, mup, mun, w2blk, w0cat, a2blk, a0cat, g2, k_k, k_a, r_k, seg, expand, s_len, n_ctx, tm):
    m, width = slab.shape
    nt = m // tm
    hb = tm // HALO_ROWS
    nblk8 = m // HALO_ROWS
    start_tiles = (0, s_len // tm)
    end_tiles = (s_len // tm - 1, nt - 1)
    full = lambda arr: pl.BlockSpec(arr.shape, lambda i: (0,) * arr.ndim)
    consts = (mup, mun, w2blk, w0cat, a2blk, a0cat, g2, k_k, k_a, r_k, seg, expand)
    out_spec = pl.BlockSpec((tm, RWKV_WIDTH), lambda i: (i, 0))
    n_out = len(PREP_OUT_DTYPES)
    return pl.pallas_call(
        functools.partial(_prep_kernel, tm=tm, start_tiles=start_tiles, end_tiles=end_tiles),
        grid=(nt,),
        in_specs=[pl.BlockSpec((tm, width), lambda i: (i, 0)),
                  pl.BlockSpec((HALO_ROWS, width), lambda i: (jnp.maximum(i * hb - 1, 0), 0)),
                  pl.BlockSpec((HALO_ROWS, width), lambda i: (jnp.minimum((i + 1) * hb, nblk8 - 1), 0))]
                 + [full(c) for c in consts],
        out_specs=[out_spec] * n_out,
        out_shape=[jax.ShapeDtypeStruct((m, RWKV_WIDTH), dt) for dt in PREP_OUT_DTYPES],
        compiler_params=pltpu.CompilerParams(
            dimension_semantics=("arbitrary",),
            vmem_limit_bytes=_vmem(2 * tm * width * 4 + 2 * n_out * tm * RWKV_WIDTH * 4
                                   + 12 * tm * RWKV_WIDTH * 4 + 8 * 1024 * 1024)),
        name="rwkv_prep",
    )(slab, slab, slab, *consts)


def _stack(x, m0):
    return jnp.concatenate([jnp.where(m0, x, 0.0), jnp.where(m0, 0.0, x)], axis=0)


def _scan_streams(r_ref, v_ref, kk_ref, lw_ref, kd_ref, b_ref, reverse):
    L = CHUNK
    lw = lw_ref[...]
    ti = lax.broadcasted_iota(jnp.int32, (L, L), 0)
    tj = lax.broadcasted_iota(jnp.int32, (L, L), 1)
    tri = jnp.where((ti <= tj) if reverse else (ti >= tj), 1.0, 0.0).astype(BF16)
    lw_hi = lw.astype(BF16)
    lw_lo = (lw - lw_hi.astype(F32)).astype(BF16)
    c = _dot(tri, lw_hi) + _dot(tri, lw_lo)
    p_cum = jnp.exp(c)
    p_inv = jnp.exp(-c)
    p_prev = jnp.exp(c - lw)
    last = 0 if reverse else L - 1
    p_end = p_cum[last:last + 1, :]
    b_t = b_ref[...] * p_inv
    k_t = kd_ref[...] * p_inv
    gi = lax.broadcasted_iota(jnp.int32, (2 * L, 2 * L), 0)
    gj = lax.broadcasted_iota(jnp.int32, (2 * L, 2 * L), 1)
    same = (gi // L) == (gj // L)
    strict = same & ((gi < gj) if reverse else (gi > gj))
    incl = same & ((gi <= gj) if reverse else (gi >= gj))
    return dict(a=-kk_ref[...] * p_prev, r=r_ref[...] * p_cum, b=b_t, k=k_t, bh=b_t * p_end, kh=k_t * p_end,
                v=v_ref[...], p_end=p_end, strict=strict, incl=incl)


def _scan_kernel(rf, vf, kkf, lwf, kdf, bf, rb, vb, kkb, lwb, kdb, bb, *rest, n_cast):
    cast_in, (yf, yb), cast_out, s_ref = rest[:n_cast], rest[n_cast:n_cast + 2], rest[n_cast + 2:-1], rest[-1]

    @pl.when(pl.program_id(0) == 0)
    def _():
        s_ref[...] = jnp.zeros_like(s_ref)

    for src, dst in zip(cast_in, cast_out):
        dst[...] = src[...].astype(dst.dtype)

    L = CHUNK
    streams = (_scan_streams(rf, vf, kkf, lwf, kdf, bf, False), _scan_streams(rb, vb, kkb, lwb, kdb, bb, True))
    y_refs = (yf, yb)
    m0 = lax.broadcasted_iota(jnp.int32, (L, PAIR), 1) < RWKV_HEAD_DIM
    chains = [(d, p) for d in range(2) for p in range(N_PAIRS)]

    def side(name, d, p):
        return streams[d][name][:, p * PAIR:(p + 1) * PAIR]

    def stacked(name, d, p):
        return _stack(side(name, d, p), m0)

    head0_rows = (lax.broadcasted_iota(jnp.int32, (4 * L, PAIR), 0) // L) % 2 == 0
    same_head = ((lax.broadcasted_iota(jnp.int32, (PAIR, PAIR), 0) // RWKV_HEAD_DIM)
                 == (lax.broadcasted_iota(jnp.int32, (PAIR, PAIR), 1) // RWKV_HEAD_DIM))
    ar = [jnp.concatenate([stacked("a", d, p), stacked("r", d, p)], axis=0).astype(BF16) for d, p in chains]
    bk = [jnp.concatenate([side("b", d, p), side("k", d, p)], axis=0).astype(BF16) for d, p in chains]
    v_s = [stacked("v", d, p).astype(BF16) for d, p in chains]
    state = [s_ref[d, p] for d, p in chains]
    g = [_dot_nt(x, z) for x, z in zip(ar, bk)]
    xs = [_dot_nt(x, s.astype(BF16)) for x, s in zip(ar, state)]
    a_ab, a_kv, a_rb = [], [], []
    for (d, _), gc in zip(chains, g):
        strict, incl = streams[d]["strict"], streams[d]["incl"]
        gr = pltpu.roll(gc, RWKV_HEAD_DIM, 1)
        vs_b = jnp.where(head0_rows, gc, gr)
        vs_k = jnp.where(head0_rows, gr, gc)
        a_ab.append(jnp.where(strict, vs_b[:2 * L], 0.0).astype(BF16))
        a_kv.append(jnp.concatenate([jnp.where(strict, vs_k[:2 * L], 0.0),
                                     jnp.where(incl, vs_k[2 * L:], 0.0)], axis=0).astype(BF16))
        a_rb.append(jnp.where(incl, vs_b[2 * L:], 0.0).astype(BF16))
    av = [_dot(a, v) for a, v in zip(a_kv, v_s)]
    w = [x[:2 * L] + y[:2 * L] for x, y in zip(xs, av)]
    am = a_ab
    w = [wc + _dot(a, wc.astype(BF16)) for a, wc in zip(am, w)]
    for _ in range(5):
        am = [_dot(a, a).astype(BF16) for a in am]
        w = [wc + _dot(a, wc.astype(BF16)) for a, wc in zip(am, w)]
    sa = [wc.astype(BF16) for wc in w]
    ys = [x[2 * L:] + y[2 * L:] + _dot(a, s) for x, y, a, s in zip(xs, av, a_rb, sa)]
    for (d, p), yc in zip(chains, ys):
        y_refs[d][:, p * PAIR:(p + 1) * PAIR] = yc[:L] + yc[L:]
    for (d, p), s_old, wc in zip(chains, state, w):
        sa_v = jnp.concatenate([wc[:L] + wc[L:], side("v", d, p)], axis=0).astype(BF16)
        bhkh = jnp.concatenate([side("bh", d, p), side("kh", d, p)], axis=0).astype(BF16)
        upd = jnp.where(same_head, _dot_tn(sa_v, bhkh), 0.0)
        s_ref[d, p] = s_old * side("p_end", d, p) + upd


def _cast_slices(w, l, n_steps):
    _, n_rows, n_cols = w.shape
    n_blk = 1
    while n_blk * 2 <= n_steps and n_rows % (n_blk * 2) == 0 and (n_rows // (n_blk * 2)) % 16 == 0:
        n_blk *= 2
    rows = n_rows // n_blk
    step = lambda i: jnp.minimum(i, n_blk - 1)
    return (pl.BlockSpec((None, rows, n_cols), lambda i: (l, step(i), 0)),
            pl.BlockSpec((rows, n_cols), lambda i: (step(i), 0)),
            jax.ShapeDtypeStruct((n_rows, n_cols), BF16))


def _rwkv_scan(r, v, kk, lwf, lwb, kdf, kdb, bf, bb, s_len, n_ctx, cast_weights, l):
    m = r.shape[0]
    nc = m // CHUNK
    nc_lat = s_len // CHUNK
    nc_ctx = n_ctx // CHUNK
    fwd = lambda i: (jnp.where(i < nc_ctx, nc_lat + i, i - nc_ctx), 0)
    bwd = lambda i: (nc - 1 - i, 0)
    fs = pl.BlockSpec((CHUNK, RWKV_WIDTH), fwd)
    bs = pl.BlockSpec((CHUNK, RWKV_WIDTH), bwd)
    cast_in, cast_out, cast_shapes = zip(*[_cast_slices(w, l, nc) for w in cast_weights])
    cast_bytes = sum(spec.block_shape[0] * spec.block_shape[1] * 6 for spec in cast_out)
    outs = pl.pallas_call(
        functools.partial(_scan_kernel, n_cast=len(cast_weights)),
        grid=(nc,),
        in_specs=[fs] * 6 + [bs] * 6 + list(cast_in),
        out_specs=[fs, bs] + list(cast_out),
        out_shape=[jax.ShapeDtypeStruct((m, RWKV_WIDTH), F32)] * 2 + list(cast_shapes),
        scratch_shapes=[pltpu.VMEM((2, N_PAIRS, PAIR, PAIR), F32)],
        compiler_params=pltpu.CompilerParams(
            dimension_semantics=("arbitrary",),
            vmem_limit_bytes=_vmem(2 * 14 * CHUNK * RWKV_WIDTH * 4 + 2 * N_PAIRS * PAIR * PAIR * 4
                                   + 2 * cast_bytes + 24 * 1024 * 1024)),
        name="rwkv_scan",
    )(r, v, kk, lwf, kdf, bf, r, v, kk, lwb, kdb, bb, *cast_weights)
    return outs[0], outs[1], outs[2:]


def _out_kernel(att_ref, yf_ref, yb_ref, g_ref, bon_ref, x_ref, wa_ref, wr_ref, lnw_ref, lnb_ref,
                seg_ref, exp_ref, gpm_ref, gt1_ref, gmlp_ref, sc2_ref, sh2_ref, x1_ref, h2_ref):
    mix_att = _dot(att_ref[...], wa_ref[...])
    inv_n = 1.0 / RWKV_HEAD_DIM
    seg, expand = seg_ref[...], exp_ref[...]
    rows = att_ref.shape[0] // OUT_SUB_BLOCKS
    blocks = [pl.ds(n * rows, rows) for n in range(OUT_SUB_BLOCKS)]
    y = [yf_ref[b, :] + yb_ref[b, :] for b in blocks]
    s1 = [_bdot(v, seg) for v in y]
    yc = [v - _split_dot(s, expand) * inv_n for v, s in zip(y, s1)]
    s2 = [_bdot(c * c, seg) for c in yc]
    var = [_bdot(s, expand) * inv_n for s in s2]
    rec = [((c * lax.rsqrt(vr + GN_EPS) * lnw_ref[...] + lnb_ref[...]) + bon_ref[b, :]) * g_ref[b, :]
           for c, vr, b in zip(yc, var, blocks)]
    mix = mix_att + _bdot(jnp.concatenate(rec, axis=0), wr_ref[...])
    ms = jnp.mean(mix * mix, axis=-1, keepdims=True)
    x1 = x_ref[...] + (mix * lax.rsqrt(ms + NORM_EPS)) * (gt1_ref[...] * gpm_ref[...])
    x1_ref[...] = x1
    ms2 = jnp.mean(x1 * x1, axis=-1, keepdims=True)
    h2 = (x1 * lax.rsqrt(ms2 + NORM_EPS)) * (gmlp_ref[...] * (1.0 + sc2_ref[...])) + sh2_ref[...]
    h2_ref[...] = h2.astype(BF16)


def _out_proj(att, yf, yb, g, bonus, x, w_out, lnw, lnb, seg, expand, gpm, gt1, gmlp, sc2, sh2, tm):
    s_len, d = x.shape
    assert ATT_WIDTH == RWKV_WIDTH and w_out.shape == (ATT_WIDTH + RWKV_WIDTH, d)
    row = lambda w: pl.BlockSpec((tm, w), lambda i: (i, 0))
    full = lambda arr: pl.BlockSpec(arr.shape, lambda i: (0,) * arr.ndim)
    w_half = lambda n: pl.BlockSpec((ATT_WIDTH, d), lambda i: (n, 0))
    consts = (lnw, lnb, seg, expand, gpm, gt1, gmlp, sc2, sh2)
    return pl.pallas_call(
        _out_kernel,
        grid=(s_len // tm,),
        in_specs=[row(ATT_WIDTH)] + [row(RWKV_WIDTH)] * 4 + [row(d), w_half(0), w_half(1)]
                 + [full(c) for c in consts],
        out_specs=[row(d), row(d)],
        out_shape=[jax.ShapeDtypeStruct((s_len, d), F32), jax.ShapeDtypeStruct((s_len, d), BF16)],
        compiler_params=pltpu.CompilerParams(
            dimension_semantics=("arbitrary",),
            vmem_limit_bytes=_vmem(2 * tm * (ATT_WIDTH * 2 + 4 * RWKV_WIDTH * 4 + d * 4 + d * 4 + d * 2)
                                   + 2 * 2 * d * d + 8 * tm * d * 4 + 4 * 1024 * 1024)),
        name="out_proj",
    )(att, yf, yb, g, bonus, x, w_out, w_out, *consts)


def _mlp_kernel(h_ref, w1_ref, w2_ref, x1_ref, g_ref, gt_ref, o_ref, acc_ref):
    f = pl.program_id(1)

    @pl.when(f == 0)
    def _():
        acc_ref[...] = jnp.zeros_like(acc_ref)

    u = jnp.maximum(_dot(h_ref[...], w1_ref[...]), 0.0)
    acc_ref[...] += _dot((u * u).astype(BF16), w2_ref[...])

    @pl.when(f == pl.num_programs(1) - 1)
    def _():
        ff = acc_ref[...]
        ms = jnp.mean(ff * ff, axis=-1, keepdims=True)
        o_ref[...] = x1_ref[...] + (ff * lax.rsqrt(ms + NORM_EPS)) * (gt_ref[...] * g_ref[...])


def _mlp(h2, w1, w2, x1, gain, gt2, tm, tf):
    s_len, d = x1.shape
    dff = w1.shape[1]
    return pl.pallas_call(
        _mlp_kernel,
        grid=(s_len // tm, dff // tf),
        in_specs=[pl.BlockSpec((tm, d), lambda i, f: (i, 0)),
                  pl.BlockSpec((d, tf), lambda i, f: (0, f)),
                  pl.BlockSpec((tf, d), lambda i, f: (f, 0)),
                  pl.BlockSpec((tm, d), lambda i, f: (i, 0)),
                  pl.BlockSpec((1, d), lambda i, f: (0, 0)),
                  pl.BlockSpec((1, d), lambda i, f: (0, 0))],
        out_specs=pl.BlockSpec((tm, d), lambda i, f: (i, 0)),
        out_shape=jax.ShapeDtypeStruct((s_len, d), F32),
        scratch_shapes=[pltpu.VMEM((tm, d), F32)],
        compiler_params=pltpu.CompilerParams(
            dimension_semantics=("arbitrary", "arbitrary"),
            vmem_limit_bytes=_vmem(2 * (tm * d * 2 + 2 * d * tf * 2 + 2 * tm * d * 4)
                                   + tm * d * 4 + 3 * tm * tf * 4 + 4 * 1024 * 1024)),
        name="mlp",
    )(h2, w1, w2, x1, gain, gt2)


def _rope_tables(s_len, rows_padded):
    rows = s_len // GRID_W
    n_freq = ATT_HEAD_DIM // 4
    inv_freq = ROPE_BASE ** (-jnp.arange(n_freq, dtype=F32) / n_freq)
    ang_r = jnp.arange(rows, dtype=F32)[:, None] * inv_freq[None, :]
    ang_c = jnp.arange(GRID_W, dtype=F32)[:, None] * inv_freq[None, :]
    zr = jnp.zeros((rows, 2 * n_freq), F32)
    zc = jnp.zeros((GRID_W, 2 * n_freq), F32)
    pad = lambda t: jnp.pad(t, ((0, rows_padded - rows), (0, 0)))[:, None, :]
    row_cos = pad(jnp.concatenate([jnp.cos(ang_r), jnp.cos(ang_r), zr], axis=-1))
    row_sin = pad(jnp.concatenate([-jnp.sin(ang_r), jnp.sin(ang_r), zr], axis=-1))
    col_cos = jnp.concatenate([zc, jnp.cos(ang_c), jnp.cos(ang_c)], axis=-1)
    col_sin = jnp.concatenate([zc, -jnp.sin(ang_c), jnp.sin(ang_c)], axis=-1)
    return row_cos, row_sin, col_cos, col_sin


class _Tiles(NamedTuple):
    cat_rows: int
    prep_rows: int
    out_rows: int
    mlp_rows: int
    mlp_ff: int


def _tile_sizes(s_len, n_ctx):
    cat_rows = 768 if (s_len + n_ctx) % 768 == 0 and s_len % 768 + n_ctx == 768 else n_ctx
    return _Tiles(cat_rows=cat_rows, prep_rows=256, out_rows=512, mlp_rows=512, mlp_ff=1024)


def _block_diag2(w):
    _, k, n = w.shape
    z = jnp.zeros((k, n), w.dtype)
    return jnp.concatenate([jnp.concatenate([w[0], z], axis=1), jnp.concatenate([z, w[1]], axis=1)], axis=0)


def _layer(x2, ctx2, c8, l, w_ada, b_ada, g_pre_mix, g_post_mix, g_pre_mlp, g_post_mlp, w_in, attn_sink,
           rwkv_mu_prev, rwkv_mu_next, rwkv_w0, rwkv_w2, rwkv_a0, rwkv_a2, rwkv_g2, rwkv_k_k, rwkv_k_a,
           rwkv_r_k, rwkv_ln_w, rwkv_ln_b, w_out, w_mlp_in, w_mlp_out):
    s_len, d = x2.shape
    n_ctx = ctx2.shape[0]
    row = lambda v: v.reshape(1, -1)

    mod = _ada(c8, w_ada[l], row(b_ada[l]))
    sh1, sc1, gt1, sh2, sc2, gt2 = [mod[:2, j * d:(j + 1) * d] for j in range(N_MOD)]

    tiles = _tile_sizes(s_len, n_ctx)
    h = _norm_mod(x2, ctx2, row(g_pre_mix[l]), sc1.reshape(2, 1, d), sh1.reshape(2, 1, d), tiles.cat_rows)

    rope_tables = _rope_tables(s_len, ((s_len + n_ctx) // tiles.cat_rows) * (tiles.cat_rows // GRID_W))
    p_in = _proj(h, w_in[l].astype(BF16), rope_tables, s_len, tiles.cat_rows, w_in.shape[2] // 3,
                 N_Q_HEADS + N_KV_HEADS)

    att = _attention(p_in, attn_sink[l], s_len, n_ctx)

    head = jnp.arange(RWKV_WIDTH) // RWKV_HEAD_DIM
    seg = (head[:, None] == jnp.arange(LANES)[None, :]).astype(BF16)
    expand = seg.T
    streams = _rwkv_prep(
        p_in, row(rwkv_mu_prev[l]), row(rwkv_mu_next[l]),
        (0.5 * _block_diag2(rwkv_w2[l])).astype(BF16), 0.5 * rwkv_w0[l].reshape(1, -1),
        (0.5 * _block_diag2(rwkv_a2[l])).astype(BF16), 0.5 * rwkv_a0[l].reshape(1, -1),
        rwkv_g2[l].astype(BF16), row(rwkv_k_k[l]), row(rwkv_k_a[l]), rwkv_r_k[l].reshape(1, -1),
        seg, expand, s_len, n_ctx, tiles.prep_rows)
    r, v, kk, g, bonus, lwf, lwb, kdf, kdb, bf, bb = streams
    yf, yb, (w_out_b, w_mlp_in_b, w_mlp_out_b) = _rwkv_scan(
        r, v, kk, lwf, lwb, kdf, kdb, bf, bb, s_len, n_ctx, (w_out, w_mlp_in, w_mlp_out), l)

    x1, h2 = _out_proj(att, yf, yb, g, bonus, x2, w_out_b,
                       row(rwkv_ln_w[l]), row(rwkv_ln_b[l]), seg, expand,
                       row(g_post_mix[l]), gt1[0:1], row(g_pre_mlp[l]), sc2[0:1], sh2[0:1], tiles.out_rows)
    return _mlp(h2, w_mlp_in_b, w_mlp_out_b, x1, row(g_post_mlp[l]), gt2[0:1], tiles.mlp_rows, tiles.mlp_ff)


def kernel(x, c, ctx, c_ctx, w_ada, b_ada, g_pre_mix, g_post_mix, g_pre_mlp, g_post_mlp, w_in, attn_sink,
           rwkv_mu_prev, rwkv_mu_next, rwkv_w0, rwkv_w2, rwkv_a0, rwkv_a2, rwkv_g2, rwkv_k_k, rwkv_k_a,
           rwkv_r_k, rwkv_ln_w, rwkv_ln_b, w_out, w_mlp_in, w_mlp_out):
    b, s_len, d = x.shape
    depth = w_ada.shape[0]
    assert b == 1 and depth == 1, "single sample, single layer (context stream is not carried to a next layer)"
    c8 = jnp.zeros((8, d), F32).at[0].set(c[0]).at[1].set(c_ctx)
    out = _layer(x[0], ctx[0], c8, 0, w_ada, b_ada, g_pre_mix, g_post_mix, g_pre_mlp, g_post_mlp, w_in,
                 attn_sink, rwkv_mu_prev, rwkv_mu_next, rwkv_w0, rwkv_w2, rwkv_a0, rwkv_a2, rwkv_g2,
                 rwkv_k_k, rwkv_k_a, rwkv_r_k, rwkv_ln_w, rwkv_ln_b, w_out, w_mlp_in, w_mlp_out)
    return out[None]
```

```python
import functools
from typing import NamedTuple

import jax
import jax.numpy as jnp
from jax import lax
from jax.experimental import pallas as pl
from jax.experimental.pallas import tpu as pltpu

F32 = jnp.float32
BF16 = jnp.bfloat16

GRID_W = 64
ATT_HEAD_DIM = 128
N_Q_HEADS = 8
N_KV_HEADS = 2
GQA_GROUP = N_Q_HEADS // N_KV_HEADS
ATT_WIDTH = N_Q_HEADS * ATT_HEAD_DIM
KV_WIDTH = N_KV_HEADS * ATT_HEAD_DIM
ATT_IN = ATT_WIDTH + 2 * KV_WIDTH
WINDOW = 128
BLOCK = 128
ROPE_BASE = 10000.0
RWKV_HEAD_DIM = 64
RWKV_WIDTH = 1024
N_RWKV_HEADS = RWKV_WIDTH // RWKV_HEAD_DIM
LORA_W = 64
LORA_A = 64
LORA_G = 128
N_MOD = 6
NORM_EPS = 1e-6
GN_EPS = 64e-5
DECAY_SCALE = 0.6065306597126334

LANES = 128
MXU_WIDTH = 256
CHUNK = 64
PAIR = 2 * RWKV_HEAD_DIM
N_PAIRS = RWKV_WIDTH // PAIR
PREP_OUT_DTYPES = (BF16, BF16, BF16, BF16, BF16, F32, F32, BF16, BF16, BF16, BF16)
ATTN_BLOCKS_PER_STEP = 8
HALO_ROWS = 16
OUT_SUB_BLOCKS = 4
VMEM_CAP = 56 * 1024 * 1024


def _vmem(nbytes):
    return int(min(VMEM_CAP, max(16 * 1024 * 1024, nbytes)))


def _dot(a, b):
    return jnp.dot(a, b, preferred_element_type=F32)


def _dot_nt(a, b):
    return lax.dot_general(a, b, (((1,), (1,)), ((), ())), preferred_element_type=F32)


def _dot_tn(a, b):
    return lax.dot_general(a, b, (((0,), (0,)), ((), ())), preferred_element_type=F32)


def _bdot(a, b):
    return _dot(a.astype(BF16), b.astype(BF16))


def _bdot_nt(a, b):
    return _dot_nt(a.astype(BF16), b.astype(BF16))


def _bdot_tn(a, b):
    return _dot_tn(a.astype(BF16), b.astype(BF16))


def _split_dot(x, m):
    hi = x.astype(BF16)
    lo = (x - hi.astype(F32)).astype(BF16)
    return _dot(hi, m) + _dot(lo, m)


def _sigmoid(z):
    return 0.5 * jnp.tanh(0.5 * z) + 0.5


def _ada_kernel(c_ref, w_ref, b_ref, o_ref):
    c = c_ref[...]
    s = c * jax.nn.sigmoid(c)
    o_ref[...] = _bdot(s, w_ref[...]) + b_ref[...]


def _ada(c8, w_ada, b_ada):
    d, n = w_ada.shape
    tn = 2048
    return pl.pallas_call(
        _ada_kernel,
        grid=(n // tn,),
        in_specs=[pl.BlockSpec((8, d), lambda j: (0, 0)),
                  pl.BlockSpec((d, tn), lambda j: (0, j)),
                  pl.BlockSpec((1, tn), lambda j: (0, j))],
        out_specs=pl.BlockSpec((8, tn), lambda j: (0, j)),
        out_shape=jax.ShapeDtypeStruct((8, n), F32),
        compiler_params=pltpu.CompilerParams(
            dimension_semantics=("arbitrary",),
            vmem_limit_bytes=_vmem(2 * d * tn * 4 + 4 * 1024 * 1024)),
        name="ada",
    )(c8, w_ada, b_ada)


def _norm_mod_kernel(x_ref, c_ref, g_ref, sc_ref, sh_ref, o_ref, *, lat_last):
    def emit(x, who, rows):
        ms = jnp.mean(x * x, axis=-1, keepdims=True)
        y = (x * lax.rsqrt(ms + NORM_EPS)) * (g_ref[...] * (1.0 + sc_ref[who])) + sh_ref[who]
        o_ref[rows, :] = y.astype(BF16)

    i = pl.program_id(0)
    last = pl.num_programs(0) - 1

    @pl.when(i < last)
    def _():
        emit(x_ref[...], 0, slice(None))

    @pl.when(i == last)
    def _():
        if lat_last:
            emit(x_ref[:lat_last, :], 0, slice(0, lat_last))
        emit(c_ref[...], 1, slice(lat_last, None))


def _norm_mod(x2, ctx2, gain, sc2, sh2, tm):
    s_len, d = x2.shape
    n_ctx = ctx2.shape[0]
    n_tiles = (s_len + n_ctx) // tm
    lat_last = s_len - (n_tiles - 1) * tm
    assert n_tiles * tm == s_len + n_ctx and lat_last + n_ctx == tm and lat_last % 16 == 0
    both = pl.BlockSpec((2, 1, d), lambda i: (0, 0, 0))
    return pl.pallas_call(
        functools.partial(_norm_mod_kernel, lat_last=lat_last),
        grid=(n_tiles,),
        in_specs=[pl.BlockSpec((tm, d), lambda i: (jnp.minimum(i, (s_len - 1) // tm), 0)),
                  pl.BlockSpec((n_ctx, d), lambda i: (0, 0)),
                  pl.BlockSpec((1, d), lambda i: (0, 0)),
                  both, both],
        out_specs=pl.BlockSpec((tm, d), lambda i: (i, 0)),
        out_shape=jax.ShapeDtypeStruct((n_tiles * tm, d), BF16),
        compiler_params=pltpu.CompilerParams(
            dimension_semantics=("arbitrary",),
            vmem_limit_bytes=_vmem(2 * tm * d * 6 + 2 * n_ctx * d * 4 + 4 * tm * d * 4 + 4 * 1024 * 1024)),
        name="norm_mod",
    )(x2, ctx2, gain, sc2, sh2)


def _rope_heads(acc, cos, sin, n_heads):
    lane = lax.broadcasted_iota(jnp.int32, (1, ATT_HEAD_DIM), 1)
    first = (lane % 64) < 32
    outs = []
    for h in range(n_heads):
        xh = acc[:, h * ATT_HEAD_DIM:(h + 1) * ATT_HEAD_DIM]
        partner = jnp.where(first, pltpu.roll(xh, 96, 1), pltpu.roll(xh, 32, 1))
        outs.append(xh * cos + partner * sin)
    return outs


def _proj_kernel(a_ref, w_ref, rcos_ref, rsin_ref, ccos_ref, csin_ref, o_ref, *, rope_heads, s_len):
    acc = _dot(a_ref[...], w_ref[...])
    tm = a_ref.shape[0]
    i, j = pl.program_id(0), pl.program_id(1)
    row = i * tm + lax.broadcasted_iota(jnp.int32, (tm, 1), 0)
    rotate = row < jnp.where(j == 0, s_len, 0)
    cos = jnp.where(rotate, (rcos_ref[...] + ccos_ref[...][None]).reshape(tm, ATT_HEAD_DIM), 1.0)
    sin = jnp.where(rotate, (rsin_ref[...] + csin_ref[...][None]).reshape(tm, ATT_HEAD_DIM), 0.0)
    for h, r in enumerate(_rope_heads(acc, cos, sin, rope_heads)):
        o_ref[:, h * ATT_HEAD_DIM:(h + 1) * ATT_HEAD_DIM] = r.astype(o_ref.dtype)
    o_ref[:, rope_heads * ATT_HEAD_DIM:] = acc[:, rope_heads * ATT_HEAD_DIM:].astype(o_ref.dtype)


def _proj(a, w, rope_tables, s_len, tm, tn, rope_heads):
    m, k = a.shape
    n = w.shape[1]
    n_tiles = m // tm
    assert rope_heads * ATT_HEAD_DIM <= tn and tn % LANES == 0 and tm % GRID_W == 0
    g = tm // GRID_W
    row_spec = pl.BlockSpec((g, 1, ATT_HEAD_DIM), lambda i, j: (i, 0, 0))
    col_spec = pl.BlockSpec((GRID_W, ATT_HEAD_DIM), lambda i, j: (0, 0))
    return pl.pallas_call(
        functools.partial(_proj_kernel, rope_heads=rope_heads, s_len=s_len),
        grid=(n_tiles, pl.cdiv(n, tn)),
        in_specs=[pl.BlockSpec((tm, k), lambda i, j: (i, 0)),
                  pl.BlockSpec((k, tn), lambda i, j: (0, j)),
                  row_spec, row_spec, col_spec, col_spec],
        out_specs=pl.BlockSpec((tm, tn), lambda i, j: (i, j)),
        out_shape=jax.ShapeDtypeStruct((m, n), BF16),
        compiler_params=pltpu.CompilerParams(
            dimension_semantics=("arbitrary", "arbitrary"),
            vmem_limit_bytes=_vmem(2 * (tm * k * 2 + k * tn * 2 + tm * tn * 2 + 2 * tm * 512)
                                   + tm * tn * 8 + 4 * 1024 * 1024)),
        name="proj_in",
    )(a, w, *rope_tables)


def _attn_kernel(sink_ref, q_ref, *rest, nb, n_ctx):
    nq = ATTN_BLOCKS_PER_STEP
    k_refs, v_refs = rest[:nq + 2], rest[nq + 2:2 * (nq + 2)]
    kx_ref, vx_ref, o_ref = rest[2 * (nq + 2):]
    i = pl.program_id(0)
    scale = ATT_HEAD_DIM ** -0.5
    rows = GQA_GROUP * BLOCK
    span = 3 * BLOCK
    ncol = span + n_ctx
    row = lax.broadcasted_iota(jnp.int32, (rows, ncol), 0) % BLOCK
    col = lax.broadcasted_iota(jnp.int32, (rows, ncol), 1)
    in_window = jnp.abs(col - BLOCK - row) <= WINDOW
    chains = [(t, hk) for t in range(nq) for hk in range(N_KV_HEADS)]
    valid = []
    for t in range(nq):
        blk = nq * i + t
        lo = jnp.where(blk > 0, 0, BLOCK)
        hi = jnp.where(blk < nb - 1, span, 2 * BLOCK)
        valid.append((in_window & (col >= lo) & (col < hi)) | (col >= span))
    ks = lambda hk: slice(hk * ATT_HEAD_DIM, (hk + 1) * ATT_HEAD_DIM)
    q = [(jnp.concatenate(
        [q_ref[t * BLOCK:(t + 1) * BLOCK, (hk * GQA_GROUP + g) * ATT_HEAD_DIM:(hk * GQA_GROUP + g + 1) * ATT_HEAD_DIM]
         for g in range(GQA_GROUP)], axis=0).astype(F32) * scale).astype(BF16) for t, hk in chains]
    k_all = [jnp.concatenate([k_refs[t][:, ks(hk)], k_refs[t + 1][:, ks(hk)], k_refs[t + 2][:, ks(hk)],
                              kx_ref[:, ks(hk)]], axis=0).astype(BF16) for t, hk in chains]
    v_all = [jnp.concatenate([v_refs[t][:, ks(hk)], v_refs[t + 1][:, ks(hk)], v_refs[t + 2][:, ks(hk)],
                              vx_ref[:, ks(hk)]], axis=0).astype(BF16) for t, hk in chains]
    sink = [jnp.concatenate([jnp.full((BLOCK, 1), sink_ref[hk * GQA_GROUP + g], F32) for g in range(GQA_GROUP)],
                            axis=0) for _, hk in chains]
    s = [jnp.where(valid[t], _dot_nt(qh, kh), -jnp.inf) for (t, _), qh, kh in zip(chains, q, k_all)]
    m = [jnp.maximum(jnp.max(sh, axis=-1, keepdims=True), sk) for sh, sk in zip(s, sink)]
    e = [jnp.exp(sh - mh) for sh, mh in zip(s, m)]
    denom = [jnp.sum(eh, axis=-1, keepdims=True) + jnp.exp(sk - mh) for eh, sk, mh in zip(e, sink, m)]
    o = [_dot(eh.astype(BF16), vh) * (1.0 / dh) for eh, vh, dh in zip(e, v_all, denom)]
    for (t, hk), oc in zip(chains, o):
        for g in range(GQA_GROUP):
            h = hk * GQA_GROUP + g
            o_ref[t * BLOCK:(t + 1) * BLOCK, h * ATT_HEAD_DIM:(h + 1) * ATT_HEAD_DIM] = (
                oc[g * BLOCK:(g + 1) * BLOCK].astype(BF16))


def _attention(qkv, sink, s_len, n_ctx):
    nq = ATTN_BLOCKS_PER_STEP
    nb = s_len // BLOCK
    assert nb % nq == 0 and s_len % n_ctx == 0
    kcol = ATT_WIDTH // KV_WIDTH
    vcol = kcol + 1
    ctx_blk = s_len // n_ctx
    kv = lambda u, c: pl.BlockSpec((BLOCK, KV_WIDTH), lambda i: (jnp.clip(nq * i - 1 + u, 0, nb - 1), c))
    return pl.pallas_call(
        functools.partial(_attn_kernel, nb=nb, n_ctx=n_ctx),
        grid=(nb // nq,),
        in_specs=[pl.BlockSpec(memory_space=pltpu.SMEM),
                  pl.BlockSpec((nq * BLOCK, ATT_WIDTH), lambda i: (i, 0))]
                 + [kv(u, kcol) for u in range(nq + 2)] + [kv(u, vcol) for u in range(nq + 2)]
                 + [pl.BlockSpec((n_ctx, KV_WIDTH), lambda i: (ctx_blk, kcol)),
                    pl.BlockSpec((n_ctx, KV_WIDTH), lambda i: (ctx_blk, vcol))],
        out_specs=pl.BlockSpec((nq * BLOCK, ATT_WIDTH), lambda i: (i, 0)),
        out_shape=jax.ShapeDtypeStruct((s_len, ATT_WIDTH), BF16),
        compiler_params=pltpu.CompilerParams(
            dimension_semantics=("arbitrary",),
            vmem_limit_bytes=_vmem(40 * 1024 * 1024)),
        name="attention",
    )(sink, qkv, *([qkv] * (2 * (nq + 2) + 2)))


def _prep_kernel(x_ref, xp_ref, xn_ref, mup_ref, mun_ref, w2_ref, w0_ref, a2_ref, a0_ref, g2_ref,
                 kk_ref, ka_ref, rk_ref, seg_ref, exp_ref,
                 r_o, v_o, kk_o, g_o, bon_o, lwf_o, lwb_o, kdf_o, kdb_o, bf_o, bb_o,
                 *, tm, start_tiles, end_tiles):
    i = pl.program_id(0)
    r_w = RWKV_WIDTH
    slab = slice(ATT_IN, ATT_IN + mup_ref.shape[1])
    x = x_ref[:, slab].astype(F32)
    is_start = functools.reduce(jnp.logical_or, [i == t for t in start_tiles])
    is_end = functools.reduce(jnp.logical_or, [i == t for t in end_tiles])
    keep_prev = jnp.full((1, 1), jnp.where(is_start, 0, 1), jnp.int32) == 1
    keep_next = jnp.full((1, 1), jnp.where(is_end, 0, 1), jnp.int32) == 1
    prev_row = jnp.where(keep_prev, xp_ref[HALO_ROWS - 1:HALO_ROWS, slab].astype(F32), 0.0)
    next_row = jnp.where(keep_next, xn_ref[0:1, slab].astype(F32), 0.0)
    xp = pltpu.roll(x, 1, 0)
    xn = pltpu.roll(x, tm - 1, 0)
    row8 = lax.broadcasted_iota(jnp.int32, (8, 1), 0)
    xp = jnp.concatenate([jnp.where(row8 == 0, prev_row, xp[:8]), xp[8:]], axis=0)
    xn = jnp.concatenate([xn[:tm - 8], jnp.where(row8 == 7, next_row, xn[tm - 8:])], axis=0)
    mup = mup_ref[...]
    mun = mun_ref[...]
    s = x * (1.0 - mup - mun) + xp * mup + xn * mun

    r = s[:, :r_w]
    k = s[:, r_w:2 * r_w]
    v = s[:, 2 * r_w:3 * r_w]
    o1 = 3 * r_w
    o2 = o1 + 2 * LORA_W
    o3 = o2 + 2 * LORA_A
    w_lo = s[:, o1:o2]
    a_lo = s[:, o2:o3]
    g_lo = s[:, o3:]

    tw = jnp.tanh(w0_ref[...] + _bdot(jnp.tanh(w_lo), w2_ref[...]))
    ta = jnp.tanh(a0_ref[...] + _bdot(a_lo, a2_ref[...]))
    lw = (-0.5 * DECAY_SCALE) * tw - 0.5 * DECAY_SCALE
    g = _bdot(_sigmoid(g_lo), g2_ref[...])

    kkf = k * kk_ref[...]
    ssq = _split_dot(_bdot(kkf * kkf, seg_ref[...]), exp_ref[...])
    kk = kkf * jnp.minimum(lax.rsqrt(ssq), 1e12)
    kkh = k * (0.5 * ka_ref[...])
    k1 = k - kkh
    kd_f = k1 + kkh * ta[:, :r_w]
    kd_b = k1 + kkh * ta[:, r_w:]
    k_bonus = 0.5 * (kd_f + kd_b)
    bonus = _split_dot(_bdot(r * k_bonus * rk_ref[...], seg_ref[...]), exp_ref[...]) * v
    kk_half = 0.5 * kk

    r_o[...] = r.astype(r_o.dtype)
    v_o[...] = v.astype(v_o.dtype)
    kk_o[...] = kk.astype(kk_o.dtype)
    g_o[...] = g.astype(g_o.dtype)
    bon_o[...] = bonus.astype(bon_o.dtype)
    lwf_o[...] = lw[:, :r_w]
    lwb_o[...] = lw[:, r_w:]
    kdf_o[...] = kd_f.astype(kdf_o.dtype)
    kdb_o[...] = kd_b.astype(kdb_o.dtype)
    bf_o[...] = (kk_half + kk_half * ta[:, :r_w]).astype(bf_o.dtype)
    bb_o[...] = (kk_half + kk_half * ta[:, r_w:]).astype(bb_o.dtype)


def _rwkv_prep(slab, mup, mun, w2blk, w0cat, a2blk, a0cat, g2, k_k, k_a, r_k, seg, expand, s_len, n_ctx, tm):
    m, width = slab.shape
    nt = m // tm
    hb = tm // HALO_ROWS
    nblk8 = m // HALO_ROWS
    start_tiles = (0, s_len // tm)
    end_tiles = (s_len // tm - 1, nt - 1)
    full = lambda arr: pl.BlockSpec(arr.shape, lambda i: (0,) * arr.ndim)
    consts = (mup, mun, w2blk, w0cat, a2blk, a0cat, g2, k_k, k_a, r_k, seg, expand)
    out_spec = pl.BlockSpec((tm, RWKV_WIDTH), lambda i: (i, 0))
    n_out = len(PREP_OUT_DTYPES)
    return pl.pallas_call(
        functools.partial(_prep_kernel, tm=tm, start_tiles=start_tiles, end_tiles=end_tiles),
        grid=(nt,),
        in_specs=[pl.BlockSpec((tm, width), lambda i: (i, 0)),
                  pl.BlockSpec((HALO_ROWS, width), lambda i: (jnp.maximum(i * hb - 1, 0), 0)),
                  pl.BlockSpec((HALO_ROWS, width), lambda i: (jnp.minimum((i + 1) * hb, nblk8 - 1), 0))]
                 + [full(c) for c in consts],
        out_specs=[out_spec] * n_out,
        out_shape=[jax.ShapeDtypeStruct((m, RWKV_WIDTH), dt) for dt in PREP_OUT_DTYPES],
        compiler_params=pltpu.CompilerParams(
            dimension_semantics=("arbitrary",),
            vmem_limit_bytes=_vmem(2 * tm * width * 4 + 2 * n_out * tm * RWKV_WIDTH * 4
                                   + 12 * tm * RWKV_WIDTH * 4 + 8 * 1024 * 1024)),
        name="rwkv_prep",
    )(slab, slab, slab, *consts)


def _stack(x, m0):
    return jnp.concatenate([jnp.where(m0, x, 0.0), jnp.where(m0, 0.0, x)], axis=0)


def _scan_streams(r_ref, v_ref, kk_ref, lw_ref, kd_ref, b_ref, reverse):
    L = CHUNK
    lw = lw_ref[...]
    ti = lax.broadcasted_iota(jnp.int32, (L, L), 0)
    tj = lax.broadcasted_iota(jnp.int32, (L, L), 1)
    tri = jnp.where((ti <= tj) if reverse else (ti >= tj), 1.0, 0.0).astype(BF16)
    lw_hi = lw.astype(BF16)
    lw_lo = (lw - lw_hi.astype(F32)).astype(BF16)
    c = _dot(tri, lw_hi) + _dot(tri, lw_lo)
    p_cum = jnp.exp(c)
    p_inv = jnp.exp(-c)
    p_prev = jnp.exp(c - lw)
    last = 0 if reverse else L - 1
    p_end = p_cum[last:last + 1, :]
    b_t = b_ref[...] * p_inv
    k_t = kd_ref[...] * p_inv
    gi = lax.broadcasted_iota(jnp.int32, (2 * L, 2 * L), 0)
    gj = lax.broadcasted_iota(jnp.int32, (2 * L, 2 * L), 1)
    same = (gi // L) == (gj // L)
    strict = same & ((gi < gj) if reverse else (gi > gj))
    incl = same & ((gi <= gj) if reverse else (gi >= gj))
    return dict(a=-kk_ref[...] * p_prev, r=r_ref[...] * p_cum, b=b_t, k=k_t, bh=b_t * p_end, kh=k_t * p_end,
                v=v_ref[...], p_end=p_end, strict=strict, incl=incl)


def _scan_kernel(rf, vf, kkf, lwf, kdf, bf, rb, vb, kkb, lwb, kdb, bb, *rest, n_cast):
    cast_in, (yf, yb), cast_out, s_ref = rest[:n_cast], rest[n_cast:n_cast + 2], rest[n_cast + 2:-1], rest[-1]

    @pl.when(pl.program_id(0) == 0)
    def _():
        s_ref[...] = jnp.zeros_like(s_ref)

    for src, dst in zip(cast_in, cast_out):
        dst[...] = src[...].astype(dst.dtype)

    L = CHUNK
    streams = (_scan_streams(rf, vf, kkf, lwf, kdf, bf, False), _scan_streams(rb, vb, kkb, lwb, kdb, bb, True))
    y_refs = (yf, yb)
    m0 = lax.broadcasted_iota(jnp.int32, (L, PAIR), 1) < RWKV_HEAD_DIM
    chains = [(d, p) for d in range(2) for p in range(N_PAIRS)]

    def side(name, d, p):
        return streams[d][name][:, p * PAIR:(p + 1) * PAIR]

    def stacked(name, d, p):
        return _stack(side(name, d, p), m0)

    head0_rows = (lax.broadcasted_iota(jnp.int32, (4 * L, PAIR), 0) // L) % 2 == 0
    same_head = ((lax.broadcasted_iota(jnp.int32, (PAIR, PAIR), 0) // RWKV_HEAD_DIM)
                 == (lax.broadcasted_iota(jnp.int32, (PAIR, PAIR), 1) // RWKV_HEAD_DIM))
    ar = [jnp.concatenate([stacked("a", d, p), stacked("r", d, p)], axis=0).astype(BF16) for d, p in chains]
    bk = [jnp.concatenate([side("b", d, p), side("k", d, p)], axis=0).astype(BF16) for d, p in chains]
    v_s = [stacked("v", d, p).astype(BF16) for d, p in chains]
    state = [s_ref[d, p] for d, p in chains]
    g = [_dot_nt(x, z) for x, z in zip(ar, bk)]
    xs = [_dot_nt(x, s.astype(BF16)) for x, s in zip(ar, state)]
    a_ab, a_kv, a_rb = [], [], []
    for (d, _), gc in zip(chains, g):
        strict, incl = streams[d]["strict"], streams[d]["incl"]
        gr = pltpu.roll(gc, RWKV_HEAD_DIM, 1)
        vs_b = jnp.where(head0_rows, gc, gr)
        vs_k = jnp.where(head0_rows, gr, gc)
        a_ab.append(jnp.where(strict, vs_b[:2 * L], 0.0).astype(BF16))
        a_kv.append(jnp.concatenate([jnp.where(strict, vs_k[:2 * L], 0.0),
                                     jnp.where(incl, vs_k[2 * L:], 0.0)], axis=0).astype(BF16))
        a_rb.append(jnp.where(incl, vs_b[2 * L:], 0.0).astype(BF16))
    av = [_dot(a, v) for a, v in zip(a_kv, v_s)]
    w = [x[:2 * L] + y[:2 * L] for x, y in zip(xs, av)]
    am = a_ab
    w = [wc + _dot(a, wc.astype(BF16)) for a, wc in zip(am, w)]
    for _ in range(5):
        am = [_dot(a, a).astype(BF16) for a in am]
        w = [wc + _dot(a, wc.astype(BF16)) for a, wc in zip(am, w)]
    sa = [wc.astype(BF16) for wc in w]
    ys = [x[2 * L:] + y[2 * L:] + _dot(a, s) for x, y, a, s in zip(xs, av, a_rb, sa)]
    for (d, p), yc in zip(chains, ys):
        y_refs[d][:, p * PAIR:(p + 1) * PAIR] = yc[:L] + yc[L:]
    for (d, p), s_old, wc in zip(chains, state, w):
        sa_v = jnp.concatenate([wc[:L] + wc[L:], side("v", d, p)], axis=0).astype(BF16)
        bhkh = jnp.concatenate([side("bh", d, p), side("kh", d, p)], axis=0).astype(BF16)
        upd = jnp.where(same_head, _dot_tn(sa_v, bhkh), 0.0)
        s_ref[d, p] = s_old * side("p_end", d, p) + upd


def _cast_slices(w, l, n_steps):
    _, n_rows, n_cols = w.shape
    n_blk = 1
    while n_blk * 2 <= n_steps and n_rows % (n_blk * 2) == 0 and (n_rows // (n_blk * 2)) % 16 == 0:
        n_blk *= 2
    rows = n_rows // n_blk
    step = lambda i: jnp.minimum(i, n_blk - 1)
    return (pl.BlockSpec((None, rows, n_cols), lambda i: (l, step(i), 0)),
            pl.BlockSpec((rows, n_cols), lambda i: (step(i), 0)),
            jax.ShapeDtypeStruct((n_rows, n_cols), BF16))


def _rwkv_scan(r, v, kk, lwf, lwb, kdf, kdb, bf, bb, s_len, n_ctx, cast_weights, l):
    m = r.shape[0]
    nc = m // CHUNK
    nc_lat = s_len // CHUNK
    nc_ctx = n_ctx // CHUNK
    fwd = lambda i: (jnp.where(i < nc_ctx, nc_lat + i, i - nc_ctx), 0)
    bwd = lambda i: (nc - 1 - i, 0)
    fs = pl.BlockSpec((CHUNK, RWKV_WIDTH), fwd)
    bs = pl.BlockSpec((CHUNK, RWKV_WIDTH), bwd)
    cast_in, cast_out, cast_shapes = zip(*[_cast_slices(w, l, nc) for w in cast_weights])
    cast_bytes = sum(spec.block_shape[0] * spec.block_shape[1] * 6 for spec in cast_out)
    outs = pl.pallas_call(
        functools.partial(_scan_kernel, n_cast=len(cast_weights)),
        grid=(nc,),
        in_specs=[fs] * 6 + [bs] * 6 + list(cast_in),
        out_specs=[fs, bs] + list(cast_out),
        out_shape=[jax.ShapeDtypeStruct((m, RWKV_WIDTH), F32)] * 2 + list(cast_shapes),
        scratch_shapes=[pltpu.VMEM((2, N_PAIRS, PAIR, PAIR), F32)],
        compiler_params=pltpu.CompilerParams(
            dimension_semantics=("arbitrary",),
            vmem_limit_bytes=_vmem(2 * 14 * CHUNK * RWKV_WIDTH * 4 + 2 * N_PAIRS * PAIR * PAIR * 4
                                   + 2 * cast_bytes + 24 * 1024 * 1024)),
        name="rwkv_scan",
    )(r, v, kk, lwf, kdf, bf, r, v, kk, lwb, kdb, bb, *cast_weights)
    return outs[0], outs[1], outs[2:]


def _out_kernel(att_ref, yf_ref, yb_ref, g_ref, bon_ref, x_ref, wa_ref, wr_ref, lnw_ref, lnb_ref,
                seg_ref, exp_ref, gpm_ref, gt1_ref, gmlp_ref, sc2_ref, sh2_ref, x1_ref, h2_ref):
    mix_att = _dot(att_ref[...], wa_ref[...])
    inv_n = 1.0 / RWKV_HEAD_DIM
    seg, expand = seg_ref[...], exp_ref[...]
    rows = att_ref.shape[0] // OUT_SUB_BLOCKS
    blocks = [pl.ds(n * rows, rows) for n in range(OUT_SUB_BLOCKS)]
    y = [yf_ref[b, :] + yb_ref[b, :] for b in blocks]
    s1 = [_bdot(v, seg) for v in y]
    yc = [v - _split_dot(s, expand) * inv_n for v, s in zip(y, s1)]
    s2 = [_bdot(c * c, seg) for c in yc]
    var = [_bdot(s, expand) * inv_n for s in s2]
    rec = [((c * lax.rsqrt(vr + GN_EPS) * lnw_ref[...] + lnb_ref[...]) + bon_ref[b, :]) * g_ref[b, :]
           for c, vr, b in zip(yc, var, blocks)]
    mix = mix_att + _bdot(jnp.concatenate(rec, axis=0), wr_ref[...])
    ms = jnp.mean(mix * mix, axis=-1, keepdims=True)
    x1 = x_ref[...] + (mix * lax.rsqrt(ms + NORM_EPS)) * (gt1_ref[...] * gpm_ref[...])
    x1_ref[...] = x1
    ms2 = jnp.mean(x1 * x1, axis=-1, keepdims=True)
    h2 = (x1 * lax.rsqrt(ms2 + NORM_EPS)) * (gmlp_ref[...] * (1.0 + sc2_ref[...])) + sh2_ref[...]
    h2_ref[...] = h2.astype(BF16)


def _out_proj(att, yf, yb, g, bonus, x, w_out, lnw, lnb, seg, expand, gpm, gt1, gmlp, sc2, sh2, tm):
    s_len, d = x.shape
    assert ATT_WIDTH == RWKV_WIDTH and w_out.shape == (ATT_WIDTH + RWKV_WIDTH, d)
    row = lambda w: pl.BlockSpec((tm, w), lambda i: (i, 0))
    full = lambda arr: pl.BlockSpec(arr.shape, lambda i: (0,) * arr.ndim)
    w_half = lambda n: pl.BlockSpec((ATT_WIDTH, d), lambda i: (n, 0))
    consts = (lnw, lnb, seg, expand, gpm, gt1, gmlp, sc2, sh2)
    return pl.pallas_call(
        _out_kernel,
        grid=(s_len // tm,),
        in_specs=[row(ATT_WIDTH)] + [row(RWKV_WIDTH)] * 4 + [row(d), w_half(0), w_half(1)]
                 + [full(c) for c in consts],
        out_specs=[row(d), row(d)],
        out_shape=[jax.ShapeDtypeStruct((s_len, d), F32), jax.ShapeDtypeStruct((s_len, d), BF16)],
        compiler_params=pltpu.CompilerParams(
            dimension_semantics=("arbitrary",),
            vmem_limit_bytes=_vmem(2 * tm * (ATT_WIDTH * 2 + 4 * RWKV_WIDTH * 4 + d * 4 + d * 4 + d * 2)
                                   + 2 * 2 * d * d + 8 * tm * d * 4 + 4 * 1024 * 1024)),
        name="out_proj",
    )(att, yf, yb, g, bonus, x, w_out, w_out, *consts)


def _mlp_kernel(h_ref, w1_ref, w2_ref, x1_ref, g_ref, gt_ref, o_ref, acc_ref):
    f = pl.program_id(1)

    @pl.when(f == 0)
    def _():
        acc_ref[...] = jnp.zeros_like(acc_ref)

    u = jnp.maximum(_dot(h_ref[...], w1_ref[...]), 0.0)
    acc_ref[...] += _dot((u * u).astype(BF16), w2_ref[...])

    @pl.when(f == pl.num_programs(1) - 1)
    def _():
        ff = acc_ref[...]
        ms = jnp.mean(ff * ff, axis=-1, keepdims=True)
        o_ref[...] = x1_ref[...] + (ff * lax.rsqrt(ms + NORM_EPS)) * (gt_ref[...] * g_ref[...])


def _mlp(h2, w1, w2, x1, gain, gt2, tm, tf):
    s_len, d = x1.shape
    dff = w1.shape[1]
    return pl.pallas_call(
        _mlp_kernel,
        grid=(s_len // tm, dff // tf),
        in_specs=[pl.BlockSpec((tm, d), lambda i, f: (i, 0)),
                  pl.BlockSpec((d, tf), lambda i, f: (0, f)),
                  pl.BlockSpec((tf, d), lambda i, f: (f, 0)),
                  pl.BlockSpec((tm, d), lambda i, f: (i, 0)),
                  pl.BlockSpec((1, d), lambda i, f: (0, 0)),
                  pl.BlockSpec((1, d), lambda i, f: (0, 0))],
        out_specs=pl.BlockSpec((tm, d), lambda i, f: (i, 0)),
        out_shape=jax.ShapeDtypeStruct((s_len, d), F32),
        scratch_shapes=[pltpu.VMEM((tm, d), F32)],
        compiler_params=pltpu.CompilerParams(
            dimension_semantics=("arbitrary", "arbitrary"),
            vmem_limit_bytes=_vmem(2 * (tm * d * 2 + 2 * d * tf * 2 + 2 * tm * d * 4)
                                   + tm * d * 4 + 3 * tm * tf * 4 + 4 * 1024 * 1024)),
        name="mlp",
    )(h2, w1, w2, x1, gain, gt2)


def _rope_tables(s_len, rows_padded):
    rows = s_len // GRID_W
    n_freq = ATT_HEAD_DIM // 4
    inv_freq = ROPE_BASE ** (-jnp.arange(n_freq, dtype=F32) / n_freq)
    ang_r = jnp.arange(rows, dtype=F32)[:, None] * inv_freq[None, :]
    ang_c = jnp.arange(GRID_W, dtype=F32)[:, None] * inv_freq[None, :]
    zr = jnp.zeros((rows, 2 * n_freq), F32)
    zc = jnp.zeros((GRID_W, 2 * n_freq), F32)
    pad = lambda t: jnp.pad(t, ((0, rows_padded - rows), (0, 0)))[:, None, :]
    row_cos = pad(jnp.concatenate([jnp.cos(ang_r), jnp.cos(ang_r), zr], axis=-1))
    row_sin = pad(jnp.concatenate([-jnp.sin(ang_r), jnp.sin(ang_r), zr], axis=-1))
    col_cos = jnp.concatenate([zc, jnp.cos(ang_c), jnp.cos(ang_c)], axis=-1)
    col_sin = jnp.concatenate([zc, -jnp.sin(ang_c), jnp.sin(ang_c)], axis=-1)
    return row_cos, row_sin, col_cos, col_sin


class _Tiles(NamedTuple):
    cat_rows: int
    prep_rows: int
    out_rows: int
    mlp_rows: int
    mlp_ff: int


def _tile_sizes(s_len, n_ctx):
    cat_rows = 768 if (s_len + n_ctx) % 768 == 0 and s_len % 768 + n_ctx == 768 else n_ctx
    return _Tiles(cat_rows=cat_rows, prep_rows=256, out_rows=512, mlp_rows=512, mlp_ff=1024)


def _block_diag2(w):
    _, k, n = w.shape
    z = jnp.zeros((k, n), w.dtype)
    return jnp.concatenate([jnp.concatenate([w[0], z], axis=1), jnp.concatenate([z, w[1]], axis=1)], axis=0)


def _layer(x2, ctx2, c8, l, w_ada, b_ada, g_pre_mix, g_post_mix, g_pre_mlp, g_post_mlp, w_in, attn_sink,
           rwkv_mu_prev, rwkv_mu_next, rwkv_w0, rwkv_w2, rwkv_a0, rwkv_a2, rwkv_g2, rwkv_k_k, rwkv_k_a,
           rwkv_r_k, rwkv_ln_w, rwkv_ln_b, w_out, w_mlp_in, w_mlp_out):
    s_len, d = x2.shape
    n_ctx = ctx2.shape[0]
    row = lambda v: v.reshape(1, -1)

    mod = _ada(c8, w_ada[l], row(b_ada[l]))
    sh1, sc1, gt1, sh2, sc2, gt2 = [mod[:2, j * d:(j + 1) * d] for j in range(N_MOD)]

    tiles = _tile_sizes(s_len, n_ctx)
    h = _norm_mod(x2, ctx2, row(g_pre_mix[l]), sc1.reshape(2, 1, d), sh1.reshape(2, 1, d), tiles.cat_rows)

    rope_tables = _rope_tables(s_len, ((s_len + n_ctx) // tiles.cat_rows) * (tiles.cat_rows // GRID_W))
    n_in = w_in.shape[2]
    tn_in = (n_in + -n_in % (2 * MXU_WIDTH)) // 2
    p_in = _proj(h, w_in[l].astype(BF16), rope_tables, s_len, tiles.cat_rows, tn_in, N_Q_HEADS + N_KV_HEADS)

    att = _attention(p_in, attn_sink[l], s_len, n_ctx)

    head = jnp.arange(RWKV_WIDTH) // RWKV_HEAD_DIM
    seg = (head[:, None] == jnp.arange(LANES)[None, :]).astype(BF16)
    expand = seg.T
    streams = _rwkv_prep(
        p_in, row(rwkv_mu_prev[l]), row(rwkv_mu_next[l]),
        (0.5 * _block_diag2(rwkv_w2[l])).astype(BF16), 0.5 * rwkv_w0[l].reshape(1, -1),
        (0.5 * _block_diag2(rwkv_a2[l])).astype(BF16), 0.5 * rwkv_a0[l].reshape(1, -1),
        rwkv_g2[l].astype(BF16), row(rwkv_k_k[l]), row(rwkv_k_a[l]), rwkv_r_k[l].reshape(1, -1),
        seg, expand, s_len, n_ctx, tiles.prep_rows)
    r, v, kk, g, bonus, lwf, lwb, kdf, kdb, bf, bb = streams
    yf, yb, (w_out_b, w_mlp_in_b, w_mlp_out_b) = _rwkv_scan(
        r, v, kk, lwf, lwb, kdf, kdb, bf, bb, s_len, n_ctx, (w_out, w_mlp_in, w_mlp_out), l)

    x1, h2 = _out_proj(att, yf, yb, g, bonus, x2, w_out_b,
                       row(rwkv_ln_w[l]), row(rwkv_ln_b[l]), seg, expand,
                       row(g_post_mix[l]), gt1[0:1], row(g_pre_mlp[l]), sc2[0:1], sh2[0:1], tiles.out_rows)
    return _mlp(h2, w_mlp_in_b, w_mlp_out_b, x1, row(g_post_mlp[l]), gt2[0:1], tiles.mlp_rows, tiles.mlp_ff)


def kernel(x, c, ctx, c_ctx, w_ada, b_ada, g_pre_mix, g_post_mix, g_pre_mlp, g_post_mlp, w_in, attn_sink,
           rwkv_mu_prev, rwkv_mu_next, rwkv_w0, rwkv_w2, rwkv_a0, rwkv_a2, rwkv_g2, rwkv_k_k, rwkv_k_a,
           rwkv_r_k, rwkv_ln_w, rwkv_ln_b, w_out, w_mlp_in, w_mlp_out):
    b, s_len, d = x.shape
    depth = w_ada.shape[0]
    assert b == 1 and depth == 1, "single sample, single layer (context stream is not carried to a next layer)"
    c8 = jnp.zeros((8, d), F32).at[0].set(c[0]).at[1].set(c_ctx)
    out = _layer(x[0], ctx[0], c8, 0, w_ada, b_ada, g_pre_mix, g_post_mix, g_pre_mlp, g_post_mlp, w_in,
                 attn_sink, rwkv_mu_prev, rwkv_mu_next, rwkv_w0, rwkv_w2, rwkv_a0, rwkv_a2, rwkv_g2,
                 rwkv_k_k, rwkv_k_a, rwkv_r_k, rwkv_ln_w, rwkv_ln_b, w_out, w_mlp_in, w_mlp_out)
    return out[None]
```

```python
import functools
from typing import NamedTuple

import jax
import jax.numpy as jnp
from jax import lax
from jax.experimental import pallas as pl
from jax.experimental.pallas import tpu as pltpu

F32 = jnp.float32
BF16 = jnp.bfloat16

GRID_W = 64
ATT_HEAD_DIM = 128
N_Q_HEADS = 8
N_KV_HEADS = 2
GQA_GROUP = N_Q_HEADS // N_KV_HEADS
ATT_WIDTH = N_Q_HEADS * ATT_HEAD_DIM
KV_WIDTH = N_KV_HEADS * ATT_HEAD_DIM
ATT_IN = ATT_WIDTH + 2 * KV_WIDTH
WINDOW = 128
BLOCK = 128
ROPE_BASE = 10000.0
RWKV_HEAD_DIM = 64
RWKV_WIDTH = 1024
N_RWKV_HEADS = RWKV_WIDTH // RWKV_HEAD_DIM
LORA_W = 64
LORA_A = 64
LORA_G = 128
N_MOD = 6
NORM_EPS = 1e-6
GN_EPS = 64e-5
DECAY_SCALE = 0.6065306597126334

LANES = 128
MXU_WIDTH = 256
CHUNK = 64
PAIR = 2 * RWKV_HEAD_DIM
N_PAIRS = RWKV_WIDTH // PAIR
PREP_OUT_DTYPES = (BF16, BF16, BF16, BF16, BF16, F32, F32, BF16, BF16, BF16, BF16)
ATTN_BLOCKS_PER_STEP = 8
HALO_ROWS = 16
OUT_SUB_BLOCKS = 4
VMEM_CAP = 56 * 1024 * 1024


def _vmem(nbytes):
    return int(min(VMEM_CAP, max(16 * 1024 * 1024, nbytes)))


def _dot(a, b):
    return jnp.dot(a, b, preferred_element_type=F32)


def _dot_nt(a, b):
    return lax.dot_general(a, b, (((1,), (1,)), ((), ())), preferred_element_type=F32)


def _dot_tn(a, b):
    return lax.dot_general(a, b, (((0,), (0,)), ((), ())), preferred_element_type=F32)


def _bdot(a, b):
    return _dot(a.astype(BF16), b.astype(BF16))


def _bdot_nt(a, b):
    return _dot_nt(a.astype(BF16), b.astype(BF16))


def _bdot_tn(a, b):
    return _dot_tn(a.astype(BF16), b.astype(BF16))


def _split_dot(x, m):
    hi = x.astype(BF16)
    lo = (x - hi.astype(F32)).astype(BF16)
    return _dot(hi, m) + _dot(lo, m)


def _sigmoid(z):
    return 0.5 * jnp.tanh(0.5 * z) + 0.5


def _ada_kernel(c_ref, w_ref, b_ref, o_ref):
    c = c_ref[...]
    s = c * jax.nn.sigmoid(c)
    o_ref[...] = _bdot(s, w_ref[...]) + b_ref[...]


def _ada(c8, w_ada, b_ada):
    d, n = w_ada.shape
    tn = 1024
    return pl.pallas_call(
        _ada_kernel,
        grid=(n // tn,),
        in_specs=[pl.BlockSpec((8, d), lambda j: (0, 0)),
                  pl.BlockSpec((d, tn), lambda j: (0, j)),
                  pl.BlockSpec((1, tn), lambda j: (0, j))],
        out_specs=pl.BlockSpec((8, tn), lambda j: (0, j)),
        out_shape=jax.ShapeDtypeStruct((8, n), F32),
        compiler_params=pltpu.CompilerParams(
            dimension_semantics=("arbitrary",),
            vmem_limit_bytes=_vmem(2 * d * tn * 4 + 4 * 1024 * 1024)),
        name="ada",
    )(c8, w_ada, b_ada)


def _norm_mod_kernel(x_ref, c_ref, g_ref, sc_ref, sh_ref, o_ref, *, lat_last):
    def emit(x, who, rows):
        ms = jnp.mean(x * x, axis=-1, keepdims=True)
        y = (x * lax.rsqrt(ms + NORM_EPS)) * (g_ref[...] * (1.0 + sc_ref[who])) + sh_ref[who]
        o_ref[rows, :] = y.astype(BF16)

    i = pl.program_id(0)
    last = pl.num_programs(0) - 1

    @pl.when(i < last)
    def _():
        emit(x_ref[...], 0, slice(None))

    @pl.when(i == last)
    def _():
        if lat_last:
            emit(x_ref[:lat_last, :], 0, slice(0, lat_last))
        emit(c_ref[...], 1, slice(lat_last, None))


def _norm_mod(x2, ctx2, gain, sc2, sh2, tm):
    s_len, d = x2.shape
    n_ctx = ctx2.shape[0]
    n_tiles = (s_len + n_ctx) // tm
    lat_last = s_len - (n_tiles - 1) * tm
    assert n_tiles * tm == s_len + n_ctx and lat_last + n_ctx == tm and lat_last % 16 == 0
    both = pl.BlockSpec((2, 1, d), lambda i: (0, 0, 0))
    return pl.pallas_call(
        functools.partial(_norm_mod_kernel, lat_last=lat_last),
        grid=(n_tiles,),
        in_specs=[pl.BlockSpec((tm, d), lambda i: (jnp.minimum(i, (s_len - 1) // tm), 0)),
                  pl.BlockSpec((n_ctx, d), lambda i: (0, 0)),
                  pl.BlockSpec((1, d), lambda i: (0, 0)),
                  both, both],
        out_specs=pl.BlockSpec((tm, d), lambda i: (i, 0)),
        out_shape=jax.ShapeDtypeStruct((n_tiles * tm, d), BF16),
        compiler_params=pltpu.CompilerParams(
            dimension_semantics=("arbitrary",),
            vmem_limit_bytes=_vmem(2 * tm * d * 6 + 2 * n_ctx * d * 4 + 4 * tm * d * 4 + 4 * 1024 * 1024)),
        name="norm_mod",
    )(x2, ctx2, gain, sc2, sh2)


def _rope_heads(acc, cos, sin, n_heads):
    lane = lax.broadcasted_iota(jnp.int32, (1, ATT_HEAD_DIM), 1)
    first = (lane % 64) < 32
    outs = []
    for h in range(n_heads):
        xh = acc[:, h * ATT_HEAD_DIM:(h + 1) * ATT_HEAD_DIM]
        partner = jnp.where(first, pltpu.roll(xh, 96, 1), pltpu.roll(xh, 32, 1))
        outs.append(xh * cos + partner * sin)
    return outs


def _proj_kernel(a_ref, w_ref, rcos_ref, rsin_ref, ccos_ref, csin_ref, o_ref, *, rope_heads, s_len):
    acc = _dot(a_ref[...], w_ref[...])
    tm = a_ref.shape[0]
    i, j = pl.program_id(0), pl.program_id(1)
    row = i * tm + lax.broadcasted_iota(jnp.int32, (tm, 1), 0)
    rotate = row < jnp.where(j == 0, s_len, 0)
    cos = jnp.where(rotate, (rcos_ref[...] + ccos_ref[...][None]).reshape(tm, ATT_HEAD_DIM), 1.0)
    sin = jnp.where(rotate, (rsin_ref[...] + csin_ref[...][None]).reshape(tm, ATT_HEAD_DIM), 0.0)
    for h, r in enumerate(_rope_heads(acc, cos, sin, rope_heads)):
        o_ref[:, h * ATT_HEAD_DIM:(h + 1) * ATT_HEAD_DIM] = r.astype(o_ref.dtype)
    o_ref[:, rope_heads * ATT_HEAD_DIM:] = acc[:, rope_heads * ATT_HEAD_DIM:].astype(o_ref.dtype)


def _proj(a, w, rope_tables, s_len, tm, tn, rope_heads):
    m, k = a.shape
    n = w.shape[1]
    n_tiles = m // tm
    assert rope_heads * ATT_HEAD_DIM <= tn and tn % LANES == 0 and tm % GRID_W == 0
    g = tm // GRID_W
    row_spec = pl.BlockSpec((g, 1, ATT_HEAD_DIM), lambda i, j: (i, 0, 0))
    col_spec = pl.BlockSpec((GRID_W, ATT_HEAD_DIM), lambda i, j: (0, 0))
    return pl.pallas_call(
        functools.partial(_proj_kernel, rope_heads=rope_heads, s_len=s_len),
        grid=(n_tiles, pl.cdiv(n, tn)),
        in_specs=[pl.BlockSpec((tm, k), lambda i, j: (i, 0)),
                  pl.BlockSpec((k, tn), lambda i, j: (0, j)),
                  row_spec, row_spec, col_spec, col_spec],
        out_specs=pl.BlockSpec((tm, tn), lambda i, j: (i, j)),
        out_shape=jax.ShapeDtypeStruct((m, n), BF16),
        compiler_params=pltpu.CompilerParams(
            dimension_semantics=("arbitrary", "arbitrary"),
            vmem_limit_bytes=_vmem(2 * (tm * k * 2 + k * tn * 2 + tm * tn * 2 + 2 * tm * 512)
                                   + tm * tn * 8 + 4 * 1024 * 1024)),
        name="proj_in",
    )(a, w, *rope_tables)


def _attn_kernel(sink_ref, q_ref, *rest, nb, n_ctx):
    nq = ATTN_BLOCKS_PER_STEP
    k_refs, v_refs = rest[:nq + 2], rest[nq + 2:2 * (nq + 2)]
    kx_ref, vx_ref, o_ref = rest[2 * (nq + 2):]
    i = pl.program_id(0)
    scale = ATT_HEAD_DIM ** -0.5
    rows = GQA_GROUP * BLOCK
    span = 3 * BLOCK
    ncol = span + n_ctx
    row = lax.broadcasted_iota(jnp.int32, (rows, ncol), 0) % BLOCK
    col = lax.broadcasted_iota(jnp.int32, (rows, ncol), 1)
    in_window = jnp.abs(col - BLOCK - row) <= WINDOW
    chains = [(t, hk) for t in range(nq) for hk in range(N_KV_HEADS)]
    valid = []
    for t in range(nq):
        blk = nq * i + t
        lo = jnp.where(blk > 0, 0, BLOCK)
        hi = jnp.where(blk < nb - 1, span, 2 * BLOCK)
        valid.append((in_window & (col >= lo) & (col < hi)) | (col >= span))
    ks = lambda hk: slice(hk * ATT_HEAD_DIM, (hk + 1) * ATT_HEAD_DIM)
    q = [(jnp.concatenate(
        [q_ref[t * BLOCK:(t + 1) * BLOCK, (hk * GQA_GROUP + g) * ATT_HEAD_DIM:(hk * GQA_GROUP + g + 1) * ATT_HEAD_DIM]
         for g in range(GQA_GROUP)], axis=0).astype(F32) * scale).astype(BF16) for t, hk in chains]
    k_all = [jnp.concatenate([k_refs[t][:, ks(hk)], k_refs[t + 1][:, ks(hk)], k_refs[t + 2][:, ks(hk)],
                              kx_ref[:, ks(hk)]], axis=0).astype(BF16) for t, hk in chains]
    v_all = [jnp.concatenate([v_refs[t][:, ks(hk)], v_refs[t + 1][:, ks(hk)], v_refs[t + 2][:, ks(hk)],
                              vx_ref[:, ks(hk)]], axis=0).astype(BF16) for t, hk in chains]
    sink = [jnp.concatenate([jnp.full((BLOCK, 1), sink_ref[hk * GQA_GROUP + g], F32) for g in range(GQA_GROUP)],
                            axis=0) for _, hk in chains]
    s = [jnp.where(valid[t], _dot_nt(qh, kh), -jnp.inf) for (t, _), qh, kh in zip(chains, q, k_all)]
    m = [jnp.maximum(jnp.max(sh, axis=-1, keepdims=True), sk) for sh, sk in zip(s, sink)]
    e = [jnp.exp(sh - mh) for sh, mh in zip(s, m)]
    denom = [jnp.sum(eh, axis=-1, keepdims=True) + jnp.exp(sk - mh) for eh, sk, mh in zip(e, sink, m)]
    o = [_dot(eh.astype(BF16), vh) * (1.0 / dh) for eh, vh, dh in zip(e, v_all, denom)]
    for (t, hk), oc in zip(chains, o):
        for g in range(GQA_GROUP):
            h = hk * GQA_GROUP + g
            o_ref[t * BLOCK:(t + 1) * BLOCK, h * ATT_HEAD_DIM:(h + 1) * ATT_HEAD_DIM] = (
                oc[g * BLOCK:(g + 1) * BLOCK].astype(BF16))


def _attention(qkv, sink, s_len, n_ctx):
    nq = ATTN_BLOCKS_PER_STEP
    nb = s_len // BLOCK
    assert nb % nq == 0 and s_len % n_ctx == 0
    kcol = ATT_WIDTH // KV_WIDTH
    vcol = kcol + 1
    ctx_blk = s_len // n_ctx
    kv = lambda u, c: pl.BlockSpec((BLOCK, KV_WIDTH), lambda i: (jnp.clip(nq * i - 1 + u, 0, nb - 1), c))
    return pl.pallas_call(
        functools.partial(_attn_kernel, nb=nb, n_ctx=n_ctx),
        grid=(nb // nq,),
        in_specs=[pl.BlockSpec(memory_space=pltpu.SMEM),
                  pl.BlockSpec((nq * BLOCK, ATT_WIDTH), lambda i: (i, 0))]
                 + [kv(u, kcol) for u in range(nq + 2)] + [kv(u, vcol) for u in range(nq + 2)]
                 + [pl.BlockSpec((n_ctx, KV_WIDTH), lambda i: (ctx_blk, kcol)),
                    pl.BlockSpec((n_ctx, KV_WIDTH), lambda i: (ctx_blk, vcol))],
        out_specs=pl.BlockSpec((nq * BLOCK, ATT_WIDTH), lambda i: (i, 0)),
        out_shape=jax.ShapeDtypeStruct((s_len, ATT_WIDTH), BF16),
        compiler_params=pltpu.CompilerParams(
            dimension_semantics=("arbitrary",),
            vmem_limit_bytes=_vmem(40 * 1024 * 1024)),
        name="attention",
    )(sink, qkv, *([qkv] * (2 * (nq + 2) + 2)))


def _prep_kernel(x_ref, xp_ref, xn_ref, mup_ref, mun_ref, w2_ref, w0_ref, a2_ref, a0_ref, g2_ref,
                 kk_ref, ka_ref, rk_ref, seg_ref, exp_ref,
                 r_o, v_o, kk_o, g_o, bon_o, lwf_o, lwb_o, kdf_o, kdb_o, bf_o, bb_o,
                 *, tm, start_tiles, end_tiles):
    i = pl.program_id(0)
    r_w = RWKV_WIDTH
    slab = slice(ATT_IN, ATT_IN + mup_ref.shape[1])
    x = x_ref[:, slab].astype(F32)
    is_start = functools.reduce(jnp.logical_or, [i == t for t in start_tiles])
    is_end = functools.reduce(jnp.logical_or, [i == t for t in end_tiles])
    keep_prev = jnp.full((1, 1), jnp.where(is_start, 0, 1), jnp.int32) == 1
    keep_next = jnp.full((1, 1), jnp.where(is_end, 0, 1), jnp.int32) == 1
    prev_row = jnp.where(keep_prev, xp_ref[HALO_ROWS - 1:HALO_ROWS, slab].astype(F32), 0.0)
    next_row = jnp.where(keep_next, xn_ref[0:1, slab].astype(F32), 0.0)
    xp = pltpu.roll(x, 1, 0)
    xn = pltpu.roll(x, tm - 1, 0)
    row8 = lax.broadcasted_iota(jnp.int32, (8, 1), 0)
    xp = jnp.concatenate([jnp.where(row8 == 0, prev_row, xp[:8]), xp[8:]], axis=0)
    xn = jnp.concatenate([xn[:tm - 8], jnp.where(row8 == 7, next_row, xn[tm - 8:])], axis=0)
    mup = mup_ref[...]
    mun = mun_ref[...]
    s = x * (1.0 - mup - mun) + xp * mup + xn * mun

    r = s[:, :r_w]
    k = s[:, r_w:2 * r_w]
    v = s[:, 2 * r_w:3 * r_w]
    o1 = 3 * r_w
    o2 = o1 + 2 * LORA_W
    o3 = o2 + 2 * LORA_A
    w_lo = s[:, o1:o2]
    a_lo = s[:, o2:o3]
    g_lo = s[:, o3:]

    tw = jnp.tanh(w0_ref[...] + _bdot(jnp.tanh(w_lo), w2_ref[...]))
    ta = jnp.tanh(a0_ref[...] + _bdot(a_lo, a2_ref[...]))
    lw = (-0.5 * DECAY_SCALE) * tw - 0.5 * DECAY_SCALE
    g = _bdot(_sigmoid(g_lo), g2_ref[...])

    kkf = k * kk_ref[...]
    ssq = _split_dot(_bdot(kkf * kkf, seg_ref[...]), exp_ref[...])
    kk = kkf * jnp.minimum(lax.rsqrt(ssq), 1e12)
    kkh = k * (0.5 * ka_ref[...])
    k1 = k - kkh
    kd_f = k1 + kkh * ta[:, :r_w]
    kd_b = k1 + kkh * ta[:, r_w:]
    k_bonus = 0.5 * (kd_f + kd_b)
    bonus = _split_dot(_bdot(r * k_bonus * rk_ref[...], seg_ref[...]), exp_ref[...]) * v
    kk_half = 0.5 * kk

    r_o[...] = r.astype(r_o.dtype)
    v_o[...] = v.astype(v_o.dtype)
    kk_o[...] = kk.astype(kk_o.dtype)
    g_o[...] = g.astype(g_o.dtype)
    bon_o[...] = bonus.astype(bon_o.dtype)
    lwf_o[...] = lw[:, :r_w]
    lwb_o[...] = lw[:, r_w:]
    kdf_o[...] = kd_f.astype(kdf_o.dtype)
    kdb_o[...] = kd_b.astype(kdb_o.dtype)
    bf_o[...] = (kk_half + kk_half * ta[:, :r_w]).astype(bf_o.dtype)
    bb_o[...] = (kk_half + kk_half * ta[:, r_w:]).astype(bb_o.dtype)


def _rwkv_prep(slab, mup, mun, w2blk, w0cat, a2blk, a0cat, g2, k_k, k_a, r_k, seg, expand, s_len, n_ctx, tm):
    m, width = slab.shape
    nt = m // tm
    hb = tm // HALO_ROWS
    nblk8 = m // HALO_ROWS
    start_tiles = (0, s_len // tm)
    end_tiles = (s_len // tm - 1, nt - 1)
    full = lambda arr: pl.BlockSpec(arr.shape, lambda i: (0,) * arr.ndim)
    consts = (mup, mun, w2blk, w0cat, a2blk, a0cat, g2, k_k, k_a, r_k, seg, expand)
    out_spec = pl.BlockSpec((tm, RWKV_WIDTH), lambda i: (i, 0))
    n_out = len(PREP_OUT_DTYPES)
    return pl.pallas_call(
        functools.partial(_prep_kernel, tm=tm, start_tiles=start_tiles, end_tiles=end_tiles),
        grid=(nt,),
        in_specs=[pl.BlockSpec((tm, width), lambda i: (i, 0)),
                  pl.BlockSpec((HALO_ROWS, width), lambda i: (jnp.maximum(i * hb - 1, 0), 0)),
                  pl.BlockSpec((HALO_ROWS, width), lambda i: (jnp.minimum((i + 1) * hb, nblk8 - 1), 0))]
                 + [full(c) for c in consts],
        out_specs=[out_spec] * n_out,
        out_shape=[jax.ShapeDtypeStruct((m, RWKV_WIDTH), dt) for dt in PREP_OUT_DTYPES],
        compiler_params=pltpu.CompilerParams(
            dimension_semantics=("arbitrary",),
            vmem_limit_bytes=_vmem(2 * tm * width * 4 + 2 * n_out * tm * RWKV_WIDTH * 4
                                   + 12 * tm * RWKV_WIDTH * 4 + 8 * 1024 * 1024)),
        name="rwkv_prep",
    )(slab, slab, slab, *consts)


def _stack(x, m0):
    return jnp.concatenate([jnp.where(m0, x, 0.0), jnp.where(m0, 0.0, x)], axis=0)


def _scan_streams(r_ref, v_ref, kk_ref, lw_ref, kd_ref, b_ref, reverse):
    L = CHUNK
    lw = lw_ref[...]
    ti = lax.broadcasted_iota(jnp.int32, (L, L), 0)
    tj = lax.broadcasted_iota(jnp.int32, (L, L), 1)
    tri = jnp.where((ti <= tj) if reverse else (ti >= tj), 1.0, 0.0).astype(BF16)
    lw_hi = lw.astype(BF16)
    lw_lo = (lw - lw_hi.astype(F32)).astype(BF16)
    c = _dot(tri, lw_hi) + _dot(tri, lw_lo)
    p_cum = jnp.exp(c)
    p_inv = jnp.exp(-c)
    p_prev = jnp.exp(c - lw)
    last = 0 if reverse else L - 1
    p_end = p_cum[last:last + 1, :]
    b_t = b_ref[...] * p_inv
    k_t = kd_ref[...] * p_inv
    gi = lax.broadcasted_iota(jnp.int32, (2 * L, 2 * L), 0)
    gj = lax.broadcasted_iota(jnp.int32, (2 * L, 2 * L), 1)
    same = (gi // L) == (gj // L)
    strict = same & ((gi < gj) if reverse else (gi > gj))
    incl = same & ((gi <= gj) if reverse else (gi >= gj))
    return dict(a=-kk_ref[...] * p_prev, r=r_ref[...] * p_cum, b=b_t, k=k_t, bh=b_t * p_end, kh=k_t * p_end,
                v=v_ref[...], p_end=p_end, strict=strict, incl=incl)


def _scan_kernel(rf, vf, kkf, lwf, kdf, bf, rb, vb, kkb, lwb, kdb, bb, *rest, n_cast):
    cast_in, (yf, yb), cast_out, s_ref = rest[:n_cast], rest[n_cast:n_cast + 2], rest[n_cast + 2:-1], rest[-1]

    @pl.when(pl.program_id(0) == 0)
    def _():
        s_ref[...] = jnp.zeros_like(s_ref)

    for src, dst in zip(cast_in, cast_out):
        dst[...] = src[...].astype(dst.dtype)

    L = CHUNK
    streams = (_scan_streams(rf, vf, kkf, lwf, kdf, bf, False), _scan_streams(rb, vb, kkb, lwb, kdb, bb, True))
    y_refs = (yf, yb)
    m0 = lax.broadcasted_iota(jnp.int32, (L, PAIR), 1) < RWKV_HEAD_DIM
    chains = [(d, p) for d in range(2) for p in range(N_PAIRS)]

    def side(name, d, p):
        return streams[d][name][:, p * PAIR:(p + 1) * PAIR]

    def stacked(name, d, p):
        return _stack(side(name, d, p), m0)

    head0_rows = (lax.broadcasted_iota(jnp.int32, (4 * L, PAIR), 0) // L) % 2 == 0
    same_head = ((lax.broadcasted_iota(jnp.int32, (PAIR, PAIR), 0) // RWKV_HEAD_DIM)
                 == (lax.broadcasted_iota(jnp.int32, (PAIR, PAIR), 1) // RWKV_HEAD_DIM))
    ar = [jnp.concatenate([stacked("a", d, p), stacked("r", d, p)], axis=0).astype(BF16) for d, p in chains]
    bk = [jnp.concatenate([side("b", d, p), side("k", d, p)], axis=0).astype(BF16) for d, p in chains]
    v_s = [stacked("v", d, p).astype(BF16) for d, p in chains]
    state = [s_ref[d, p] for d, p in chains]
    g = [_dot_nt(x, z) for x, z in zip(ar, bk)]
    xs = [_dot_nt(x, s.astype(BF16)) for x, s in zip(ar, state)]
    a_ab, a_kv, a_rb = [], [], []
    for (d, _), gc in zip(chains, g):
        strict, incl = streams[d]["strict"], streams[d]["incl"]
        gr = pltpu.roll(gc, RWKV_HEAD_DIM, 1)
        vs_b = jnp.where(head0_rows, gc, gr)
        vs_k = jnp.where(head0_rows, gr, gc)
        a_ab.append(jnp.where(strict, vs_b[:2 * L], 0.0).astype(BF16))
        a_kv.append(jnp.concatenate([jnp.where(strict, vs_k[:2 * L], 0.0),
                                     jnp.where(incl, vs_k[2 * L:], 0.0)], axis=0).astype(BF16))
        a_rb.append(jnp.where(incl, vs_b[2 * L:], 0.0).astype(BF16))
    av = [_dot(a, v) for a, v in zip(a_kv, v_s)]
    w = [x[:2 * L] + y[:2 * L] for x, y in zip(xs, av)]
    am = a_ab
    w = [wc + _dot(a, wc.astype(BF16)) for a, wc in zip(am, w)]
    for _ in range(5):
        am = [_dot(a, a).astype(BF16) for a in am]
        w = [wc + _dot(a, wc.astype(BF16)) for a, wc in zip(am, w)]
    sa = [wc.astype(BF16) for wc in w]
    ys = [x[2 * L:] + y[2 * L:] + _dot(a, s) for x, y, a, s in zip(xs, av, a_rb, sa)]
    for (d, p), yc in zip(chains, ys):
        y_refs[d][:, p * PAIR:(p + 1) * PAIR] = yc[:L] + yc[L:]
    for (d, p), s_old, wc in zip(chains, state, w):
        sa_v = jnp.concatenate([wc[:L] + wc[L:], side("v", d, p)], axis=0).astype(BF16)
        bhkh = jnp.concatenate([side("bh", d, p), side("kh", d, p)], axis=0).astype(BF16)
        upd = jnp.where(same_head, _dot_tn(sa_v, bhkh), 0.0)
        s_ref[d, p] = s_old * side("p_end", d, p) + upd


def _cast_slices(w, l, n_steps):
    _, n_rows, n_cols = w.shape
    n_blk = 1
    while n_blk * 2 <= n_steps and n_rows % (n_blk * 2) == 0 and (n_rows // (n_blk * 2)) % 16 == 0:
        n_blk *= 2
    rows = n_rows // n_blk
    step = lambda i: jnp.minimum(i, n_blk - 1)
    return (pl.BlockSpec((None, rows, n_cols), lambda i: (l, step(i), 0)),
            pl.BlockSpec((rows, n_cols), lambda i: (step(i), 0)),
            jax.ShapeDtypeStruct((n_rows, n_cols), BF16))


def _rwkv_scan(r, v, kk, lwf, lwb, kdf, kdb, bf, bb, s_len, n_ctx, cast_weights, l):
    m = r.shape[0]
    nc = m // CHUNK
    nc_lat = s_len // CHUNK
    nc_ctx = n_ctx // CHUNK
    fwd = lambda i: (jnp.where(i < nc_ctx, nc_lat + i, i - nc_ctx), 0)
    bwd = lambda i: (nc - 1 - i, 0)
    fs = pl.BlockSpec((CHUNK, RWKV_WIDTH), fwd)
    bs = pl.BlockSpec((CHUNK, RWKV_WIDTH), bwd)
    cast_in, cast_out, cast_shapes = zip(*[_cast_slices(w, l, nc) for w in cast_weights])
    cast_bytes = sum(spec.block_shape[0] * spec.block_shape[1] * 6 for spec in cast_out)
    outs = pl.pallas_call(
        functools.partial(_scan_kernel, n_cast=len(cast_weights)),
        grid=(nc,),
        in_specs=[fs] * 6 + [bs] * 6 + list(cast_in),
        out_specs=[fs, bs] + list(cast_out),
        out_shape=[jax.ShapeDtypeStruct((m, RWKV_WIDTH), F32)] * 2 + list(cast_shapes),
        scratch_shapes=[pltpu.VMEM((2, N_PAIRS, PAIR, PAIR), F32)],
        compiler_params=pltpu.CompilerParams(
            dimension_semantics=("arbitrary",),
            vmem_limit_bytes=_vmem(2 * 14 * CHUNK * RWKV_WIDTH * 4 + 2 * N_PAIRS * PAIR * PAIR * 4
                                   + 2 * cast_bytes + 24 * 1024 * 1024)),
        name="rwkv_scan",
    )(r, v, kk, lwf, kdf, bf, r, v, kk, lwb, kdb, bb, *cast_weights)
    return outs[0], outs[1], outs[2:]


def _out_kernel(att_ref, yf_ref, yb_ref, g_ref, bon_ref, x_ref, wa_ref, wr_ref, lnw_ref, lnb_ref,
                seg_ref, exp_ref, gpm_ref, gt1_ref, gmlp_ref, sc2_ref, sh2_ref, x1_ref, h2_ref):
    mix_att = _dot(att_ref[...], wa_ref[...])
    inv_n = 1.0 / RWKV_HEAD_DIM
    seg, expand = seg_ref[...], exp_ref[...]
    rows = att_ref.shape[0] // OUT_SUB_BLOCKS
    blocks = [pl.ds(n * rows, rows) for n in range(OUT_SUB_BLOCKS)]
    y = [yf_ref[b, :] + yb_ref[b, :] for b in blocks]
    s1 = [_bdot(v, seg) for v in y]
    yc = [v - _split_dot(s, expand) * inv_n for v, s in zip(y, s1)]
    s2 = [_bdot(c * c, seg) for c in yc]
    var = [_bdot(s, expand) * inv_n for s in s2]
    rec = [((c * lax.rsqrt(vr + GN_EPS) * lnw_ref[...] + lnb_ref[...]) + bon_ref[b, :]) * g_ref[b, :]
           for c, vr, b in zip(yc, var, blocks)]
    mix = mix_att + _bdot(jnp.concatenate(rec, axis=0), wr_ref[...])
    ms = jnp.mean(mix * mix, axis=-1, keepdims=True)
    x1 = x_ref[...] + (mix * lax.rsqrt(ms + NORM_EPS)) * (gt1_ref[...] * gpm_ref[...])
    x1_ref[...] = x1
    ms2 = jnp.mean(x1 * x1, axis=-1, keepdims=True)
    h2 = (x1 * lax.rsqrt(ms2 + NORM_EPS)) * (gmlp_ref[...] * (1.0 + sc2_ref[...])) + sh2_ref[...]
    h2_ref[...] = h2.astype(BF16)


def _out_proj(att, yf, yb, g, bonus, x, w_out, lnw, lnb, seg, expand, gpm, gt1, gmlp, sc2, sh2, tm):
    s_len, d = x.shape
    assert ATT_WIDTH == RWKV_WIDTH and w_out.shape == (ATT_WIDTH + RWKV_WIDTH, d)
    row = lambda w: pl.BlockSpec((tm, w), lambda i: (i, 0))
    full = lambda arr: pl.BlockSpec(arr.shape, lambda i: (0,) * arr.ndim)
    w_half = lambda n: pl.BlockSpec((ATT_WIDTH, d), lambda i: (n, 0))
    consts = (lnw, lnb, seg, expand, gpm, gt1, gmlp, sc2, sh2)
    return pl.pallas_call(
        _out_kernel,
        grid=(s_len // tm,),
        in_specs=[row(ATT_WIDTH)] + [row(RWKV_WIDTH)] * 4 + [row(d), w_half(0), w_half(1)]
                 + [full(c) for c in consts],
        out_specs=[row(d), row(d)],
        out_shape=[jax.ShapeDtypeStruct((s_len, d), F32), jax.ShapeDtypeStruct((s_len, d), BF16)],
        compiler_params=pltpu.CompilerParams(
            dimension_semantics=("arbitrary",),
            vmem_limit_bytes=_vmem(2 * tm * (ATT_WIDTH * 2 + 4 * RWKV_WIDTH * 4 + d * 4 + d * 4 + d * 2)
                                   + 2 * 2 * d * d + 8 * tm * d * 4 + 4 * 1024 * 1024)),
        name="out_proj",
    )(att, yf, yb, g, bonus, x, w_out, w_out, *consts)


def _mlp_kernel(h_ref, w1_ref, w2_ref, x1_ref, g_ref, gt_ref, o_ref, acc_ref):
    f = pl.program_id(1)

    @pl.when(f == 0)
    def _():
        acc_ref[...] = jnp.zeros_like(acc_ref)

    u = jnp.maximum(_dot(h_ref[...], w1_ref[...]), 0.0)
    acc_ref[...] += _dot((u * u).astype(BF16), w2_ref[...])

    @pl.when(f == pl.num_programs(1) - 1)
    def _():
        ff = acc_ref[...]
        ms = jnp.mean(ff * ff, axis=-1, keepdims=True)
        o_ref[...] = x1_ref[...] + (ff * lax.rsqrt(ms + NORM_EPS)) * (gt_ref[...] * g_ref[...])


def _mlp(h2, w1, w2, x1, gain, gt2, tm, tf):
    s_len, d = x1.shape
    dff = w1.shape[1]
    return pl.pallas_call(
        _mlp_kernel,
        grid=(s_len // tm, dff // tf),
        in_specs=[pl.BlockSpec((tm, d), lambda i, f: (i, 0)),
                  pl.BlockSpec((d, tf), lambda i, f: (0, f)),
                  pl.BlockSpec((tf, d), lambda i, f: (f, 0)),
                  pl.BlockSpec((tm, d), lambda i, f: (i, 0)),
                  pl.BlockSpec((1, d), lambda i, f: (0, 0)),
                  pl.BlockSpec((1, d), lambda i, f: (0, 0))],
        out_specs=pl.BlockSpec((tm, d), lambda i, f: (i, 0)),
        out_shape=jax.ShapeDtypeStruct((s_len, d), F32),
        scratch_shapes=[pltpu.VMEM((tm, d), F32)],
        compiler_params=pltpu.CompilerParams(
            dimension_semantics=("arbitrary", "arbitrary"),
            vmem_limit_bytes=_vmem(2 * (tm * d * 2 + 2 * d * tf * 2 + 2 * tm * d * 4)
                                   + tm * d * 4 + 3 * tm * tf * 4 + 4 * 1024 * 1024)),
        name="mlp",
    )(h2, w1, w2, x1, gain, gt2)


def _rope_tables(s_len, rows_padded):
    rows = s_len // GRID_W
    n_freq = ATT_HEAD_DIM // 4
    inv_freq = ROPE_BASE ** (-jnp.arange(n_freq, dtype=F32) / n_freq)
    ang_r = jnp.arange(rows, dtype=F32)[:, None] * inv_freq[None, :]
    ang_c = jnp.arange(GRID_W, dtype=F32)[:, None] * inv_freq[None, :]
    zr = jnp.zeros((rows, 2 * n_freq), F32)
    zc = jnp.zeros((GRID_W, 2 * n_freq), F32)
    pad = lambda t: jnp.pad(t, ((0, rows_padded - rows), (0, 0)))[:, None, :]
    row_cos = pad(jnp.concatenate([jnp.cos(ang_r), jnp.cos(ang_r), zr], axis=-1))
    row_sin = pad(jnp.concatenate([-jnp.sin(ang_r), jnp.sin(ang_r), zr], axis=-1))
    col_cos = jnp.concatenate([zc, jnp.cos(ang_c), jnp.cos(ang_c)], axis=-1)
    col_sin = jnp.concatenate([zc, -jnp.sin(ang_c), jnp.sin(ang_c)], axis=-1)
    return row_cos, row_sin, col_cos, col_sin


class _Tiles(NamedTuple):
    cat_rows: int
    prep_rows: int
    out_rows: int
    mlp_rows: int
    mlp_ff: int


def _tile_sizes(s_len, n_ctx):
    cat_rows = 768 if (s_len + n_ctx) % 768 == 0 and s_len % 768 + n_ctx == 768 else n_ctx
    return _Tiles(cat_rows=cat_rows, prep_rows=256, out_rows=512, mlp_rows=512, mlp_ff=1024)


def _block_diag2(w):
    _, k, n = w.shape
    z = jnp.zeros((k, n), w.dtype)
    return jnp.concatenate([jnp.concatenate([w[0], z], axis=1), jnp.concatenate([z, w[1]], axis=1)], axis=0)


def _layer(x2, ctx2, c8, l, w_ada, b_ada, g_pre_mix, g_post_mix, g_pre_mlp, g_post_mlp, w_in, attn_sink,
           rwkv_mu_prev, rwkv_mu_next, rwkv_w0, rwkv_w2, rwkv_a0, rwkv_a2, rwkv_g2, rwkv_k_k, rwkv_k_a,
           rwkv_r_k, rwkv_ln_w, rwkv_ln_b, w_out, w_mlp_in, w_mlp_out):
    s_len, d = x2.shape
    n_ctx = ctx2.shape[0]
    row = lambda v: v.reshape(1, -1)

    mod = _ada(c8, w_ada[l], row(b_ada[l]))
    sh1, sc1, gt1, sh2, sc2, gt2 = [mod[:2, j * d:(j + 1) * d] for j in range(N_MOD)]

    tiles = _tile_sizes(s_len, n_ctx)
    h = _norm_mod(x2, ctx2, row(g_pre_mix[l]), sc1.reshape(2, 1, d), sh1.reshape(2, 1, d), tiles.cat_rows)

    rope_tables = _rope_tables(s_len, ((s_len + n_ctx) // tiles.cat_rows) * (tiles.cat_rows // GRID_W))
    n_in = w_in.shape[2]
    tn_in = (n_in + -n_in % (2 * MXU_WIDTH)) // 2
    p_in = _proj(h, w_in[l].astype(BF16), rope_tables, s_len, tiles.cat_rows, tn_in, N_Q_HEADS + N_KV_HEADS)

    att = _attention(p_in, attn_sink[l], s_len, n_ctx)

    head = jnp.arange(RWKV_WIDTH) // RWKV_HEAD_DIM
    seg = (head[:, None] == jnp.arange(LANES)[None, :]).astype(BF16)
    expand = seg.T
    streams = _rwkv_prep(
        p_in, row(rwkv_mu_prev[l]), row(rwkv_mu_next[l]),
        (0.5 * _block_diag2(rwkv_w2[l])).astype(BF16), 0.5 * rwkv_w0[l].reshape(1, -1),
        (0.5 * _block_diag2(rwkv_a2[l])).astype(BF16), 0.5 * rwkv_a0[l].reshape(1, -1),
        rwkv_g2[l].astype(BF16), row(rwkv_k_k[l]), row(rwkv_k_a[l]), rwkv_r_k[l].reshape(1, -1),
        seg, expand, s_len, n_ctx, tiles.prep_rows)
    r, v, kk, g, bonus, lwf, lwb, kdf, kdb, bf, bb = streams
    yf, yb, (w_out_b, w_mlp_in_b, w_mlp_out_b) = _rwkv_scan(
        r, v, kk, lwf, lwb, kdf, kdb, bf, bb, s_len, n_ctx, (w_out, w_mlp_in, w_mlp_out), l)

    x1, h2 = _out_proj(att, yf, yb, g, bonus, x2, w_out_b,
                       row(rwkv_ln_w[l]), row(rwkv_ln_b[l]), seg, expand,
                       row(g_post_mix[l]), gt1[0:1], row(g_pre_mlp[l]), sc2[0:1], sh2[0:1], tiles.out_rows)
    return _mlp(h2, w_mlp_in_b, w_mlp_out_b, x1, row(g_post_mlp[l]), gt2[0:1], tiles.mlp_rows, tiles.mlp_ff)


def kernel(x, c, ctx, c_ctx, w_ada, b_ada, g_pre_mix, g_post_mix, g_pre_mlp, g_post_mlp, w_in, attn_sink,
           rwkv_mu_prev, rwkv_mu_next, rwkv_w0, rwkv_w2, rwkv_a0, rwkv_a2, rwkv_g2, rwkv_k_k, rwkv_k_a,
           rwkv_r_k, rwkv_ln_w, rwkv_ln_b, w_out, w_mlp_in, w_mlp_out):
    b, s_len, d = x.shape
    depth = w_ada.shape[0]
    assert b == 1 and depth == 1, "single sample, single layer (context stream is not carried to a next layer)"
    c8 = jnp.zeros((8, d), F32).at[0].set(c[0]).at[1].set(c_ctx)
    out = _layer(x[0], ctx[0], c8, 0, w_ada, b_ada, g_pre_mix, g_post_mix, g_pre_mlp, g_post_mlp, w_in,
                 attn_sink, rwkv_mu_prev, rwkv_mu_next, rwkv_w0, rwkv_w2, rwkv_a0, rwkv_a2, rwkv_g2,
                 rwkv_k_k, rwkv_k_a, rwkv_r_k, rwkv_ln_w, rwkv_ln_b, w_out, w_mlp_in, w_mlp_out)
    return out[None]
```
